```python
import math
import jax, jax.numpy as jnp
from jax import lax
import numpy as np

D_MODEL = 2048
BATCH = 2
SEQ = 4096
DEPTH = 2

N_BRANCH = 4
BRANCH_WIDTH = 512
HEAD_DIM = 64
ATT_BLOCK = 128
ROPE_THETA = 500000.0
ROPE_DIM = HEAD_DIM // 4
NORM_EPS = 1e-6
GLA_HEADS = 4
GLA_DK = 64
GLA_DV = BRANCH_WIDTH // GLA_HEADS
GLA_LOWRANK = 16
GLA_TAU = 16.0
GLA_CHUNK = 64
GLA_SUB = 16
S5_WIDTH = BRANCH_WIDTH
S5_GROUP = 16
S5_GROUPS = S5_WIDTH // S5_GROUP
S5_STATE = 64
S5_DT_MIN = 0.001
S5_DT_MAX = 0.1
DIL_HEADS = BRANCH_WIDTH // HEAD_DIM
DIL_CONFIGS = ((128, 1), (512, 4), (2048, 16))
DIL_MAX_DILATION = 16
SWA_HEADS = BRANCH_WIDTH // HEAD_DIM
SWA_KV_HEADS = 2
SWA_WINDOW = 128
FFN_MULTIPLE = 256
FFN_HIDDEN = -((-8 * D_MODEL) // (3 * FFN_MULTIPLE)) * FFN_MULTIPLE
IN_SPLITS = (GLA_HEADS * GLA_DK, GLA_HEADS * GLA_DK, GLA_HEADS * GLA_DV, GLA_HEADS * GLA_DV, GLA_LOWRANK,
             S5_WIDTH,
             DIL_HEADS * HEAD_DIM, DIL_HEADS * HEAD_DIM, DIL_HEADS * HEAD_DIM,
             SWA_HEADS * HEAD_DIM, SWA_KV_HEADS * HEAD_DIM, SWA_KV_HEADS * HEAD_DIM,
             N_BRANCH * D_MODEL)
D_IN = sum(IN_SPLITS)

kernel_name = 'hybrid_gla_s5_dilated_swa_block'


def rms_norm(x, g):
    xf = x.astype(jnp.float32)
    y = xf * lax.rsqrt(jnp.mean(xf * xf, axis=-1, keepdims=True) + NORM_EPS)
    return (y * g.astype(jnp.float32)).astype(x.dtype)


def rotary_partial(x, positions):
    half = ROPE_DIM // 2
    inv_freq = ROPE_THETA ** (-jnp.arange(half, dtype=jnp.float32) / half)
    ang = positions.astype(jnp.float32)[:, :, None] * inv_freq
    cos, sin = jnp.cos(ang)[:, :, None, :], jnp.sin(ang)[:, :, None, :]
    xr = x[..., :ROPE_DIM].astype(jnp.float32)
    x1, x2 = xr[..., :half], xr[..., half:]
    rot = jnp.concatenate([x1 * cos - x2 * sin, x2 * cos + x1 * sin], axis=-1)
    return jnp.concatenate([rot.astype(x.dtype), x[..., ROPE_DIM:]], axis=-1)


def banded_attention(q, k, v, max_dist):
    f = jnp.float32
    b, l, hq, hd = q.shape
    hkv = k.shape[2]
    grp = hq // hkv
    t = ATT_BLOCK
    nb = l // t
    qb = q.astype(f).reshape(b, nb, t, hkv, grp, hd)

    def with_prev(z):
        z = z.astype(f).reshape(b, nb, t, hkv, hd)
        prev = jnp.concatenate([jnp.zeros_like(z[:, :1]), z[:, :-1]], axis=1)
        return jnp.concatenate([prev, z], axis=2)

    kk, vv = with_prev(k), with_prev(v)
    s = jnp.einsum('bnqhgd,bnkhd->bnhgqk', qb, kk) * (hd ** -0.5)
    qi = jnp.arange(t)[:, None]
    kj = jnp.arange(2 * t)[None, :]
    dist = t + qi - kj
    blk = jnp.arange(nb)[:, None, None]
    valid = (dist >= 0) & (dist <= max_dist) & (blk * t + kj - t >= 0)
    s = jnp.where(valid[None, :, None, None], s, -jnp.inf)
    m = jnp.max(s, axis=-1, keepdims=True)
    p = jnp.exp(s - m)
    den = jnp.sum(p, axis=-1)
    o = jnp.einsum('bnhgqk,bnkhd->bnqhgd', p, vv) / den.transpose(0, 1, 4, 2, 3)[..., None]
    lse = (m[..., 0] + jnp.log(den)).transpose(0, 1, 4, 2, 3)
    return o.reshape(b, l, hq, hd).astype(q.dtype), lse.reshape(b, l, hq)


def to_strided(z, dil):
    b, lp, h, hd = z.shape
    return z.reshape(b, lp // dil, dil, h, hd).transpose(0, 2, 1, 3, 4).reshape(b * dil, lp // dil, h, hd)


def from_strided(z, b, dil):
    bd, ls = z.shape[:2]
    rest = z.shape[2:]
    z = z.reshape((b, dil, ls) + rest)
    z = jnp.swapaxes(z, 1, 2)
    return z.reshape((b, ls * dil) + rest)


def dilated_attention(q, k, v):
    f = jnp.float32
    b, l, h, hd = q.shape
    span = ATT_BLOCK * DIL_MAX_DILATION
    lp = -(-l // span) * span
    pad = ((0, 0), (0, lp - l), (0, 0), (0, 0))
    q, k, v = jnp.pad(q, pad), jnp.pad(k, pad), jnp.pad(v, pad)
    outs, lses = [], []
    for window, dil in DIL_CONFIGS:
        o, lse = banded_attention(to_strided(q, dil), to_strided(k, dil), to_strided(v, dil), window // dil)
        outs.append(from_strided(o, b, dil)[:, :l].astype(f))
        lses.append(from_strided(lse, b, dil)[:, :l])
    wts = jax.nn.softmax(jnp.stack(lses, axis=0), axis=0)
    return jnp.einsum('cblh,cblhd->blhd', wts, jnp.stack(outs, axis=0))


def gla_chunked(q, k, v, log_a):
    f = jnp.float32
    b, l, h, dk = q.shape
    dv = v.shape[-1]
    c = GLA_CHUNK
    n = l // c
    ns = c // GLA_SUB
    t = GLA_SUB

    def chunks(z):
        return z.astype(f).reshape(b, n, c, h, z.shape[-1]).transpose(0, 3, 1, 2, 4)

    q, k, v, g = chunks(q), chunks(k), chunks(v), chunks(log_a)
    q = q * (dk ** -0.5)
    cum = jnp.cumsum(g, axis=3)
    qs = q.reshape(b, h, n, ns, t, dk)
    ks = k.reshape(b, h, n, ns, t, dk)
    cs = cum.reshape(b, h, n, ns, t, dk)
    ref = cs[:, :, :, :, 0] - g.reshape(b, h, n, ns, t, dk)[:, :, :, :, 0]
    q_ref = qs * jnp.exp(cs - ref[:, :, :, :, None])
    earlier = (jnp.arange(c) // t)[None, :] < jnp.arange(ns)[:, None]
    k_ref = k[:, :, :, None] * jnp.exp(jnp.where(earlier[:, :, None], ref[:, :, :, :, None] - cum[:, :, :, None], -jnp.inf))
    a_off = jnp.einsum('bhnstk,bhnsjk->bhnstj', q_ref, k_ref)
    tri = jnp.arange(t)[:, None] >= jnp.arange(t)[None, :]
    dec = jnp.exp(jnp.where(tri[:, :, None], cs[..., :, None, :] - cs[..., None, :, :], -jnp.inf))
    a_diag = jnp.einsum('bhnstk,bhnsuk,bhnstuk->bhnstu', qs, ks, dec)
    a = a_off + (a_diag[:, :, :, :, :, None, :] * jnp.eye(ns, dtype=f)[:, None, :, None]).reshape(b, h, n, ns, t, c)
    o_intra = jnp.einsum('bhnstj,bhnjv->bhnstv', a, v).reshape(b, h, n, c, dv)
    last = cum[:, :, :, -1]
    kv = jnp.einsum('bhnck,bhncv->nbhkv', k * jnp.exp(last[:, :, :, None] - cum), v)

    def step(state, inp):
        decay, upd = inp
        return state * decay[..., None] + upd, state

    _, states = lax.scan(step, jnp.zeros((b, h, dk, dv), f), (jnp.exp(last).transpose(2, 0, 1, 3), kv))
    o_inter = jnp.einsum('bhnck,nbhkv->bhncv', q * jnp.exp(cum), states)
    return (o_intra + o_inter).transpose(0, 2, 3, 1, 4).reshape(b, l, h, dv)


def s5_mixer(u, lam_re, lam_im, log_dt, b_re, b_im, c_re, c_im, d, glu_w, glu_b):
    f = jnp.float32
    bsz, l, _ = u.shape
    uf = u.astype(f).reshape(bsz, l, S5_GROUPS, S5_GROUP)
    dt = jnp.exp(log_dt.astype(f))[:, None]
    lr, li = lam_re.astype(f), lam_im.astype(f)
    mag = jnp.exp(lr * dt)
    ab_re, ab_im = mag * jnp.cos(li * dt), mag * jnp.sin(li * dt)
    den = lr * lr + li * li
    z_re = ((ab_re - 1.0) * lr + ab_im * li) / den
    z_im = (ab_im * lr - (ab_re - 1.0) * li) / den
    br, bi = b_re.astype(f), b_im.astype(f)
    bb_re = z_re[..., None] * br - z_im[..., None] * bi
    bb_im = z_re[..., None] * bi + z_im[..., None] * br
    x_re = jnp.einsum('gnc,blgc->blgn', bb_re, uf)
    x_im = jnp.einsum('gnc,blgc->blgn', bb_im, uf)
    a_re = jnp.broadcast_to(ab_re, x_re.shape)
    a_im = jnp.broadcast_to(ab_im, x_im.shape)

    def combine(e1, e2):
        a1r, a1i, b1r, b1i = e1
        a2r, a2i, b2r, b2i = e2
        return (a1r * a2r - a1i * a2i, a1r * a2i + a1i * a2r,
                a2r * b1r - a2i * b1i + b2r, a2r * b1i + a2i * b1r + b2i)

    _, _, h_re, h_im = lax.associative_scan(combine, (a_re, a_im, x_re, x_im), axis=1)
    y = (jnp.einsum('gcn,blgn->blgc', c_re.astype(f), h_re)
         - jnp.einsum('gcn,blgn->blgc', c_im.astype(f), h_im)
         + d.astype(f) * uf).reshape(bsz, l, S5_WIDTH)
    z = jax.nn.gelu(y)
    return z * jax.nn.sigmoid(z @ glu_w.astype(f) + glu_b.astype(f))


def hybrid_mixer(h, positions, w_in, gla_a2, gla_a_b, gla_norm_g, s5_lambda_re, s5_lambda_im, s5_log_dt,
                 s5_b_re, s5_b_im, s5_c_re, s5_c_im, s5_d, s5_glu_w, s5_glu_b, swa_sinks, w_branch, w_out):
    f = jnp.float32
    b, l, _ = h.shape
    proj = h @ w_in
    points = np.cumsum(IN_SPLITS)[:-1].tolist()
    (gq, gk, gv, gr, glr, s5u, cq, ck, cv, sq, sk, sv, gates) = jnp.split(proj, points, axis=-1)

    def heads(z, nh):
        return z.reshape(b, l, nh, -1)

    log_a = jax.nn.log_sigmoid((glr @ gla_a2 + gla_a_b).astype(f)) / GLA_TAU
    o = gla_chunked(heads(gq, GLA_HEADS), heads(gk, GLA_HEADS), heads(gv, GLA_HEADS), heads(log_a, GLA_HEADS))
    o_gla = rms_norm(o, gla_norm_g) * jax.nn.silu(heads(gr, GLA_HEADS).astype(f))
    o_s5 = s5_mixer(s5u, s5_lambda_re, s5_lambda_im, s5_log_dt, s5_b_re, s5_b_im, s5_c_re, s5_c_im,
                    s5_d, s5_glu_w, s5_glu_b)
    o_dil = dilated_attention(rotary_partial(heads(cq, DIL_HEADS), positions),
                              rotary_partial(heads(ck, DIL_HEADS), positions), heads(cv, DIL_HEADS))
    o, lse = banded_attention(rotary_partial(heads(sq, SWA_HEADS), positions),
                              rotary_partial(heads(sk, SWA_KV_HEADS), positions),
                              heads(sv, SWA_KV_HEADS), SWA_WINDOW - 1)
    o_swa = o.astype(f) * jax.nn.sigmoid(lse - swa_sinks.astype(f))[..., None]
    branch_outs = (o_gla, o_s5, o_dil, o_swa)
    gate = jax.nn.sigmoid(gates.reshape(b, l, N_BRANCH, D_MODEL))
    mixed = sum(gate[:, :, m] * (br.reshape(b, l, BRANCH_WIDTH).astype(h.dtype) @ w_branch[m])
                for m, br in enumerate(branch_outs))
    return mixed @ w_out


def swiglu(h, w_gate, w_up, w_down):
    return (jax.nn.silu(h @ w_gate) * (h @ w_up)) @ w_down


def setup_inputs(seed: int = 0) -> dict:
    key = jax.random.key(seed)
    ks = jax.random.split(key, 26)
    f = jnp.float32

    def nrm(k, shape, scale):
        return scale * jax.random.normal(k, shape, f)

    def gain(k, shape):
        return 1.0 + 0.02 * jax.random.normal(k, shape, f)

    n_idx = jnp.arange(S5_STATE, dtype=f)
    gshape = (DEPTH, S5_GROUPS, S5_STATE)
    return {
        'x': jax.random.normal(ks[0], (BATCH, SEQ, D_MODEL), f),
        'positions': jnp.arange(SEQ, dtype=jnp.int32)[None, :] + jax.random.randint(ks[1], (BATCH, 1), 0, 1024, dtype=jnp.int32),
        'norm1_g': gain(ks[2], (DEPTH, D_MODEL)),
        'w_in': nrm(ks[3], (DEPTH, D_MODEL, D_IN), D_MODEL ** -0.5),
        'gla_a2': nrm(ks[4], (DEPTH, GLA_LOWRANK, GLA_HEADS * GLA_DK), GLA_LOWRANK ** -0.5),
        'gla_a_b': nrm(ks[5], (DEPTH, GLA_HEADS * GLA_DK), 0.1),
        'gla_norm_g': gain(ks[6], (DEPTH, GLA_HEADS, GLA_DV)),
        's5_lambda_re': -0.5 + nrm(ks[7], gshape, 0.01),
        's5_lambda_im': math.pi * n_idx + nrm(ks[8], gshape, 0.01),
        's5_log_dt': jax.random.uniform(ks[9], (DEPTH, S5_GROUPS), f, math.log(S5_DT_MIN), math.log(S5_DT_MAX)),
        's5_b_re': nrm(ks[10], (DEPTH, S5_GROUPS, S5_STATE, S5_GROUP), (2 * S5_GROUP) ** -0.5),
        's5_b_im': nrm(ks[11], (DEPTH, S5_GROUPS, S5_STATE, S5_GROUP), (2 * S5_GROUP) ** -0.5),
        's5_c_re': nrm(ks[12], (DEPTH, S5_GROUPS, S5_GROUP, S5_STATE), S5_STATE ** -0.5),
        's5_c_im': nrm(ks[13], (DEPTH, S5_GROUPS, S5_GROUP, S5_STATE), S5_STATE ** -0.5),
        's5_d': nrm(ks[14], (DEPTH, S5_GROUPS, S5_GROUP), 1.0),
        's5_glu_w': nrm(ks[15], (DEPTH, S5_WIDTH, S5_WIDTH), S5_WIDTH ** -0.5),
        's5_glu_b': nrm(ks[16], (DEPTH, S5_WIDTH), 0.01),
        'swa_sinks': nrm(ks[17], (DEPTH, SWA_HEADS), 1.0),
        'w_branch': nrm(ks[18], (DEPTH, N_BRANCH, BRANCH_WIDTH, D_MODEL), BRANCH_WIDTH ** -0.5),
        'w_out': nrm(ks[19], (DEPTH, D_MODEL, D_MODEL), D_MODEL ** -0.5),
        'norm2_g': gain(ks[20], (DEPTH, D_MODEL)),
        'w_ffn_gate': nrm(ks[21], (DEPTH, D_MODEL, FFN_HIDDEN), D_MODEL ** -0.5),
        'w_ffn_up': nrm(ks[22], (DEPTH, D_MODEL, FFN_HIDDEN), D_MODEL ** -0.5),
        'w_ffn_down': nrm(ks[23], (DEPTH, FFN_HIDDEN, D_MODEL), FFN_HIDDEN ** -0.5),
        'final_norm_g': gain(ks[24], (D_MODEL,)),
    }


def reference(x, positions, norm1_g, w_in, gla_a2, gla_a_b, gla_norm_g, s5_lambda_re, s5_lambda_im, s5_log_dt,
              s5_b_re, s5_b_im, s5_c_re, s5_c_im, s5_d, s5_glu_w, s5_glu_b, swa_sinks, w_branch, w_out,
              norm2_g, w_ffn_gate, w_ffn_up, w_ffn_down, final_norm_g):
    for i in range(DEPTH):
        h = rms_norm(x, norm1_g[i])
        x = x + hybrid_mixer(h, positions, w_in[i], gla_a2[i], gla_a_b[i], gla_norm_g[i], s5_lambda_re[i],
                             s5_lambda_im[i], s5_log_dt[i], s5_b_re[i], s5_b_im[i], s5_c_re[i], s5_c_im[i],
                             s5_d[i], s5_glu_w[i], s5_glu_b[i], swa_sinks[i], w_branch[i], w_out[i])
        h = rms_norm(x, norm2_g[i])
        x = x + swiglu(h, w_ffn_gate[i], w_ffn_up[i], w_ffn_down[i])
    return rms_norm(x, final_norm_g)
```

```python
import functools
import math

import jax
import jax.numpy as jnp
import numpy as np
from jax import lax
from jax.experimental import pallas as pl
from jax.experimental.pallas import tpu as pltpu

F32 = jnp.float32
BF16 = jnp.bfloat16
HIGHEST = lax.Precision.HIGHEST

D_MODEL = 2048
DEPTH = 2
N_BRANCH = 4
BRANCH_WIDTH = 512
HEAD_DIM = 64
ATT_BLOCK = 128
ROPE_THETA = 500000.0
ROPE_DIM = HEAD_DIM // 4
NORM_EPS = 1e-6
GLA_HEADS = 4
GLA_DK = 64
GLA_DV = BRANCH_WIDTH // GLA_HEADS
GLA_LOWRANK = 16
GLA_TAU = 16.0
S5_GROUP = 16
S5_GROUPS = BRANCH_WIDTH // S5_GROUP
S5_STATE = 64
DIL_CONFIGS = ((128, 1), (512, 4), (2048, 16))
DIL_SPAN = ATT_BLOCK * 16
SWA_HEADS = BRANCH_WIDTH // HEAD_DIM
SWA_KV_HEADS = 2
SWA_WINDOW = 128
FFN_HIDDEN = -((-8 * D_MODEL) // (3 * 256)) * 256

LANES = 128
SUBLANES = 8
VMEM_LIMIT = 56 * 1024 * 1024

C_GQ, C_GK, C_GV, C_GR, C_S5 = 0, 256, 512, 1024, 1536
C_CQ, C_CK, C_CV, C_SQ, C_SK, C_SV, C_GLR = 2048, 2560, 3072, 3584, 4096, 4224, 4352
N_SMALL = 4480
N_ORIG_SMALL = 4368
GLR_ORIG = 1536

TM_PROJ, TN_PROJ = 1024, 640
TM_MERGE, TN_MERGE = 512, 256
TM_FFN, TF_FFN = 512, 512
GLA_CH = 128
GLA_NCS = 4
GLA_SUB = 8
GLA_LEVELS = (8, 16, 32, 64, 128)
S5_CH = 16


def _cparams(sem):
    return pltpu.CompilerParams(dimension_semantics=sem, vmem_limit_bytes=VMEM_LIMIT)


def _rms(x, g):
    ms = jnp.mean(x * x, axis=-1, keepdims=True)
    return x * lax.rsqrt(ms + NORM_EPS) * g


def _bdot(a, b):
    return jnp.dot(a, b, preferred_element_type=F32)


def _bdot_nt(a, b):
    return lax.dot_general(a, b, (((1,), (1,)), ((), ())), preferred_element_type=F32)


def _inproj_kernel(x_ref, g_ref, w_ref, o_ref, h_ref):
    @pl.when(pl.program_id(1) == 0)
    def _():
        h_ref[...] = _rms(x_ref[...], g_ref[...]).astype(BF16)

    o_ref[...] = _bdot(h_ref[...], w_ref[...])


def _inproj(x, g, w):
    t, d = x.shape
    n = w.shape[1]
    tm, tn = min(TM_PROJ, t), TN_PROJ
    return pl.pallas_call(
        _inproj_kernel,
        grid=(t // tm, n // tn),
        in_specs=[
            pl.BlockSpec((tm, d), lambda i, j: (i, 0)),
            pl.BlockSpec((1, d), lambda i, j: (0, 0)),
            pl.BlockSpec((d, tn), lambda i, j: (0, j)),
        ],
        out_specs=pl.BlockSpec((tm, tn), lambda i, j: (i, j)),
        out_shape=jax.ShapeDtypeStruct((t, n), F32),
        scratch_shapes=[pltpu.VMEM((tm, d), BF16)],
        compiler_params=_cparams(("parallel", "arbitrary")),
        name="inproj",
    )(x, g, w)


def _rope_kernel(pos_ref, cos_ref, sa_ref, sb_ref):
    pos = pos_ref[0].astype(F32)
    lane = lax.broadcasted_iota(jnp.int32, (SUBLANES, LANES), 1)
    d = lane % HEAD_DIM
    half = ROPE_DIM // 2
    fi = (d % half).astype(F32) / half
    inv = jnp.power(jnp.full((SUBLANES, LANES), ROPE_THETA, F32), -fi)[0:1]
    d1 = d[0:1]
    ang = pos * inv
    c, s = jnp.cos(ang), jnp.sin(ang)
    cos_ref[0] = jnp.where(d1 < ROPE_DIM, c, 1.0)
    sa_ref[0] = jnp.where(d1 < half, -s, 0.0)
    sb_ref[0] = jnp.where((d1 >= half) & (d1 < ROPE_DIM), s, 0.0)


def _rope_tables(positions):
    b, l = positions.shape
    spec = pl.BlockSpec((1, l, LANES), lambda i: (i, 0, 0))
    shp = jax.ShapeDtypeStruct((b, l, LANES), F32)
    return pl.pallas_call(
        _rope_kernel,
        grid=(b,),
        in_specs=[pl.BlockSpec((1, l, 1), lambda i: (i, 0, 0))],
        out_specs=[spec, spec, spec],
        out_shape=[shp, shp, shp],
        compiler_params=_cparams(("parallel",)),
        name="rope_tables",
    )(positions.reshape(b, l, 1))


def _rope_apply(x, c, sa, sb):
    half = ROPE_DIM // 2
    return x * c + pltpu.roll(x, LANES - half, axis=1) * sa + pltpu.roll(x, half, axis=1) * sb


def _band_block(q, kp, ko, vp, vo, has_prev, strict_prev):
    t = ATT_BLOCK
    lane = lax.broadcasted_iota(jnp.int32, (t, LANES), 1)
    in_a = lane < HEAD_DIM
    q2 = jnp.concatenate([jnp.where(in_a, q, 0.0), jnp.where(in_a, 0.0, q)], axis=0).astype(BF16)
    sp = _bdot_nt(q2, kp.astype(BF16))
    so = _bdot_nt(q2, ko.astype(BF16))
    qi = lax.broadcasted_iota(jnp.int32, (2 * t, t), 0) % t
    kj = lax.broadcasted_iota(jnp.int32, (2 * t, t), 1)
    thr = qi + jnp.where(has_prev, 0, t)
    prev_ok = (kj > thr) if strict_prev else (kj >= thr)
    neg = -jnp.inf
    sp = jnp.where(prev_ok, sp, neg)
    so = jnp.where(kj <= qi, so, neg)
    m = jnp.maximum(jnp.max(sp, axis=-1, keepdims=True), jnp.max(so, axis=-1, keepdims=True))
    pp = jnp.exp(sp - m)
    po = jnp.exp(so - m)
    den = jnp.sum(pp, axis=-1, keepdims=True) + jnp.sum(po, axis=-1, keepdims=True)
    o2 = _bdot(pp.astype(BF16), vp.astype(BF16)) + _bdot(po.astype(BF16), vo.astype(BF16))
    o = jnp.where(in_a, o2[:t], o2[t:])
    mf = jnp.where(in_a, m[:t], m[t:])
    df = jnp.where(in_a, den[:t], den[t:])
    return mf, df, o


def _dil_kernel(q_ref, k_ref, v_ref, c_ref, sa_ref, sb_ref, o_ref, qs, ks, ms, ds, os_):
    l = q_ref.shape[1]
    c, sa, sb = c_ref[0], sa_ref[0], sb_ref[0]
    qs[...] = _rope_apply(q_ref[0], c, sa, sb) * (HEAD_DIM**-0.5)
    ks[...] = _rope_apply(k_ref[0], c, sa, sb)
    t = ATT_BLOCK
    nblk = DIL_SPAN // t

    for sblk in range(l // DIL_SPAN):
        base = sblk * DIL_SPAN
        for ci, (_, dil) in enumerate(DIL_CONFIGS):

            def body(idx, carry, dil=dil, ci=ci, base=base):
                r = idx % dil
                n = idx // dil
                loc = r + dil * t * n
                start = base + loc
                has_prev = start >= dil * t
                pstart = jnp.where(has_prev, start - dil * t, start)
                if dil == 1:
                    start = pl.multiple_of(start, t)
                    pstart = pl.multiple_of(pstart, t)
                    loc = pl.multiple_of(loc, t)
                    rows, prow, lrow = pl.ds(start, t), pl.ds(pstart, t), pl.ds(loc, t)
                else:
                    rows = pl.ds(start, t, stride=dil)
                    prow = pl.ds(pstart, t, stride=dil)
                    lrow = pl.ds(loc, t, stride=dil)
                mf, df, o = _band_block(
                    qs[rows, :], ks[prow, :], ks[rows, :], v_ref[0, prow, :], v_ref[0, rows, :], has_prev, False
                )
                ms[ci, lrow, :] = mf
                ds[ci, lrow, :] = df
                os_[ci, lrow, :] = o
                return carry

            lax.fori_loop(0, nblk, body, 0)

        m0, m1, m2 = ms[0], ms[1], ms[2]
        mx = jnp.maximum(jnp.maximum(m0, m1), m2)
        w0, w1, w2 = jnp.exp(m0 - mx), jnp.exp(m1 - mx), jnp.exp(m2 - mx)
        den = w0 * ds[0] + w1 * ds[1] + w2 * ds[2]
        num = w0 * os_[0] + w1 * os_[1] + w2 * os_[2]
        o_ref[0, base : base + DIL_SPAN, :] = (num / den).astype(o_ref.dtype)


def _dilated(proj3, tabs):
    b, l, _ = proj3.shape
    assert l % DIL_SPAN == 0
    cos, sa, sb = tabs

    def col(c0):
        return pl.BlockSpec((1, l, LANES), lambda i, p: (i, 0, c0 // LANES + p))

    tab = pl.BlockSpec((1, l, LANES), lambda i, p: (i, 0, 0))
    return pl.pallas_call(
        _dil_kernel,
        grid=(b, BRANCH_WIDTH // LANES),
        in_specs=[col(C_CQ), col(C_CK), col(C_CV), tab, tab, tab],
        out_specs=pl.BlockSpec((1, l, LANES), lambda i, p: (i, 0, p)),
        out_shape=jax.ShapeDtypeStruct((b, l, BRANCH_WIDTH), BF16),
        scratch_shapes=[
            pltpu.VMEM((l, LANES), F32),
            pltpu.VMEM((l, LANES), F32),
            pltpu.VMEM((3, DIL_SPAN, LANES), F32),
            pltpu.VMEM((3, DIL_SPAN, LANES), F32),
            pltpu.VMEM((3, DIL_SPAN, LANES), F32),
        ],
        compiler_params=_cparams(("parallel", "parallel")),
        name="dilated_attn",
    )(proj3, proj3, proj3, cos, sa, sb)


def _swa_kernel(q_ref, k_ref, v_ref, c_ref, sa_ref, sb_ref, sink_ref, o_ref, qs, ks, vs):
    l = q_ref.shape[1]
    c, sa, sb = c_ref[0], sa_ref[0], sb_ref[0]
    qs[...] = _rope_apply(q_ref[0], c, sa, sb) * (HEAD_DIM**-0.5)
    g = pl.program_id(1) // (SWA_HEADS // SWA_KV_HEADS // 2)
    lane = lax.broadcasted_iota(jnp.int32, (l, LANES), 1)
    keep = (lane // HEAD_DIM) == g
    kr = _rope_apply(k_ref[0], c, sa, sb)
    ks[...] = jnp.where(keep, kr, pltpu.roll(kr, HEAD_DIM, axis=1))
    v = v_ref[0]
    vs[...] = jnp.where(keep, v, pltpu.roll(v, HEAD_DIM, axis=1))
    sink = sink_ref[0]
    t = ATT_BLOCK

    def body(n, carry):
        start = pl.multiple_of(n * t, t)
        has_prev = n > 0
        pstart = pl.multiple_of(jnp.where(has_prev, start - t, start), t)
        rows, prow = pl.ds(start, t), pl.ds(pstart, t)
        mf, df, o = _band_block(qs[rows, :], ks[prow, :], ks[rows, :], vs[prow, :], vs[rows, :], has_prev, True)
        lse = mf + jnp.log(df)
        o_ref[0, rows, :] = (o / df * jax.nn.sigmoid(lse - sink)).astype(o_ref.dtype)
        return carry

    lax.fori_loop(0, l // t, body, 0)


def _swa(proj3, tabs, sinks):
    b, l, _ = proj3.shape
    cos, sa, sb = tabs
    npair = BRANCH_WIDTH // LANES
    sink_l = jnp.repeat(sinks.astype(F32), HEAD_DIM).reshape(npair, 1, LANES)
    tab = pl.BlockSpec((1, l, LANES), lambda i, p: (i, 0, 0))
    return pl.pallas_call(
        _swa_kernel,
        grid=(b, npair),
        in_specs=[
            pl.BlockSpec((1, l, LANES), lambda i, p: (i, 0, C_SQ // LANES + p)),
            pl.BlockSpec((1, l, LANES), lambda i, p: (i, 0, C_SK // LANES)),
            pl.BlockSpec((1, l, LANES), lambda i, p: (i, 0, C_SV // LANES)),
            tab,
            tab,
            tab,
            pl.BlockSpec((1, 1, LANES), lambda i, p: (p, 0, 0)),
        ],
        out_specs=pl.BlockSpec((1, l, LANES), lambda i, p: (i, 0, p)),
        out_shape=jax.ShapeDtypeStruct((b, l, BRANCH_WIDTH), BF16),
        scratch_shapes=[pltpu.VMEM((l, LANES), F32)] * 3,
        compiler_params=_cparams(("parallel", "parallel")),
        name="swa_attn",
    )(proj3, proj3, proj3, cos, sa, sb, sink_l)


@functools.lru_cache(maxsize=None)
def _gla_consts():
    ch = GLA_CH
    r = np.arange(ch)[:, None]
    j = np.arange(ch)[None, :]
    mats = []
    for m in GLA_LEVELS:
        lo = (r // m) * m
        mats.append(((j >= lo) & (j <= r)).astype(np.float32))
        mats.append(((j > r) & (j <= lo + m - 1)).astype(np.float32))
    dstack = np.concatenate(mats, axis=0)
    lm = []
    for m in GLA_LEVELS[:-1]:
        mk = (((r // m) % 2 == 1) & ((j // m) == (r // m) - 1)).astype(np.float32)
        lm.append(np.tile(mk, (GLA_HEADS, 1)))
    lmask = np.stack(lm)
    dmask = np.tile(((r // GLA_SUB) == (j // GLA_SUB)).astype(np.float32), (1, GLA_HEADS))
    nsub = ch // GLA_SUB
    e = np.zeros((GLA_SUB, GLA_HEADS * GLA_DK, GLA_HEADS * ch), np.float32)
    for u in range(GLA_SUB):
        for h in range(GLA_HEADS):
            for s in range(nsub):
                e[u, h * GLA_DK : (h + 1) * GLA_DK, h * ch + GLA_SUB * s + u] = 1.0
    rr = np.arange(GLA_HEADS * GLA_DV)[:, None]
    cc = np.arange(GLA_HEADS * GLA_DK)[None, :]
    bd = ((rr // GLA_DV) == (cc // GLA_DK)).astype(np.float32)
    return dstack, lmask, dmask, e, bd


def _split3(x):
    hi = x.astype(BF16)
    r1 = x - hi.astype(F32)
    mid = r1.astype(BF16)
    lo = (r1 - mid.astype(F32)).astype(BF16)
    return hi, mid, lo


def _bcast_sub(x, u):
    r, w = x.shape
    x3 = x.reshape(r // GLA_SUB, GLA_SUB, w)
    return jnp.broadcast_to(x3[:, u : u + 1, :], x3.shape).reshape(r, w)


def _gla_kernel(
    q_ref, k_ref, v_ref, r_ref, glr_ref, a2_ref, ab_ref, ng_ref, dst_ref, lmask_ref, dmask_ref, e_ref, bd_ref,
    o_ref, st_ref,
):
    ch = GLA_CH
    rs = q_ref.shape[0]
    ncs = rs // ch
    nlev = len(GLA_LEVELS)
    hk = GLA_HEADS * GLA_DK

    @pl.when(pl.program_id(1) == 0)
    def _():
        st_ref[...] = jnp.zeros_like(st_ref)

    q = q_ref[...] * (GLA_DK**-0.5)
    k = k_ref[...]
    z = jnp.dot(glr_ref[...], a2_ref[...], precision=HIGHEST, preferred_element_type=F32) + ab_ref[...]
    g = (jnp.minimum(z, 0.0) - jnp.log1p(jnp.exp(-jnp.abs(z)))) * (1.0 / GLA_TAU)
    g3 = _split3(g)
    dst = dst_ref[...]

    es = []
    for c in range(ncs):
        rows = slice(c * ch, (c + 1) * ch)
        es.append(_bdot(dst, g3[0][rows]) + _bdot(dst, g3[1][rows]) + _bdot(dst, g3[2][rows]))

    def eq(c, li):
        return es[c][(2 * li) * ch : (2 * li + 1) * ch]

    def ek(c, li):
        return es[c][(2 * li + 1) * ch : (2 * li + 2) * ch]

    cs = jnp.concatenate([eq(c, 0) for c in range(ncs)], axis=0)
    tsub = lax.broadcasted_iota(jnp.int32, (rs, hk), 0) % GLA_SUB
    arep = jnp.zeros((rs, GLA_HEADS * ch), F32)
    for u in range(GLA_SUB):
        dec = jnp.exp(jnp.where(tsub >= u, cs - _bcast_sub(cs, u), -jnp.inf))
        p = q * _bcast_sub(k, u) * dec
        arep = arep + _bdot(p.astype(BF16), e_ref[u])
    arep = arep * jnp.concatenate([dmask_ref[...]] * ncs, axis=0)

    lane_k = lax.broadcasted_iota(jnp.int32, (ch, hk), 1) // GLA_DK
    ng = ng_ref[...]
    bd = bd_ref[...]
    for c in range(ncs):
        rows = slice(c * ch, (c + 1) * ch)
        qc, kc = q[rows], k[rows]
        vc = v_ref[rows, :]
        aoff = jnp.zeros((GLA_HEADS * ch, ch), F32)
        for li in range(nlev - 1):
            qe = qc * jnp.exp(eq(c, li))
            ke = (kc * jnp.exp(ek(c, li))).astype(BF16)
            qst = jnp.concatenate([jnp.where(lane_k == h, qe, 0.0) for h in range(GLA_HEADS)], axis=0).astype(BF16)
            aoff = aoff + _bdot_nt(qst, ke) * lmask_ref[li]
        cum = eq(c, nlev - 1)
        st = st_ref[...]
        o_inter = _bdot_nt((qc * jnp.exp(cum)).astype(BF16), st.astype(BF16))
        vb = vc.astype(BF16)
        outs = []
        for h in range(GLA_HEADS):
            a_h = aoff[h * ch : (h + 1) * ch] + arep[rows, h * ch : (h + 1) * ch]
            o_h = _bdot(a_h.astype(BF16), vb[:, h * GLA_DV : (h + 1) * GLA_DV]) + o_inter[:, h * GLA_DV : (h + 1) * GLA_DV]
            ms = jnp.mean(o_h * o_h, axis=-1, keepdims=True)
            outs.append(o_h * lax.rsqrt(ms + NORM_EPS))
        o = jnp.concatenate(outs, axis=1) * ng
        rc = r_ref[rows, :]
        o_ref[rows, :] = (o * (rc * jax.nn.sigmoid(rc))).astype(o_ref.dtype)
        ke_last = (kc * jnp.exp(ek(c, nlev - 1))).astype(BF16)
        kv = _bdot(vc.T.astype(BF16), ke_last)
        st_ref[...] = st * jnp.exp(cum[ch - 1 : ch, :]) + kv * bd


def _gla(proj, b, a2, ab, ng):
    t = proj.shape[0]
    l = t // b
    rs = GLA_CH * GLA_NCS
    assert l % rs == 0
    ns = l // rs
    dstack, lmask, dmask, e, bd = _gla_consts()
    a2p = jnp.zeros((LANES, GLA_HEADS * GLA_DK), F32).at[:GLA_LOWRANK].set(a2.astype(F32))

    def rowblk(w, c0):
        return pl.BlockSpec((rs, w), lambda i, s: (i * ns + s, c0 // w))

    def full(shape):
        nd = len(shape)
        return pl.BlockSpec(shape, lambda i, s: (0,) * nd)

    hk, hv = GLA_HEADS * GLA_DK, GLA_HEADS * GLA_DV
    return pl.pallas_call(
        _gla_kernel,
        grid=(b, ns),
        in_specs=[
            rowblk(hk, C_GQ),
            rowblk(hk, C_GK),
            rowblk(hv, C_GV),
            rowblk(hv, C_GR),
            rowblk(LANES, C_GLR),
            full((LANES, hk)),
            full((1, hk)),
            full((1, hv)),
            full(dstack.shape),
            full(lmask.shape),
            full(dmask.shape),
            full(e.shape),
            full(bd.shape),
        ],
        out_specs=pl.BlockSpec((rs, hv), lambda i, s: (i * ns + s, 0)),
        out_shape=jax.ShapeDtypeStruct((t, hv), BF16),
        scratch_shapes=[pltpu.VMEM((hv, hk), F32)],
        compiler_params=_cparams(("parallel", "arbitrary")),
        name="gla",
    )(
        proj, proj, proj, proj, proj, a2p, ab.reshape(1, hk).astype(F32), ng.reshape(1, hv).astype(F32),
        jnp.asarray(dstack, BF16), jnp.asarray(lmask), jnp.asarray(dmask), jnp.asarray(e, BF16), jnp.asarray(bd),
    )


def _s5_params(lam_re, lam_im, log_dt, b_re, b_im, c_re, c_im, d, nch):
    f = F32
    hp = HIGHEST
    cs = S5_CH
    dt = jnp.exp(log_dt.astype(f))[:, None]
    lr, li = lam_re.astype(f), lam_im.astype(f)
    mag = jnp.exp(lr * dt)
    ab_re, ab_im = mag * jnp.cos(li * dt), mag * jnp.sin(li * dt)
    den = lr * lr + li * li
    z_re = ((ab_re - 1.0) * lr + ab_im * li) / den
    z_im = (ab_im * lr - (ab_re - 1.0) * li) / den
    br, bi = b_re.astype(f), b_im.astype(f)
    bb_re = z_re[..., None] * br - z_im[..., None] * bi
    bb_im = z_re[..., None] * bi + z_im[..., None] * br

    def apow(p):
        p = jnp.asarray(p, f)[:, None, None]
        m = jnp.exp(p * (lr * dt))
        return m * jnp.cos(p * (li * dt)), m * jnp.sin(p * (li * dt))

    cr, ci = c_re.astype(f), c_im.astype(f)
    p_re, p_im = apow(np.arange(cs + 1))
    ab_b_re = p_re[..., None] * bb_re - p_im[..., None] * bb_im
    ab_b_im = p_re[..., None] * bb_im + p_im[..., None] * bb_re
    kk = jnp.einsum("gcn,tgnd->tgcd", cr, ab_b_re, precision=hp) - jnp.einsum("gcn,tgnd->tgcd", ci, ab_b_im, precision=hp)
    jj = np.arange(cs)[:, None]
    tt = np.arange(cs)[None, :]
    lag = np.clip(tt - jj, 0, cs)
    tmat = jnp.where((tt >= jj)[:, :, None, None, None], kk[lag], 0.0)
    tmat = tmat.transpose(2, 0, 4, 1, 3).reshape(S5_GROUPS, cs * S5_GROUP, cs * S5_GROUP)
    rev = np.arange(cs - 1, -1, -1)
    mb = jnp.concatenate([ab_b_re[rev], ab_b_im[rev]], axis=2)
    mb = mb.transpose(1, 0, 3, 2).reshape(S5_GROUPS, cs * S5_GROUP, 2 * S5_STATE)
    q_re, q_im = p_re[1:], p_im[1:]
    mc_re = cr[None] * q_re[:, :, None, :] - ci[None] * q_im[:, :, None, :]
    mc_im = -(cr[None] * q_im[:, :, None, :] + ci[None] * q_re[:, :, None, :])
    mc = jnp.concatenate([mc_re, mc_im], axis=3)
    mc = mc.transpose(1, 3, 0, 2).reshape(S5_GROUPS, 2 * S5_STATE, cs * S5_GROUP)
    nstep = max(1, int(math.log2(nch)))
    s_re, s_im = apow(cs * (2 ** np.arange(nstep)))
    ar = jnp.concatenate([s_re, s_re], axis=2).transpose(1, 0, 2)
    ai = jnp.concatenate([-s_im, s_im], axis=2).transpose(1, 0, 2)
    dd = jnp.tile(d.astype(f), (1, cs)).reshape(S5_GROUPS, 1, cs * S5_GROUP)
    tmb = jnp.concatenate([tmat, mb], axis=2).astype(BF16)
    return tmb, mc.astype(BF16), ar, ai, dd


def _s5_kernel(u_ref, tmb_ref, mc_ref, ar_ref, ai_ref, d_ref, y_ref, *, nch):
    u = u_ref[0]
    w = S5_CH * S5_GROUP
    ye = _bdot(u.astype(BF16), tmb_ref[0])
    y, e = ye[:, :w], ye[:, w:]
    cidx = lax.broadcasted_iota(jnp.int32, e.shape, 0) % nch
    ar, ai = ar_ref[0], ai_ref[0]
    for kstep in range(ar.shape[0]):
        s = 2**kstep
        if s >= nch:
            break
        sh = jnp.where(cidx >= s, pltpu.roll(e, s, axis=0), 0.0)
        e = e + sh * ar[kstep : kstep + 1] + pltpu.roll(sh, S5_STATE, axis=1) * ai[kstep : kstep + 1]
    hprev = jnp.where(cidx >= 1, pltpu.roll(e, 1, axis=0), 0.0)
    y_ref[0] = y + _bdot(hprev.astype(BF16), mc_ref[0]) + d_ref[0] * u


def _s5_core(u, b, params):
    t = u.shape[0]
    l = t // b
    nch = l // S5_CH
    w = S5_CH * S5_GROUP
    tmb, mc, ar, ai, dd = params
    ug = u.reshape(b * nch, S5_CH, S5_GROUPS, S5_GROUP).transpose(2, 0, 1, 3).reshape(S5_GROUPS, b * nch, w)

    def gspec(shape):
        return pl.BlockSpec((1,) + shape, lambda g: (g, 0, 0))

    yg = pl.pallas_call(
        functools.partial(_s5_kernel, nch=nch),
        grid=(S5_GROUPS,),
        in_specs=[
            gspec((b * nch, w)),
            gspec(tmb.shape[1:]),
            gspec(mc.shape[1:]),
            gspec(ar.shape[1:]),
            gspec(ai.shape[1:]),
            gspec((1, w)),
        ],
        out_specs=gspec((b * nch, w)),
        out_shape=jax.ShapeDtypeStruct((S5_GROUPS, b * nch, w), F32),
        compiler_params=_cparams(("parallel",)),
        name="s5_core",
    )(ug, tmb, mc, ar, ai, dd)
    return yg.reshape(S5_GROUPS, b * nch, S5_CH, S5_GROUP).transpose(1, 2, 0, 3).reshape(t, BRANCH_WIDTH)


def _merge_kernel(
    x_ref, g1_ref, gla_ref, ys5_ref, dil_ref, swa_ref, gluw_ref, glub_ref, wg0, wg1, wg2, wg3, wb_ref, wo_ref,
    o_ref, h_s, s5_s, acc_s,
):
    j = pl.program_id(1)

    @pl.when(j == 0)
    def _():
        h_s[...] = _rms(x_ref[...], g1_ref[...]).astype(BF16)
        zz = jax.nn.gelu(ys5_ref[...])
        gate = jax.nn.sigmoid(_bdot(zz.astype(BF16), gluw_ref[...]) + glub_ref[...])
        s5_s[...] = (zz * gate).astype(BF16)
        acc_s[...] = jnp.zeros_like(acc_s)

    h = h_s[...]
    branches = (gla_ref[...], s5_s[...], dil_ref[...], swa_ref[...])
    mixed = None
    for m, (wg, br) in enumerate(zip((wg0, wg1, wg2, wg3), branches)):
        term = jax.nn.sigmoid(_bdot(h, wg[...])) * _bdot(br, wb_ref[m])
        mixed = term if mixed is None else mixed + term
    acc_s[...] += _bdot(mixed.astype(BF16), wo_ref[...])

    @pl.when(j == pl.num_programs(1) - 1)
    def _():
        o_ref[...] = x_ref[...] + acc_s[...]


def _merge(x, g1, o_gla, y_s5, o_dil, o_swa, gluw, glub, wgate, wb, wo):
    t, d = x.shape
    tm, tn = min(TM_MERGE, t), TN_MERGE
    nj = d // tn
    bw = BRANCH_WIDTH

    def rowblk(w):
        return pl.BlockSpec((tm, w), lambda i, j: (i, 0))

    def gate_spec(m):
        return pl.BlockSpec((d, tn), lambda i, j, m=m: (0, m * nj + j))

    return pl.pallas_call(
        _merge_kernel,
        grid=(t // tm, nj),
        in_specs=[
            rowblk(d),
            pl.BlockSpec((1, d), lambda i, j: (0, 0)),
            rowblk(bw),
            rowblk(bw),
            rowblk(bw),
            rowblk(bw),
            pl.BlockSpec((bw, bw), lambda i, j: (0, 0)),
            pl.BlockSpec((1, bw), lambda i, j: (0, 0)),
            gate_spec(0),
            gate_spec(1),
            gate_spec(2),
            gate_spec(3),
            pl.BlockSpec((N_BRANCH, bw, tn), lambda i, j: (0, 0, j)),
            pl.BlockSpec((tn, d), lambda i, j: (j, 0)),
        ],
        out_specs=rowblk(d),
        out_shape=jax.ShapeDtypeStruct((t, d), F32),
        scratch_shapes=[pltpu.VMEM((tm, d), BF16), pltpu.VMEM((tm, bw), BF16), pltpu.VMEM((tm, d), F32)],
        compiler_params=_cparams(("parallel", "arbitrary")),
        name="merge",
    )(x, g1, o_gla, y_s5, o_dil, o_swa, gluw, glub, wgate, wgate, wgate, wgate, wb, wo)


def _ffn_kernel(x_ref, g2_ref, wg_ref, wu_ref, wd_ref, gf_ref, o_ref, h_s, acc_s, *, final_norm):
    j = pl.program_id(1)

    @pl.when(j == 0)
    def _():
        h_s[...] = _rms(x_ref[...], g2_ref[...]).astype(BF16)
        acc_s[...] = jnp.zeros_like(acc_s)

    h = h_s[...]
    gate = _bdot(h, wg_ref[...])
    act = (gate * jax.nn.sigmoid(gate)) * _bdot(h, wu_ref[...])
    acc_s[...] += _bdot(act.astype(BF16), wd_ref[...])

    @pl.when(j == pl.num_programs(1) - 1)
    def _():
        y = x_ref[...] + acc_s[...]
        o_ref[...] = _rms(y, gf_ref[...]) if final_norm else y


def _ffn(x, g2, wg, wu, wd, gf, final_norm):
    t, d = x.shape
    fh = wg.shape[1]
    tm, tf = min(TM_FFN, t), TF_FFN
    return pl.pallas_call(
        functools.partial(_ffn_kernel, final_norm=final_norm),
        grid=(t // tm, fh // tf),
        in_specs=[
            pl.BlockSpec((tm, d), lambda i, j: (i, 0)),
            pl.BlockSpec((1, d), lambda i, j: (0, 0)),
            pl.BlockSpec((d, tf), lambda i, j: (0, j)),
            pl.BlockSpec((d, tf), lambda i, j: (0, j)),
            pl.BlockSpec((tf, d), lambda i, j: (j, 0)),
            pl.BlockSpec((1, d), lambda i, j: (0, 0)),
        ],
        out_specs=pl.BlockSpec((tm, d), lambda i, j: (i, 0)),
        out_shape=jax.ShapeDtypeStruct((t, d), F32),
        scratch_shapes=[pltpu.VMEM((tm, d), BF16), pltpu.VMEM((tm, d), F32)],
        compiler_params=_cparams(("parallel", "arbitrary")),
        name="ffn",
    )(x, g2, wg, wu, wd, gf)


def _small_weight(w_in):
    pad = jnp.zeros(w_in.shape[:-1] + (N_SMALL - N_ORIG_SMALL,), w_in.dtype)
    return jnp.concatenate(
        [
            w_in[..., :GLR_ORIG],
            w_in[..., GLR_ORIG + GLA_LOWRANK : N_ORIG_SMALL],
            w_in[..., GLR_ORIG : GLR_ORIG + GLA_LOWRANK],
            pad,
        ],
        axis=-1,
    ).astype(BF16)


def kernel(x, positions, norm1_g, w_in, gla_a2, gla_a_b, gla_norm_g, s5_lambda_re, s5_lambda_im, s5_log_dt, s5_b_re, s5_b_im, s5_c_re, s5_c_im, s5_d, s5_glu_w, s5_glu_b, swa_sinks, w_branch, w_out, norm2_g, w_ffn_gate, w_ffn_up, w_ffn_down, final_norm_g):
    b, l, d = x.shape
    t = b * l
    depth = w_in.shape[0]
    xs = x.reshape(t, d).astype(F32)
    w_small = _small_weight(w_in)
    w_gates = w_in[..., N_ORIG_SMALL:].astype(BF16)
    w_br = w_branch.astype(BF16)
    w_o = w_out.astype(BF16)
    w_fg, w_fu, w_fd = w_ffn_gate.astype(BF16), w_ffn_up.astype(BF16), w_ffn_down.astype(BF16)
    glu_w = s5_glu_w.astype(BF16)
    tabs = _rope_tables(positions)
    gf = final_norm_g.reshape(1, d).astype(F32)
    for i in range(depth):
        g1 = norm1_g[i].reshape(1, d).astype(F32)
        proj = _inproj(xs, g1, w_small[i])
        proj3 = proj.reshape(b, l, N_SMALL)
        o_gla = _gla(proj, b, gla_a2[i], gla_a_b[i], gla_norm_g[i])
        s5p = _s5_params(s5_lambda_re[i], s5_lambda_im[i], s5_log_dt[i], s5_b_re[i], s5_b_im[i], s5_c_re[i],
                         s5_c_im[i], s5_d[i], l // S5_CH)
        y_s5 = _s5_core(proj[:, C_S5 : C_S5 + BRANCH_WIDTH], b, s5p)
        o_dil = _dilated(proj3, tabs).reshape(t, BRANCH_WIDTH)
        o_swa = _swa(proj3, tabs, swa_sinks[i]).reshape(t, BRANCH_WIDTH)
        xs = _merge(xs, g1, o_gla, y_s5, o_dil, o_swa, glu_w[i], s5_glu_b[i].reshape(1, -1).astype(F32),
                    w_gates[i], w_br[i], w_o[i])
        xs = _ffn(xs, norm2_g[i].reshape(1, d).astype(F32), w_fg[i], w_fu[i], w_fd[i], gf, i == depth - 1)
    return xs.reshape(b, l, d).astype(x.dtype)
```

```python
import functools
import math

import jax
import jax.numpy as jnp
import numpy as np
from jax import lax
from jax.experimental import pallas as pl
from jax.experimental.pallas import tpu as pltpu

F32 = jnp.float32
BF16 = jnp.bfloat16
HIGHEST = lax.Precision.HIGHEST

D_MODEL = 2048
DEPTH = 2
N_BRANCH = 4
BRANCH_WIDTH = 512
HEAD_DIM = 64
ATT_BLOCK = 128
ROPE_THETA = 500000.0
ROPE_DIM = HEAD_DIM // 4
NORM_EPS = 1e-6
GLA_HEADS = 4
GLA_DK = 64
GLA_DV = BRANCH_WIDTH // GLA_HEADS
GLA_LOWRANK = 16
GLA_TAU = 16.0
S5_GROUP = 16
S5_GROUPS = BRANCH_WIDTH // S5_GROUP
S5_STATE = 64
DIL_CONFIGS = ((128, 1), (512, 4), (2048, 16))
DIL_SPAN = ATT_BLOCK * 16
SWA_HEADS = BRANCH_WIDTH // HEAD_DIM
SWA_KV_HEADS = 2
SWA_WINDOW = 128
FFN_HIDDEN = -((-8 * D_MODEL) // (3 * 256)) * 256

LANES = 128
SUBLANES = 8
VMEM_LIMIT = 56 * 1024 * 1024

C_GQ, C_GK, C_GV, C_GR, C_S5 = 0, 256, 512, 1024, 1536
C_CQ, C_CK, C_CV, C_SQ, C_SK, C_SV, C_GLR = 2048, 2560, 3072, 3584, 4096, 4224, 4352
N_SMALL = 4608
N_ORIG_SMALL = 4368
GLR_ORIG = 1536

TM_PROJ, TN_PROJ = 1024, 768
TM_MERGE, TN_MERGE = 512, 256
TM_FFN, TF_FFN = 512, 512
GLA_CH = 128
GLA_NCS = 4
GLA_SUB = 8
GLA_LEVELS = (8, 16, 32, 64, 128)
S5_CH = 16
S5_GPB = LANES // S5_GROUP
S5_RT = 512
ATT_UNROLL = 8


def _cparams(sem):
    return pltpu.CompilerParams(dimension_semantics=sem, vmem_limit_bytes=VMEM_LIMIT)


def _rms(x, g):
    ms = jnp.mean(x * x, axis=-1, keepdims=True)
    return x * lax.rsqrt(ms + NORM_EPS) * g


def _bdot(a, b):
    return jnp.dot(a, b, preferred_element_type=F32)


def _bdot_nt(a, b):
    return lax.dot_general(a, b, (((1,), (1,)), ((), ())), preferred_element_type=F32)


def _inproj_kernel(x_ref, g_ref, w_ref, o_ref, h_ref):
    @pl.when(pl.program_id(1) == 0)
    def _():
        h_ref[...] = _rms(x_ref[...], g_ref[...]).astype(BF16)

    o_ref[...] = _bdot(h_ref[...], w_ref[...])


def _inproj(x, g, w_all, layer):
    t, d = x.shape
    n = N_SMALL
    tm, tn = min(TM_PROJ, t), TN_PROJ
    return pl.pallas_call(
        _inproj_kernel,
        grid=(t // tm, n // tn),
        in_specs=[
            pl.BlockSpec((tm, d), lambda i, j: (i, 0)),
            pl.BlockSpec((1, d), lambda i, j: (0, 0)),
            pl.BlockSpec((None, d, tn), lambda i, j: (layer, 0, j)),
        ],
        out_specs=pl.BlockSpec((tm, tn), lambda i, j: (i, j)),
        out_shape=jax.ShapeDtypeStruct((t, n), F32),
        scratch_shapes=[pltpu.VMEM((tm, d), BF16)],
        compiler_params=_cparams(("parallel", "arbitrary")),
        name="inproj",
    )(x, g, w_all)


def _rope_kernel(pos_ref, cos_ref, sa_ref, sb_ref):
    pos = pos_ref[0].astype(F32)
    lane = lax.broadcasted_iota(jnp.int32, (SUBLANES, LANES), 1)
    d = lane % HEAD_DIM
    half = ROPE_DIM // 2
    fi = (d % half).astype(F32) / half
    inv = jnp.power(jnp.full((SUBLANES, LANES), ROPE_THETA, F32), -fi)[0:1]
    d1 = d[0:1]
    ang = pos * inv
    c, s = jnp.cos(ang), jnp.sin(ang)
    cos_ref[0] = jnp.where(d1 < ROPE_DIM, c, 1.0)
    sa_ref[0] = jnp.where(d1 < half, -s, 0.0)
    sb_ref[0] = jnp.where((d1 >= half) & (d1 < ROPE_DIM), s, 0.0)


def _rope_tables(positions):
    b, l = positions.shape
    spec = pl.BlockSpec((1, l, LANES), lambda i: (i, 0, 0))
    shp = jax.ShapeDtypeStruct((b, l, LANES), F32)
    return pl.pallas_call(
        _rope_kernel,
        grid=(b,),
        in_specs=[pl.BlockSpec((1, l, 1), lambda i: (i, 0, 0))],
        out_specs=[spec, spec, spec],
        out_shape=[shp, shp, shp],
        compiler_params=_cparams(("parallel",)),
        name="rope_tables",
    )(positions.reshape(b, l, 1))


def _rope_apply(x, c, sa, sb):
    half = ROPE_DIM // 2
    return x * c + pltpu.roll(x, LANES - half, axis=1) * sa + pltpu.roll(x, half, axis=1) * sb


def _band_block(q, kp, ko, vp, vo, has_prev, strict_prev):
    t = ATT_BLOCK
    lane = lax.broadcasted_iota(jnp.int32, (t, LANES), 1)
    in_a = lane < HEAD_DIM
    q2 = jnp.concatenate([jnp.where(in_a, q, 0.0), jnp.where(in_a, 0.0, q)], axis=0).astype(BF16)
    s = _bdot_nt(q2, jnp.concatenate([kp, ko], axis=0).astype(BF16))
    qi = lax.broadcasted_iota(jnp.int32, (2 * t, 2 * t), 0) % t
    kj = lax.broadcasted_iota(jnp.int32, (2 * t, 2 * t), 1)
    lo = qi + ((1 if strict_prev else 0) + jnp.where(has_prev, 0, t))
    ok = ((kj < t) & (kj >= lo)) | ((kj >= t) & (kj - t <= qi))
    s = jnp.where(ok, s, -jnp.inf)
    m = jnp.max(s, axis=-1, keepdims=True)
    p = jnp.exp(s - m).astype(BF16)
    ones = jnp.ones((2 * t, LANES), BF16)
    vcat = jnp.concatenate([jnp.concatenate([vp, vo], axis=0).astype(BF16), ones], axis=1)
    od = _bdot(p, vcat)
    o = jnp.where(in_a, od[:t, :LANES], od[t:, :LANES])
    df = jnp.where(in_a, od[:t, LANES:], od[t:, LANES:])
    mf = jnp.where(in_a, m[:t], m[t:])
    return mf, df, o


def _dil_kernel(q_ref, k_ref, v_ref, c_ref, sa_ref, sb_ref, o_ref, qs, ks, ms, ds, os_):
    l = q_ref.shape[1]
    c, sa, sb = c_ref[0], sa_ref[0], sb_ref[0]
    qs[...] = _rope_apply(q_ref[0], c, sa, sb) * (HEAD_DIM**-0.5)
    ks[...] = _rope_apply(k_ref[0], c, sa, sb)
    t = ATT_BLOCK
    nblk = DIL_SPAN // t

    for sblk in range(l // DIL_SPAN):
        base = sblk * DIL_SPAN
        for ci, (_, dil) in enumerate(DIL_CONFIGS):

            def body(idx, carry, dil=dil, ci=ci, base=base):
                r = idx % dil
                n = idx // dil
                loc = r + dil * t * n
                start = base + loc
                has_prev = start >= dil * t
                pstart = jnp.where(has_prev, start - dil * t, start)
                if dil == 1:
                    start = pl.multiple_of(start, t)
                    pstart = pl.multiple_of(pstart, t)
                    loc = pl.multiple_of(loc, t)
                    rows, prow, lrow = pl.ds(start, t), pl.ds(pstart, t), pl.ds(loc, t)
                else:
                    rows = pl.ds(start, t, stride=dil)
                    prow = pl.ds(pstart, t, stride=dil)
                    lrow = pl.ds(loc, t, stride=dil)
                mf, df, o = _band_block(
                    qs[rows, :], ks[prow, :], ks[rows, :], v_ref[0, prow, :], v_ref[0, rows, :], has_prev, False
                )
                ms[ci, lrow, :] = mf
                ds[ci, lrow, :] = df
                os_[ci, lrow, :] = o
                return carry

            lax.fori_loop(0, nblk, body, 0, unroll=ATT_UNROLL)

        m0, m1, m2 = ms[0], ms[1], ms[2]
        mx = jnp.maximum(jnp.maximum(m0, m1), m2)
        w0, w1, w2 = jnp.exp(m0 - mx), jnp.exp(m1 - mx), jnp.exp(m2 - mx)
        den = w0 * ds[0] + w1 * ds[1] + w2 * ds[2]
        num = w0 * os_[0] + w1 * os_[1] + w2 * os_[2]
        o_ref[0, base : base + DIL_SPAN, :] = (num / den).astype(o_ref.dtype)


def _dilated(proj3, tabs):
    b, l, _ = proj3.shape
    assert l % DIL_SPAN == 0
    cos, sa, sb = tabs

    def col(c0):
        return pl.BlockSpec((1, l, LANES), lambda i, p: (i, 0, c0 // LANES + p))

    tab = pl.BlockSpec((1, l, LANES), lambda i, p: (i, 0, 0))
    return pl.pallas_call(
        _dil_kernel,
        grid=(b, BRANCH_WIDTH // LANES),
        in_specs=[col(C_CQ), col(C_CK), col(C_CV), tab, tab, tab],
        out_specs=pl.BlockSpec((1, l, LANES), lambda i, p: (i, 0, p)),
        out_shape=jax.ShapeDtypeStruct((b, l, BRANCH_WIDTH), BF16),
        scratch_shapes=[
            pltpu.VMEM((l, LANES), F32),
            pltpu.VMEM((l, LANES), F32),
            pltpu.VMEM((3, DIL_SPAN, LANES), F32),
            pltpu.VMEM((3, DIL_SPAN, LANES), F32),
            pltpu.VMEM((3, DIL_SPAN, LANES), F32),
        ],
        compiler_params=_cparams(("parallel", "parallel")),
        name="dilated_attn",
    )(proj3, proj3, proj3, cos, sa, sb)


def _swa_kernel(q_ref, k_ref, v_ref, c_ref, sa_ref, sb_ref, sink_ref, o_ref, qs, ks, vs):
    l = q_ref.shape[1]
    c, sa, sb = c_ref[0], sa_ref[0], sb_ref[0]
    qs[...] = _rope_apply(q_ref[0], c, sa, sb) * (HEAD_DIM**-0.5)
    g = pl.program_id(1) // (SWA_HEADS // SWA_KV_HEADS // 2)
    lane = lax.broadcasted_iota(jnp.int32, (l, LANES), 1)
    keep = (lane // HEAD_DIM) == g
    kr = _rope_apply(k_ref[0], c, sa, sb)
    ks[...] = jnp.where(keep, kr, pltpu.roll(kr, HEAD_DIM, axis=1))
    v = v_ref[0]
    vs[...] = jnp.where(keep, v, pltpu.roll(v, HEAD_DIM, axis=1))
    sink = sink_ref[0]
    t = ATT_BLOCK

    def body(n, carry):
        start = pl.multiple_of(n * t, t)
        has_prev = n > 0
        pstart = pl.multiple_of(jnp.where(has_prev, start - t, start), t)
        rows, prow = pl.ds(start, t), pl.ds(pstart, t)
        mf, df, o = _band_block(qs[rows, :], ks[prow, :], ks[rows, :], vs[prow, :], vs[rows, :], has_prev, True)
        lse = mf + jnp.log(df)
        o_ref[0, rows, :] = (o / df * jax.nn.sigmoid(lse - sink)).astype(o_ref.dtype)
        return carry

    lax.fori_loop(0, l // t, body, 0, unroll=ATT_UNROLL)


def _swa(proj3, tabs, sinks):
    b, l, _ = proj3.shape
    cos, sa, sb = tabs
    npair = BRANCH_WIDTH // LANES
    sink_l = jnp.repeat(sinks.astype(F32), HEAD_DIM).reshape(npair, 1, LANES)
    tab = pl.BlockSpec((1, l, LANES), lambda i, p: (i, 0, 0))
    return pl.pallas_call(
        _swa_kernel,
        grid=(b, npair),
        in_specs=[
            pl.BlockSpec((1, l, LANES), lambda i, p: (i, 0, C_SQ // LANES + p)),
            pl.BlockSpec((1, l, LANES), lambda i, p: (i, 0, C_SK // LANES)),
            pl.BlockSpec((1, l, LANES), lambda i, p: (i, 0, C_SV // LANES)),
            tab,
            tab,
            tab,
            pl.BlockSpec((1, 1, LANES), lambda i, p: (p, 0, 0)),
        ],
        out_specs=pl.BlockSpec((1, l, LANES), lambda i, p: (i, 0, p)),
        out_shape=jax.ShapeDtypeStruct((b, l, BRANCH_WIDTH), BF16),
        scratch_shapes=[pltpu.VMEM((l, LANES), F32)] * 3,
        compiler_params=_cparams(("parallel", "parallel")),
        name="swa_attn",
    )(proj3, proj3, proj3, cos, sa, sb, sink_l)


@functools.lru_cache(maxsize=None)
def _gla_consts():
    ch = GLA_CH
    r = np.arange(ch)[:, None]
    j = np.arange(ch)[None, :]
    mats = []
    for m in GLA_LEVELS:
        lo = (r // m) * m
        mats.append(((j >= lo) & (j <= r)).astype(np.float32))
        mats.append(((j > r) & (j <= lo + m - 1)).astype(np.float32))
    dstack = np.concatenate(mats, axis=0)
    lm = []
    for m in GLA_LEVELS[:-1]:
        mk = (((r // m) % 2 == 1) & ((j // m) == (r // m) - 1)).astype(np.float32)
        lm.append(np.tile(mk, (GLA_HEADS, 1)))
    lmask = np.stack(lm)
    dmask = np.tile(((r // GLA_SUB) == (j // GLA_SUB)).astype(np.float32), (1, GLA_HEADS))
    nsub = ch // GLA_SUB
    e = np.zeros((GLA_SUB, GLA_HEADS * GLA_DK, GLA_HEADS * ch), np.float32)
    for u in range(GLA_SUB):
        for h in range(GLA_HEADS):
            for s in range(nsub):
                e[u, h * GLA_DK : (h + 1) * GLA_DK, h * ch + GLA_SUB * s + u] = 1.0
    rr = np.arange(GLA_HEADS * GLA_DV)[:, None]
    cc = np.arange(GLA_HEADS * GLA_DK)[None, :]
    bd = ((rr // GLA_DV) == (cc // GLA_DK)).astype(np.float32)
    return dstack, lmask, dmask, e, bd


def _split3(x):
    hi = x.astype(BF16)
    r1 = x - hi.astype(F32)
    mid = r1.astype(BF16)
    lo = (r1 - mid.astype(F32)).astype(BF16)
    return hi, mid, lo


def _bcast_sub(x, u):
    r, w = x.shape
    x3 = x.reshape(r // GLA_SUB, GLA_SUB, w)
    return jnp.broadcast_to(x3[:, u : u + 1, :], x3.shape).reshape(r, w)


def _gla_kernel(
    q_ref, k_ref, v_ref, r_ref, glr_ref, a2_ref, ab_ref, ng_ref, dst_ref, lmask_ref, dmask_ref, e_ref, bd_ref,
    o_ref, st_ref,
):
    ch = GLA_CH
    rs = q_ref.shape[0]
    ncs = rs // ch
    nlev = len(GLA_LEVELS)
    hk = GLA_HEADS * GLA_DK

    @pl.when(pl.program_id(1) == 0)
    def _():
        st_ref[...] = jnp.zeros_like(st_ref)

    q = q_ref[...] * (GLA_DK**-0.5)
    k = k_ref[...]
    z = jnp.dot(glr_ref[...], a2_ref[...], precision=HIGHEST, preferred_element_type=F32) + ab_ref[...]
    g = (jnp.minimum(z, 0.0) - jnp.log1p(jnp.exp(-jnp.abs(z)))) * (1.0 / GLA_TAU)
    g3 = _split3(g)
    dst = dst_ref[...]

    es = []
    for c in range(ncs):
        rows = slice(c * ch, (c + 1) * ch)
        es.append(_bdot(dst, g3[0][rows]) + _bdot(dst, g3[1][rows]) + _bdot(dst, g3[2][rows]))

    def eq(c, li):
        return es[c][(2 * li) * ch : (2 * li + 1) * ch]

    def ek(c, li):
        return es[c][(2 * li + 1) * ch : (2 * li + 2) * ch]

    cs = jnp.concatenate([eq(c, 0) for c in range(ncs)], axis=0)
    tsub = lax.broadcasted_iota(jnp.int32, (rs, hk), 0) % GLA_SUB
    arep = jnp.zeros((rs, GLA_HEADS * ch), F32)
    for u in range(GLA_SUB):
        dec = jnp.exp(jnp.where(tsub >= u, cs - _bcast_sub(cs, u), -jnp.inf))
        p = q * _bcast_sub(k, u) * dec
        arep = arep + _bdot(p.astype(BF16), e_ref[u])
    arep = arep * jnp.concatenate([dmask_ref[...]] * ncs, axis=0)

    lane_k = lax.broadcasted_iota(jnp.int32, (ch, hk), 1) // GLA_DK
    ng = ng_ref[...]
    bd = bd_ref[...]
    for c in range(ncs):
        rows = slice(c * ch, (c + 1) * ch)
        qc, kc = q[rows], k[rows]
        vc = v_ref[rows, :]
        aoff = jnp.zeros((GLA_HEADS * ch, ch), F32)
        for li in range(nlev - 1):
            qe = qc * jnp.exp(eq(c, li))
            ke = (kc * jnp.exp(ek(c, li))).astype(BF16)
            qst = jnp.concatenate([jnp.where(lane_k == h, qe, 0.0) for h in range(GLA_HEADS)], axis=0).astype(BF16)
            aoff = aoff + _bdot_nt(qst, ke) * lmask_ref[li]
        cum = eq(c, nlev - 1)
        st = st_ref[...]
        o_inter = _bdot_nt((qc * jnp.exp(cum)).astype(BF16), st.astype(BF16))
        vb = vc.astype(BF16)
        outs = []
        for h in range(GLA_HEADS):
            a_h = aoff[h * ch : (h + 1) * ch] + arep[rows, h * ch : (h + 1) * ch]
            o_h = _bdot(a_h.astype(BF16), vb[:, h * GLA_DV : (h + 1) * GLA_DV]) + o_inter[:, h * GLA_DV : (h + 1) * GLA_DV]
            ms = jnp.mean(o_h * o_h, axis=-1, keepdims=True)
            outs.append(o_h * lax.rsqrt(ms + NORM_EPS))
        o = jnp.concatenate(outs, axis=1) * ng
        rc = r_ref[rows, :]
        o_ref[rows, :] = (o * (rc * jax.nn.sigmoid(rc))).astype(o_ref.dtype)
        ke_last = (kc * jnp.exp(ek(c, nlev - 1))).astype(BF16)
        kv = _bdot(vc.T.astype(BF16), ke_last)
        st_ref[...] = st * jnp.exp(cum[ch - 1 : ch, :]) + kv * bd


def _gla(proj, b, a2, ab, ng):
    t = proj.shape[0]
    l = t // b
    rs = GLA_CH * GLA_NCS
    assert l % rs == 0
    ns = l // rs
    dstack, lmask, dmask, e, bd = _gla_consts()
    a2p = jnp.zeros((LANES, GLA_HEADS * GLA_DK), F32).at[:GLA_LOWRANK].set(a2.astype(F32))

    def rowblk(w, c0):
        return pl.BlockSpec((rs, w), lambda i, s: (i * ns + s, c0 // w))

    def full(shape):
        nd = len(shape)
        return pl.BlockSpec(shape, lambda i, s: (0,) * nd)

    hk, hv = GLA_HEADS * GLA_DK, GLA_HEADS * GLA_DV
    return pl.pallas_call(
        _gla_kernel,
        grid=(b, ns),
        in_specs=[
            rowblk(hk, C_GQ),
            rowblk(hk, C_GK),
            rowblk(hv, C_GV),
            rowblk(hv, C_GR),
            rowblk(LANES, C_GLR),
            full((LANES, hk)),
            full((1, hk)),
            full((1, hv)),
            full(dstack.shape),
            full(lmask.shape),
            full(dmask.shape),
            full(e.shape),
            full(bd.shape),
        ],
        out_specs=pl.BlockSpec((rs, hv), lambda i, s: (i * ns + s, 0)),
        out_shape=jax.ShapeDtypeStruct((t, hv), BF16),
        scratch_shapes=[pltpu.VMEM((hv, hk), F32)],
        compiler_params=_cparams(("parallel", "arbitrary")),
        name="gla",
    )(
        proj, proj, proj, proj, proj, a2p, ab.reshape(1, hk).astype(F32), ng.reshape(1, hv).astype(F32),
        jnp.asarray(dstack, BF16), jnp.asarray(lmask), jnp.asarray(dmask), jnp.asarray(e, BF16), jnp.asarray(bd),
    )


def _s5_params(lam_re, lam_im, log_dt, b_re, b_im, c_re, c_im, d, nch):
    f = F32
    hp = HIGHEST
    cs = S5_CH
    dt = jnp.exp(log_dt.astype(f))[:, None]
    lr, li = lam_re.astype(f), lam_im.astype(f)
    mag = jnp.exp(lr * dt)
    ab_re, ab_im = mag * jnp.cos(li * dt), mag * jnp.sin(li * dt)
    den = lr * lr + li * li
    z_re = ((ab_re - 1.0) * lr + ab_im * li) / den
    z_im = (ab_im * lr - (ab_re - 1.0) * li) / den
    br, bi = b_re.astype(f), b_im.astype(f)
    bb_re = z_re[..., None] * br - z_im[..., None] * bi
    bb_im = z_re[..., None] * bi + z_im[..., None] * br

    def apow(p):
        p = jnp.asarray(p, f)[:, None, None]
        m = jnp.exp(p * (lr * dt))
        return m * jnp.cos(p * (li * dt)), m * jnp.sin(p * (li * dt))

    cr, ci = c_re.astype(f), c_im.astype(f)
    p_re, p_im = apow(np.arange(cs + 1))
    ab_b_re = p_re[..., None] * bb_re - p_im[..., None] * bb_im
    ab_b_im = p_re[..., None] * bb_im + p_im[..., None] * bb_re
    kk = jnp.einsum("gcn,tgnd->tgcd", cr, ab_b_re, precision=hp) - jnp.einsum("gcn,tgnd->tgcd", ci, ab_b_im, precision=hp)
    nb, gpb = S5_GROUPS // S5_GPB, S5_GPB
    eye = jnp.eye(gpb, dtype=f)
    kcat = jnp.einsum("tpgcd,gh->ptgdhc", kk[:cs].reshape(cs, nb, gpb, S5_GROUP, S5_GROUP), eye)
    kcat = kcat.reshape(nb, cs * LANES, LANES)
    rev = np.arange(cs - 1, -1, -1)
    mb = jnp.stack([ab_b_re[rev], ab_b_im[rev]], axis=2)
    mb = jnp.einsum("jpgrnd,gh->pjgdhrn", mb.reshape(cs, nb, gpb, 2, S5_STATE, S5_GROUP), eye)
    mb = mb.reshape(nb, cs * LANES, gpb * 2 * S5_STATE)
    q_re, q_im = p_re[1:], p_im[1:]
    mc_re = cr[None] * q_re[:, :, None, :] - ci[None] * q_im[:, :, None, :]
    mc_im = -(cr[None] * q_im[:, :, None, :] + ci[None] * q_re[:, :, None, :])
    mc = jnp.stack([mc_re, mc_im], axis=2)
    mc = jnp.einsum("tpgrcn,gh->pgrnthc", mc.reshape(cs, nb, gpb, 2, S5_GROUP, S5_STATE), eye)
    mc = mc.reshape(nb, gpb * 2 * S5_STATE, cs * LANES)
    nstep = max(1, int(math.log2(nch)))
    s_re, s_im = apow(cs * (2 ** np.arange(nstep)))
    ar = jnp.concatenate([s_re, s_re], axis=2).reshape(nstep, nb, gpb * 2 * S5_STATE).transpose(1, 0, 2)
    ai = jnp.concatenate([-s_im, s_im], axis=2).reshape(nstep, nb, gpb * 2 * S5_STATE).transpose(1, 0, 2)
    dd = d.astype(f).reshape(nb, 1, LANES)
    return kcat.astype(BF16), mb.astype(BF16), mc.astype(BF16), ar, ai, dd


def _s5_kernel(u_ref, kcat_ref, mb_ref, mc_ref, ar_ref, ai_ref, d_ref, y_ref, *, nchb):
    cs = S5_CH
    t = u_ref.shape[0]
    nch = t // cs
    w = 2 * S5_STATE
    ucat = jnp.concatenate([u_ref[pl.ds(j, nch, stride=cs), :] for j in range(cs)], axis=1).astype(BF16)
    e_all = _bdot(ucat, mb_ref[0])
    cidx = lax.broadcasted_iota(jnp.int32, (nch, w), 0) % nchb
    ar, ai = ar_ref[0], ai_ref[0]
    hp = []
    for g in range(S5_GPB):
        e = e_all[:, g * w : (g + 1) * w]
        for kstep in range(ar.shape[0]):
            s = 2**kstep
            if s >= nchb:
                break
            sh = jnp.where(cidx >= s, pltpu.roll(e, s, axis=0), 0.0)
            e = (e + sh * ar[kstep : kstep + 1, g * w : (g + 1) * w]
                 + pltpu.roll(sh, S5_STATE, axis=1) * ai[kstep : kstep + 1, g * w : (g + 1) * w])
        hp.append(jnp.where(cidx >= 1, pltpu.roll(e, 1, axis=0), 0.0).astype(BF16))
    ycar = _bdot(jnp.concatenate(hp, axis=1), mc_ref[0])
    for tt in range(cs):
        y_ref[pl.ds(tt, nch, stride=cs), :] = ycar[:, tt * LANES : (tt + 1) * LANES]

    rt = min(S5_RT, t)
    rmod = lax.broadcasted_iota(jnp.int32, (rt, LANES), 0) % cs
    dvec = d_ref[0]
    kcat = kcat_ref[0]

    def tile(i, carry):
        rows = pl.ds(pl.multiple_of(i * rt, rt), rt)
        ut = u_ref[rows, :]
        ush = [ut] + [jnp.where(rmod >= tau, pltpu.roll(ut, tau, axis=0), 0.0) for tau in range(1, cs)]
        ush = jnp.concatenate(ush, axis=1).astype(BF16)
        y_ref[rows, :] = y_ref[rows, :] + _bdot(ush, kcat) + dvec * ut
        return carry

    lax.fori_loop(0, t // rt, tile, 0)


def _s5_core(proj, b, params):
    t = proj.shape[0]
    nchb = t // b // S5_CH
    kcat, mb, mc, ar, ai, dd = params

    def bspec(shape):
        return pl.BlockSpec((1,) + shape, lambda p: (p, 0, 0))

    return pl.pallas_call(
        functools.partial(_s5_kernel, nchb=nchb),
        grid=(S5_GROUPS // S5_GPB,),
        in_specs=[
            pl.BlockSpec((t, LANES), lambda p: (0, C_S5 // LANES + p)),
            bspec(kcat.shape[1:]),
            bspec(mb.shape[1:]),
            bspec(mc.shape[1:]),
            bspec(ar.shape[1:]),
            bspec(ai.shape[1:]),
            bspec((1, LANES)),
        ],
        out_specs=pl.BlockSpec((t, LANES), lambda p: (0, p)),
        out_shape=jax.ShapeDtypeStruct((t, BRANCH_WIDTH), F32),
        compiler_params=_cparams(("parallel",)),
        name="s5_core",
    )(proj, kcat, mb, mc, ar, ai, dd)


def _merge_kernel(
    x_ref, g1_ref, gla_ref, ys5_ref, dil_ref, swa_ref, gluw_ref, glub_ref, wg0, wg1, wg2, wg3, wb_ref, wo_ref,
    o_ref, h_s, s5_s, acc_s,
):
    j = pl.program_id(1)

    @pl.when(j == 0)
    def _():
        h_s[...] = _rms(x_ref[...], g1_ref[...]).astype(BF16)
        zz = jax.nn.gelu(ys5_ref[...])
        gate = jax.nn.sigmoid(_bdot(zz.astype(BF16), gluw_ref[...]) + glub_ref[...])
        s5_s[...] = (zz * gate).astype(BF16)
        acc_s[...] = jnp.zeros_like(acc_s)

    h = h_s[...]
    branches = (gla_ref[...], s5_s[...], dil_ref[...], swa_ref[...])
    mixed = None
    for m, (wg, br) in enumerate(zip((wg0, wg1, wg2, wg3), branches)):
        term = jax.nn.sigmoid(_bdot(h, wg[...])) * _bdot(br, wb_ref[m])
        mixed = term if mixed is None else mixed + term
    acc_s[...] += _bdot(mixed.astype(BF16), wo_ref[...])

    @pl.when(j == pl.num_programs(1) - 1)
    def _():
        o_ref[...] = x_ref[...] + acc_s[...]


def _merge(x, g1, o_gla, y_s5, o_dil, o_swa, gluw, glub, w_all, wb, wo, layer):
    t, d = x.shape
    tm, tn = min(TM_MERGE, t), TN_MERGE
    nj = d // tn
    bw = BRANCH_WIDTH

    def rowblk(w):
        return pl.BlockSpec((tm, w), lambda i, j: (i, 0))

    def gate_spec(m):
        return pl.BlockSpec((None, d, tn), lambda i, j, m=m: (layer, 0, N_SMALL // tn + m * nj + j))

    return pl.pallas_call(
        _merge_kernel,
        grid=(t // tm, nj),
        in_specs=[
            rowblk(d),
            pl.BlockSpec((1, d), lambda i, j: (0, 0)),
            rowblk(bw),
            rowblk(bw),
            rowblk(bw),
            rowblk(bw),
            pl.BlockSpec((None, bw, bw), lambda i, j: (layer, 0, 0)),
            pl.BlockSpec((1, bw), lambda i, j: (0, 0)),
            gate_spec(0),
            gate_spec(1),
            gate_spec(2),
            gate_spec(3),
            pl.BlockSpec((None, N_BRANCH, bw, tn), lambda i, j: (layer, 0, 0, j)),
            pl.BlockSpec((None, tn, d), lambda i, j: (layer, j, 0)),
        ],
        out_specs=rowblk(d),
        out_shape=jax.ShapeDtypeStruct((t, d), F32),
        scratch_shapes=[pltpu.VMEM((tm, d), BF16), pltpu.VMEM((tm, bw), BF16), pltpu.VMEM((tm, d), F32)],
        compiler_params=_cparams(("parallel", "arbitrary")),
        name="merge",
    )(x, g1, o_gla, y_s5, o_dil, o_swa, gluw, glub, w_all, w_all, w_all, w_all, wb, wo)


def _ffn_kernel(x_ref, g2_ref, wg_ref, wu_ref, wd_ref, gf_ref, o_ref, h_s, acc_s, *, final_norm):
    j = pl.program_id(1)

    @pl.when(j == 0)
    def _():
        h_s[...] = _rms(x_ref[...], g2_ref[...]).astype(BF16)
        acc_s[...] = jnp.zeros_like(acc_s)

    h = h_s[...]
    gate = _bdot(h, wg_ref[...])
    act = (gate * jax.nn.sigmoid(gate)) * _bdot(h, wu_ref[...])
    acc_s[...] += _bdot(act.astype(BF16), wd_ref[...])

    @pl.when(j == pl.num_programs(1) - 1)
    def _():
        y = x_ref[...] + acc_s[...]
        o_ref[...] = _rms(y, gf_ref[...]) if final_norm else y


def _ffn(x, g2, wg, wu, wd, gf, layer, final_norm):
    t, d = x.shape
    fh = wg.shape[-1]
    tm, tf = min(TM_FFN, t), TF_FFN
    return pl.pallas_call(
        functools.partial(_ffn_kernel, final_norm=final_norm),
        grid=(t // tm, fh // tf),
        in_specs=[
            pl.BlockSpec((tm, d), lambda i, j: (i, 0)),
            pl.BlockSpec((1, d), lambda i, j: (0, 0)),
            pl.BlockSpec((None, d, tf), lambda i, j: (layer, 0, j)),
            pl.BlockSpec((None, d, tf), lambda i, j: (layer, 0, j)),
            pl.BlockSpec((None, tf, d), lambda i, j: (layer, j, 0)),
            pl.BlockSpec((1, d), lambda i, j: (0, 0)),
        ],
        out_specs=pl.BlockSpec((tm, d), lambda i, j: (i, 0)),
        out_shape=jax.ShapeDtypeStruct((t, d), F32),
        scratch_shapes=[pltpu.VMEM((tm, d), BF16), pltpu.VMEM((tm, d), F32)],
        compiler_params=_cparams(("parallel", "arbitrary")),
        name="ffn",
    )(x, g2, wg, wu, wd, gf)


def _pack_w_in(w_in):
    pad = jnp.zeros(w_in.shape[:-1] + (N_SMALL - N_ORIG_SMALL,), BF16)
    wb = w_in.astype(BF16)
    return jnp.concatenate(
        [
            wb[..., :GLR_ORIG],
            wb[..., GLR_ORIG + GLA_LOWRANK : N_ORIG_SMALL],
            wb[..., GLR_ORIG : GLR_ORIG + GLA_LOWRANK],
            pad,
            wb[..., N_ORIG_SMALL:],
        ],
        axis=-1,
    )


def kernel(x, positions, norm1_g, w_in, gla_a2, gla_a_b, gla_norm_g, s5_lambda_re, s5_lambda_im, s5_log_dt, s5_b_re, s5_b_im, s5_c_re, s5_c_im, s5_d, s5_glu_w, s5_glu_b, swa_sinks, w_branch, w_out, norm2_g, w_ffn_gate, w_ffn_up, w_ffn_down, final_norm_g):
    b, l, d = x.shape
    t = b * l
    depth = w_in.shape[0]
    xs = x.reshape(t, d).astype(F32)
    w_all = _pack_w_in(w_in)
    w_br = w_branch.astype(BF16)
    w_o = w_out.astype(BF16)
    w_fg, w_fu, w_fd = w_ffn_gate.astype(BF16), w_ffn_up.astype(BF16), w_ffn_down.astype(BF16)
    glu_w = s5_glu_w.astype(BF16)
    tabs = _rope_tables(positions)
    gf = final_norm_g.reshape(1, d).astype(F32)
    for i in range(depth):
        g1 = norm1_g[i].reshape(1, d).astype(F32)
        proj = _inproj(xs, g1, w_all, i)
        proj3 = proj.reshape(b, l, N_SMALL)
        o_gla = _gla(proj, b, gla_a2[i], gla_a_b[i], gla_norm_g[i])
        s5p = _s5_params(s5_lambda_re[i], s5_lambda_im[i], s5_log_dt[i], s5_b_re[i], s5_b_im[i], s5_c_re[i],
                         s5_c_im[i], s5_d[i], l // S5_CH)
        y_s5 = _s5_core(proj, b, s5p)
        o_dil = _dilated(proj3, tabs).reshape(t, BRANCH_WIDTH)
        o_swa = _swa(proj3, tabs, swa_sinks[i]).reshape(t, BRANCH_WIDTH)
        xs = _merge(xs, g1, o_gla, y_s5, o_dil, o_swa, glu_w, s5_glu_b[i].reshape(1, -1).astype(F32),
                    w_all, w_br, w_o, i)
        xs = _ffn(xs, norm2_g[i].reshape(1, d).astype(F32), w_fg, w_fu, w_fd, gf, i, i == depth - 1)
    return xs.reshape(b, l, d).astype(x.dtype)
```

```python
import functools
import math

import jax
import jax.numpy as jnp
import numpy as np
from jax import lax
from jax.experimental import pallas as pl
from jax.experimental.pallas import tpu as pltpu

F32 = jnp.float32
BF16 = jnp.bfloat16
HIGHEST = lax.Precision.HIGHEST

D_MODEL = 2048
DEPTH = 2
N_BRANCH = 4
BRANCH_WIDTH = 512
HEAD_DIM = 64
ATT_BLOCK = 128
ROPE_THETA = 500000.0
ROPE_DIM = HEAD_DIM // 4
NORM_EPS = 1e-6
GLA_HEADS = 4
GLA_DK = 64
GLA_DV = BRANCH_WIDTH // GLA_HEADS
GLA_LOWRANK = 16
GLA_TAU = 16.0
S5_GROUP = 16
S5_GROUPS = BRANCH_WIDTH // S5_GROUP
S5_STATE = 64
DIL_CONFIGS = ((128, 1), (512, 4), (2048, 16))
DIL_SPAN = ATT_BLOCK * 16
SWA_HEADS = BRANCH_WIDTH // HEAD_DIM
SWA_KV_HEADS = 2
SWA_WINDOW = 128
FFN_HIDDEN = -((-8 * D_MODEL) // (3 * 256)) * 256

LANES = 128
SUBLANES = 8
VMEM_LIMIT = 56 * 1024 * 1024

C_GQ, C_GK, C_GV, C_GR, C_S5 = 0, 256, 512, 1024, 1536
C_CQ, C_CK, C_CV, C_SQ, C_SK, C_SV, C_GLR = 2048, 2560, 3072, 3584, 4096, 4224, 4352
N_SMALL = 4608
N_ORIG_SMALL = 4368
GLR_ORIG = 1536

TM_PROJ, TN_PROJ = 1024, 768
TM_MERGE, TN_MERGE = 512, 256
TM_FFN, TF_FFN = 512, 512
GLA_CH = 128
GLA_NCS = 4
GLA_SUB = 8
GLA_LEVELS = (8, 16, 32, 64, 128)
PACK_RT, PACK_CW = 256, 1024
S5_CH = 16
S5_GPB = LANES // S5_GROUP
S5_RT = 512
ATT_UNROLL = 8


def _cparams(sem):
    return pltpu.CompilerParams(dimension_semantics=sem, vmem_limit_bytes=VMEM_LIMIT)


def _rms(x, g):
    ms = jnp.mean(x * x, axis=-1, keepdims=True)
    return x * lax.rsqrt(ms + NORM_EPS) * g


def _bdot(a, b):
    return jnp.dot(a, b, preferred_element_type=F32)


def _bdot_nt(a, b):
    return lax.dot_general(a, b, (((1,), (1,)), ((), ())), preferred_element_type=F32)


def _inproj_kernel(x_ref, g_ref, w_ref, o_ref, h_ref):
    @pl.when(pl.program_id(1) == 0)
    def _():
        h_ref[...] = _rms(x_ref[...], g_ref[...]).astype(BF16)

    o_ref[...] = _bdot(h_ref[...], w_ref[...])


def _inproj(x, g, w_all, layer):
    t, d = x.shape
    n = N_SMALL
    tm, tn = min(TM_PROJ, t), TN_PROJ
    return pl.pallas_call(
        _inproj_kernel,
        grid=(t // tm, n // tn),
        in_specs=[
            pl.BlockSpec((tm, d), lambda i, j: (i, 0)),
            pl.BlockSpec((1, d), lambda i, j: (0, 0)),
            pl.BlockSpec((None, d, tn), lambda i, j: (layer, 0, j)),
        ],
        out_specs=pl.BlockSpec((tm, tn), lambda i, j: (i, j)),
        out_shape=jax.ShapeDtypeStruct((t, n), F32),
        scratch_shapes=[pltpu.VMEM((tm, d), BF16)],
        compiler_params=_cparams(("parallel", "arbitrary")),
        name="inproj",
    )(x, g, w_all)


def _rope_kernel(pos_ref, cos_ref, sa_ref, sb_ref):
    pos = pos_ref[0].astype(F32)
    lane = lax.broadcasted_iota(jnp.int32, (SUBLANES, LANES), 1)
    d = lane % HEAD_DIM
    half = ROPE_DIM // 2
    fi = (d % half).astype(F32) / half
    inv = jnp.power(jnp.full((SUBLANES, LANES), ROPE_THETA, F32), -fi)[0:1]
    d1 = d[0:1]
    ang = pos * inv
    c, s = jnp.cos(ang), jnp.sin(ang)
    cos_ref[0] = jnp.where(d1 < ROPE_DIM, c, 1.0)
    sa_ref[0] = jnp.where(d1 < half, -s, 0.0)
    sb_ref[0] = jnp.where((d1 >= half) & (d1 < ROPE_DIM), s, 0.0)


def _rope_tables(positions):
    b, l = positions.shape
    spec = pl.BlockSpec((1, l, LANES), lambda i: (i, 0, 0))
    shp = jax.ShapeDtypeStruct((b, l, LANES), F32)
    return pl.pallas_call(
        _rope_kernel,
        grid=(b,),
        in_specs=[pl.BlockSpec((1, l, 1), lambda i: (i, 0, 0))],
        out_specs=[spec, spec, spec],
        out_shape=[shp, shp, shp],
        compiler_params=_cparams(("parallel",)),
        name="rope_tables",
    )(positions.reshape(b, l, 1))


def _rope_apply(x, c, sa, sb):
    half = ROPE_DIM // 2
    return x * c + pltpu.roll(x, LANES - half, axis=1) * sa + pltpu.roll(x, half, axis=1) * sb


def _band_block(q, kp, ko, vp, vo, has_prev, strict_prev):
    t = ATT_BLOCK
    lane = lax.broadcasted_iota(jnp.int32, (t, LANES), 1)
    in_a = lane < HEAD_DIM
    q2 = jnp.concatenate([jnp.where(in_a, q, 0.0), jnp.where(in_a, 0.0, q)], axis=0).astype(BF16)
    s = _bdot_nt(q2, jnp.concatenate([kp, ko], axis=0).astype(BF16))
    qi = lax.broadcasted_iota(jnp.int32, (2 * t, 2 * t), 0) % t
    kj = lax.broadcasted_iota(jnp.int32, (2 * t, 2 * t), 1)
    lo = qi + ((1 if strict_prev else 0) + jnp.where(has_prev, 0, t))
    ok = ((kj < t) & (kj >= lo)) | ((kj >= t) & (kj - t <= qi))
    s = jnp.where(ok, s, -jnp.inf)
    m = jnp.max(s, axis=-1, keepdims=True)
    p = jnp.exp(s - m).astype(BF16)
    ones = jnp.ones((2 * t, LANES), BF16)
    vcat = jnp.concatenate([jnp.concatenate([vp, vo], axis=0).astype(BF16), ones], axis=1)
    od = _bdot(p, vcat)
    o = jnp.where(in_a, od[:t, :LANES], od[t:, :LANES])
    df = jnp.where(in_a, od[:t, LANES:], od[t:, LANES:])
    mf = jnp.where(in_a, m[:t], m[t:])
    return mf, df, o


def _dil_kernel(q_ref, k_ref, v_ref, c_ref, sa_ref, sb_ref, o_ref, qs, ks, ms, ds, os_):
    l = q_ref.shape[1]
    c, sa, sb = c_ref[0], sa_ref[0], sb_ref[0]
    qs[...] = _rope_apply(q_ref[0], c, sa, sb) * (HEAD_DIM**-0.5)
    ks[...] = _rope_apply(k_ref[0], c, sa, sb)
    t = ATT_BLOCK
    nblk = DIL_SPAN // t

    for sblk in range(l // DIL_SPAN):
        base = sblk * DIL_SPAN
        for ci, (_, dil) in enumerate(DIL_CONFIGS):

            def body(idx, carry, dil=dil, ci=ci, base=base):
                r = idx % dil
                n = idx // dil
                loc = r + dil * t * n
                start = base + loc
                has_prev = start >= dil * t
                pstart = jnp.where(has_prev, start - dil * t, start)
                if dil == 1:
                    start = pl.multiple_of(start, t)
                    pstart = pl.multiple_of(pstart, t)
                    loc = pl.multiple_of(loc, t)
                    rows, prow, lrow = pl.ds(start, t), pl.ds(pstart, t), pl.ds(loc, t)
                else:
                    rows = pl.ds(start, t, stride=dil)
                    prow = pl.ds(pstart, t, stride=dil)
                    lrow = pl.ds(loc, t, stride=dil)
                mf, df, o = _band_block(
                    qs[rows, :], ks[prow, :], ks[rows, :], v_ref[0, prow, :], v_ref[0, rows, :], has_prev, False
                )
                ms[ci, lrow, :] = mf
                ds[ci, lrow, :] = df
                os_[ci, lrow, :] = o
                return carry

            lax.fori_loop(0, nblk, body, 0, unroll=ATT_UNROLL)

        m0, m1, m2 = ms[0], ms[1], ms[2]
        mx = jnp.maximum(jnp.maximum(m0, m1), m2)
        w0, w1, w2 = jnp.exp(m0 - mx), jnp.exp(m1 - mx), jnp.exp(m2 - mx)
        den = w0 * ds[0] + w1 * ds[1] + w2 * ds[2]
        num = w0 * os_[0] + w1 * os_[1] + w2 * os_[2]
        o_ref[0, base : base + DIL_SPAN, :] = (num / den).astype(o_ref.dtype)


def _dilated(proj3, tabs):
    b, l, _ = proj3.shape
    assert l % DIL_SPAN == 0
    cos, sa, sb = tabs

    def col(c0):
        return pl.BlockSpec((1, l, LANES), lambda i, p: (i, 0, c0 // LANES + p))

    tab = pl.BlockSpec((1, l, LANES), lambda i, p: (i, 0, 0))
    return pl.pallas_call(
        _dil_kernel,
        grid=(b, BRANCH_WIDTH // LANES),
        in_specs=[col(C_CQ), col(C_CK), col(C_CV), tab, tab, tab],
        out_specs=pl.BlockSpec((1, l, LANES), lambda i, p: (i, 0, p)),
        out_shape=jax.ShapeDtypeStruct((b, l, BRANCH_WIDTH), BF16),
        scratch_shapes=[
            pltpu.VMEM((l, LANES), F32),
            pltpu.VMEM((l, LANES), F32),
            pltpu.VMEM((3, DIL_SPAN, LANES), F32),
            pltpu.VMEM((3, DIL_SPAN, LANES), F32),
            pltpu.VMEM((3, DIL_SPAN, LANES), F32),
        ],
        compiler_params=_cparams(("parallel", "parallel")),
        name="dilated_attn",
    )(proj3, proj3, proj3, cos, sa, sb)


def _swa_kernel(q_ref, k_ref, v_ref, c_ref, sa_ref, sb_ref, sink_ref, o_ref, qs, ks, vs):
    l = q_ref.shape[1]
    c, sa, sb = c_ref[0], sa_ref[0], sb_ref[0]
    qs[...] = _rope_apply(q_ref[0], c, sa, sb) * (HEAD_DIM**-0.5)
    g = pl.program_id(1) // (SWA_HEADS // SWA_KV_HEADS // 2)
    lane = lax.broadcasted_iota(jnp.int32, (l, LANES), 1)
    keep = (lane // HEAD_DIM) == g
    kr = _rope_apply(k_ref[0], c, sa, sb)
    ks[...] = jnp.where(keep, kr, pltpu.roll(kr, HEAD_DIM, axis=1))
    v = v_ref[0]
    vs[...] = jnp.where(keep, v, pltpu.roll(v, HEAD_DIM, axis=1))
    sink = sink_ref[0]
    t = ATT_BLOCK

    def body(n, carry):
        start = pl.multiple_of(n * t, t)
        has_prev = n > 0
        pstart = pl.multiple_of(jnp.where(has_prev, start - t, start), t)
        rows, prow = pl.ds(start, t), pl.ds(pstart, t)
        mf, df, o = _band_block(qs[rows, :], ks[prow, :], ks[rows, :], vs[prow, :], vs[rows, :], has_prev, True)
        lse = mf + jnp.log(df)
        o_ref[0, rows, :] = (o / df * jax.nn.sigmoid(lse - sink)).astype(o_ref.dtype)
        return carry

    lax.fori_loop(0, l // t, body, 0, unroll=ATT_UNROLL)


def _swa(proj3, tabs, sinks):
    b, l, _ = proj3.shape
    cos, sa, sb = tabs
    npair = BRANCH_WIDTH // LANES
    sink_l = jnp.repeat(sinks.astype(F32), HEAD_DIM).reshape(npair, 1, LANES)
    tab = pl.BlockSpec((1, l, LANES), lambda i, p: (i, 0, 0))
    return pl.pallas_call(
        _swa_kernel,
        grid=(b, npair),
        in_specs=[
            pl.BlockSpec((1, l, LANES), lambda i, p: (i, 0, C_SQ // LANES + p)),
            pl.BlockSpec((1, l, LANES), lambda i, p: (i, 0, C_SK // LANES)),
            pl.BlockSpec((1, l, LANES), lambda i, p: (i, 0, C_SV // LANES)),
            tab,
            tab,
            tab,
            pl.BlockSpec((1, 1, LANES), lambda i, p: (p, 0, 0)),
        ],
        out_specs=pl.BlockSpec((1, l, LANES), lambda i, p: (i, 0, p)),
        out_shape=jax.ShapeDtypeStruct((b, l, BRANCH_WIDTH), BF16),
        scratch_shapes=[pltpu.VMEM((l, LANES), F32)] * 3,
        compiler_params=_cparams(("parallel", "parallel")),
        name="swa_attn",
    )(proj3, proj3, proj3, cos, sa, sb, sink_l)


@functools.lru_cache(maxsize=None)
def _gla_consts():
    ch = GLA_CH
    r = np.arange(ch)[:, None]
    j = np.arange(ch)[None, :]
    mats = []
    for m in GLA_LEVELS:
        lo = (r // m) * m
        mats.append(((j >= lo) & (j <= r)).astype(np.float32))
        mats.append(((j > r) & (j <= lo + m - 1)).astype(np.float32))
    dstack = np.concatenate(mats, axis=0)
    lm = []
    for m in GLA_LEVELS[:-1]:
        mk = (((r // m) % 2 == 1) & ((j // m) == (r // m) - 1)).astype(np.float32)
        lm.append(np.tile(mk, (GLA_HEADS, 1)))
    lmask = np.stack(lm)
    dmask = np.tile(((r // GLA_SUB) == (j // GLA_SUB)).astype(np.float32), (1, GLA_HEADS))
    nsub = ch // GLA_SUB
    e = np.zeros((GLA_SUB, GLA_HEADS * GLA_DK, GLA_HEADS * ch), np.float32)
    for u in range(GLA_SUB):
        for h in range(GLA_HEADS):
            for s in range(nsub):
                e[u, h * GLA_DK : (h + 1) * GLA_DK, h * ch + GLA_SUB * s + u] = 1.0
    rr = np.arange(GLA_HEADS * GLA_DV)[:, None]
    cc = np.arange(GLA_HEADS * GLA_DK)[None, :]
    bd = ((rr // GLA_DV) == (cc // GLA_DK)).astype(np.float32)
    return dstack, lmask, dmask, e, bd


def _split3(x):
    hi = x.astype(BF16)
    r1 = x - hi.astype(F32)
    mid = r1.astype(BF16)
    lo = (r1 - mid.astype(F32)).astype(BF16)
    return hi, mid, lo


def _bcast_sub(x, u):
    r, w = x.shape
    x3 = x.reshape(r // GLA_SUB, GLA_SUB, w)
    return jnp.broadcast_to(x3[:, u : u + 1, :], x3.shape).reshape(r, w)


def _gla_kernel(
    q_ref, k_ref, v_ref, r_ref, glr_ref, a2_ref, ab_ref, ng_ref, dst_ref, lmask_ref, dmask_ref, e_ref, bd_ref,
    o_ref, st_ref,
):
    ch = GLA_CH
    rs = q_ref.shape[0]
    ncs = rs // ch
    nlev = len(GLA_LEVELS)
    hk = GLA_HEADS * GLA_DK

    @pl.when(pl.program_id(1) == 0)
    def _():
        st_ref[...] = jnp.zeros_like(st_ref)

    q = q_ref[...] * (GLA_DK**-0.5)
    k = k_ref[...]
    z = jnp.dot(glr_ref[...], a2_ref[...], precision=HIGHEST, preferred_element_type=F32) + ab_ref[...]
    g = (jnp.minimum(z, 0.0) - jnp.log1p(jnp.exp(-jnp.abs(z)))) * (1.0 / GLA_TAU)
    g3 = _split3(g)
    dst = dst_ref[...]

    es = []
    for c in range(ncs):
        rows = slice(c * ch, (c + 1) * ch)
        es.append(_bdot(dst, g3[0][rows]) + _bdot(dst, g3[1][rows]) + _bdot(dst, g3[2][rows]))

    def eq(c, li):
        return es[c][(2 * li) * ch : (2 * li + 1) * ch]

    def ek(c, li):
        return es[c][(2 * li + 1) * ch : (2 * li + 2) * ch]

    cs = jnp.concatenate([eq(c, 0) for c in range(ncs)], axis=0)
    tsub = lax.broadcasted_iota(jnp.int32, (rs, hk), 0) % GLA_SUB
    arep = jnp.zeros((rs, GLA_HEADS * ch), F32)
    for u in range(GLA_SUB):
        dec = jnp.exp(jnp.where(tsub >= u, cs - _bcast_sub(cs, u), -jnp.inf))
        p = q * _bcast_sub(k, u) * dec
        arep = arep + _bdot(p.astype(BF16), e_ref[u])
    arep = arep * jnp.concatenate([dmask_ref[...]] * ncs, axis=0)

    lane_k = lax.broadcasted_iota(jnp.int32, (ch, hk), 1) // GLA_DK
    ng = ng_ref[...]
    bd = bd_ref[...]
    for c in range(ncs):
        rows = slice(c * ch, (c + 1) * ch)
        qc, kc = q[rows], k[rows]
        vc = v_ref[rows, :]
        aoff = jnp.zeros((GLA_HEADS * ch, ch), F32)
        for li in range(nlev - 1):
            qe = qc * jnp.exp(eq(c, li))
            ke = (kc * jnp.exp(ek(c, li))).astype(BF16)
            qst = jnp.concatenate([jnp.where(lane_k == h, qe, 0.0) for h in range(GLA_HEADS)], axis=0).astype(BF16)
            aoff = aoff + _bdot_nt(qst, ke) * lmask_ref[li]
        cum = eq(c, nlev - 1)
        st = st_ref[...]
        o_inter = _bdot_nt((qc * jnp.exp(cum)).astype(BF16), st.astype(BF16))
        vb = vc.astype(BF16)
        outs = []
        for h in range(GLA_HEADS):
            a_h = aoff[h * ch : (h + 1) * ch] + arep[rows, h * ch : (h + 1) * ch]
            o_h = _bdot(a_h.astype(BF16), vb[:, h * GLA_DV : (h + 1) * GLA_DV]) + o_inter[:, h * GLA_DV : (h + 1) * GLA_DV]
            ms = jnp.mean(o_h * o_h, axis=-1, keepdims=True)
            outs.append(o_h * lax.rsqrt(ms + NORM_EPS))
        o = jnp.concatenate(outs, axis=1) * ng
        rc = r_ref[rows, :]
        o_ref[rows, :] = (o * (rc * jax.nn.sigmoid(rc))).astype(o_ref.dtype)
        ke_last = (kc * jnp.exp(ek(c, nlev - 1))).astype(BF16)
        kv = _bdot(vc.T.astype(BF16), ke_last)
        st_ref[...] = st * jnp.exp(cum[ch - 1 : ch, :]) + kv * bd


def _gla(proj, b, a2, ab, ng):
    t = proj.shape[0]
    l = t // b
    rs = GLA_CH * GLA_NCS
    assert l % rs == 0
    ns = l // rs
    dstack, lmask, dmask, e, bd = _gla_consts()
    a2p = jnp.zeros((LANES, GLA_HEADS * GLA_DK), F32).at[:GLA_LOWRANK].set(a2.astype(F32))

    def rowblk(w, c0):
        return pl.BlockSpec((rs, w), lambda i, s: (i * ns + s, c0 // w))

    def full(shape):
        nd = len(shape)
        return pl.BlockSpec(shape, lambda i, s: (0,) * nd)

    hk, hv = GLA_HEADS * GLA_DK, GLA_HEADS * GLA_DV
    return pl.pallas_call(
        _gla_kernel,
        grid=(b, ns),
        in_specs=[
            rowblk(hk, C_GQ),
            rowblk(hk, C_GK),
            rowblk(hv, C_GV),
            rowblk(hv, C_GR),
            rowblk(LANES, C_GLR),
            full((LANES, hk)),
            full((1, hk)),
            full((1, hv)),
            full(dstack.shape),
            full(lmask.shape),
            full(dmask.shape),
            full(e.shape),
            full(bd.shape),
        ],
        out_specs=pl.BlockSpec((rs, hv), lambda i, s: (i * ns + s, 0)),
        out_shape=jax.ShapeDtypeStruct((t, hv), BF16),
        scratch_shapes=[pltpu.VMEM((hv, hk), F32)],
        compiler_params=_cparams(("parallel", "arbitrary")),
        name="gla",
    )(
        proj, proj, proj, proj, proj, a2p, ab.reshape(1, hk).astype(F32), ng.reshape(1, hv).astype(F32),
        jnp.asarray(dstack, BF16), jnp.asarray(lmask), jnp.asarray(dmask), jnp.asarray(e, BF16), jnp.asarray(bd),
    )


def _s5_params(lam_re, lam_im, log_dt, b_re, b_im, c_re, c_im, d, nch):
    f = F32
    cs = S5_CH
    dt = jnp.exp(log_dt.astype(f))[:, None]
    lr, li = lam_re.astype(f), lam_im.astype(f)
    mag = jnp.exp(lr * dt)
    ab_re, ab_im = mag * jnp.cos(li * dt), mag * jnp.sin(li * dt)
    den = lr * lr + li * li
    z_re = ((ab_re - 1.0) * lr + ab_im * li) / den
    z_im = (ab_im * lr - (ab_re - 1.0) * li) / den
    br, bi = b_re.astype(f), b_im.astype(f)
    bb_re = z_re[..., None] * br - z_im[..., None] * bi
    bb_im = z_re[..., None] * bi + z_im[..., None] * br

    def apow(p):
        p = jnp.asarray(p, f)[:, None, None]
        m = jnp.exp(p * (lr * dt))
        return m * jnp.cos(p * (li * dt)), m * jnp.sin(p * (li * dt))

    cr, ci = c_re.astype(f), c_im.astype(f)
    p_re, p_im = apow(np.arange(cs + 1))
    nb, gpb = S5_GROUPS // S5_GPB, S5_GPB
    sw = gpb * 2 * S5_STATE
    eye = jnp.eye(gpb, dtype=f)

    def b_blockdiag(x):
        x = x.reshape(nb, gpb, S5_STATE, S5_GROUP).transpose(0, 1, 3, 2)
        x = jnp.concatenate([x, x], axis=-1)
        return (x[:, :, :, None, :] * eye[None, :, None, :, None]).reshape(nb, LANES, sw)

    def c_blockdiag(x):
        x = x.reshape(nb, gpb, S5_GROUP, S5_STATE).transpose(0, 1, 3, 2)
        x = jnp.concatenate([x, x], axis=2)
        return (x[:, :, :, None, :] * eye[None, :, None, :, None]).reshape(nb, sw, LANES)

    def lanes(a, b_):
        return jnp.stack([a, b_], axis=2).reshape(a.shape[0], nb, sw).transpose(1, 0, 2)

    rev = np.arange(cs - 1, -1, -1)
    pr, pi = p_re[rev], p_im[rev]
    qr, qi = p_re[1:], p_im[1:]
    pa, pb = lanes(pr, pi), lanes(-pi, pr)
    qa, qb = lanes(qr, -qi).transpose(0, 2, 1), lanes(-qi, -qr).transpose(0, 2, 1)
    nstep = max(1, int(math.log2(nch)))
    s_re, s_im = apow(cs * (2 ** np.arange(nstep)))
    ar, ai = lanes(s_re, s_re), lanes(-s_im, s_im)
    dd = d.astype(f).reshape(nb, 1, LANES)
    return b_blockdiag(bb_re), b_blockdiag(bb_im), c_blockdiag(cr), c_blockdiag(ci), pa, pb, qa, qb, ar, ai, dd


def _s5_kernel(u_ref, br_ref, bi_ref, cr_ref, ci_ref, pa_ref, pb_ref, qa_ref, qb_ref, ar_ref, ai_ref, d_ref,
               y_ref, mb_s, mc_s, kc_s, *, nchb):
    cs = S5_CH
    t = u_ref.shape[0]
    nch = t // cs
    w = 2 * S5_STATE
    br, bi = br_ref[0], bi_ref[0]
    crb, cib = cr_ref[0], ci_ref[0]
    is_re = (lax.broadcasted_iota(jnp.int32, crb.shape, 0) // S5_STATE) % 2 == 0
    c_real = jnp.where(is_re, crb, -cib)
    pa, pb, qa, qb = pa_ref[0], pb_ref[0], qa_ref[0], qb_ref[0]
    for j in range(cs):
        blk = slice(j * LANES, (j + 1) * LANES)
        mbj = br * pa[j : j + 1] + bi * pb[j : j + 1]
        mb_s[blk, :] = mbj.astype(BF16)
        kc_s[blk, :] = jnp.dot(mbj, c_real, precision=HIGHEST, preferred_element_type=F32).astype(BF16)
        mc_s[:, blk] = (crb * qa[:, j : j + 1] + cib * qb[:, j : j + 1]).astype(BF16)
    ucat = jnp.concatenate([u_ref[pl.ds(j, nch, stride=cs), :] for j in range(cs)], axis=1).astype(BF16)
    e_all = _bdot(ucat, mb_s[...])
    cidx = lax.broadcasted_iota(jnp.int32, (nch, w), 0) % nchb
    ar, ai = ar_ref[0], ai_ref[0]
    hp = []
    for g in range(S5_GPB):
        e = e_all[:, g * w : (g + 1) * w]
        for kstep in range(ar.shape[0]):
            s = 2**kstep
            if s >= nchb:
                break
            sh = jnp.where(cidx >= s, pltpu.roll(e, s, axis=0), 0.0)
            e = (e + sh * ar[kstep : kstep + 1, g * w : (g + 1) * w]
                 + pltpu.roll(sh, S5_STATE, axis=1) * ai[kstep : kstep + 1, g * w : (g + 1) * w])
        hp.append(jnp.where(cidx >= 1, pltpu.roll(e, 1, axis=0), 0.0).astype(BF16))
    ycar = _bdot(jnp.concatenate(hp, axis=1), mc_s[...])
    for tt in range(cs):
        y_ref[pl.ds(tt, nch, stride=cs), :] = ycar[:, tt * LANES : (tt + 1) * LANES]

    rt = min(S5_RT, t)
    rmod = lax.broadcasted_iota(jnp.int32, (rt, LANES), 0) % cs
    dvec = d_ref[0]

    def tile(i, carry):
        rows = pl.ds(pl.multiple_of(i * rt, rt), rt)
        ut = u_ref[rows, :]
        ush = [jnp.where(rmod >= tau, pltpu.roll(ut, tau, axis=0), 0.0) for tau in range(cs - 1, 0, -1)] + [ut]
        ush = jnp.concatenate(ush, axis=1).astype(BF16)
        y_ref[rows, :] = y_ref[rows, :] + _bdot(ush, kc_s[...]) + dvec * ut
        return carry

    lax.fori_loop(0, t // rt, tile, 0)


def _s5_core(proj, b, params):
    t = proj.shape[0]
    nchb = t // b // S5_CH
    sw = S5_GPB * 2 * S5_STATE

    def bspec(a):
        return pl.BlockSpec((1,) + a.shape[1:], lambda p: (p, 0, 0))

    return pl.pallas_call(
        functools.partial(_s5_kernel, nchb=nchb),
        grid=(S5_GROUPS // S5_GPB,),
        in_specs=[pl.BlockSpec((t, LANES), lambda p: (0, C_S5 // LANES + p))] + [bspec(a) for a in params],
        out_specs=pl.BlockSpec((t, LANES), lambda p: (0, p)),
        out_shape=jax.ShapeDtypeStruct((t, BRANCH_WIDTH), F32),
        scratch_shapes=[
            pltpu.VMEM((S5_CH * LANES, sw), BF16),
            pltpu.VMEM((sw, S5_CH * LANES), BF16),
            pltpu.VMEM((S5_CH * LANES, LANES), BF16),
        ],
        compiler_params=_cparams(("parallel",)),
        name="s5_core",
    )(proj, *params)


def _merge_kernel(
    x_ref, g1_ref, gla_ref, ys5_ref, dil_ref, swa_ref, gluw_ref, glub_ref, wg0, wg1, wg2, wg3, wb_ref, wo_ref,
    o_ref, h_s, s5_s, acc_s,
):
    j = pl.program_id(1)

    @pl.when(j == 0)
    def _():
        h_s[...] = _rms(x_ref[...], g1_ref[...]).astype(BF16)
        zz = jax.nn.gelu(ys5_ref[...])
        gate = jax.nn.sigmoid(_bdot(zz.astype(BF16), gluw_ref[...]) + glub_ref[...])
        s5_s[...] = (zz * gate).astype(BF16)
        acc_s[...] = jnp.zeros_like(acc_s)

    h = h_s[...]
    branches = (gla_ref[...], s5_s[...], dil_ref[...], swa_ref[...])
    mixed = None
    for m, (wg, br) in enumerate(zip((wg0, wg1, wg2, wg3), branches)):
        term = jax.nn.sigmoid(_bdot(h, wg[...])) * _bdot(br, wb_ref[m])
        mixed = term if mixed is None else mixed + term
    acc_s[...] += _bdot(mixed.astype(BF16), wo_ref[...])

    @pl.when(j == pl.num_programs(1) - 1)
    def _():
        o_ref[...] = x_ref[...] + acc_s[...]


def _merge(x, g1, o_gla, y_s5, o_dil, o_swa, gluw, glub, w_all, wb, wo, layer):
    t, d = x.shape
    tm, tn = min(TM_MERGE, t), TN_MERGE
    nj = d // tn
    bw = BRANCH_WIDTH

    def rowblk(w):
        return pl.BlockSpec((tm, w), lambda i, j: (i, 0))

    def gate_spec(m):
        return pl.BlockSpec((None, d, tn), lambda i, j, m=m: (layer, 0, N_SMALL // tn + m * nj + j))

    return pl.pallas_call(
        _merge_kernel,
        grid=(t // tm, nj),
        in_specs=[
            rowblk(d),
            pl.BlockSpec((1, d), lambda i, j: (0, 0)),
            rowblk(bw),
            rowblk(bw),
            rowblk(bw),
            rowblk(bw),
            pl.BlockSpec((None, bw, bw), lambda i, j: (layer, 0, 0)),
            pl.BlockSpec((1, bw), lambda i, j: (0, 0)),
            gate_spec(0),
            gate_spec(1),
            gate_spec(2),
            gate_spec(3),
            pl.BlockSpec((None, N_BRANCH, bw, tn), lambda i, j: (layer, 0, 0, j)),
            pl.BlockSpec((None, tn, d), lambda i, j: (layer, j, 0)),
        ],
        out_specs=rowblk(d),
        out_shape=jax.ShapeDtypeStruct((t, d), F32),
        scratch_shapes=[pltpu.VMEM((tm, d), BF16), pltpu.VMEM((tm, bw), BF16), pltpu.VMEM((tm, d), F32)],
        compiler_params=_cparams(("parallel", "arbitrary")),
        name="merge",
    )(x, g1, o_gla, y_s5, o_dil, o_swa, gluw, glub, w_all, w_all, w_all, w_all, wb, wo)


def _ffn_kernel(x_ref, g2_ref, wg_ref, wu_ref, wd_ref, gf_ref, o_ref, h_s, acc_s, *, final_norm):
    j = pl.program_id(1)

    @pl.when(j == 0)
    def _():
        h_s[...] = _rms(x_ref[...], g2_ref[...]).astype(BF16)
        acc_s[...] = jnp.zeros_like(acc_s)

    h = h_s[...]
    gate = _bdot(h, wg_ref[...])
    act = (gate * jax.nn.sigmoid(gate)) * _bdot(h, wu_ref[...])
    acc_s[...] += _bdot(act.astype(BF16), wd_ref[...])

    @pl.when(j == pl.num_programs(1) - 1)
    def _():
        y = x_ref[...] + acc_s[...]
        o_ref[...] = _rms(y, gf_ref[...]) if final_norm else y


def _ffn(x, g2, wg, wu, wd, gf, layer, final_norm):
    t, d = x.shape
    fh = wg.shape[-1]
    tm, tf = min(TM_FFN, t), TF_FFN
    return pl.pallas_call(
        functools.partial(_ffn_kernel, final_norm=final_norm),
        grid=(t // tm, fh // tf),
        in_specs=[
            pl.BlockSpec((tm, d), lambda i, j: (i, 0)),
            pl.BlockSpec((1, d), lambda i, j: (0, 0)),
            pl.BlockSpec((None, d, tf), lambda i, j: (layer, 0, j)),
            pl.BlockSpec((None, d, tf), lambda i, j: (layer, 0, j)),
            pl.BlockSpec((None, tf, d), lambda i, j: (layer, j, 0)),
            pl.BlockSpec((1, d), lambda i, j: (0, 0)),
        ],
        out_specs=pl.BlockSpec((tm, d), lambda i, j: (i, 0)),
        out_shape=jax.ShapeDtypeStruct((t, d), F32),
        scratch_shapes=[pltpu.VMEM((tm, d), BF16), pltpu.VMEM((tm, d), F32)],
        compiler_params=_cparams(("parallel", "arbitrary")),
        name="ffn",
    )(x, g2, wg, wu, wd, gf)


def _pack_kernel(w_ref, o_ref):
    d_in = w_ref.shape[1]
    sh = GLA_LOWRANK

    def shifted(src0, dst0, width):
        assert src0 % LANES == sh and dst0 % LANES == 0
        for c in range(0, width, PACK_CW):
            cw = min(PACK_CW, width - c)
            a0 = src0 + c - sh
            blk = w_ref[:, a0 : min(a0 + cw + LANES, d_in)]
            o_ref[:, dst0 + c : dst0 + c + cw] = blk[:, sh : sh + cw].astype(BF16)

    o_ref[:, :GLR_ORIG] = w_ref[:, :GLR_ORIG].astype(BF16)
    shifted(GLR_ORIG + GLA_LOWRANK, GLR_ORIG, N_ORIG_SMALL - GLR_ORIG - GLA_LOWRANK)
    head = w_ref[:, GLR_ORIG : GLR_ORIG + LANES]
    lane = lax.broadcasted_iota(jnp.int32, head.shape, 1)
    o_ref[:, C_GLR : C_GLR + LANES] = jnp.where(lane < GLA_LOWRANK, head, 0.0).astype(BF16)
    o_ref[:, C_GLR + LANES : N_SMALL] = jnp.zeros((o_ref.shape[0], N_SMALL - C_GLR - LANES), BF16)
    shifted(N_ORIG_SMALL, N_SMALL, d_in - N_ORIG_SMALL)


def _pack_w_in(w_in):
    depth, d, d_in = w_in.shape
    n_out = N_SMALL + d_in - N_ORIG_SMALL
    rt = PACK_RT
    return pl.pallas_call(
        _pack_kernel,
        grid=(depth, d // rt),
        in_specs=[pl.BlockSpec((None, rt, d_in), lambda i, r: (i, r, 0))],
        out_specs=pl.BlockSpec((None, rt, n_out), lambda i, r: (i, r, 0)),
        out_shape=jax.ShapeDtypeStruct((depth, d, n_out), BF16),
        compiler_params=_cparams(("parallel", "parallel")),
        name="pack_w_in",
    )(w_in)


def kernel(x, positions, norm1_g, w_in, gla_a2, gla_a_b, gla_norm_g, s5_lambda_re, s5_lambda_im, s5_log_dt, s5_b_re, s5_b_im, s5_c_re, s5_c_im, s5_d, s5_glu_w, s5_glu_b, swa_sinks, w_branch, w_out, norm2_g, w_ffn_gate, w_ffn_up, w_ffn_down, final_norm_g):
    b, l, d = x.shape
    t = b * l
    depth = w_in.shape[0]
    xs = x.reshape(t, d).astype(F32)
    w_all = _pack_w_in(w_in)
    w_br = w_branch.astype(BF16)
    w_o = w_out.astype(BF16)
    w_fg, w_fu, w_fd = w_ffn_gate.astype(BF16), w_ffn_up.astype(BF16), w_ffn_down.astype(BF16)
    glu_w = s5_glu_w.astype(BF16)
    tabs = _rope_tables(positions)
    gf = final_norm_g.reshape(1, d).astype(F32)
    for i in range(depth):
        g1 = norm1_g[i].reshape(1, d).astype(F32)
        proj = _inproj(xs, g1, w_all, i)
        proj3 = proj.reshape(b, l, N_SMALL)
        o_gla = _gla(proj, b, gla_a2[i], gla_a_b[i], gla_norm_g[i])
        s5p = _s5_params(s5_lambda_re[i], s5_lambda_im[i], s5_log_dt[i], s5_b_re[i], s5_b_im[i], s5_c_re[i],
                         s5_c_im[i], s5_d[i], l // S5_CH)
        y_s5 = _s5_core(proj, b, s5p)
        o_dil = _dilated(proj3, tabs).reshape(t, BRANCH_WIDTH)
        o_swa = _swa(proj3, tabs, swa_sinks[i]).reshape(t, BRANCH_WIDTH)
        xs = _merge(xs, g1, o_gla, y_s5, o_dil, o_swa, glu_w, s5_glu_b[i].reshape(1, -1).astype(F32),
                    w_all, w_br, w_o, i)
        xs = _ffn(xs, norm2_g[i].reshape(1, d).astype(F32), w_fg, w_fu, w_fd, gf, i, i == depth - 1)
    return xs.reshape(b, l, d).astype(x.dtype)
```

```python
import functools
import math

import jax
import jax.numpy as jnp
import numpy as np
from jax import lax
from jax.experimental import pallas as pl
from jax.experimental.pallas import tpu as pltpu

F32 = jnp.float32
BF16 = jnp.bfloat16
HIGHEST = lax.Precision.HIGHEST

D_MODEL = 2048
DEPTH = 2
N_BRANCH = 4
BRANCH_WIDTH = 512
HEAD_DIM = 64
ATT_BLOCK = 128
ROPE_THETA = 500000.0
ROPE_DIM = HEAD_DIM // 4
NORM_EPS = 1e-6
GLA_HEADS = 4
GLA_DK = 64
GLA_DV = BRANCH_WIDTH // GLA_HEADS
GLA_LOWRANK = 16
GLA_TAU = 16.0
S5_GROUP = 16
S5_GROUPS = BRANCH_WIDTH // S5_GROUP
S5_STATE = 64
DIL_CONFIGS = ((128, 1), (512, 4), (2048, 16))
DIL_SPAN = ATT_BLOCK * 16
SWA_HEADS = BRANCH_WIDTH // HEAD_DIM
SWA_KV_HEADS = 2
SWA_WINDOW = 128
FFN_HIDDEN = -((-8 * D_MODEL) // (3 * 256)) * 256

LANES = 128
SUBLANES = 8
VMEM_LIMIT = 56 * 1024 * 1024

C_GQ, C_GK, C_GV, C_GR, C_S5 = 0, 256, 512, 1024, 1536
C_CQ, C_CK, C_CV, C_SQ, C_SK, C_SV, C_GLR = 2048, 2560, 3072, 3584, 4096, 4224, 4352
N_SMALL = 4608
N_ORIG_SMALL = 4368
GLR_ORIG = 1536

TM_PROJ, TN_PROJ = 1024, 768
TM_MERGE, TN_MERGE = 512, 256
TM_FFN, TF_FFN = 512, 512
GLA_CH = 128
GLA_NCS = 4
GLA_SUB = 8
GLA_LEVELS = (8, 16, 32, 64, 128)
PACK_CB = 256
S5_CH = 16
S5_GPB = LANES // S5_GROUP
S5_RT = 512
S5_TSPLIT = 4
ATT_UNROLL = 8


def _cparams(sem):
    return pltpu.CompilerParams(dimension_semantics=sem, vmem_limit_bytes=VMEM_LIMIT)


def _rms(x, g):
    ms = jnp.mean(x * x, axis=-1, keepdims=True)
    return x * lax.rsqrt(ms + NORM_EPS) * g


def _bdot(a, b):
    return jnp.dot(a, b, preferred_element_type=F32)


def _bdot_nt(a, b):
    return lax.dot_general(a, b, (((1,), (1,)), ((), ())), preferred_element_type=F32)


def _inproj_kernel(x_ref, g_ref, w_ref, o_ref, h_ref):
    @pl.when(pl.program_id(1) == 0)
    def _():
        h_ref[...] = _rms(x_ref[...], g_ref[...]).astype(BF16)

    o_ref[...] = _bdot(h_ref[...], w_ref[...])


def _inproj(x, g, w_all, layer):
    t, d = x.shape
    n = N_SMALL
    tm, tn = min(TM_PROJ, t), TN_PROJ
    return pl.pallas_call(
        _inproj_kernel,
        grid=(t // tm, n // tn),
        in_specs=[
            pl.BlockSpec((tm, d), lambda i, j: (i, 0)),
            pl.BlockSpec((1, d), lambda i, j: (0, 0)),
            pl.BlockSpec((None, d, tn), lambda i, j: (layer, 0, j)),
        ],
        out_specs=pl.BlockSpec((tm, tn), lambda i, j: (i, j)),
        out_shape=jax.ShapeDtypeStruct((t, n), F32),
        scratch_shapes=[pltpu.VMEM((tm, d), BF16)],
        compiler_params=_cparams(("parallel", "arbitrary")),
        name="inproj",
    )(x, g, w_all)


def _rope_kernel(pos_ref, cos_ref, sa_ref, sb_ref):
    pos = pos_ref[0].astype(F32)
    lane = lax.broadcasted_iota(jnp.int32, (SUBLANES, LANES), 1)
    d = lane % HEAD_DIM
    half = ROPE_DIM // 2
    fi = (d % half).astype(F32) / half
    inv = jnp.power(jnp.full((SUBLANES, LANES), ROPE_THETA, F32), -fi)[0:1]
    d1 = d[0:1]
    ang = pos * inv
    c, s = jnp.cos(ang), jnp.sin(ang)
    cos_ref[0] = jnp.where(d1 < ROPE_DIM, c, 1.0)
    sa_ref[0] = jnp.where(d1 < half, -s, 0.0)
    sb_ref[0] = jnp.where((d1 >= half) & (d1 < ROPE_DIM), s, 0.0)


def _rope_tables(positions):
    b, l = positions.shape
    spec = pl.BlockSpec((1, l, LANES), lambda i: (i, 0, 0))
    shp = jax.ShapeDtypeStruct((b, l, LANES), F32)
    return pl.pallas_call(
        _rope_kernel,
        grid=(b,),
        in_specs=[pl.BlockSpec((1, l, 1), lambda i: (i, 0, 0))],
        out_specs=[spec, spec, spec],
        out_shape=[shp, shp, shp],
        compiler_params=_cparams(("parallel",)),
        name="rope_tables",
    )(positions.reshape(b, l, 1))


def _rope_apply(x, c, sa, sb):
    half = ROPE_DIM // 2
    return x * c + pltpu.roll(x, LANES - half, axis=1) * sa + pltpu.roll(x, half, axis=1) * sb


def _band_block(q, kp, ko, vp, vo, has_prev, strict_prev):
    t = ATT_BLOCK
    lane = lax.broadcasted_iota(jnp.int32, (t, LANES), 1)
    in_a = lane < HEAD_DIM
    q2 = jnp.concatenate([jnp.where(in_a, q, 0.0), jnp.where(in_a, 0.0, q)], axis=0).astype(BF16)
    s = _bdot_nt(q2, jnp.concatenate([kp, ko], axis=0).astype(BF16))
    qi = lax.broadcasted_iota(jnp.int32, (2 * t, 2 * t), 0) % t
    kj = lax.broadcasted_iota(jnp.int32, (2 * t, 2 * t), 1)
    lo = qi + ((1 if strict_prev else 0) + jnp.where(has_prev, 0, t))
    ok = ((kj < t) & (kj >= lo)) | ((kj >= t) & (kj - t <= qi))
    s = jnp.where(ok, s, -jnp.inf)
    m = jnp.max(s, axis=-1, keepdims=True)
    p = jnp.exp(s - m).astype(BF16)
    ones = jnp.ones((2 * t, LANES), BF16)
    vcat = jnp.concatenate([jnp.concatenate([vp, vo], axis=0).astype(BF16), ones], axis=1)
    od = _bdot(p, vcat)
    o = jnp.where(in_a, od[:t, :LANES], od[t:, :LANES])
    df = jnp.where(in_a, od[:t, LANES:], od[t:, LANES:])
    mf = jnp.where(in_a, m[:t], m[t:])
    return mf, df, o


def _dil_kernel(q_ref, k_ref, v_ref, c_ref, sa_ref, sb_ref, o_ref, qs, ks, ms, ds, os_):
    l = q_ref.shape[1]
    c, sa, sb = c_ref[0], sa_ref[0], sb_ref[0]
    qs[...] = _rope_apply(q_ref[0], c, sa, sb) * (HEAD_DIM**-0.5)
    ks[...] = _rope_apply(k_ref[0], c, sa, sb)
    t = ATT_BLOCK
    nblk = DIL_SPAN // t

    for sblk in range(l // DIL_SPAN):
        base = sblk * DIL_SPAN
        for ci, (_, dil) in enumerate(DIL_CONFIGS):

            def body(idx, carry, dil=dil, ci=ci, base=base):
                r = idx % dil
                n = idx // dil
                loc = r + dil * t * n
                start = base + loc
                has_prev = start >= dil * t
                pstart = jnp.where(has_prev, start - dil * t, start)
                if dil == 1:
                    start = pl.multiple_of(start, t)
                    pstart = pl.multiple_of(pstart, t)
                    loc = pl.multiple_of(loc, t)
                    rows, prow, lrow = pl.ds(start, t), pl.ds(pstart, t), pl.ds(loc, t)
                else:
                    rows = pl.ds(start, t, stride=dil)
                    prow = pl.ds(pstart, t, stride=dil)
                    lrow = pl.ds(loc, t, stride=dil)
                mf, df, o = _band_block(
                    qs[rows, :], ks[prow, :], ks[rows, :], v_ref[0, prow, :], v_ref[0, rows, :], has_prev, False
                )
                ms[ci, lrow, :] = mf
                ds[ci, lrow, :] = df
                os_[ci, lrow, :] = o
                return carry

            lax.fori_loop(0, nblk, body, 0, unroll=ATT_UNROLL)

        m0, m1, m2 = ms[0], ms[1], ms[2]
        mx = jnp.maximum(jnp.maximum(m0, m1), m2)
        w0, w1, w2 = jnp.exp(m0 - mx), jnp.exp(m1 - mx), jnp.exp(m2 - mx)
        den = w0 * ds[0] + w1 * ds[1] + w2 * ds[2]
        num = w0 * os_[0] + w1 * os_[1] + w2 * os_[2]
        o_ref[0, base : base + DIL_SPAN, :] = (num / den).astype(o_ref.dtype)


def _dilated(proj3, tabs):
    b, l, _ = proj3.shape
    assert l % DIL_SPAN == 0
    cos, sa, sb = tabs

    def col(c0):
        return pl.BlockSpec((1, l, LANES), lambda i, p: (i, 0, c0 // LANES + p))

    tab = pl.BlockSpec((1, l, LANES), lambda i, p: (i, 0, 0))
    return pl.pallas_call(
        _dil_kernel,
        grid=(b, BRANCH_WIDTH // LANES),
        in_specs=[col(C_CQ), col(C_CK), col(C_CV), tab, tab, tab],
        out_specs=pl.BlockSpec((1, l, LANES), lambda i, p: (i, 0, p)),
        out_shape=jax.ShapeDtypeStruct((b, l, BRANCH_WIDTH), BF16),
        scratch_shapes=[
            pltpu.VMEM((l, LANES), F32),
            pltpu.VMEM((l, LANES), F32),
            pltpu.VMEM((3, DIL_SPAN, LANES), F32),
            pltpu.VMEM((3, DIL_SPAN, LANES), F32),
            pltpu.VMEM((3, DIL_SPAN, LANES), F32),
        ],
        compiler_params=_cparams(("parallel", "parallel")),
        name="dilated_attn",
    )(proj3, proj3, proj3, cos, sa, sb)


def _swa_kernel(q_ref, k_ref, v_ref, c_ref, sa_ref, sb_ref, sink_ref, o_ref, qs, ks, vs):
    l = q_ref.shape[1]
    c, sa, sb = c_ref[0], sa_ref[0], sb_ref[0]
    qs[...] = _rope_apply(q_ref[0], c, sa, sb) * (HEAD_DIM**-0.5)
    g = pl.program_id(1) // (SWA_HEADS // SWA_KV_HEADS // 2)
    lane = lax.broadcasted_iota(jnp.int32, (l, LANES), 1)
    keep = (lane // HEAD_DIM) == g
    kr = _rope_apply(k_ref[0], c, sa, sb)
    ks[...] = jnp.where(keep, kr, pltpu.roll(kr, HEAD_DIM, axis=1))
    v = v_ref[0]
    vs[...] = jnp.where(keep, v, pltpu.roll(v, HEAD_DIM, axis=1))
    sink = sink_ref[0]
    t = ATT_BLOCK

    def body(n, carry):
        start = pl.multiple_of(n * t, t)
        has_prev = n > 0
        pstart = pl.multiple_of(jnp.where(has_prev, start - t, start), t)
        rows, prow = pl.ds(start, t), pl.ds(pstart, t)
        mf, df, o = _band_block(qs[rows, :], ks[prow, :], ks[rows, :], vs[prow, :], vs[rows, :], has_prev, True)
        lse = mf + jnp.log(df)
        o_ref[0, rows, :] = (o / df * jax.nn.sigmoid(lse - sink)).astype(o_ref.dtype)
        return carry

    lax.fori_loop(0, l // t, body, 0, unroll=ATT_UNROLL)


def _swa(proj3, tabs, sinks):
    b, l, _ = proj3.shape
    cos, sa, sb = tabs
    npair = BRANCH_WIDTH // LANES
    sink_l = jnp.repeat(sinks.astype(F32), HEAD_DIM).reshape(npair, 1, LANES)
    tab = pl.BlockSpec((1, l, LANES), lambda i, p: (i, 0, 0))
    return pl.pallas_call(
        _swa_kernel,
        grid=(b, npair),
        in_specs=[
            pl.BlockSpec((1, l, LANES), lambda i, p: (i, 0, C_SQ // LANES + p)),
            pl.BlockSpec((1, l, LANES), lambda i, p: (i, 0, C_SK // LANES)),
            pl.BlockSpec((1, l, LANES), lambda i, p: (i, 0, C_SV // LANES)),
            tab,
            tab,
            tab,
            pl.BlockSpec((1, 1, LANES), lambda i, p: (p, 0, 0)),
        ],
        out_specs=pl.BlockSpec((1, l, LANES), lambda i, p: (i, 0, p)),
        out_shape=jax.ShapeDtypeStruct((b, l, BRANCH_WIDTH), BF16),
        scratch_shapes=[pltpu.VMEM((l, LANES), F32)] * 3,
        compiler_params=_cparams(("parallel", "parallel")),
        name="swa_attn",
    )(proj3, proj3, proj3, cos, sa, sb, sink_l)


@functools.lru_cache(maxsize=None)
def _gla_consts():
    ch = GLA_CH
    r = np.arange(ch)[:, None]
    j = np.arange(ch)[None, :]
    dstack = (j <= r).astype(np.float32)
    lm = []
    for m in GLA_LEVELS[:-1]:
        mk = (((r // m) % 2 == 1) & ((j // m) == (r // m) - 1)).astype(np.float32)
        lm.append(np.tile(mk, (GLA_HEADS, 1)))
    lmask = np.stack(lm)
    dmask = np.tile(((r // GLA_SUB) == (j // GLA_SUB)).astype(np.float32), (1, GLA_HEADS))
    nsub = ch // GLA_SUB
    e = np.zeros((GLA_SUB, GLA_HEADS * GLA_DK, GLA_HEADS * ch), np.float32)
    for u in range(GLA_SUB):
        for h in range(GLA_HEADS):
            for s in range(nsub):
                e[u, h * GLA_DK : (h + 1) * GLA_DK, h * ch + GLA_SUB * s + u] = 1.0
    rr = np.arange(GLA_HEADS * GLA_DV)[:, None]
    cc = np.arange(GLA_HEADS * GLA_DK)[None, :]
    bd = ((rr // GLA_DV) == (cc // GLA_DK)).astype(np.float32)
    return dstack, lmask, dmask, e, bd


def _split3(x):
    hi = x.astype(BF16)
    r1 = x - hi.astype(F32)
    mid = r1.astype(BF16)
    lo = (r1 - mid.astype(F32)).astype(BF16)
    return hi, mid, lo


def _bcast_grp(x, m, u):
    r, w = x.shape
    x3 = x.reshape(r // m, m, w)
    return jnp.broadcast_to(x3[:, u : u + 1, :], x3.shape).reshape(r, w)


def _bcast_sub(x, u):
    return _bcast_grp(x, GLA_SUB, u)


def _gla_kernel(
    q_ref, k_ref, v_ref, r_ref, glr_ref, a2_ref, ab_ref, ng_ref, dst_ref, lmask_ref, dmask_ref, e_ref, bd_ref,
    o_ref, st_ref,
):
    ch = GLA_CH
    rs = q_ref.shape[0]
    ncs = rs // ch
    nlev = len(GLA_LEVELS)
    hk = GLA_HEADS * GLA_DK

    @pl.when(pl.program_id(1) == 0)
    def _():
        st_ref[...] = jnp.zeros_like(st_ref)

    q = q_ref[...] * (GLA_DK**-0.5)
    k = k_ref[...]
    z = jnp.dot(glr_ref[...], a2_ref[...], precision=HIGHEST, preferred_element_type=F32) + ab_ref[...]
    g = (jnp.minimum(z, 0.0) - jnp.log1p(jnp.exp(-jnp.abs(z)))) * (1.0 / GLA_TAU)
    g3 = _split3(g)
    dst = dst_ref[...]

    cums, excl = [], []
    for c in range(ncs):
        rows = slice(c * ch, (c + 1) * ch)
        cums.append(_bdot(dst, g3[0][rows]) + _bdot(dst, g3[1][rows]) + _bdot(dst, g3[2][rows]))
        excl.append(cums[c] - g[rows])

    def eq(c, li):
        m = GLA_LEVELS[li]
        return cums[c] - _bcast_grp(excl[c], m, 0)

    def ek(c, li):
        m = GLA_LEVELS[li]
        return _bcast_grp(cums[c], m, m - 1) - cums[c]

    cs = jnp.concatenate([eq(c, 0) for c in range(ncs)], axis=0)
    tsub = lax.broadcasted_iota(jnp.int32, (rs, hk), 0) % GLA_SUB
    arep = jnp.zeros((rs, GLA_HEADS * ch), F32)
    for u in range(GLA_SUB):
        dec = jnp.exp(jnp.where(tsub >= u, cs - _bcast_sub(cs, u), -jnp.inf))
        p = q * _bcast_sub(k, u) * dec
        arep = arep + _bdot(p.astype(BF16), e_ref[u])
    arep = arep * jnp.concatenate([dmask_ref[...]] * ncs, axis=0)

    lane_k = lax.broadcasted_iota(jnp.int32, (ch, hk), 1) // GLA_DK
    ng = ng_ref[...]
    bd = bd_ref[...]
    for c in range(ncs):
        rows = slice(c * ch, (c + 1) * ch)
        qc, kc = q[rows], k[rows]
        vc = v_ref[rows, :]
        aoff = jnp.zeros((GLA_HEADS * ch, ch), F32)
        for li in range(nlev - 1):
            qe = qc * jnp.exp(eq(c, li))
            ke = (kc * jnp.exp(ek(c, li))).astype(BF16)
            qst = jnp.concatenate([jnp.where(lane_k == h, qe, 0.0) for h in range(GLA_HEADS)], axis=0).astype(BF16)
            aoff = aoff + _bdot_nt(qst, ke) * lmask_ref[li]
        cum = eq(c, nlev - 1)
        st = st_ref[...]
        o_inter = _bdot_nt((qc * jnp.exp(cum)).astype(BF16), st.astype(BF16))
        vb = vc.astype(BF16)
        outs = []
        for h in range(GLA_HEADS):
            a_h = aoff[h * ch : (h + 1) * ch] + arep[rows, h * ch : (h + 1) * ch]
            o_h = _bdot(a_h.astype(BF16), vb[:, h * GLA_DV : (h + 1) * GLA_DV]) + o_inter[:, h * GLA_DV : (h + 1) * GLA_DV]
            ms = jnp.mean(o_h * o_h, axis=-1, keepdims=True)
            outs.append(o_h * lax.rsqrt(ms + NORM_EPS))
        o = jnp.concatenate(outs, axis=1) * ng
        rc = r_ref[rows, :]
        o_ref[rows, :] = (o * (rc * jax.nn.sigmoid(rc))).astype(o_ref.dtype)
        ke_last = (kc * jnp.exp(ek(c, nlev - 1))).astype(BF16)
        kv = _bdot(vc.T.astype(BF16), ke_last)
        st_ref[...] = st * jnp.exp(cum[ch - 1 : ch, :]) + kv * bd


def _gla(proj, b, a2, ab, ng):
    t = proj.shape[0]
    l = t // b
    rs = GLA_CH * GLA_NCS
    assert l % rs == 0
    ns = l // rs
    dstack, lmask, dmask, e, bd = _gla_consts()
    a2p = jnp.zeros((LANES, GLA_HEADS * GLA_DK), F32).at[:GLA_LOWRANK].set(a2.astype(F32))

    def rowblk(w, c0):
        return pl.BlockSpec((rs, w), lambda i, s: (i * ns + s, c0 // w))

    def full(shape):
        nd = len(shape)
        return pl.BlockSpec(shape, lambda i, s: (0,) * nd)

    hk, hv = GLA_HEADS * GLA_DK, GLA_HEADS * GLA_DV
    return pl.pallas_call(
        _gla_kernel,
        grid=(b, ns),
        in_specs=[
            rowblk(hk, C_GQ),
            rowblk(hk, C_GK),
            rowblk(hv, C_GV),
            rowblk(hv, C_GR),
            rowblk(LANES, C_GLR),
            full((LANES, hk)),
            full((1, hk)),
            full((1, hv)),
            full(dstack.shape),
            full(lmask.shape),
            full(dmask.shape),
            full(e.shape),
            full(bd.shape),
        ],
        out_specs=pl.BlockSpec((rs, hv), lambda i, s: (i * ns + s, 0)),
        out_shape=jax.ShapeDtypeStruct((t, hv), BF16),
        scratch_shapes=[pltpu.VMEM((hv, hk), F32)],
        compiler_params=_cparams(("parallel", "arbitrary")),
        name="gla",
    )(
        proj, proj, proj, proj, proj, a2p, ab.reshape(1, hk).astype(F32), ng.reshape(1, hv).astype(F32),
        jnp.asarray(dstack, BF16), jnp.asarray(lmask), jnp.asarray(dmask), jnp.asarray(e, BF16), jnp.asarray(bd),
    )


def _s5_params(lam_re, lam_im, log_dt, b_re, b_im, c_re, c_im, d, nch):
    f = F32
    cs = S5_CH
    dt = jnp.exp(log_dt.astype(f))[:, None]
    lr, li = lam_re.astype(f), lam_im.astype(f)
    mag = jnp.exp(lr * dt)
    ab_re, ab_im = mag * jnp.cos(li * dt), mag * jnp.sin(li * dt)
    den = lr * lr + li * li
    z_re = ((ab_re - 1.0) * lr + ab_im * li) / den
    z_im = (ab_im * lr - (ab_re - 1.0) * li) / den
    br, bi = b_re.astype(f), b_im.astype(f)
    bb_re = z_re[..., None] * br - z_im[..., None] * bi
    bb_im = z_re[..., None] * bi + z_im[..., None] * br

    def apow(p):
        p = jnp.asarray(p, f)[:, None, None]
        m = jnp.exp(p * (lr * dt))
        return m * jnp.cos(p * (li * dt)), m * jnp.sin(p * (li * dt))

    cr, ci = c_re.astype(f), c_im.astype(f)
    p_re, p_im = apow(np.arange(cs + 1))
    nb, gpb = S5_GROUPS // S5_GPB, S5_GPB
    sw = gpb * 2 * S5_STATE
    eye = jnp.eye(gpb, dtype=f)

    def b_blockdiag(x):
        x = x.reshape(nb, gpb, S5_STATE, S5_GROUP).transpose(0, 1, 3, 2)
        x = jnp.concatenate([x, x], axis=-1)
        return (x[:, :, :, None, :] * eye[None, :, None, :, None]).reshape(nb, LANES, sw)

    def c_blockdiag(x):
        x = x.reshape(nb, gpb, S5_GROUP, S5_STATE).transpose(0, 1, 3, 2)
        x = jnp.concatenate([x, x], axis=2)
        return (x[:, :, :, None, :] * eye[None, :, None, :, None]).reshape(nb, sw, LANES)

    def lanes(a, b_):
        return jnp.stack([a, b_], axis=2).reshape(a.shape[0], nb, sw).transpose(1, 0, 2)

    rev = np.arange(cs - 1, -1, -1)
    pr, pi = p_re[rev], p_im[rev]
    qr, qi = p_re[1:], p_im[1:]
    pa, pb = lanes(pr, pi), lanes(-pi, pr)
    qa, qb = lanes(qr, -qi).transpose(0, 2, 1), lanes(-qi, -qr).transpose(0, 2, 1)
    nstep = max(1, int(math.log2(nch)))
    s_re, s_im = apow(cs * (2 ** np.arange(nstep)))
    ar, ai = lanes(s_re, s_re), lanes(-s_im, s_im)
    dd = d.astype(f).reshape(nb, 1, LANES)
    return b_blockdiag(bb_re), b_blockdiag(bb_im), c_blockdiag(cr), c_blockdiag(ci), pa, pb, qa, qb, ar, ai, dd


def _s5_kernel(u_ref, br_ref, bi_ref, cr_ref, ci_ref, pa_ref, pb_ref, qa_ref, qb_ref, ar_ref, ai_ref, d_ref,
               y_ref, mb_s, mc_s, kc_s, *, nchb):
    cs = S5_CH
    t = u_ref.shape[0]
    nch = t // cs
    w = 2 * S5_STATE
    br, bi = br_ref[0], bi_ref[0]
    crb, cib = cr_ref[0], ci_ref[0]
    is_re = (lax.broadcasted_iota(jnp.int32, crb.shape, 0) // S5_STATE) % 2 == 0
    c_real = jnp.where(is_re, crb, -cib)
    pa, pb, qa, qb = pa_ref[0], pb_ref[0], qa_ref[0], qb_ref[0]
    for j in range(cs):
        blk = slice(j * LANES, (j + 1) * LANES)
        mb_s[blk, :] = (br * pa[j : j + 1] + bi * pb[j : j + 1]).astype(BF16)
        mc_s[:, blk] = (crb * qa[:, j : j + 1] + cib * qb[:, j : j + 1]).astype(BF16)
    kc = _bdot(mb_s[...], c_real.astype(BF16)).astype(BF16)
    gl = cs // S5_TSPLIT * LANES
    for qq in range(S5_TSPLIT):
        kc_s[:, qq * LANES : (qq + 1) * LANES] = kc[(S5_TSPLIT - 1 - qq) * gl : (S5_TSPLIT - qq) * gl]
    ucat = jnp.concatenate([u_ref[pl.ds(j, nch, stride=cs), :] for j in range(cs)], axis=1).astype(BF16)
    e_all = _bdot(ucat, mb_s[...])
    cidx = lax.broadcasted_iota(jnp.int32, (nch, w), 0) % nchb
    ar, ai = ar_ref[0], ai_ref[0]
    hp = []
    for g in range(S5_GPB):
        e = e_all[:, g * w : (g + 1) * w]
        for kstep in range(ar.shape[0]):
            s = 2**kstep
            if s >= nchb:
                break
            sh = jnp.where(cidx >= s, pltpu.roll(e, s, axis=0), 0.0)
            e = (e + sh * ar[kstep : kstep + 1, g * w : (g + 1) * w]
                 + pltpu.roll(sh, S5_STATE, axis=1) * ai[kstep : kstep + 1, g * w : (g + 1) * w])
        hp.append(jnp.where(cidx >= 1, pltpu.roll(e, 1, axis=0), 0.0).astype(BF16))
    ycar = _bdot(jnp.concatenate(hp, axis=1), mc_s[...])
    for tt in range(cs):
        y_ref[pl.ds(tt, nch, stride=cs), :] = ycar[:, tt * LANES : (tt + 1) * LANES]

    rt = min(S5_RT, t)
    rmod = lax.broadcasted_iota(jnp.int32, (rt, LANES), 0) % cs
    dvec = d_ref[0]
    glag = cs // S5_TSPLIT

    def tile(i, carry):
        rows = pl.ds(pl.multiple_of(i * rt, rt), rt)
        ut = u_ref[rows, :]
        ush = [jnp.where(rmod >= tau, pltpu.roll(ut, tau, axis=0), 0.0) for tau in range(glag - 1, 0, -1)] + [ut]
        zz = _bdot(jnp.concatenate(ush, axis=1).astype(BF16), kc_s[...])
        acc = y_ref[rows, :] + dvec * ut + zz[:, :LANES]
        for qq in range(1, S5_TSPLIT):
            part = pltpu.roll(zz[:, qq * LANES : (qq + 1) * LANES], qq * glag, axis=0)
            acc = acc + jnp.where(rmod >= qq * glag, part, 0.0)
        y_ref[rows, :] = acc
        return carry

    lax.fori_loop(0, t // rt, tile, 0)


def _s5_core(proj, b, params):
    t = proj.shape[0]
    nchb = t // b // S5_CH
    sw = S5_GPB * 2 * S5_STATE

    def bspec(a):
        return pl.BlockSpec((1,) + a.shape[1:], lambda p: (p, 0, 0))

    return pl.pallas_call(
        functools.partial(_s5_kernel, nchb=nchb),
        grid=(S5_GROUPS // S5_GPB,),
        in_specs=[pl.BlockSpec((t, LANES), lambda p: (0, C_S5 // LANES + p))] + [bspec(a) for a in params],
        out_specs=pl.BlockSpec((t, LANES), lambda p: (0, p)),
        out_shape=jax.ShapeDtypeStruct((t, BRANCH_WIDTH), F32),
        scratch_shapes=[
            pltpu.VMEM((S5_CH * LANES, sw), BF16),
            pltpu.VMEM((sw, S5_CH * LANES), BF16),
            pltpu.VMEM((S5_CH // S5_TSPLIT * LANES, S5_TSPLIT * LANES), BF16),
        ],
        compiler_params=_cparams(("parallel",)),
        name="s5_core",
    )(proj, *params)


def _merge_kernel(
    x_ref, g1_ref, gla_ref, ys5_ref, dil_ref, swa_ref, gluw_ref, glub_ref, wg0, wg1, wg2, wg3, wb_ref, wo_ref,
    o_ref, h_s, s5_s, acc_s,
):
    j = pl.program_id(1)

    @pl.when(j == 0)
    def _():
        h_s[...] = _rms(x_ref[...], g1_ref[...]).astype(BF16)
        zz = jax.nn.gelu(ys5_ref[...])
        gate = jax.nn.sigmoid(_bdot(zz.astype(BF16), gluw_ref[...]) + glub_ref[...])
        s5_s[...] = (zz * gate).astype(BF16)
        acc_s[...] = jnp.zeros_like(acc_s)

    h = h_s[...]
    branches = (gla_ref[...], s5_s[...], dil_ref[...], swa_ref[...])
    mixed = None
    for m, (wg, br) in enumerate(zip((wg0, wg1, wg2, wg3), branches)):
        term = jax.nn.sigmoid(_bdot(h, wg[...])) * _bdot(br, wb_ref[m])
        mixed = term if mixed is None else mixed + term
    acc_s[...] += _bdot(mixed.astype(BF16), wo_ref[...])

    @pl.when(j == pl.num_programs(1) - 1)
    def _():
        o_ref[...] = x_ref[...] + acc_s[...]


def _merge(x, g1, o_gla, y_s5, o_dil, o_swa, gluw, glub, w_all, wb, wo, layer):
    t, d = x.shape
    tm, tn = min(TM_MERGE, t), TN_MERGE
    nj = d // tn
    bw = BRANCH_WIDTH

    def rowblk(w):
        return pl.BlockSpec((tm, w), lambda i, j: (i, 0))

    def gate_spec(m):
        return pl.BlockSpec((None, d, tn), lambda i, j, m=m: (layer, 0, N_SMALL // tn + m * nj + j))

    return pl.pallas_call(
        _merge_kernel,
        grid=(t // tm, nj),
        in_specs=[
            rowblk(d),
            pl.BlockSpec((1, d), lambda i, j: (0, 0)),
            rowblk(bw),
            rowblk(bw),
            rowblk(bw),
            rowblk(bw),
            pl.BlockSpec((None, bw, bw), lambda i, j: (layer, 0, 0)),
            pl.BlockSpec((1, bw), lambda i, j: (0, 0)),
            gate_spec(0),
            gate_spec(1),
            gate_spec(2),
            gate_spec(3),
            pl.BlockSpec((None, N_BRANCH, bw, tn), lambda i, j: (layer, 0, 0, j)),
            pl.BlockSpec((None, tn, d), lambda i, j: (layer, j, 0)),
        ],
        out_specs=rowblk(d),
        out_shape=jax.ShapeDtypeStruct((t, d), F32),
        scratch_shapes=[pltpu.VMEM((tm, d), BF16), pltpu.VMEM((tm, bw), BF16), pltpu.VMEM((tm, d), F32)],
        compiler_params=_cparams(("parallel", "arbitrary")),
        name="merge",
    )(x, g1, o_gla, y_s5, o_dil, o_swa, gluw, glub, w_all, w_all, w_all, w_all, wb, wo)


def _ffn_kernel(x_ref, g2_ref, wg_ref, wu_ref, wd_ref, gf_ref, o_ref, h_s, acc_s, *, final_norm):
    j = pl.program_id(1)

    @pl.when(j == 0)
    def _():
        h_s[...] = _rms(x_ref[...], g2_ref[...]).astype(BF16)
        acc_s[...] = jnp.zeros_like(acc_s)

    h = h_s[...]
    gate = _bdot(h, wg_ref[...])
    act = (gate * jax.nn.sigmoid(gate)) * _bdot(h, wu_ref[...])
    acc_s[...] += _bdot(act.astype(BF16), wd_ref[...])

    @pl.when(j == pl.num_programs(1) - 1)
    def _():
        y = x_ref[...] + acc_s[...]
        o_ref[...] = _rms(y, gf_ref[...]) if final_norm else y


def _ffn(x, g2, wg, wu, wd, gf, layer, final_norm):
    t, d = x.shape
    fh = wg.shape[-1]
    tm, tf = min(TM_FFN, t), TF_FFN
    return pl.pallas_call(
        functools.partial(_ffn_kernel, final_norm=final_norm),
        grid=(t // tm, fh // tf),
        in_specs=[
            pl.BlockSpec((tm, d), lambda i, j: (i, 0)),
            pl.BlockSpec((1, d), lambda i, j: (0, 0)),
            pl.BlockSpec((None, d, tf), lambda i, j: (layer, 0, j)),
            pl.BlockSpec((None, d, tf), lambda i, j: (layer, 0, j)),
            pl.BlockSpec((None, tf, d), lambda i, j: (layer, j, 0)),
            pl.BlockSpec((1, d), lambda i, j: (0, 0)),
        ],
        out_specs=pl.BlockSpec((tm, d), lambda i, j: (i, 0)),
        out_shape=jax.ShapeDtypeStruct((t, d), F32),
        scratch_shapes=[pltpu.VMEM((tm, d), BF16), pltpu.VMEM((tm, d), F32)],
        compiler_params=_cparams(("parallel", "arbitrary")),
        name="ffn",
    )(x, g2, wg, wu, wd, gf)


def _pack_kernel(w_ref, o_ref):
    x = w_ref[...]
    row = lax.broadcasted_iota(jnp.int32, x.shape, 0)
    keep = jnp.where(pl.program_id(1) == C_GLR // PACK_CB, GLA_LOWRANK, PACK_CB)
    o_ref[...] = jnp.where(row < keep, x, 0.0).T.astype(BF16)


def _pack_row0(c):
    cb = PACK_CB
    u = GLA_LOWRANK
    front = jnp.where(c < GLR_ORIG // cb, c * (cb // u), c * (cb // u) + 1)
    back = jnp.where(c == C_GLR // cb, GLR_ORIG // u, c * (cb // u) - (N_SMALL - N_ORIG_SMALL) // u)
    return jnp.where(c < C_GLR // cb, front, back) * u


def _pack_w_in(w_in):
    depth, d, d_in = w_in.shape
    n_out = N_SMALL + d_in - N_ORIG_SMALL
    cb = PACK_CB
    assert GLR_ORIG % cb == 0 and C_GLR % cb == 0 and N_SMALL % cb == 0 and n_out % cb == 0
    assert C_GLR == N_ORIG_SMALL - GLA_LOWRANK and C_GLR + cb == N_SMALL
    return pl.pallas_call(
        _pack_kernel,
        grid=(depth, n_out // cb),
        in_specs=[pl.BlockSpec((None, pl.Element(cb), pl.Element(d)), lambda i, c: (i, _pack_row0(c), 0))],
        out_specs=pl.BlockSpec((None, d, cb), lambda i, c: (i, 0, c)),
        out_shape=jax.ShapeDtypeStruct((depth, d, n_out), BF16),
        compiler_params=_cparams(("parallel", "parallel")),
        name="pack_w_in",
    )(jnp.swapaxes(w_in, 1, 2))


def kernel(x, positions, norm1_g, w_in, gla_a2, gla_a_b, gla_norm_g, s5_lambda_re, s5_lambda_im, s5_log_dt, s5_b_re, s5_b_im, s5_c_re, s5_c_im, s5_d, s5_glu_w, s5_glu_b, swa_sinks, w_branch, w_out, norm2_g, w_ffn_gate, w_ffn_up, w_ffn_down, final_norm_g):
    b, l, d = x.shape
    t = b * l
    depth = w_in.shape[0]
    xs = x.reshape(t, d).astype(F32)
    w_all = _pack_w_in(w_in)
    w_br = w_branch.astype(BF16)
    w_o = w_out.astype(BF16)
    w_fg, w_fu, w_fd = w_ffn_gate.astype(BF16), w_ffn_up.astype(BF16), w_ffn_down.astype(BF16)
    glu_w = s5_glu_w.astype(BF16)
    tabs = _rope_tables(positions)
    gf = final_norm_g.reshape(1, d).astype(F32)
    for i in range(depth):
        g1 = norm1_g[i].reshape(1, d).astype(F32)
        proj = _inproj(xs, g1, w_all, i)
        proj3 = proj.reshape(b, l, N_SMALL)
        o_gla = _gla(proj, b, gla_a2[i], gla_a_b[i], gla_norm_g[i])
        s5p = _s5_params(s5_lambda_re[i], s5_lambda_im[i], s5_log_dt[i], s5_b_re[i], s5_b_im[i], s5_c_re[i],
                         s5_c_im[i], s5_d[i], l // S5_CH)
        y_s5 = _s5_core(proj, b, s5p)
        o_dil = _dilated(proj3, tabs).reshape(t, BRANCH_WIDTH)
        o_swa = _swa(proj3, tabs, swa_sinks[i]).reshape(t, BRANCH_WIDTH)
        xs = _merge(xs, g1, o_gla, y_s5, o_dil, o_swa, glu_w, s5_glu_b[i].reshape(1, -1).astype(F32),
                    w_all, w_br, w_o, i)
        xs = _ffn(xs, norm2_g[i].reshape(1, d).astype(F32), w_fg, w_fu, w_fd, gf, i, i == depth - 1)
    return xs.reshape(b, l, d).astype(x.dtype)
```

```python
import functools
import math

import jax
import jax.numpy as jnp
import numpy as np
from jax import lax
from jax.experimental import pallas as pl
from jax.experimental.pallas import tpu as pltpu

F32 = jnp.float32
BF16 = jnp.bfloat16
HIGHEST = lax.Precision.HIGHEST

D_MODEL = 2048
DEPTH = 2
N_BRANCH = 4
BRANCH_WIDTH = 512
HEAD_DIM = 64
ATT_BLOCK = 128
ROPE_THETA = 500000.0
ROPE_DIM = HEAD_DIM // 4
NORM_EPS = 1e-6
GLA_HEADS = 4
GLA_DK = 64
GLA_DV = BRANCH_WIDTH // GLA_HEADS
GLA_LOWRANK = 16
GLA_TAU = 16.0
S5_GROUP = 16
S5_GROUPS = BRANCH_WIDTH // S5_GROUP
S5_STATE = 64
DIL_CONFIGS = ((128, 1), (512, 4), (2048, 16))
DIL_SPAN = ATT_BLOCK * 16
SWA_HEADS = BRANCH_WIDTH // HEAD_DIM
SWA_KV_HEADS = 2
SWA_WINDOW = 128
FFN_HIDDEN = -((-8 * D_MODEL) // (3 * 256)) * 256

LANES = 128
SUBLANES = 8
VMEM_LIMIT = 56 * 1024 * 1024

C_GQ, C_GK, C_GV, C_GR, C_S5 = 0, 256, 512, 1024, 1536
C_CQ, C_CK, C_CV, C_SQ, C_SK, C_SV, C_GLR = 2048, 2560, 3072, 3584, 4096, 4224, 4352
N_SMALL = 4608
N_ORIG_SMALL = 4368
GLR_ORIG = 1536

TM_PROJ, TN_PROJ = 1024, 768
TM_MERGE, TN_MERGE = 512, 256
TM_FFN, TF_FFN = 512, 512
GLA_CH = 128
GLA_NCS = 4
GLA_SUB = 8
GLA_LEVELS = (8, 16, 32, 64, 128)
PACK_CB = 256
S5_CH = 16
S5_GPB = LANES // S5_GROUP
S5_RT = 512
S5_TSPLIT = 4
ATT_UNROLL = 8


def _cparams(sem):
    return pltpu.CompilerParams(dimension_semantics=sem, vmem_limit_bytes=VMEM_LIMIT)


def _rms(x, g):
    ms = jnp.mean(x * x, axis=-1, keepdims=True)
    return x * lax.rsqrt(ms + NORM_EPS) * g


def _bdot(a, b):
    return jnp.dot(a, b, preferred_element_type=F32)


def _bdot_nt(a, b):
    return lax.dot_general(a, b, (((1,), (1,)), ((), ())), preferred_element_type=F32)


def _rope_apply(x, c, sa, sb):
    half = ROPE_DIM // 2
    return x * c + pltpu.roll(x, LANES - half, axis=1) * sa + pltpu.roll(x, half, axis=1) * sb


def _inproj_kernel(x_ref, g_ref, w_ref, o_ref, h_ref):
    @pl.when(pl.program_id(1) == 0)
    def _():
        h_ref[...] = _rms(x_ref[...], g_ref[...]).astype(BF16)

    o_ref[...] = _bdot(h_ref[...], w_ref[...])


def _inproj(x, g, w_all, layer):
    t, d = x.shape
    n = N_SMALL
    tm, tn = min(TM_PROJ, t), TN_PROJ
    return pl.pallas_call(
        _inproj_kernel,
        grid=(t // tm, n // tn),
        in_specs=[
            pl.BlockSpec((tm, d), lambda i, j: (i, 0)),
            pl.BlockSpec((1, d), lambda i, j: (0, 0)),
            pl.BlockSpec((None, d, tn), lambda i, j: (layer, 0, j)),
        ],
        out_specs=pl.BlockSpec((tm, tn), lambda i, j: (i, j)),
        out_shape=jax.ShapeDtypeStruct((t, n), F32),
        scratch_shapes=[pltpu.VMEM((tm, d), BF16)],
        compiler_params=_cparams(("parallel", "arbitrary")),
        name="inproj",
    )(x, g, w_all)


def _rope_kernel(pos_ref, cos_ref, sa_ref, sb_ref):
    pos = pos_ref[0].astype(F32)
    lane = lax.broadcasted_iota(jnp.int32, (SUBLANES, LANES), 1)
    d = lane % HEAD_DIM
    half = ROPE_DIM // 2
    fi = (d % half).astype(F32) / half
    inv = jnp.power(jnp.full((SUBLANES, LANES), ROPE_THETA, F32), -fi)[0:1]
    d1 = d[0:1]
    ang = pos * inv
    c, s = jnp.cos(ang), jnp.sin(ang)
    cos_ref[0] = jnp.where(d1 < ROPE_DIM, c, 1.0)
    sa_ref[0] = jnp.where(d1 < half, -s, 0.0)
    sb_ref[0] = jnp.where((d1 >= half) & (d1 < ROPE_DIM), s, 0.0)


def _rope_tables(positions):
    b, l = positions.shape
    spec = pl.BlockSpec((1, l, LANES), lambda i: (i, 0, 0))
    shp = jax.ShapeDtypeStruct((b, l, LANES), F32)
    return pl.pallas_call(
        _rope_kernel,
        grid=(b,),
        in_specs=[pl.BlockSpec((1, l, 1), lambda i: (i, 0, 0))],
        out_specs=[spec, spec, spec],
        out_shape=[shp, shp, shp],
        compiler_params=_cparams(("parallel",)),
        name="rope_tables",
    )(positions.reshape(b, l, 1))


def _band_bias(strict_prev):
    t = ATT_BLOCK
    qi = np.arange(2 * t)[:, None] % t
    kj = np.arange(2 * t)[None, :]
    own = (kj >= t) & (kj - t <= qi)
    prev = (kj < t) & (kj >= qi + (1 if strict_prev else 0))
    ninf = np.float32(-np.inf)
    return np.stack([np.where(own | prev, 0.0, ninf), np.where(own, 0.0, ninf)]).astype(np.float32)


def _band_block(q, kp, ko, vp, vo, bias):
    t = ATT_BLOCK
    lane = lax.broadcasted_iota(jnp.int32, (t, LANES), 1)
    in_a = lane < HEAD_DIM
    q2 = jnp.concatenate([jnp.where(in_a, q, 0.0), jnp.where(in_a, 0.0, q)], axis=0).astype(BF16)
    s = _bdot_nt(q2, jnp.concatenate([kp, ko], axis=0).astype(BF16)) + bias
    m = jnp.max(s, axis=-1, keepdims=True)
    p = jnp.exp(s - m).astype(BF16)
    ones = jnp.ones((2 * t, LANES), BF16)
    vcat = jnp.concatenate([jnp.concatenate([vp, vo], axis=0).astype(BF16), ones], axis=1)
    od = _bdot(p, vcat)
    o = jnp.where(in_a, od[:t, :LANES], od[t:, :LANES])
    df = jnp.where(in_a, od[:t, LANES:], od[t:, LANES:])
    mf = jnp.where(in_a, m[:t], m[t:])
    return mf, df, o


def _dil_kernel(q_ref, k_ref, v_ref, c_ref, sa_ref, sb_ref, bias_ref, o_ref, qs, ks, ms, ds, os_):
    l = q_ref.shape[1]
    c, sa, sb = c_ref[0], sa_ref[0], sb_ref[0]
    qs[...] = _rope_apply(q_ref[0], c, sa, sb) * (HEAD_DIM**-0.5)
    ks[...] = _rope_apply(k_ref[0], c, sa, sb)
    t = ATT_BLOCK
    nblk = DIL_SPAN // t

    for sblk in range(l // DIL_SPAN):
        base = sblk * DIL_SPAN
        for ci, (_, dil) in enumerate(DIL_CONFIGS):

            def body(idx, carry, dil=dil, ci=ci, base=base):
                r = idx % dil
                n = idx // dil
                loc = r + dil * t * n
                start = base + loc
                has_prev = start >= dil * t
                pstart = jnp.where(has_prev, start - dil * t, start)
                if dil == 1:
                    start = pl.multiple_of(start, t)
                    pstart = pl.multiple_of(pstart, t)
                    loc = pl.multiple_of(loc, t)
                    rows, prow, lrow = pl.ds(start, t), pl.ds(pstart, t), pl.ds(loc, t)
                else:
                    rows = pl.ds(start, t, stride=dil)
                    prow = pl.ds(pstart, t, stride=dil)
                    lrow = pl.ds(loc, t, stride=dil)
                mf, df, o = _band_block(
                    qs[rows, :], ks[prow, :], ks[rows, :], v_ref[0, prow, :], v_ref[0, rows, :],
                    bias_ref[jnp.where(has_prev, 0, 1)],
                )
                ms[ci, lrow, :] = mf
                ds[ci, lrow, :] = df
                os_[ci, lrow, :] = o
                return carry

            lax.fori_loop(0, nblk, body, 0, unroll=ATT_UNROLL)

        m0, m1, m2 = ms[0], ms[1], ms[2]
        mx = jnp.maximum(jnp.maximum(m0, m1), m2)
        w0, w1, w2 = jnp.exp(m0 - mx), jnp.exp(m1 - mx), jnp.exp(m2 - mx)
        den = w0 * ds[0] + w1 * ds[1] + w2 * ds[2]
        num = w0 * os_[0] + w1 * os_[1] + w2 * os_[2]
        o_ref[0, base : base + DIL_SPAN, :] = (num / den).astype(o_ref.dtype)


def _dilated(proj3, tabs):
    b, l, _ = proj3.shape
    assert l % DIL_SPAN == 0
    bias = _band_bias(False)

    def col(c0):
        return pl.BlockSpec((1, l, LANES), lambda i, p: (i, 0, c0 // LANES + p))

    tab = pl.BlockSpec((1, l, LANES), lambda i, p: (i, 0, 0))
    return pl.pallas_call(
        _dil_kernel,
        grid=(b, BRANCH_WIDTH // LANES),
        in_specs=[col(C_CQ), col(C_CK), col(C_CV), tab, tab, tab, pl.BlockSpec(bias.shape, lambda i, p: (0, 0, 0))],
        out_specs=pl.BlockSpec((1, l, LANES), lambda i, p: (i, 0, p)),
        out_shape=jax.ShapeDtypeStruct((b, l, BRANCH_WIDTH), BF16),
        scratch_shapes=[pltpu.VMEM((l, LANES), F32)] * 2
        + [pltpu.VMEM((len(DIL_CONFIGS), DIL_SPAN, LANES), F32)] * 3,
        compiler_params=_cparams(("parallel", "parallel")),
        name="dilated_attn",
    )(proj3, proj3, proj3, *tabs, jnp.asarray(bias))


def _swa_kernel(q_ref, k_ref, v_ref, c_ref, sa_ref, sb_ref, bias_ref, sink_ref, o_ref, qs, ks, vs):
    l = q_ref.shape[1]
    c, sa, sb = c_ref[0], sa_ref[0], sb_ref[0]
    qs[...] = _rope_apply(q_ref[0], c, sa, sb) * (HEAD_DIM**-0.5)
    g = pl.program_id(1) // (SWA_HEADS // SWA_KV_HEADS // 2)
    lane = lax.broadcasted_iota(jnp.int32, (l, LANES), 1)
    keep = (lane // HEAD_DIM) == g
    k, v = _rope_apply(k_ref[0], c, sa, sb), v_ref[0]
    ks[...] = jnp.where(keep, k, pltpu.roll(k, HEAD_DIM, axis=1))
    vs[...] = jnp.where(keep, v, pltpu.roll(v, HEAD_DIM, axis=1))
    sink = sink_ref[0]
    t = ATT_BLOCK

    def body(n, carry):
        start = pl.multiple_of(n * t, t)
        has_prev = n > 0
        pstart = pl.multiple_of(jnp.where(has_prev, start - t, start), t)
        rows, prow = pl.ds(start, t), pl.ds(pstart, t)
        mf, df, o = _band_block(qs[rows, :], ks[prow, :], ks[rows, :], vs[prow, :], vs[rows, :],
                                bias_ref[jnp.where(has_prev, 0, 1)])
        lse = mf + jnp.log(df)
        o_ref[0, rows, :] = (o / df * jax.nn.sigmoid(lse - sink)).astype(o_ref.dtype)
        return carry

    lax.fori_loop(0, l // t, body, 0, unroll=ATT_UNROLL)


def _swa(proj3, tabs, sinks):
    b, l, _ = proj3.shape
    npair = BRANCH_WIDTH // LANES
    sink_l = jnp.repeat(sinks.astype(F32), HEAD_DIM).reshape(npair, 1, LANES)
    bias = _band_bias(True)
    tab = pl.BlockSpec((1, l, LANES), lambda i, p: (i, 0, 0))
    return pl.pallas_call(
        _swa_kernel,
        grid=(b, npair),
        in_specs=[
            pl.BlockSpec((1, l, LANES), lambda i, p: (i, 0, C_SQ // LANES + p)),
            pl.BlockSpec((1, l, LANES), lambda i, p: (i, 0, C_SK // LANES)),
            pl.BlockSpec((1, l, LANES), lambda i, p: (i, 0, C_SV // LANES)),
            tab,
            tab,
            tab,
            pl.BlockSpec(bias.shape, lambda i, p: (0, 0, 0)),
            pl.BlockSpec((1, 1, LANES), lambda i, p: (p, 0, 0)),
        ],
        out_specs=pl.BlockSpec((1, l, LANES), lambda i, p: (i, 0, p)),
        out_shape=jax.ShapeDtypeStruct((b, l, BRANCH_WIDTH), BF16),
        scratch_shapes=[pltpu.VMEM((l, LANES), F32)] * 3,
        compiler_params=_cparams(("parallel", "parallel")),
        name="swa_attn",
    )(proj3, proj3, proj3, *tabs, jnp.asarray(bias), sink_l)


@functools.lru_cache(maxsize=None)
def _gla_consts():
    ch = GLA_CH
    r = np.arange(ch)[:, None]
    j = np.arange(ch)[None, :]
    dstack = (j <= r).astype(np.float32)
    lm = []
    for m in GLA_LEVELS[:-1]:
        mk = (((r // m) % 2 == 1) & ((j // m) == (r // m) - 1)).astype(np.float32)
        lm.append(np.tile(mk, (GLA_HEADS, 1)))
    lmask = np.stack(lm)
    dmask = np.tile(((r // GLA_SUB) == (j // GLA_SUB)).astype(np.float32), (1, GLA_HEADS))
    nsub = ch // GLA_SUB
    e = np.zeros((GLA_SUB, GLA_HEADS * GLA_DK, GLA_HEADS * ch), np.float32)
    for u in range(GLA_SUB):
        for h in range(GLA_HEADS):
            for s in range(nsub):
                e[u, h * GLA_DK : (h + 1) * GLA_DK, h * ch + GLA_SUB * s + u] = 1.0
    rr = np.arange(GLA_HEADS * GLA_DV)[:, None]
    cc = np.arange(GLA_HEADS * GLA_DK)[None, :]
    bd = ((rr // GLA_DV) == (cc // GLA_DK)).astype(np.float32)
    return dstack, lmask, dmask, e, bd


def _split3(x):
    hi = x.astype(BF16)
    r1 = x - hi.astype(F32)
    mid = r1.astype(BF16)
    lo = (r1 - mid.astype(F32)).astype(BF16)
    return hi, mid, lo


def _bcast_grp(x, m, u):
    r, w = x.shape
    x3 = x.reshape(r // m, m, w)
    return jnp.broadcast_to(x3[:, u : u + 1, :], x3.shape).reshape(r, w)


def _bcast_sub(x, u):
    return _bcast_grp(x, GLA_SUB, u)


def _gla_kernel(
    q_ref, k_ref, v_ref, r_ref, glr_ref, a2_ref, ab_ref, ng_ref, dst_ref, lmask_ref, dmask_ref, e_ref, bd_ref,
    o_ref, st_ref,
):
    ch = GLA_CH
    rs = q_ref.shape[0]
    ncs = rs // ch
    nlev = len(GLA_LEVELS)
    hk = GLA_HEADS * GLA_DK

    @pl.when(pl.program_id(1) == 0)
    def _():
        st_ref[...] = jnp.zeros_like(st_ref)

    q = q_ref[...] * (GLA_DK**-0.5)
    k = k_ref[...]
    z = _bdot(glr_ref[...].astype(BF16), a2_ref[...]) + ab_ref[...]
    g = (jnp.minimum(z, 0.0) - jnp.log1p(jnp.exp(-jnp.abs(z)))) * (1.0 / GLA_TAU)
    g3 = _split3(g)
    dst = dst_ref[...]

    cums, excl = [], []
    for c in range(ncs):
        rows = slice(c * ch, (c + 1) * ch)
        cums.append(_bdot(dst, g3[0][rows]) + _bdot(dst, g3[1][rows]) + _bdot(dst, g3[2][rows]))
        excl.append(cums[c] - g[rows])

    def eq(c, li):
        m = GLA_LEVELS[li]
        return cums[c] - _bcast_grp(excl[c], m, 0)

    def ek(c, li):
        m = GLA_LEVELS[li]
        return _bcast_grp(cums[c], m, m - 1) - cums[c]

    cs = jnp.concatenate([eq(c, 0) for c in range(ncs)], axis=0)
    tsub = lax.broadcasted_iota(jnp.int32, (rs, hk), 0) % GLA_SUB
    arep = jnp.zeros((rs, GLA_HEADS * ch), F32)
    for u in range(GLA_SUB):
        dec = jnp.exp(jnp.where(tsub >= u, cs - _bcast_sub(cs, u), -jnp.inf))
        p = q * _bcast_sub(k, u) * dec
        arep = arep + _bdot(p.astype(BF16), e_ref[u])
    arep = arep * jnp.concatenate([dmask_ref[...]] * ncs, axis=0)

    lane_k = lax.broadcasted_iota(jnp.int32, (ch, hk), 1) // GLA_DK
    ng = ng_ref[...]
    bd = bd_ref[...]
    for c in range(ncs):
        rows = slice(c * ch, (c + 1) * ch)
        qc, kc = q[rows], k[rows]
        vc = v_ref[rows, :]
        aoff = jnp.zeros((GLA_HEADS * ch, ch), F32)
        for li in range(nlev - 1):
            qe = qc * jnp.exp(eq(c, li))
            ke = (kc * jnp.exp(ek(c, li))).astype(BF16)
            qst = jnp.concatenate([jnp.where(lane_k == h, qe, 0.0) for h in range(GLA_HEADS)], axis=0).astype(BF16)
            aoff = aoff + _bdot_nt(qst, ke) * lmask_ref[li]
        cum = eq(c, nlev - 1)
        st = st_ref[...]
        o_inter = _bdot_nt((qc * jnp.exp(cum)).astype(BF16), st.astype(BF16))
        vb = vc.astype(BF16)
        outs = []
        for h in range(GLA_HEADS):
            a_h = aoff[h * ch : (h + 1) * ch] + arep[rows, h * ch : (h + 1) * ch]
            o_h = _bdot(a_h.astype(BF16), vb[:, h * GLA_DV : (h + 1) * GLA_DV]) + o_inter[:, h * GLA_DV : (h + 1) * GLA_DV]
            ms = jnp.mean(o_h * o_h, axis=-1, keepdims=True)
            outs.append(o_h * lax.rsqrt(ms + NORM_EPS))
        o = jnp.concatenate(outs, axis=1) * ng
        rc = r_ref[rows, :]
        o_ref[rows, :] = (o * (rc * jax.nn.sigmoid(rc))).astype(o_ref.dtype)
        ke_last = (kc * jnp.exp(ek(c, nlev - 1))).astype(BF16)
        kv = _bdot(vc.T.astype(BF16), ke_last)
        st_ref[...] = st * jnp.exp(cum[ch - 1 : ch, :]) + kv * bd


def _gla(proj, b, a2, ab, ng):
    t = proj.shape[0]
    l = t // b
    rs = GLA_CH * GLA_NCS
    assert l % rs == 0
    ns = l // rs
    dstack, lmask, dmask, e, bd = _gla_consts()
    a2p = jnp.zeros((LANES, GLA_HEADS * GLA_DK), BF16).at[:GLA_LOWRANK].set(a2.astype(BF16))

    def rowblk(w, c0):
        return pl.BlockSpec((rs, w), lambda i, s: (i * ns + s, c0 // w))

    def full(shape):
        nd = len(shape)
        return pl.BlockSpec(shape, lambda i, s: (0,) * nd)

    hk, hv = GLA_HEADS * GLA_DK, GLA_HEADS * GLA_DV
    return pl.pallas_call(
        _gla_kernel,
        grid=(b, ns),
        in_specs=[
            rowblk(hk, C_GQ),
            rowblk(hk, C_GK),
            rowblk(hv, C_GV),
            rowblk(hv, C_GR),
            rowblk(LANES, C_GLR),
            full((LANES, hk)),
            full((1, hk)),
            full((1, hv)),
            full(dstack.shape),
            full(lmask.shape),
            full(dmask.shape),
            full(e.shape),
            full(bd.shape),
        ],
        out_specs=pl.BlockSpec((rs, hv), lambda i, s: (i * ns + s, 0)),
        out_shape=jax.ShapeDtypeStruct((t, hv), BF16),
        scratch_shapes=[pltpu.VMEM((hv, hk), F32)],
        compiler_params=_cparams(("parallel", "arbitrary")),
        name="gla",
    )(
        proj, proj, proj, proj, proj, a2p, ab.reshape(1, hk).astype(F32), ng.reshape(1, hv).astype(F32),
        jnp.asarray(dstack, BF16), jnp.asarray(lmask), jnp.asarray(dmask), jnp.asarray(e, BF16), jnp.asarray(bd),
    )


def _s5_params(lam_re, lam_im, log_dt, b_re, b_im, c_re, c_im, d, nch):
    f = F32
    cs = S5_CH
    dt = jnp.exp(log_dt.astype(f))[:, None]
    lr, li = lam_re.astype(f), lam_im.astype(f)
    mag = jnp.exp(lr * dt)
    ab_re, ab_im = mag * jnp.cos(li * dt), mag * jnp.sin(li * dt)
    den = lr * lr + li * li
    z_re = ((ab_re - 1.0) * lr + ab_im * li) / den
    z_im = (ab_im * lr - (ab_re - 1.0) * li) / den
    br, bi = b_re.astype(f), b_im.astype(f)
    bb_re = z_re[..., None] * br - z_im[..., None] * bi
    bb_im = z_re[..., None] * bi + z_im[..., None] * br

    def apow(p):
        p = jnp.asarray(p, f)[:, None, None]
        m = jnp.exp(p * (lr * dt))
        return m * jnp.cos(p * (li * dt)), m * jnp.sin(p * (li * dt))

    cr, ci = c_re.astype(f), c_im.astype(f)
    p_re, p_im = apow(np.arange(cs + 1))
    nb, gpb = S5_GROUPS // S5_GPB, S5_GPB
    hw = gpb * S5_STATE
    eye = jnp.eye(gpb, dtype=f)

    def b_blockdiag(x):
        x = x.reshape(nb, gpb, S5_STATE, S5_GROUP).transpose(0, 1, 3, 2)
        x = (x[:, :, :, None, :] * eye[None, :, None, :, None]).reshape(nb, LANES, hw)
        return jnp.concatenate([x, x], axis=-1)

    def c_blockdiag(x):
        x = x.reshape(nb, gpb, S5_GROUP, S5_STATE).transpose(0, 1, 3, 2)
        return (x[:, :, :, None, :] * eye[None, :, None, :, None]).reshape(nb, hw, LANES)

    def lanes(a, b_):
        p = a.shape[0]
        return jnp.concatenate([a.reshape(p, nb, hw), b_.reshape(p, nb, hw)], axis=-1).transpose(1, 0, 2)

    rev = np.arange(cs - 1, -1, -1)
    pr, pi = p_re[rev], p_im[rev]
    pa, pb = lanes(pr, pi), lanes(-pi, pr)
    acol_re = jnp.broadcast_to(ab_re.reshape(nb, hw, 1), (nb, hw, LANES))
    acol_im = jnp.broadcast_to(ab_im.reshape(nb, hw, 1), (nb, hw, LANES))
    nstep = max(1, int(math.log2(nch)))
    s_re, s_im = apow(cs * (2 ** np.arange(nstep)))
    ar = s_re.reshape(nstep, nb, hw).transpose(1, 0, 2)
    ai = s_im.reshape(nstep, nb, hw).transpose(1, 0, 2)
    dd = d.astype(f).reshape(nb, 1, LANES)
    return (b_blockdiag(bb_re), b_blockdiag(bb_im), c_blockdiag(cr), c_blockdiag(ci), acol_re, acol_im,
            pa, pb, ar, ai, dd)


def _s5_kernel(u_ref, br_ref, bi_ref, cr_ref, ci_ref, acr_ref, aci_ref, pa_ref, pb_ref, ar_ref, ai_ref, d_ref,
               y_ref, mb_s, mc_s, kc_s, *, nchb):
    cs = S5_CH
    t = u_ref.shape[0]
    nch = t // cs
    hw = S5_GPB * S5_STATE
    br, bi = br_ref[0], bi_ref[0]
    pa, pb = pa_ref[0], pb_ref[0]
    for j in range(cs):
        mb_s[j * LANES : (j + 1) * LANES, :] = (br * pa[j : j + 1] + bi * pb[j : j + 1]).astype(BF16)
    x_re, x_im = cr_ref[0], ci_ref[0]
    kc = _bdot(mb_s[...], jnp.concatenate([x_re, -x_im], axis=0).astype(BF16)).astype(BF16)
    a_re, a_im = acr_ref[0], aci_ref[0]
    for tt in range(cs):
        x_re, x_im = x_re * a_re - x_im * a_im, x_re * a_im + x_im * a_re
        mc_s[:hw, tt * LANES : (tt + 1) * LANES] = x_re.astype(BF16)
        mc_s[hw:, tt * LANES : (tt + 1) * LANES] = (-x_im).astype(BF16)
    gl = cs // S5_TSPLIT * LANES
    for qq in range(S5_TSPLIT):
        kc_s[:, qq * LANES : (qq + 1) * LANES] = kc[(S5_TSPLIT - 1 - qq) * gl : (S5_TSPLIT - qq) * gl]
    ucat = jnp.concatenate([u_ref[pl.ds(j, nch, stride=cs), :] for j in range(cs)], axis=1).astype(BF16)
    e_all = _bdot(ucat, mb_s[...])
    cidx = lax.broadcasted_iota(jnp.int32, (nch, LANES), 0) % nchb
    ar, ai = ar_ref[0], ai_ref[0]
    hp_re, hp_im = [], []
    for g in range(hw // LANES):
        lo = g * LANES
        e_re, e_im = e_all[:, lo : lo + LANES], e_all[:, hw + lo : hw + lo + LANES]
        for kstep in range(ar.shape[0]):
            s = 2**kstep
            if s >= nchb:
                break
            s_re = jnp.where(cidx >= s, pltpu.roll(e_re, s, axis=0), 0.0)
            s_im = jnp.where(cidx >= s, pltpu.roll(e_im, s, axis=0), 0.0)
            k_re, k_im = ar[kstep : kstep + 1, lo : lo + LANES], ai[kstep : kstep + 1, lo : lo + LANES]
            e_re, e_im = e_re + s_re * k_re - s_im * k_im, e_im + s_re * k_im + s_im * k_re
        hp_re.append(jnp.where(cidx >= 1, pltpu.roll(e_re, 1, axis=0), 0.0).astype(BF16))
        hp_im.append(jnp.where(cidx >= 1, pltpu.roll(e_im, 1, axis=0), 0.0).astype(BF16))
    ycar = _bdot(jnp.concatenate(hp_re + hp_im, axis=1), mc_s[...])
    for tt in range(cs):
        y_ref[pl.ds(tt, nch, stride=cs), :] = ycar[:, tt * LANES : (tt + 1) * LANES]

    rt = min(S5_RT, t)
    rmod = lax.broadcasted_iota(jnp.int32, (rt, LANES), 0) % cs
    dvec = d_ref[0]
    glag = cs // S5_TSPLIT

    def tile(i, carry):
        rows = pl.ds(pl.multiple_of(i * rt, rt), rt)
        ut = u_ref[rows, :]
        ush = [jnp.where(rmod >= tau, pltpu.roll(ut, tau, axis=0), 0.0) for tau in range(glag - 1, 0, -1)] + [ut]
        zz = _bdot(jnp.concatenate(ush, axis=1).astype(BF16), kc_s[...])
        acc = y_ref[rows, :] + dvec * ut + zz[:, :LANES]
        for qq in range(1, S5_TSPLIT):
            part = pltpu.roll(zz[:, qq * LANES : (qq + 1) * LANES], qq * glag, axis=0)
            acc = acc + jnp.where(rmod >= qq * glag, part, 0.0)
        y_ref[rows, :] = acc
        return carry

    lax.fori_loop(0, t // rt, tile, 0)


def _s5_core(proj, b, params):
    t = proj.shape[0]
    nchb = t // b // S5_CH
    sw = S5_GPB * 2 * S5_STATE

    def bspec(a):
        return pl.BlockSpec((1,) + a.shape[1:], lambda p: (p, 0, 0))

    return pl.pallas_call(
        functools.partial(_s5_kernel, nchb=nchb),
        grid=(S5_GROUPS // S5_GPB,),
        in_specs=[pl.BlockSpec((t, LANES), lambda p: (0, C_S5 // LANES + p))] + [bspec(a) for a in params],
        out_specs=pl.BlockSpec((t, LANES), lambda p: (0, p)),
        out_shape=jax.ShapeDtypeStruct((t, BRANCH_WIDTH), F32),
        scratch_shapes=[
            pltpu.VMEM((S5_CH * LANES, sw), BF16),
            pltpu.VMEM((sw, S5_CH * LANES), BF16),
            pltpu.VMEM((S5_CH // S5_TSPLIT * LANES, S5_TSPLIT * LANES), BF16),
        ],
        compiler_params=_cparams(("parallel",)),
        name="s5_core",
    )(proj, *params)


def _merge_kernel(
    x_ref, g1_ref, gla_ref, ys5_ref, dil_ref, swa_ref, gluw_ref, glub_ref, wg0, wg1, wg2, wg3, wb_ref, wo_ref,
    o_ref, h_s, s5_s, acc_s,
):
    j = pl.program_id(1)

    @pl.when(j == 0)
    def _():
        h_s[...] = _rms(x_ref[...], g1_ref[...]).astype(BF16)
        zz = jax.nn.gelu(ys5_ref[...])
        gate = jax.nn.sigmoid(_bdot(zz.astype(BF16), gluw_ref[...]) + glub_ref[...])
        s5_s[...] = (zz * gate).astype(BF16)
        acc_s[...] = jnp.zeros_like(acc_s)

    h = h_s[...]
    branches = (gla_ref[...], s5_s[...], dil_ref[...], swa_ref[...])
    mixed = None
    for m, (wg, br) in enumerate(zip((wg0, wg1, wg2, wg3), branches)):
        term = jax.nn.sigmoid(_bdot(h, wg[...])) * _bdot(br, wb_ref[m])
        mixed = term if mixed is None else mixed + term
    acc_s[...] += _bdot(mixed.astype(BF16), wo_ref[...])

    @pl.when(j == pl.num_programs(1) - 1)
    def _():
        o_ref[...] = x_ref[...] + acc_s[...]


def _merge(x, g1, o_gla, y_s5, o_dil, o_swa, gluw, glub, w_all, wb, wo, layer):
    t, d = x.shape
    tm, tn = min(TM_MERGE, t), TN_MERGE
    nj = d // tn
    bw = BRANCH_WIDTH

    def rowblk(w):
        return pl.BlockSpec((tm, w), lambda i, j: (i, 0))

    def gate_spec(m):
        return pl.BlockSpec((None, d, tn), lambda i, j, m=m: (layer, 0, N_SMALL // tn + m * nj + j))

    return pl.pallas_call(
        _merge_kernel,
        grid=(t // tm, nj),
        in_specs=[
            rowblk(d),
            pl.BlockSpec((1, d), lambda i, j: (0, 0)),
            rowblk(bw),
            rowblk(bw),
            rowblk(bw),
            rowblk(bw),
            pl.BlockSpec((None, bw, bw), lambda i, j: (layer, 0, 0)),
            pl.BlockSpec((1, bw), lambda i, j: (0, 0)),
            gate_spec(0),
            gate_spec(1),
            gate_spec(2),
            gate_spec(3),
            pl.BlockSpec((None, N_BRANCH, bw, tn), lambda i, j: (layer, 0, 0, j)),
            pl.BlockSpec((None, tn, d), lambda i, j: (layer, j, 0)),
        ],
        out_specs=rowblk(d),
        out_shape=jax.ShapeDtypeStruct((t, d), F32),
        scratch_shapes=[pltpu.VMEM((tm, d), BF16), pltpu.VMEM((tm, bw), BF16), pltpu.VMEM((tm, d), F32)],
        compiler_params=_cparams(("parallel", "arbitrary")),
        name="merge",
    )(x, g1, o_gla, y_s5, o_dil, o_swa, gluw, glub, w_all, w_all, w_all, w_all, wb, wo)


def _ffn_kernel(x_ref, g2_ref, wg_ref, wu_ref, wd_ref, gf_ref, o_ref, h_s, acc_s, *, final_norm):
    j = pl.program_id(1)

    @pl.when(j == 0)
    def _():
        h_s[...] = _rms(x_ref[...], g2_ref[...]).astype(BF16)
        acc_s[...] = jnp.zeros_like(acc_s)

    h = h_s[...]
    gate = _bdot(h, wg_ref[...])
    act = (gate * jax.nn.sigmoid(gate)) * _bdot(h, wu_ref[...])
    acc_s[...] += _bdot(act.astype(BF16), wd_ref[...])

    @pl.when(j == pl.num_programs(1) - 1)
    def _():
        y = x_ref[...] + acc_s[...]
        o_ref[...] = _rms(y, gf_ref[...]) if final_norm else y


def _ffn(x, g2, wg, wu, wd, gf, layer, final_norm):
    t, d = x.shape
    fh = wg.shape[-1]
    tm, tf = min(TM_FFN, t), TF_FFN
    return pl.pallas_call(
        functools.partial(_ffn_kernel, final_norm=final_norm),
        grid=(t // tm, fh // tf),
        in_specs=[
            pl.BlockSpec((tm, d), lambda i, j: (i, 0)),
            pl.BlockSpec((1, d), lambda i, j: (0, 0)),
            pl.BlockSpec((None, d, tf), lambda i, j: (layer, 0, j)),
            pl.BlockSpec((None, d, tf), lambda i, j: (layer, 0, j)),
            pl.BlockSpec((None, tf, d), lambda i, j: (layer, j, 0)),
            pl.BlockSpec((1, d), lambda i, j: (0, 0)),
        ],
        out_specs=pl.BlockSpec((tm, d), lambda i, j: (i, 0)),
        out_shape=jax.ShapeDtypeStruct((t, d), F32),
        scratch_shapes=[pltpu.VMEM((tm, d), BF16), pltpu.VMEM((tm, d), F32)],
        compiler_params=_cparams(("parallel", "arbitrary")),
        name="ffn",
    )(x, g2, wg, wu, wd, gf)


def _pack_kernel(w_ref, o_ref):
    x = w_ref[...]
    row = lax.broadcasted_iota(jnp.int32, x.shape, 0)
    keep = jnp.where(pl.program_id(1) == C_GLR // PACK_CB, GLA_LOWRANK, PACK_CB)
    o_ref[...] = jnp.where(row < keep, x, 0.0).T.astype(BF16)


def _pack_row0(c):
    cb = PACK_CB
    u = GLA_LOWRANK
    front = jnp.where(c < GLR_ORIG // cb, c * (cb // u), c * (cb // u) + 1)
    back = jnp.where(c == C_GLR // cb, GLR_ORIG // u, c * (cb // u) - (N_SMALL - N_ORIG_SMALL) // u)
    return jnp.where(c < C_GLR // cb, front, back) * u


def _pack_w_in(w_in):
    depth, d, d_in = w_in.shape
    n_out = N_SMALL + d_in - N_ORIG_SMALL
    cb = PACK_CB
    assert GLR_ORIG % cb == 0 and C_GLR % cb == 0 and N_SMALL % cb == 0 and n_out % cb == 0
    assert C_GLR == N_ORIG_SMALL - GLA_LOWRANK and C_GLR + cb == N_SMALL
    return pl.pallas_call(
        _pack_kernel,
        grid=(depth, n_out // cb),
        in_specs=[pl.BlockSpec((None, pl.Element(cb), pl.Element(d)), lambda i, c: (i, _pack_row0(c), 0))],
        out_specs=pl.BlockSpec((None, d, cb), lambda i, c: (i, 0, c)),
        out_shape=jax.ShapeDtypeStruct((depth, d, n_out), BF16),
        compiler_params=_cparams(("parallel", "parallel")),
        name="pack_w_in",
    )(jnp.swapaxes(w_in, 1, 2))


def kernel(x, positions, norm1_g, w_in, gla_a2, gla_a_b, gla_norm_g, s5_lambda_re, s5_lambda_im, s5_log_dt, s5_b_re, s5_b_im, s5_c_re, s5_c_im, s5_d, s5_glu_w, s5_glu_b, swa_sinks, w_branch, w_out, norm2_g, w_ffn_gate, w_ffn_up, w_ffn_down, final_norm_g):
    b, l, d = x.shape
    t = b * l
    depth = w_in.shape[0]
    xs = x.reshape(t, d).astype(F32)
    w_all = _pack_w_in(w_in)
    w_br = w_branch.astype(BF16)
    w_o = w_out.astype(BF16)
    w_fg, w_fu, w_fd = w_ffn_gate.astype(BF16), w_ffn_up.astype(BF16), w_ffn_down.astype(BF16)
    glu_w = s5_glu_w.astype(BF16)
    tabs = _rope_tables(positions)
    gf = final_norm_g.reshape(1, d).astype(F32)
    for i in range(depth):
        g1 = norm1_g[i].reshape(1, d).astype(F32)
        proj = _inproj(xs, g1, w_all, i)
        proj3 = proj.reshape(b, l, N_SMALL)
        o_gla = _gla(proj, b, gla_a2[i], gla_a_b[i], gla_norm_g[i])
        s5p = _s5_params(s5_lambda_re[i], s5_lambda_im[i], s5_log_dt[i], s5_b_re[i], s5_b_im[i], s5_c_re[i],
                         s5_c_im[i], s5_d[i], l // S5_CH)
        y_s5 = _s5_core(proj, b, s5p)
        o_dil = _dilated(proj3, tabs).reshape(t, BRANCH_WIDTH)
        o_swa = _swa(proj3, tabs, swa_sinks[i]).reshape(t, BRANCH_WIDTH)
        xs = _merge(xs, g1, o_gla, y_s5, o_dil, o_swa, glu_w, s5_glu_b[i].reshape(1, -1).astype(F32),
                    w_all, w_br, w_o, i)
        xs = _ffn(xs, norm2_g[i].reshape(1, d).astype(F32), w_fg, w_fu, w_fd, gf, i, i == depth - 1)
    return xs.reshape(b, l, d).astype(x.dtype)
```

```python
import functools
import math

import jax
import jax.numpy as jnp
import numpy as np
from jax import lax
from jax.experimental import pallas as pl
from jax.experimental.pallas import tpu as pltpu

F32 = jnp.float32
BF16 = jnp.bfloat16
HIGHEST = lax.Precision.HIGHEST

D_MODEL = 2048
DEPTH = 2
N_BRANCH = 4
BRANCH_WIDTH = 512
HEAD_DIM = 64
ATT_BLOCK = 128
ROPE_THETA = 500000.0
ROPE_DIM = HEAD_DIM // 4
NORM_EPS = 1e-6
GLA_HEADS = 4
GLA_DK = 64
GLA_DV = BRANCH_WIDTH // GLA_HEADS
GLA_LOWRANK = 16
GLA_TAU = 16.0
S5_GROUP = 16
S5_GROUPS = BRANCH_WIDTH // S5_GROUP
S5_STATE = 64
DIL_CONFIGS = ((128, 1), (512, 4), (2048, 16))
DIL_SPAN = ATT_BLOCK * 16
SWA_HEADS = BRANCH_WIDTH // HEAD_DIM
SWA_KV_HEADS = 2
SWA_WINDOW = 128
FFN_HIDDEN = -((-8 * D_MODEL) // (3 * 256)) * 256

LANES = 128
SUBLANES = 8
VMEM_LIMIT = 56 * 1024 * 1024

C_GQ, C_GK, C_GV, C_GR, C_S5 = 0, 256, 512, 1024, 1536
C_CQ, C_CK, C_CV, C_SQ, C_SK, C_SV, C_GLR = 2048, 2560, 3072, 3584, 4096, 4224, 4352
N_SMALL = 4608
N_ORIG_SMALL = 4368
GLR_ORIG = 1536

TM_PROJ, TN_PROJ = 1024, 768
TM_MERGE, TN_MERGE = 512, 256
TM_FFN, TF_FFN = 512, 512
FFN_DOWN_CHUNK = 512
GLA_CH = 128
GLA_NCS = 4
GLA_SUB = 8
GLA_LEVELS = (8, 16, 32, 64, 128)
PACK_CB = 256
S5_CH = 16
S5_GPB = LANES // S5_GROUP
S5_RT = 512
S5_TSPLIT = 4
ATT_UNROLL = 8


def _cparams(sem):
    return pltpu.CompilerParams(dimension_semantics=sem, vmem_limit_bytes=VMEM_LIMIT)


def _rms(x, g):
    ms = jnp.mean(x * x, axis=-1, keepdims=True)
    return x * lax.rsqrt(ms + NORM_EPS) * g


def _bdot(a, b):
    return jnp.dot(a, b, preferred_element_type=F32)


def _bdot_nt(a, b):
    return lax.dot_general(a, b, (((1,), (1,)), ((), ())), preferred_element_type=F32)


def _cast_specs(ws, layer, nsteps, step_of):
    ins, outs, shapes = [], [], []
    for w in ws:
        _, r, c = w.shape
        rb = r // nsteps
        assert rb * nsteps == r and rb % (2 * SUBLANES) == 0
        ins.append(pl.BlockSpec((None, rb, c), lambda *g: (layer, step_of(*g), 0)))
        outs.append(pl.BlockSpec((rb, c), lambda *g: (step_of(*g), 0)))
        shapes.append(jax.ShapeDtypeStruct((r, c), BF16))
    return ins, outs, shapes


def _side_cast(in_refs, out_refs):
    for src, dst in zip(in_refs, out_refs):
        dst[...] = src[...].astype(BF16)


def _rope_apply(x, c, sa, sb):
    half = ROPE_DIM // 2
    return x * c + pltpu.roll(x, LANES - half, axis=1) * sa + pltpu.roll(x, half, axis=1) * sb


def _inproj_kernel(x_ref, g_ref, w_ref, o_ref, h_ref):
    @pl.when(pl.program_id(1) == 0)
    def _():
        h_ref[...] = _rms(x_ref[...], g_ref[...]).astype(BF16)

    o_ref[...] = _bdot(h_ref[...], w_ref[...])


def _inproj(x, g, w_all, layer):
    t, d = x.shape
    n = N_SMALL
    tm, tn = min(TM_PROJ, t), TN_PROJ
    return pl.pallas_call(
        _inproj_kernel,
        grid=(t // tm, n // tn),
        in_specs=[
            pl.BlockSpec((tm, d), lambda i, j: (i, 0)),
            pl.BlockSpec((1, d), lambda i, j: (0, 0)),
            pl.BlockSpec((None, d, tn), lambda i, j: (layer, 0, j)),
        ],
        out_specs=pl.BlockSpec((tm, tn), lambda i, j: (i, j)),
        out_shape=jax.ShapeDtypeStruct((t, n), F32),
        scratch_shapes=[pltpu.VMEM((tm, d), BF16)],
        compiler_params=_cparams(("parallel", "arbitrary")),
        name="inproj",
    )(x, g, w_all)


def _rope_kernel(pos_ref, cos_ref, sa_ref, sb_ref):
    pos = pos_ref[0].astype(F32)
    lane = lax.broadcasted_iota(jnp.int32, (SUBLANES, LANES), 1)
    d = lane % HEAD_DIM
    half = ROPE_DIM // 2
    fi = (d % half).astype(F32) / half
    inv = jnp.power(jnp.full((SUBLANES, LANES), ROPE_THETA, F32), -fi)[0:1]
    d1 = d[0:1]
    ang = pos * inv
    c, s = jnp.cos(ang), jnp.sin(ang)
    cos_ref[0] = jnp.where(d1 < ROPE_DIM, c, 1.0)
    sa_ref[0] = jnp.where(d1 < half, -s, 0.0)
    sb_ref[0] = jnp.where((d1 >= half) & (d1 < ROPE_DIM), s, 0.0)


def _rope_tables(positions):
    b, l = positions.shape
    spec = pl.BlockSpec((1, l, LANES), lambda i: (i, 0, 0))
    shp = jax.ShapeDtypeStruct((b, l, LANES), F32)
    return pl.pallas_call(
        _rope_kernel,
        grid=(b,),
        in_specs=[pl.BlockSpec((1, l, 1), lambda i: (i, 0, 0))],
        out_specs=[spec, spec, spec],
        out_shape=[shp, shp, shp],
        compiler_params=_cparams(("parallel",)),
        name="rope_tables",
    )(positions.reshape(b, l, 1))


def _band_bias(strict_prev):
    t = ATT_BLOCK
    qi = np.arange(2 * t)[:, None] % t
    kj = np.arange(2 * t)[None, :]
    own = (kj >= t) & (kj - t <= qi)
    prev = (kj < t) & (kj >= qi + (1 if strict_prev else 0))
    ninf = np.float32(-np.inf)
    return np.stack([np.where(own | prev, 0.0, ninf), np.where(own, 0.0, ninf)]).astype(np.float32)


def _band_block(q, kp, ko, vp, vo, bias):
    t = ATT_BLOCK
    lane = lax.broadcasted_iota(jnp.int32, (t, LANES), 1)
    in_a = lane < HEAD_DIM
    q2 = jnp.concatenate([jnp.where(in_a, q, 0.0), jnp.where(in_a, 0.0, q)], axis=0).astype(BF16)
    s = _bdot_nt(q2, jnp.concatenate([kp, ko], axis=0).astype(BF16)) + bias
    m = jnp.max(s, axis=-1, keepdims=True)
    p = jnp.exp(s - m).astype(BF16)
    ones = jnp.ones((2 * t, LANES), BF16)
    vcat = jnp.concatenate([jnp.concatenate([vp, vo], axis=0).astype(BF16), ones], axis=1)
    od = _bdot(p, vcat)
    o = jnp.where(in_a, od[:t, :LANES], od[t:, :LANES])
    df = jnp.where(in_a, od[:t, LANES:], od[t:, LANES:])
    mf = jnp.where(in_a, m[:t], m[t:])
    return mf, df, o


def _dil_kernel(q_ref, k_ref, v_ref, c_ref, sa_ref, sb_ref, bias_ref, o_ref, qs, ks, ms, ds, os_):
    l = q_ref.shape[1]
    c, sa, sb = c_ref[0], sa_ref[0], sb_ref[0]
    qs[...] = _rope_apply(q_ref[0], c, sa, sb) * (HEAD_DIM**-0.5)
    ks[...] = _rope_apply(k_ref[0], c, sa, sb)
    t = ATT_BLOCK
    nblk = DIL_SPAN // t

    for sblk in range(l // DIL_SPAN):
        base = sblk * DIL_SPAN
        for ci, (_, dil) in enumerate(DIL_CONFIGS):

            def body(idx, carry, dil=dil, ci=ci, base=base):
                r = idx % dil
                n = idx // dil
                loc = r + dil * t * n
                start = base + loc
                has_prev = start >= dil * t
                pstart = jnp.where(has_prev, start - dil * t, start)
                if dil == 1:
                    start = pl.multiple_of(start, t)
                    pstart = pl.multiple_of(pstart, t)
                    loc = pl.multiple_of(loc, t)
                    rows, prow, lrow = pl.ds(start, t), pl.ds(pstart, t), pl.ds(loc, t)
                else:
                    rows = pl.ds(start, t, stride=dil)
                    prow = pl.ds(pstart, t, stride=dil)
                    lrow = pl.ds(loc, t, stride=dil)
                mf, df, o = _band_block(
                    qs[rows, :], ks[prow, :], ks[rows, :], v_ref[0, prow, :], v_ref[0, rows, :],
                    bias_ref[jnp.where(has_prev, 0, 1)],
                )
                ms[ci, lrow, :] = mf
                ds[ci, lrow, :] = df
                os_[ci, lrow, :] = o
                return carry

            lax.fori_loop(0, nblk, body, 0, unroll=ATT_UNROLL)

        m0, m1, m2 = ms[0], ms[1], ms[2]
        mx = jnp.maximum(jnp.maximum(m0, m1), m2)
        w0, w1, w2 = jnp.exp(m0 - mx), jnp.exp(m1 - mx), jnp.exp(m2 - mx)
        den = w0 * ds[0] + w1 * ds[1] + w2 * ds[2]
        num = w0 * os_[0] + w1 * os_[1] + w2 * os_[2]
        o_ref[0, base : base + DIL_SPAN, :] = (num / den).astype(o_ref.dtype)


def _dilated(proj3, tabs):
    b, l, _ = proj3.shape
    assert l % DIL_SPAN == 0
    bias = _band_bias(False)

    def col(c0):
        return pl.BlockSpec((1, l, LANES), lambda i, p: (i, 0, c0 // LANES + p))

    tab = pl.BlockSpec((1, l, LANES), lambda i, p: (i, 0, 0))
    return pl.pallas_call(
        _dil_kernel,
        grid=(b, BRANCH_WIDTH // LANES),
        in_specs=[col(C_CQ), col(C_CK), col(C_CV), tab, tab, tab, pl.BlockSpec(bias.shape, lambda i, p: (0, 0, 0))],
        out_specs=pl.BlockSpec((1, l, LANES), lambda i, p: (i, 0, p)),
        out_shape=jax.ShapeDtypeStruct((b, l, BRANCH_WIDTH), BF16),
        scratch_shapes=[pltpu.VMEM((l, LANES), F32)] * 2
        + [pltpu.VMEM((len(DIL_CONFIGS), DIL_SPAN, LANES), F32)] * 3,
        compiler_params=_cparams(("parallel", "parallel")),
        name="dilated_attn",
    )(proj3, proj3, proj3, *tabs, jnp.asarray(bias))


def _swa_kernel(*refs, n_cast):
    q_ref, k_ref, v_ref, c_ref, sa_ref, sb_ref, bias_ref, sink_ref = refs[:8]
    o_ref = refs[8 + n_cast]
    qs, ks, vs = refs[-3:]
    _side_cast(refs[8 : 8 + n_cast], refs[9 + n_cast : 9 + 2 * n_cast])
    l = q_ref.shape[1]
    c, sa, sb = c_ref[0], sa_ref[0], sb_ref[0]
    qs[...] = _rope_apply(q_ref[0], c, sa, sb) * (HEAD_DIM**-0.5)
    g = pl.program_id(1) // (SWA_HEADS // SWA_KV_HEADS // 2)
    lane = lax.broadcasted_iota(jnp.int32, (l, LANES), 1)
    keep = (lane // HEAD_DIM) == g
    k, v = _rope_apply(k_ref[0], c, sa, sb), v_ref[0]
    ks[...] = jnp.where(keep, k, pltpu.roll(k, HEAD_DIM, axis=1))
    vs[...] = jnp.where(keep, v, pltpu.roll(v, HEAD_DIM, axis=1))
    sink = sink_ref[0]
    t = ATT_BLOCK

    def body(n, carry):
        start = pl.multiple_of(n * t, t)
        has_prev = n > 0
        pstart = pl.multiple_of(jnp.where(has_prev, start - t, start), t)
        rows, prow = pl.ds(start, t), pl.ds(pstart, t)
        mf, df, o = _band_block(qs[rows, :], ks[prow, :], ks[rows, :], vs[prow, :], vs[rows, :],
                                bias_ref[jnp.where(has_prev, 0, 1)])
        lse = mf + jnp.log(df)
        o_ref[0, rows, :] = (o / df * jax.nn.sigmoid(lse - sink)).astype(o_ref.dtype)
        return carry

    lax.fori_loop(0, l // t, body, 0, unroll=ATT_UNROLL)


def _swa(proj3, tabs, sinks, cast=(), layer=0):
    b, l, _ = proj3.shape
    npair = BRANCH_WIDTH // LANES
    sink_l = jnp.repeat(sinks.astype(F32), HEAD_DIM).reshape(npair, 1, LANES)
    bias = _band_bias(True)
    tab = pl.BlockSpec((1, l, LANES), lambda i, p: (i, 0, 0))
    c_in, c_out, c_shape = _cast_specs(cast, layer, b * npair, lambda i, p: i * npair + p)
    return pl.pallas_call(
        functools.partial(_swa_kernel, n_cast=len(cast)),
        grid=(b, npair),
        in_specs=[
            pl.BlockSpec((1, l, LANES), lambda i, p: (i, 0, C_SQ // LANES + p)),
            pl.BlockSpec((1, l, LANES), lambda i, p: (i, 0, C_SK // LANES)),
            pl.BlockSpec((1, l, LANES), lambda i, p: (i, 0, C_SV // LANES)),
            tab,
            tab,
            tab,
            pl.BlockSpec(bias.shape, lambda i, p: (0, 0, 0)),
            pl.BlockSpec((1, 1, LANES), lambda i, p: (p, 0, 0)),
        ]
        + c_in,
        out_specs=[pl.BlockSpec((1, l, LANES), lambda i, p: (i, 0, p))] + c_out,
        out_shape=[jax.ShapeDtypeStruct((b, l, BRANCH_WIDTH), BF16)] + c_shape,
        scratch_shapes=[pltpu.VMEM((l, LANES), F32)] * 3,
        compiler_params=_cparams(("parallel", "parallel")),
        name="swa_attn",
    )(proj3, proj3, proj3, *tabs, jnp.asarray(bias), sink_l, *cast)


@functools.lru_cache(maxsize=None)
def _gla_consts():
    ch = GLA_CH
    r = np.arange(ch)[:, None]
    j = np.arange(ch)[None, :]
    dstack = (j <= r).astype(np.float32)
    lm = []
    for m in GLA_LEVELS[:-1]:
        mk = (((r // m) % 2 == 1) & ((j // m) == (r // m) - 1)).astype(np.float32)
        lm.append(np.tile(mk, (GLA_HEADS, 1)))
    lmask = np.stack(lm)
    dmask = np.tile(((r // GLA_SUB) == (j // GLA_SUB)).astype(np.float32), (1, GLA_HEADS))
    nsub = ch // GLA_SUB
    e = np.zeros((GLA_SUB, GLA_HEADS * GLA_DK, GLA_HEADS * ch), np.float32)
    for u in range(GLA_SUB):
        for h in range(GLA_HEADS):
            for s in range(nsub):
                e[u, h * GLA_DK : (h + 1) * GLA_DK, h * ch + GLA_SUB * s + u] = 1.0
    rr = np.arange(GLA_HEADS * GLA_DV)[:, None]
    cc = np.arange(GLA_HEADS * GLA_DK)[None, :]
    bd = ((rr // GLA_DV) == (cc // GLA_DK)).astype(np.float32)
    return dstack, lmask, dmask, e, bd


def _split3(x):
    hi = x.astype(BF16)
    r1 = x - hi.astype(F32)
    mid = r1.astype(BF16)
    lo = (r1 - mid.astype(F32)).astype(BF16)
    return hi, mid, lo


def _bcast_grp(x, m, u):
    r, w = x.shape
    x3 = x.reshape(r // m, m, w)
    return jnp.broadcast_to(x3[:, u : u + 1, :], x3.shape).reshape(r, w)


def _bcast_sub(x, u):
    return _bcast_grp(x, GLA_SUB, u)


def _gla_kernel(*refs, n_cast):
    (q_ref, k_ref, v_ref, r_ref, glr_ref, a2_ref, ab_ref, ng_ref, dst_ref, lmask_ref, dmask_ref, e_ref,
     bd_ref) = refs[:13]
    o_ref, st_ref = refs[13 + n_cast], refs[-1]
    _side_cast(refs[13 : 13 + n_cast], refs[14 + n_cast : 14 + 2 * n_cast])
    ch = GLA_CH
    rs = q_ref.shape[0]
    ncs = rs // ch
    nlev = len(GLA_LEVELS)
    hk = GLA_HEADS * GLA_DK

    @pl.when(pl.program_id(1) == 0)
    def _():
        st_ref[...] = jnp.zeros_like(st_ref)

    q = q_ref[...] * (GLA_DK**-0.5)
    k = k_ref[...]
    z = _bdot(glr_ref[...].astype(BF16), a2_ref[...]) + ab_ref[...]
    g = (jnp.minimum(z, 0.0) - jnp.log1p(jnp.exp(-jnp.abs(z)))) * (1.0 / GLA_TAU)
    g3 = _split3(g)
    dst = dst_ref[...]

    cums, excl = [], []
    for c in range(ncs):
        rows = slice(c * ch, (c + 1) * ch)
        cums.append(_bdot(dst, g3[0][rows]) + _bdot(dst, g3[1][rows]) + _bdot(dst, g3[2][rows]))
        excl.append(cums[c] - g[rows])

    def eq(c, li):
        m = GLA_LEVELS[li]
        return cums[c] - _bcast_grp(excl[c], m, 0)

    def ek(c, li):
        m = GLA_LEVELS[li]
        return _bcast_grp(cums[c], m, m - 1) - cums[c]

    cs = jnp.concatenate([eq(c, 0) for c in range(ncs)], axis=0)
    tsub = lax.broadcasted_iota(jnp.int32, (rs, hk), 0) % GLA_SUB
    arep = jnp.zeros((rs, GLA_HEADS * ch), F32)
    for u in range(GLA_SUB):
        dec = jnp.exp(jnp.where(tsub >= u, cs - _bcast_sub(cs, u), -jnp.inf))
        p = q * _bcast_sub(k, u) * dec
        arep = arep + _bdot(p.astype(BF16), e_ref[u])
    arep = arep * jnp.concatenate([dmask_ref[...]] * ncs, axis=0)

    lane_k = lax.broadcasted_iota(jnp.int32, (ch, hk), 1) // GLA_DK
    ng = ng_ref[...]
    bd = bd_ref[...]
    for c in range(ncs):
        rows = slice(c * ch, (c + 1) * ch)
        qc, kc = q[rows], k[rows]
        vc = v_ref[rows, :]
        aoff = jnp.zeros((GLA_HEADS * ch, ch), F32)
        for li in range(nlev - 1):
            qe = qc * jnp.exp(eq(c, li))
            ke = (kc * jnp.exp(ek(c, li))).astype(BF16)
            qst = jnp.concatenate([jnp.where(lane_k == h, qe, 0.0) for h in range(GLA_HEADS)], axis=0).astype(BF16)
            aoff = aoff + _bdot_nt(qst, ke) * lmask_ref[li]
        cum = eq(c, nlev - 1)
        st = st_ref[...]
        o_inter = _bdot_nt((qc * jnp.exp(cum)).astype(BF16), st.astype(BF16))
        vb = vc.astype(BF16)
        outs = []
        for h in range(GLA_HEADS):
            a_h = aoff[h * ch : (h + 1) * ch] + arep[rows, h * ch : (h + 1) * ch]
            o_h = _bdot(a_h.astype(BF16), vb[:, h * GLA_DV : (h + 1) * GLA_DV]) + o_inter[:, h * GLA_DV : (h + 1) * GLA_DV]
            ms = jnp.mean(o_h * o_h, axis=-1, keepdims=True)
            outs.append(o_h * lax.rsqrt(ms + NORM_EPS))
        o = jnp.concatenate(outs, axis=1) * ng
        rc = r_ref[rows, :]
        o_ref[rows, :] = (o * (rc * jax.nn.sigmoid(rc))).astype(o_ref.dtype)
        ke_last = (kc * jnp.exp(ek(c, nlev - 1))).astype(BF16)
        kv = _bdot(vc.T.astype(BF16), ke_last)
        st_ref[...] = st * jnp.exp(cum[ch - 1 : ch, :]) + kv * bd


def _gla(proj, b, a2, ab, ng, cast=(), layer=0):
    t = proj.shape[0]
    l = t // b
    rs = GLA_CH * GLA_NCS
    assert l % rs == 0
    ns = l // rs
    dstack, lmask, dmask, e, bd = _gla_consts()
    c_in, c_out, c_shape = _cast_specs(cast, layer, b * ns, lambda i, s: i * ns + s)
    a2p = jnp.zeros((LANES, GLA_HEADS * GLA_DK), BF16).at[:GLA_LOWRANK].set(a2.astype(BF16))

    def rowblk(w, c0):
        return pl.BlockSpec((rs, w), lambda i, s: (i * ns + s, c0 // w))

    def full(shape):
        nd = len(shape)
        return pl.BlockSpec(shape, lambda i, s: (0,) * nd)

    hk, hv = GLA_HEADS * GLA_DK, GLA_HEADS * GLA_DV
    return pl.pallas_call(
        functools.partial(_gla_kernel, n_cast=len(cast)),
        grid=(b, ns),
        in_specs=[
            rowblk(hk, C_GQ),
            rowblk(hk, C_GK),
            rowblk(hv, C_GV),
            rowblk(hv, C_GR),
            rowblk(LANES, C_GLR),
            full((LANES, hk)),
            full((1, hk)),
            full((1, hv)),
            full(dstack.shape),
            full(lmask.shape),
            full(dmask.shape),
            full(e.shape),
            full(bd.shape),
        ]
        + c_in,
        out_specs=[pl.BlockSpec((rs, hv), lambda i, s: (i * ns + s, 0))] + c_out,
        out_shape=[jax.ShapeDtypeStruct((t, hv), BF16)] + c_shape,
        scratch_shapes=[pltpu.VMEM((hv, hk), F32)],
        compiler_params=_cparams(("parallel", "arbitrary")),
        name="gla",
    )(
        proj, proj, proj, proj, proj, a2p, ab.reshape(1, hk).astype(F32), ng.reshape(1, hv).astype(F32),
        jnp.asarray(dstack, BF16), jnp.asarray(lmask), jnp.asarray(dmask), jnp.asarray(e, BF16), jnp.asarray(bd),
        *cast,
    )


def _s5_params(lam_re, lam_im, log_dt, b_re, b_im, c_re, c_im, d, nch):
    f = F32
    cs = S5_CH
    dt = jnp.exp(log_dt.astype(f))[:, None]
    lr, li = lam_re.astype(f), lam_im.astype(f)
    mag = jnp.exp(lr * dt)
    ab_re, ab_im = mag * jnp.cos(li * dt), mag * jnp.sin(li * dt)
    den = lr * lr + li * li
    z_re = ((ab_re - 1.0) * lr + ab_im * li) / den
    z_im = (ab_im * lr - (ab_re - 1.0) * li) / den
    br, bi = b_re.astype(f), b_im.astype(f)
    bb_re = z_re[..., None] * br - z_im[..., None] * bi
    bb_im = z_re[..., None] * bi + z_im[..., None] * br

    def apow(p):
        p = jnp.asarray(p, f)[:, None, None]
        m = jnp.exp(p * (lr * dt))
        return m * jnp.cos(p * (li * dt)), m * jnp.sin(p * (li * dt))

    cr, ci = c_re.astype(f), c_im.astype(f)
    p_re, p_im = apow(np.arange(cs + 1))
    nb, gpb = S5_GROUPS // S5_GPB, S5_GPB
    hw = gpb * S5_STATE
    eye = jnp.eye(gpb, dtype=f)

    def b_blockdiag(x):
        x = x.reshape(nb, gpb, S5_STATE, S5_GROUP).transpose(0, 1, 3, 2)
        x = (x[:, :, :, None, :] * eye[None, :, None, :, None]).reshape(nb, LANES, hw)
        return jnp.concatenate([x, x], axis=-1)

    def c_blockdiag(x):
        x = x.reshape(nb, gpb, S5_GROUP, S5_STATE).transpose(0, 1, 3, 2)
        return (x[:, :, :, None, :] * eye[None, :, None, :, None]).reshape(nb, hw, LANES)

    def lanes(a, b_):
        p = a.shape[0]
        return jnp.concatenate([a.reshape(p, nb, hw), b_.reshape(p, nb, hw)], axis=-1).transpose(1, 0, 2)

    rev = np.arange(cs - 1, -1, -1)
    pr, pi = p_re[rev], p_im[rev]
    pa, pb = lanes(pr, pi), lanes(-pi, pr)
    acol_re = jnp.broadcast_to(ab_re.reshape(nb, hw, 1), (nb, hw, LANES))
    acol_im = jnp.broadcast_to(ab_im.reshape(nb, hw, 1), (nb, hw, LANES))
    nstep = max(1, int(math.log2(nch)))
    s_re, s_im = apow(cs * (2 ** np.arange(nstep)))
    ar = s_re.reshape(nstep, nb, hw).transpose(1, 0, 2)
    ai = s_im.reshape(nstep, nb, hw).transpose(1, 0, 2)
    dd = d.astype(f).reshape(nb, 1, LANES)
    return (b_blockdiag(bb_re), b_blockdiag(bb_im), c_blockdiag(cr), c_blockdiag(ci), acol_re, acol_im,
            pa, pb, ar, ai, dd)


def _s5_kernel(u_ref, br_ref, bi_ref, cr_ref, ci_ref, acr_ref, aci_ref, pa_ref, pb_ref, ar_ref, ai_ref, d_ref,
               y_ref, mb_s, mc_s, kc_s, *, nchb):
    cs = S5_CH
    t = u_ref.shape[0]
    nch = t // cs
    hw = S5_GPB * S5_STATE
    br, bi = br_ref[0], bi_ref[0]
    pa, pb = pa_ref[0], pb_ref[0]
    for j in range(cs):
        mb_s[j * LANES : (j + 1) * LANES, :] = (br * pa[j : j + 1] + bi * pb[j : j + 1]).astype(BF16)
    x_re, x_im = cr_ref[0], ci_ref[0]
    kc = _bdot(mb_s[...], jnp.concatenate([x_re, -x_im], axis=0).astype(BF16)).astype(BF16)
    a_re, a_im = acr_ref[0], aci_ref[0]
    for tt in range(cs):
        x_re, x_im = x_re * a_re - x_im * a_im, x_re * a_im + x_im * a_re
        mc_s[:hw, tt * LANES : (tt + 1) * LANES] = x_re.astype(BF16)
        mc_s[hw:, tt * LANES : (tt + 1) * LANES] = (-x_im).astype(BF16)
    gl = cs // S5_TSPLIT * LANES
    for qq in range(S5_TSPLIT):
        kc_s[:, qq * LANES : (qq + 1) * LANES] = kc[(S5_TSPLIT - 1 - qq) * gl : (S5_TSPLIT - qq) * gl]
    ucat = jnp.concatenate([u_ref[pl.ds(j, nch, stride=cs), :] for j in range(cs)], axis=1).astype(BF16)
    e_all = _bdot(ucat, mb_s[...])
    cidx = lax.broadcasted_iota(jnp.int32, (nch, LANES), 0) % nchb
    ar, ai = ar_ref[0], ai_ref[0]
    hp_re, hp_im = [], []
    for g in range(hw // LANES):
        lo = g * LANES
        e_re, e_im = e_all[:, lo : lo + LANES], e_all[:, hw + lo : hw + lo + LANES]
        for kstep in range(ar.shape[0]):
            s = 2**kstep
            if s >= nchb:
                break
            s_re = jnp.where(cidx >= s, pltpu.roll(e_re, s, axis=0), 0.0)
            s_im = jnp.where(cidx >= s, pltpu.roll(e_im, s, axis=0), 0.0)
            k_re, k_im = ar[kstep : kstep + 1, lo : lo + LANES], ai[kstep : kstep + 1, lo : lo + LANES]
            e_re, e_im = e_re + s_re * k_re - s_im * k_im, e_im + s_re * k_im + s_im * k_re
        hp_re.append(jnp.where(cidx >= 1, pltpu.roll(e_re, 1, axis=0), 0.0).astype(BF16))
        hp_im.append(jnp.where(cidx >= 1, pltpu.roll(e_im, 1, axis=0), 0.0).astype(BF16))
    ycar = _bdot(jnp.concatenate(hp_re + hp_im, axis=1), mc_s[...])
    for tt in range(cs):
        y_ref[pl.ds(tt, nch, stride=cs), :] = ycar[:, tt * LANES : (tt + 1) * LANES]

    rt = min(S5_RT, t)
    rmod = lax.broadcasted_iota(jnp.int32, (rt, LANES), 0) % cs
    dvec = d_ref[0]
    glag = cs // S5_TSPLIT

    def tile(i, carry):
        rows = pl.ds(pl.multiple_of(i * rt, rt), rt)
        ut = u_ref[rows, :]
        ush = [jnp.where(rmod >= tau, pltpu.roll(ut, tau, axis=0), 0.0) for tau in range(glag - 1, 0, -1)] + [ut]
        zz = _bdot(jnp.concatenate(ush, axis=1).astype(BF16), kc_s[...])
        acc = y_ref[rows, :] + dvec * ut + zz[:, :LANES]
        for qq in range(1, S5_TSPLIT):
            part = pltpu.roll(zz[:, qq * LANES : (qq + 1) * LANES], qq * glag, axis=0)
            acc = acc + jnp.where(rmod >= qq * glag, part, 0.0)
        y_ref[rows, :] = acc
        return carry

    lax.fori_loop(0, t // rt, tile, 0)


def _s5_core(proj, b, params):
    t = proj.shape[0]
    nchb = t // b // S5_CH
    sw = S5_GPB * 2 * S5_STATE

    def bspec(a):
        return pl.BlockSpec((1,) + a.shape[1:], lambda p: (p, 0, 0))

    return pl.pallas_call(
        functools.partial(_s5_kernel, nchb=nchb),
        grid=(S5_GROUPS // S5_GPB,),
        in_specs=[pl.BlockSpec((t, LANES), lambda p: (0, C_S5 // LANES + p))] + [bspec(a) for a in params],
        out_specs=pl.BlockSpec((t, LANES), lambda p: (0, p)),
        out_shape=jax.ShapeDtypeStruct((t, BRANCH_WIDTH), F32),
        scratch_shapes=[
            pltpu.VMEM((S5_CH * LANES, sw), BF16),
            pltpu.VMEM((sw, S5_CH * LANES), BF16),
            pltpu.VMEM((S5_CH // S5_TSPLIT * LANES, S5_TSPLIT * LANES), BF16),
        ],
        compiler_params=_cparams(("parallel",)),
        name="s5_core",
    )(proj, *params)


def _merge_kernel(
    x_ref, g1_ref, gla_ref, ys5_ref, dil_ref, swa_ref, gluw_ref, glub_ref, wg0, wg1, wg2, wg3, wb_ref, wo_ref,
    o_ref, h_s, s5_s, acc_s,
):
    j = pl.program_id(1)

    @pl.when(j == 0)
    def _():
        h_s[...] = _rms(x_ref[...], g1_ref[...]).astype(BF16)
        zz = jax.nn.gelu(ys5_ref[...])
        gate = jax.nn.sigmoid(_bdot(zz.astype(BF16), gluw_ref[...]) + glub_ref[...])
        s5_s[...] = (zz * gate).astype(BF16)
        acc_s[...] = jnp.zeros_like(acc_s)

    h = h_s[...]
    branches = (gla_ref[...], s5_s[...], dil_ref[...], swa_ref[...])
    mixed = None
    for m, (wg, br) in enumerate(zip((wg0, wg1, wg2, wg3), branches)):
        term = jax.nn.sigmoid(_bdot(h, wg[...])) * _bdot(br, wb_ref[m])
        mixed = term if mixed is None else mixed + term
    acc_s[...] += _bdot(mixed.astype(BF16), wo_ref[...])

    @pl.when(j == pl.num_programs(1) - 1)
    def _():
        o_ref[...] = x_ref[...] + acc_s[...]


def _merge(x, g1, o_gla, y_s5, o_dil, o_swa, gluw, glub, w_all, wb, wo, layer):
    t, d = x.shape
    tm, tn = min(TM_MERGE, t), TN_MERGE
    nj = d // tn
    bw = BRANCH_WIDTH

    def rowblk(w):
        return pl.BlockSpec((tm, w), lambda i, j: (i, 0))

    def gate_spec(m):
        return pl.BlockSpec((None, d, tn), lambda i, j, m=m: (layer, 0, N_SMALL // tn + m * nj + j))

    return pl.pallas_call(
        _merge_kernel,
        grid=(t // tm, nj),
        in_specs=[
            rowblk(d),
            pl.BlockSpec((1, d), lambda i, j: (0, 0)),
            rowblk(bw),
            rowblk(bw),
            rowblk(bw),
            rowblk(bw),
            pl.BlockSpec((None, bw, bw), lambda i, j: (layer, 0, 0)),
            pl.BlockSpec((1, bw), lambda i, j: (0, 0)),
            gate_spec(0),
            gate_spec(1),
            gate_spec(2),
            gate_spec(3),
            pl.BlockSpec((None, N_BRANCH, bw, tn), lambda i, j: (layer, 0, 0, j)),
            pl.BlockSpec((None, tn, d), lambda i, j: (layer, j, 0)),
        ],
        out_specs=rowblk(d),
        out_shape=jax.ShapeDtypeStruct((t, d), F32),
        scratch_shapes=[pltpu.VMEM((tm, d), BF16), pltpu.VMEM((tm, bw), BF16), pltpu.VMEM((tm, d), F32)],
        compiler_params=_cparams(("parallel", "arbitrary")),
        name="merge",
    )(x, g1, o_gla, y_s5, o_dil, o_swa, gluw, glub, w_all, w_all, w_all, w_all, wb, wo)


def _ffn_kernel(x_ref, g2_ref, wg_ref, wu_ref, wd_ref, gf_ref, o_ref, h_s, *, final_norm):
    j = pl.program_id(1)

    @pl.when(j == 0)
    def _():
        h_s[...] = _rms(x_ref[...], g2_ref[...]).astype(BF16)
        o_ref[...] = jnp.zeros_like(o_ref)

    h = h_s[...]
    gate = _bdot(h, wg_ref[...])
    act = ((gate * jax.nn.sigmoid(gate)) * _bdot(h, wu_ref[...])).astype(BF16)
    d = o_ref.shape[1]
    for c0 in range(0, d, FFN_DOWN_CHUNK):
        cols = slice(c0, c0 + FFN_DOWN_CHUNK)
        o_ref[:, cols] += _bdot(act, wd_ref[:, cols])

    @pl.when(j == pl.num_programs(1) - 1)
    def _():
        y = x_ref[...] + o_ref[...]
        o_ref[...] = _rms(y, gf_ref[...]) if final_norm else y


def _ffn(x, g2, wg, wu, wd, gf, final_norm):
    t, d = x.shape
    fh = wg.shape[-1]
    tm, tf = min(TM_FFN, t), TF_FFN
    return pl.pallas_call(
        functools.partial(_ffn_kernel, final_norm=final_norm),
        grid=(t // tm, fh // tf),
        in_specs=[
            pl.BlockSpec((tm, d), lambda i, j: (i, 0)),
            pl.BlockSpec((1, d), lambda i, j: (0, 0)),
            pl.BlockSpec((d, tf), lambda i, j: (0, j)),
            pl.BlockSpec((d, tf), lambda i, j: (0, j)),
            pl.BlockSpec((tf, d), lambda i, j: (j, 0)),
            pl.BlockSpec((1, d), lambda i, j: (0, 0)),
        ],
        out_specs=pl.BlockSpec((tm, d), lambda i, j: (i, 0)),
        out_shape=jax.ShapeDtypeStruct((t, d), F32),
        scratch_shapes=[pltpu.VMEM((tm, d), BF16)],
        compiler_params=_cparams(("parallel", "arbitrary")),
        name="ffn",
    )(x, g2, wg, wu, wd, gf)


def _pack_kernel(w_ref, o_ref):
    x = w_ref[...]
    row = lax.broadcasted_iota(jnp.int32, x.shape, 0)
    keep = jnp.where(pl.program_id(1) == C_GLR // PACK_CB, GLA_LOWRANK, PACK_CB)
    o_ref[...] = jnp.where(row < keep, x, 0.0).T.astype(BF16)


def _pack_row0(c):
    cb = PACK_CB
    u = GLA_LOWRANK
    front = jnp.where(c < GLR_ORIG // cb, c * (cb // u), c * (cb // u) + 1)
    back = jnp.where(c == C_GLR // cb, GLR_ORIG // u, c * (cb // u) - (N_SMALL - N_ORIG_SMALL) // u)
    return jnp.where(c < C_GLR // cb, front, back) * u


def _pack_w_in(w_in):
    depth, d, d_in = w_in.shape
    n_out = N_SMALL + d_in - N_ORIG_SMALL
    cb = PACK_CB
    assert GLR_ORIG % cb == 0 and C_GLR % cb == 0 and N_SMALL % cb == 0 and n_out % cb == 0
    assert C_GLR == N_ORIG_SMALL - GLA_LOWRANK and C_GLR + cb == N_SMALL
    return pl.pallas_call(
        _pack_kernel,
        grid=(depth, n_out // cb),
        in_specs=[pl.BlockSpec((None, pl.Element(cb), pl.Element(d)), lambda i, c: (i, _pack_row0(c), 0))],
        out_specs=pl.BlockSpec((None, d, cb), lambda i, c: (i, 0, c)),
        out_shape=jax.ShapeDtypeStruct((depth, d, n_out), BF16),
        compiler_params=_cparams(("parallel", "parallel")),
        name="pack_w_in",
    )(jnp.swapaxes(w_in, 1, 2))


def kernel(x, positions, norm1_g, w_in, gla_a2, gla_a_b, gla_norm_g, s5_lambda_re, s5_lambda_im, s5_log_dt, s5_b_re, s5_b_im, s5_c_re, s5_c_im, s5_d, s5_glu_w, s5_glu_b, swa_sinks, w_branch, w_out, norm2_g, w_ffn_gate, w_ffn_up, w_ffn_down, final_norm_g):
    b, l, d = x.shape
    t = b * l
    depth = w_in.shape[0]
    xs = x.reshape(t, d).astype(F32)
    w_all = _pack_w_in(w_in)
    w_br = w_branch.astype(BF16)
    w_o = w_out.astype(BF16)
    glu_w = s5_glu_w.astype(BF16)
    tabs = _rope_tables(positions)
    gf = final_norm_g.reshape(1, d).astype(F32)
    for i in range(depth):
        g1 = norm1_g[i].reshape(1, d).astype(F32)
        proj = _inproj(xs, g1, w_all, i)
        proj3 = proj.reshape(b, l, N_SMALL)
        o_gla, w_fg, w_fu = _gla(proj, b, gla_a2[i], gla_a_b[i], gla_norm_g[i], (w_ffn_gate, w_ffn_up), i)
        s5p = _s5_params(s5_lambda_re[i], s5_lambda_im[i], s5_log_dt[i], s5_b_re[i], s5_b_im[i], s5_c_re[i],
                         s5_c_im[i], s5_d[i], l // S5_CH)
        y_s5 = _s5_core(proj, b, s5p)
        o_dil = _dilated(proj3, tabs).reshape(t, BRANCH_WIDTH)
        o_swa, w_fd = _swa(proj3, tabs, swa_sinks[i], (w_ffn_down,), i)
        o_swa = o_swa.reshape(t, BRANCH_WIDTH)
        xs = _merge(xs, g1, o_gla, y_s5, o_dil, o_swa, glu_w, s5_glu_b[i].reshape(1, -1).astype(F32),
                    w_all, w_br, w_o, i)
        xs = _ffn(xs, norm2_g[i].reshape(1, d).astype(F32), w_fg, w_fu, w_fd, gf, i == depth - 1)
    return xs.reshape(b, l, d).astype(x.dtype)
```

```python
import functools
import math

import jax
import jax.numpy as jnp
import numpy as np
from jax import lax
from jax.experimental import pallas as pl
from jax.experimental.pallas import tpu as pltpu

F32 = jnp.float32
BF16 = jnp.bfloat16
HIGHEST = lax.Precision.HIGHEST

D_MODEL = 2048
DEPTH = 2
N_BRANCH = 4
BRANCH_WIDTH = 512
HEAD_DIM = 64
ATT_BLOCK = 128
ROPE_THETA = 500000.0
ROPE_DIM = HEAD_DIM // 4
NORM_EPS = 1e-6
GLA_HEADS = 4
GLA_DK = 64
GLA_DV = BRANCH_WIDTH // GLA_HEADS
GLA_LOWRANK = 16
GLA_TAU = 16.0
S5_GROUP = 16
S5_GROUPS = BRANCH_WIDTH // S5_GROUP
S5_STATE = 64
DIL_CONFIGS = ((128, 1), (512, 4), (2048, 16))
DIL_SPAN = ATT_BLOCK * 16
SWA_HEADS = BRANCH_WIDTH // HEAD_DIM
SWA_KV_HEADS = 2
SWA_WINDOW = 128
FFN_HIDDEN = -((-8 * D_MODEL) // (3 * 256)) * 256

LANES = 128
SUBLANES = 8
VMEM_LIMIT = 56 * 1024 * 1024

C_GQ, C_GK, C_GV, C_GR, C_S5 = 0, 256, 512, 1024, 1536
C_CQ, C_CK, C_CV, C_SQ, C_SK, C_SV, C_GLR = 2048, 2560, 3072, 3584, 4096, 4224, 4352
N_SMALL = 4608
N_ORIG_SMALL = 4368
GLR_ORIG = 1536

TM_PROJ, TN_PROJ = 1024, 768
TM_MERGE, TN_MERGE = 512, 512
TM_FFN, TF_FFN = 512, 512
FFN_DOWN_CHUNK = 512
GLA_CH = 128
GLA_NCS = 4
GLA_SUB = 8
GLA_LEVELS = (8, 16, 32, 64, 128)
PACK_CB = 256
PACK_PIECES = 2
S5_CH = 16
S5_GPB = LANES // S5_GROUP
S5_RT = 512
S5_TSPLIT = 4
ATT_UNROLL = 8


def _cparams(sem):
    return pltpu.CompilerParams(dimension_semantics=sem, vmem_limit_bytes=VMEM_LIMIT)


def _rms(x, g):
    ms = jnp.mean(x * x, axis=-1, keepdims=True)
    return x * lax.rsqrt(ms + NORM_EPS) * g


def _bdot(a, b):
    return jnp.dot(a, b, preferred_element_type=F32)


def _bdot_nt(a, b):
    return lax.dot_general(a, b, (((1,), (1,)), ((), ())), preferred_element_type=F32)


def _cast_specs(ws, layer, nsteps, step_of):
    ins, outs, shapes = [], [], []
    for w in ws:
        _, r, c = w.shape
        rb = r // nsteps
        assert rb * nsteps == r and rb % (2 * SUBLANES) == 0
        ins.append(pl.BlockSpec((None, rb, c), lambda *g: (layer, step_of(*g), 0)))
        outs.append(pl.BlockSpec((rb, c), lambda *g: (step_of(*g), 0)))
        shapes.append(jax.ShapeDtypeStruct((r, c), BF16))
    return ins, outs, shapes


def _side_cast(in_refs, out_refs):
    for src, dst in zip(in_refs, out_refs):
        dst[...] = src[...].astype(BF16)


def _rope_apply(x, c, sa, sb):
    half = ROPE_DIM // 2
    return x * c + pltpu.roll(x, LANES - half, axis=1) * sa + pltpu.roll(x, half, axis=1) * sb


def _inproj_kernel(x_ref, g_ref, w_ref, o_ref, h_ref):
    @pl.when(pl.program_id(1) == 0)
    def _():
        h_ref[...] = _rms(x_ref[...], g_ref[...]).astype(BF16)

    o_ref[...] = _bdot(h_ref[...], w_ref[...])


def _inproj(x, g, w_all, layer):
    t, d = x.shape
    n = N_SMALL
    tm, tn = min(TM_PROJ, t), TN_PROJ
    return pl.pallas_call(
        _inproj_kernel,
        grid=(t // tm, n // tn),
        in_specs=[
            pl.BlockSpec((tm, d), lambda i, j: (i, 0)),
            pl.BlockSpec((1, d), lambda i, j: (0, 0)),
            pl.BlockSpec((None, d, tn), lambda i, j: (layer, 0, j)),
        ],
        out_specs=pl.BlockSpec((tm, tn), lambda i, j: (i, j)),
        out_shape=jax.ShapeDtypeStruct((t, n), F32),
        scratch_shapes=[pltpu.VMEM((tm, d), BF16)],
        compiler_params=_cparams(("parallel", "arbitrary")),
        name="inproj",
    )(x, g, w_all)


def _rope_kernel(pos_ref, cos_ref, sa_ref, sb_ref):
    pos = pos_ref[0].astype(F32)
    lane = lax.broadcasted_iota(jnp.int32, (SUBLANES, LANES), 1)
    d = lane % HEAD_DIM
    half = ROPE_DIM // 2
    fi = (d % half).astype(F32) / half
    inv = jnp.power(jnp.full((SUBLANES, LANES), ROPE_THETA, F32), -fi)[0:1]
    d1 = d[0:1]
    ang = pos * inv
    c, s = jnp.cos(ang), jnp.sin(ang)
    cos_ref[0] = jnp.where(d1 < ROPE_DIM, c, 1.0)
    sa_ref[0] = jnp.where(d1 < half, -s, 0.0)
    sb_ref[0] = jnp.where((d1 >= half) & (d1 < ROPE_DIM), s, 0.0)


def _rope_tables(positions):
    b, l = positions.shape
    spec = pl.BlockSpec((1, l, LANES), lambda i: (i, 0, 0))
    shp = jax.ShapeDtypeStruct((b, l, LANES), F32)
    return pl.pallas_call(
        _rope_kernel,
        grid=(b,),
        in_specs=[pl.BlockSpec((1, l, 1), lambda i: (i, 0, 0))],
        out_specs=[spec, spec, spec],
        out_shape=[shp, shp, shp],
        compiler_params=_cparams(("parallel",)),
        name="rope_tables",
    )(positions.reshape(b, l, 1))


def _band_bias(strict_prev):
    t = ATT_BLOCK
    qi = np.arange(2 * t)[:, None] % t
    kj = np.arange(2 * t)[None, :]
    own = (kj >= t) & (kj - t <= qi)
    prev = (kj < t) & (kj >= qi + (1 if strict_prev else 0))
    ninf = np.float32(-np.inf)
    return np.stack([np.where(own | prev, 0.0, ninf), np.where(own, 0.0, ninf)]).astype(np.float32)


def _band_block(q, kp, ko, vp, vo, bias):
    t = ATT_BLOCK
    lane = lax.broadcasted_iota(jnp.int32, (t, LANES), 1)
    in_a = lane < HEAD_DIM
    q2 = jnp.concatenate([jnp.where(in_a, q, 0.0), jnp.where(in_a, 0.0, q)], axis=0).astype(BF16)
    s = _bdot_nt(q2, jnp.concatenate([kp, ko], axis=0).astype(BF16)) + bias
    m = jnp.max(s, axis=-1, keepdims=True)
    p = jnp.exp(s - m).astype(BF16)
    ones = jnp.ones((2 * t, LANES), BF16)
    vcat = jnp.concatenate([jnp.concatenate([vp, vo], axis=0).astype(BF16), ones], axis=1)
    od = _bdot(p, vcat)
    o = jnp.where(in_a, od[:t, :LANES], od[t:, :LANES])
    df = jnp.where(in_a, od[:t, LANES:], od[t:, LANES:])
    mf = jnp.where(in_a, m[:t], m[t:])
    return mf, df, o


def _dil_kernel(q_ref, k_ref, v_ref, c_ref, sa_ref, sb_ref, bias_ref, o_ref, qs, ks, ms, ds, os_):
    l = q_ref.shape[1]
    c, sa, sb = c_ref[0], sa_ref[0], sb_ref[0]
    qs[...] = _rope_apply(q_ref[0], c, sa, sb) * (HEAD_DIM**-0.5)
    ks[...] = _rope_apply(k_ref[0], c, sa, sb)
    t = ATT_BLOCK
    nblk = DIL_SPAN // t

    for sblk in range(l // DIL_SPAN):
        base = sblk * DIL_SPAN
        for ci, (_, dil) in enumerate(DIL_CONFIGS):

            def body(idx, carry, dil=dil, ci=ci, base=base):
                r = idx % dil
                n = idx // dil
                loc = r + dil * t * n
                start = base + loc
                has_prev = start >= dil * t
                pstart = jnp.where(has_prev, start - dil * t, start)
                if dil == 1:
                    start = pl.multiple_of(start, t)
                    pstart = pl.multiple_of(pstart, t)
                    loc = pl.multiple_of(loc, t)
                    rows, prow, lrow = pl.ds(start, t), pl.ds(pstart, t), pl.ds(loc, t)
                else:
                    rows = pl.ds(start, t, stride=dil)
                    prow = pl.ds(pstart, t, stride=dil)
                    lrow = pl.ds(loc, t, stride=dil)
                mf, df, o = _band_block(
                    qs[rows, :], ks[prow, :], ks[rows, :], v_ref[0, prow, :], v_ref[0, rows, :],
                    bias_ref[jnp.where(has_prev, 0, 1)],
                )
                ms[ci, lrow, :] = mf
                ds[ci, lrow, :] = df
                os_[ci, lrow, :] = o
                return carry

            lax.fori_loop(0, nblk, body, 0, unroll=ATT_UNROLL)

        m0, m1, m2 = ms[0], ms[1], ms[2]
        mx = jnp.maximum(jnp.maximum(m0, m1), m2)
        w0, w1, w2 = jnp.exp(m0 - mx), jnp.exp(m1 - mx), jnp.exp(m2 - mx)
        den = w0 * ds[0] + w1 * ds[1] + w2 * ds[2]
        num = w0 * os_[0] + w1 * os_[1] + w2 * os_[2]
        o_ref[0, base : base + DIL_SPAN, :] = (num / den).astype(o_ref.dtype)


def _dilated(proj3, tabs):
    b, l, _ = proj3.shape
    assert l % DIL_SPAN == 0
    bias = _band_bias(False)

    def col(c0):
        return pl.BlockSpec((1, l, LANES), lambda i, p: (i, 0, c0 // LANES + p))

    tab = pl.BlockSpec((1, l, LANES), lambda i, p: (i, 0, 0))
    return pl.pallas_call(
        _dil_kernel,
        grid=(b, BRANCH_WIDTH // LANES),
        in_specs=[col(C_CQ), col(C_CK), col(C_CV), tab, tab, tab, pl.BlockSpec(bias.shape, lambda i, p: (0, 0, 0))],
        out_specs=pl.BlockSpec((1, l, LANES), lambda i, p: (i, 0, p)),
        out_shape=jax.ShapeDtypeStruct((b, l, BRANCH_WIDTH), BF16),
        scratch_shapes=[pltpu.VMEM((l, LANES), F32)] * 2
        + [pltpu.VMEM((len(DIL_CONFIGS), DIL_SPAN, LANES), F32)] * 3,
        compiler_params=_cparams(("parallel", "parallel")),
        name="dilated_attn",
    )(proj3, proj3, proj3, *tabs, jnp.asarray(bias))


def _swa_kernel(*refs, n_cast):
    q_ref, k_ref, v_ref, c_ref, sa_ref, sb_ref, bias_ref, sink_ref = refs[:8]
    o_ref = refs[8 + n_cast]
    qs, ks, vs = refs[-3:]
    _side_cast(refs[8 : 8 + n_cast], refs[9 + n_cast : 9 + 2 * n_cast])
    l = q_ref.shape[1]
    c, sa, sb = c_ref[0], sa_ref[0], sb_ref[0]
    qs[...] = _rope_apply(q_ref[0], c, sa, sb) * (HEAD_DIM**-0.5)
    g = pl.program_id(1) // (SWA_HEADS // SWA_KV_HEADS // 2)
    lane = lax.broadcasted_iota(jnp.int32, (l, LANES), 1)
    keep = (lane // HEAD_DIM) == g
    k, v = _rope_apply(k_ref[0], c, sa, sb), v_ref[0]
    ks[...] = jnp.where(keep, k, pltpu.roll(k, HEAD_DIM, axis=1))
    vs[...] = jnp.where(keep, v, pltpu.roll(v, HEAD_DIM, axis=1))
    sink = sink_ref[0]
    t = ATT_BLOCK

    def body(n, carry):
        start = pl.multiple_of(n * t, t)
        has_prev = n > 0
        pstart = pl.multiple_of(jnp.where(has_prev, start - t, start), t)
        rows, prow = pl.ds(start, t), pl.ds(pstart, t)
        mf, df, o = _band_block(qs[rows, :], ks[prow, :], ks[rows, :], vs[prow, :], vs[rows, :],
                                bias_ref[jnp.where(has_prev, 0, 1)])
        lse = mf + jnp.log(df)
        o_ref[0, rows, :] = (o / df * jax.nn.sigmoid(lse - sink)).astype(o_ref.dtype)
        return carry

    lax.fori_loop(0, l // t, body, 0, unroll=ATT_UNROLL)


def _swa(proj3, tabs, sinks, cast=(), layer=0):
    b, l, _ = proj3.shape
    npair = BRANCH_WIDTH // LANES
    sink_l = jnp.repeat(sinks.astype(F32), HEAD_DIM).reshape(npair, 1, LANES)
    bias = _band_bias(True)
    tab = pl.BlockSpec((1, l, LANES), lambda i, p: (i, 0, 0))
    c_in, c_out, c_shape = _cast_specs(cast, layer, b * npair, lambda i, p: i * npair + p)
    return pl.pallas_call(
        functools.partial(_swa_kernel, n_cast=len(cast)),
        grid=(b, npair),
        in_specs=[
            pl.BlockSpec((1, l, LANES), lambda i, p: (i, 0, C_SQ // LANES + p)),
            pl.BlockSpec((1, l, LANES), lambda i, p: (i, 0, C_SK // LANES)),
            pl.BlockSpec((1, l, LANES), lambda i, p: (i, 0, C_SV // LANES)),
            tab,
            tab,
            tab,
            pl.BlockSpec(bias.shape, lambda i, p: (0, 0, 0)),
            pl.BlockSpec((1, 1, LANES), lambda i, p: (p, 0, 0)),
        ]
        + c_in,
        out_specs=[pl.BlockSpec((1, l, LANES), lambda i, p: (i, 0, p))] + c_out,
        out_shape=[jax.ShapeDtypeStruct((b, l, BRANCH_WIDTH), BF16)] + c_shape,
        scratch_shapes=[pltpu.VMEM((l, LANES), F32)] * 3,
        compiler_params=_cparams(("parallel", "parallel")),
        name="swa_attn",
    )(proj3, proj3, proj3, *tabs, jnp.asarray(bias), sink_l, *cast)


@functools.lru_cache(maxsize=None)
def _gla_consts():
    ch = GLA_CH
    r = np.arange(ch)[:, None]
    j = np.arange(ch)[None, :]
    dstack = (j <= r).astype(np.float32)
    lm = []
    for m in GLA_LEVELS[:-1]:
        mk = (((r // m) % 2 == 1) & ((j // m) == (r // m) - 1)).astype(np.float32)
        lm.append(np.tile(mk, (GLA_HEADS, 1)))
    lmask = np.stack(lm)
    dmask = np.tile(((r // GLA_SUB) == (j // GLA_SUB)).astype(np.float32), (1, GLA_HEADS))
    nsub = ch // GLA_SUB
    e = np.zeros((GLA_SUB, GLA_HEADS * GLA_DK, GLA_HEADS * ch), np.float32)
    for u in range(GLA_SUB):
        for h in range(GLA_HEADS):
            for s in range(nsub):
                e[u, h * GLA_DK : (h + 1) * GLA_DK, h * ch + GLA_SUB * s + u] = 1.0
    rr = np.arange(GLA_HEADS * GLA_DV)[:, None]
    cc = np.arange(GLA_HEADS * GLA_DK)[None, :]
    bd = ((rr // GLA_DV) == (cc // GLA_DK)).astype(np.float32)
    return dstack, lmask, dmask, e, bd


def _split3(x):
    hi = x.astype(BF16)
    r1 = x - hi.astype(F32)
    mid = r1.astype(BF16)
    lo = (r1 - mid.astype(F32)).astype(BF16)
    return hi, mid, lo


def _bcast_grp(x, m, u):
    r, w = x.shape
    x3 = x.reshape(r // m, m, w)
    return jnp.broadcast_to(x3[:, u : u + 1, :], x3.shape).reshape(r, w)


def _bcast_sub(x, u):
    return _bcast_grp(x, GLA_SUB, u)


def _gla_kernel(*refs, n_cast):
    (q_ref, k_ref, v_ref, r_ref, glr_ref, a2_ref, ab_ref, ng_ref, dst_ref, lmask_ref, dmask_ref, e_ref,
     bd_ref) = refs[:13]
    o_ref, st_ref = refs[13 + n_cast], refs[-1]
    _side_cast(refs[13 : 13 + n_cast], refs[14 + n_cast : 14 + 2 * n_cast])
    ch = GLA_CH
    rs = q_ref.shape[0]
    ncs = rs // ch
    nlev = len(GLA_LEVELS)
    hk = GLA_HEADS * GLA_DK

    @pl.when(pl.program_id(1) == 0)
    def _():
        st_ref[...] = jnp.zeros_like(st_ref)

    q = q_ref[...] * (GLA_DK**-0.5)
    k = k_ref[...]
    z = _bdot(glr_ref[...].astype(BF16), a2_ref[...]) + ab_ref[...]
    g = (jnp.minimum(z, 0.0) - jnp.log1p(jnp.exp(-jnp.abs(z)))) * (1.0 / GLA_TAU)
    g3 = _split3(g)
    dst = dst_ref[...]

    cums, excl = [], []
    for c in range(ncs):
        rows = slice(c * ch, (c + 1) * ch)
        cums.append(_bdot(dst, g3[0][rows]) + _bdot(dst, g3[1][rows]) + _bdot(dst, g3[2][rows]))
        excl.append(cums[c] - g[rows])

    def eq(c, li):
        m = GLA_LEVELS[li]
        return cums[c] - _bcast_grp(excl[c], m, 0)

    def ek(c, li):
        m = GLA_LEVELS[li]
        return _bcast_grp(cums[c], m, m - 1) - cums[c]

    cs = jnp.concatenate([eq(c, 0) for c in range(ncs)], axis=0)
    tsub = lax.broadcasted_iota(jnp.int32, (rs, hk), 0) % GLA_SUB
    arep = jnp.zeros((rs, GLA_HEADS * ch), F32)
    for u in range(GLA_SUB):
        dec = jnp.exp(jnp.where(tsub >= u, cs - _bcast_sub(cs, u), -jnp.inf))
        p = q * _bcast_sub(k, u) * dec
        arep = arep + _bdot(p.astype(BF16), e_ref[u])
    arep = arep * jnp.concatenate([dmask_ref[...]] * ncs, axis=0)

    lane_k = lax.broadcasted_iota(jnp.int32, (ch, hk), 1) // GLA_DK
    ng = ng_ref[...]
    bd = bd_ref[...]
    for c in range(ncs):
        rows = slice(c * ch, (c + 1) * ch)
        qc, kc = q[rows], k[rows]
        vc = v_ref[rows, :]
        aoff = jnp.zeros((GLA_HEADS * ch, ch), F32)
        for li in range(nlev - 1):
            qe = qc * jnp.exp(eq(c, li))
            ke = (kc * jnp.exp(ek(c, li))).astype(BF16)
            qst = jnp.concatenate([jnp.where(lane_k == h, qe, 0.0) for h in range(GLA_HEADS)], axis=0).astype(BF16)
            aoff = aoff + _bdot_nt(qst, ke) * lmask_ref[li]
        cum = eq(c, nlev - 1)
        st = st_ref[...]
        o_inter = _bdot_nt((qc * jnp.exp(cum)).astype(BF16), st.astype(BF16))
        vb = vc.astype(BF16)
        outs = []
        for h in range(GLA_HEADS):
            a_h = aoff[h * ch : (h + 1) * ch] + arep[rows, h * ch : (h + 1) * ch]
            o_h = _bdot(a_h.astype(BF16), vb[:, h * GLA_DV : (h + 1) * GLA_DV]) + o_inter[:, h * GLA_DV : (h + 1) * GLA_DV]
            ms = jnp.mean(o_h * o_h, axis=-1, keepdims=True)
            outs.append(o_h * lax.rsqrt(ms + NORM_EPS))
        o = jnp.concatenate(outs, axis=1) * ng
        rc = r_ref[rows, :]
        o_ref[rows, :] = (o * (rc * jax.nn.sigmoid(rc))).astype(o_ref.dtype)
        ke_last = (kc * jnp.exp(ek(c, nlev - 1))).astype(BF16)
        kv = _bdot(vc.T.astype(BF16), ke_last)
        st_ref[...] = st * jnp.exp(cum[ch - 1 : ch, :]) + kv * bd


def _gla(proj, b, a2, ab, ng, cast=(), layer=0):
    t = proj.shape[0]
    l = t // b
    rs = GLA_CH * GLA_NCS
    assert l % rs == 0
    ns = l // rs
    dstack, lmask, dmask, e, bd = _gla_consts()
    c_in, c_out, c_shape = _cast_specs(cast, layer, b * ns, lambda i, s: i * ns + s)
    a2p = jnp.zeros((LANES, GLA_HEADS * GLA_DK), BF16).at[:GLA_LOWRANK].set(a2.astype(BF16))

    def rowblk(w, c0):
        return pl.BlockSpec((rs, w), lambda i, s: (i * ns + s, c0 // w))

    def full(shape):
        nd = len(shape)
        return pl.BlockSpec(shape, lambda i, s: (0,) * nd)

    hk, hv = GLA_HEADS * GLA_DK, GLA_HEADS * GLA_DV
    return pl.pallas_call(
        functools.partial(_gla_kernel, n_cast=len(cast)),
        grid=(b, ns),
        in_specs=[
            rowblk(hk, C_GQ),
            rowblk(hk, C_GK),
            rowblk(hv, C_GV),
            rowblk(hv, C_GR),
            rowblk(LANES, C_GLR),
            full((LANES, hk)),
            full((1, hk)),
            full((1, hv)),
            full(dstack.shape),
            full(lmask.shape),
            full(dmask.shape),
            full(e.shape),
            full(bd.shape),
        ]
        + c_in,
        out_specs=[pl.BlockSpec((rs, hv), lambda i, s: (i * ns + s, 0))] + c_out,
        out_shape=[jax.ShapeDtypeStruct((t, hv), BF16)] + c_shape,
        scratch_shapes=[pltpu.VMEM((hv, hk), F32)],
        compiler_params=_cparams(("parallel", "arbitrary")),
        name="gla",
    )(
        proj, proj, proj, proj, proj, a2p, ab.reshape(1, hk).astype(F32), ng.reshape(1, hv).astype(F32),
        jnp.asarray(dstack, BF16), jnp.asarray(lmask), jnp.asarray(dmask), jnp.asarray(e, BF16), jnp.asarray(bd),
        *cast,
    )


def _s5_params(lam_re, lam_im, log_dt, b_re, b_im, c_re, c_im, d, nch):
    f = F32
    cs = S5_CH
    dt = jnp.exp(log_dt.astype(f))[:, None]
    lr, li = lam_re.astype(f), lam_im.astype(f)
    mag = jnp.exp(lr * dt)
    ab_re, ab_im = mag * jnp.cos(li * dt), mag * jnp.sin(li * dt)
    den = lr * lr + li * li
    z_re = ((ab_re - 1.0) * lr + ab_im * li) / den
    z_im = (ab_im * lr - (ab_re - 1.0) * li) / den
    br, bi = b_re.astype(f), b_im.astype(f)
    bb_re = z_re[..., None] * br - z_im[..., None] * bi
    bb_im = z_re[..., None] * bi + z_im[..., None] * br

    def apow(p):
        p = jnp.asarray(p, f)[:, None, None]
        m = jnp.exp(p * (lr * dt))
        return m * jnp.cos(p * (li * dt)), m * jnp.sin(p * (li * dt))

    cr, ci = c_re.astype(f), c_im.astype(f)
    p_re, p_im = apow(np.arange(cs + 1))
    nb, gpb = S5_GROUPS // S5_GPB, S5_GPB
    hw = gpb * S5_STATE
    eye = jnp.eye(gpb, dtype=f)

    def b_blockdiag(x):
        x = x.reshape(nb, gpb, S5_STATE, S5_GROUP).transpose(0, 1, 3, 2)
        x = (x[:, :, :, None, :] * eye[None, :, None, :, None]).reshape(nb, LANES, hw)
        return jnp.concatenate([x, x], axis=-1)

    def c_blockdiag(x):
        x = x.reshape(nb, gpb, S5_GROUP, S5_STATE).transpose(0, 1, 3, 2)
        return (x[:, :, :, None, :] * eye[None, :, None, :, None]).reshape(nb, hw, LANES)

    def lanes(a, b_):
        p = a.shape[0]
        return jnp.concatenate([a.reshape(p, nb, hw), b_.reshape(p, nb, hw)], axis=-1).transpose(1, 0, 2)

    rev = np.arange(cs - 1, -1, -1)
    pr, pi = p_re[rev], p_im[rev]
    pa, pb = lanes(pr, pi), lanes(-pi, pr)
    acol_re = jnp.broadcast_to(ab_re.reshape(nb, hw, 1), (nb, hw, LANES))
    acol_im = jnp.broadcast_to(ab_im.reshape(nb, hw, 1), (nb, hw, LANES))
    nstep = max(1, int(math.log2(nch)))
    s_re, s_im = apow(cs * (2 ** np.arange(nstep)))
    ar = s_re.reshape(nstep, nb, hw).transpose(1, 0, 2)
    ai = s_im.reshape(nstep, nb, hw).transpose(1, 0, 2)
    dd = d.astype(f).reshape(nb, 1, LANES)
    return (b_blockdiag(bb_re), b_blockdiag(bb_im), c_blockdiag(cr), c_blockdiag(ci), acol_re, acol_im,
            pa, pb, ar, ai, dd)


def _s5_kernel(u_ref, br_ref, bi_ref, cr_ref, ci_ref, acr_ref, aci_ref, pa_ref, pb_ref, ar_ref, ai_ref, d_ref,
               y_ref, mb_s, mc_s, kc_s, *, nchb):
    cs = S5_CH
    t = u_ref.shape[0]
    nch = t // cs
    hw = S5_GPB * S5_STATE
    br, bi = br_ref[0], bi_ref[0]
    pa, pb = pa_ref[0], pb_ref[0]
    for j in range(cs):
        mb_s[j * LANES : (j + 1) * LANES, :] = (br * pa[j : j + 1] + bi * pb[j : j + 1]).astype(BF16)
    x_re, x_im = cr_ref[0], ci_ref[0]
    kc = _bdot(mb_s[...], jnp.concatenate([x_re, -x_im], axis=0).astype(BF16)).astype(BF16)
    a_re, a_im = acr_ref[0], aci_ref[0]
    for tt in range(cs):
        x_re, x_im = x_re * a_re - x_im * a_im, x_re * a_im + x_im * a_re
        mc_s[:hw, tt * LANES : (tt + 1) * LANES] = x_re.astype(BF16)
        mc_s[hw:, tt * LANES : (tt + 1) * LANES] = (-x_im).astype(BF16)
    gl = cs // S5_TSPLIT * LANES
    for qq in range(S5_TSPLIT):
        kc_s[:, qq * LANES : (qq + 1) * LANES] = kc[(S5_TSPLIT - 1 - qq) * gl : (S5_TSPLIT - qq) * gl]
    ucat = jnp.concatenate([u_ref[pl.ds(j, nch, stride=cs), :] for j in range(cs)], axis=1).astype(BF16)
    e_all = _bdot(ucat, mb_s[...])
    cidx = lax.broadcasted_iota(jnp.int32, (nch, LANES), 0) % nchb
    ar, ai = ar_ref[0], ai_ref[0]
    hp_re, hp_im = [], []
    for g in range(hw // LANES):
        lo = g * LANES
        e_re, e_im = e_all[:, lo : lo + LANES], e_all[:, hw + lo : hw + lo + LANES]
        for kstep in range(ar.shape[0]):
            s = 2**kstep
            if s >= nchb:
                break
            s_re = jnp.where(cidx >= s, pltpu.roll(e_re, s, axis=0), 0.0)
            s_im = jnp.where(cidx >= s, pltpu.roll(e_im, s, axis=0), 0.0)
            k_re, k_im = ar[kstep : kstep + 1, lo : lo + LANES], ai[kstep : kstep + 1, lo : lo + LANES]
            e_re, e_im = e_re + s_re * k_re - s_im * k_im, e_im + s_re * k_im + s_im * k_re
        hp_re.append(jnp.where(cidx >= 1, pltpu.roll(e_re, 1, axis=0), 0.0).astype(BF16))
        hp_im.append(jnp.where(cidx >= 1, pltpu.roll(e_im, 1, axis=0), 0.0).astype(BF16))
    ycar = _bdot(jnp.concatenate(hp_re + hp_im, axis=1), mc_s[...])
    for tt in range(cs):
        y_ref[pl.ds(tt, nch, stride=cs), :] = ycar[:, tt * LANES : (tt + 1) * LANES]

    rt = min(S5_RT, t)
    rmod = lax.broadcasted_iota(jnp.int32, (rt, LANES), 0) % cs
    dvec = d_ref[0]
    glag = cs // S5_TSPLIT

    def tile(i, carry):
        rows = pl.ds(pl.multiple_of(i * rt, rt), rt)
        ut = u_ref[rows, :]
        ush = [jnp.where(rmod >= tau, pltpu.roll(ut, tau, axis=0), 0.0) for tau in range(glag - 1, 0, -1)] + [ut]
        zz = _bdot(jnp.concatenate(ush, axis=1).astype(BF16), kc_s[...])
        acc = y_ref[rows, :] + dvec * ut + zz[:, :LANES]
        for qq in range(1, S5_TSPLIT):
            part = pltpu.roll(zz[:, qq * LANES : (qq + 1) * LANES], qq * glag, axis=0)
            acc = acc + jnp.where(rmod >= qq * glag, part, 0.0)
        y_ref[rows, :] = acc
        return carry

    lax.fori_loop(0, t // rt, tile, 0)


def _s5_core(proj, b, params):
    t = proj.shape[0]
    nchb = t // b // S5_CH
    sw = S5_GPB * 2 * S5_STATE

    def bspec(a):
        return pl.BlockSpec((1,) + a.shape[1:], lambda p: (p, 0, 0))

    return pl.pallas_call(
        functools.partial(_s5_kernel, nchb=nchb),
        grid=(S5_GROUPS // S5_GPB,),
        in_specs=[pl.BlockSpec((t, LANES), lambda p: (0, C_S5 // LANES + p))] + [bspec(a) for a in params],
        out_specs=pl.BlockSpec((t, LANES), lambda p: (0, p)),
        out_shape=jax.ShapeDtypeStruct((t, BRANCH_WIDTH), F32),
        scratch_shapes=[
            pltpu.VMEM((S5_CH * LANES, sw), BF16),
            pltpu.VMEM((sw, S5_CH * LANES), BF16),
            pltpu.VMEM((S5_CH // S5_TSPLIT * LANES, S5_TSPLIT * LANES), BF16),
        ],
        compiler_params=_cparams(("parallel",)),
        name="s5_core",
    )(proj, *params)


def _merge_kernel(
    x_ref, g1_ref, gla_ref, ys5_ref, dil_ref, swa_ref, gluw_ref, glub_ref, wg0, wg1, wg2, wg3, wb_ref, wo_ref,
    o_ref, h_s, s5_s,
):
    j = pl.program_id(1)

    @pl.when(j == 0)
    def _():
        h_s[...] = _rms(x_ref[...], g1_ref[...]).astype(BF16)
        zz = jax.nn.gelu(ys5_ref[...])
        gate = jax.nn.sigmoid(_bdot(zz.astype(BF16), gluw_ref[...]) + glub_ref[...])
        s5_s[...] = (zz * gate).astype(BF16)
        o_ref[...] = jnp.zeros_like(o_ref)

    h = h_s[...]
    branches = (gla_ref[...], s5_s[...], dil_ref[...], swa_ref[...])
    mixed = None
    for m, (wg, br) in enumerate(zip((wg0, wg1, wg2, wg3), branches)):
        term = jax.nn.sigmoid(_bdot(h, wg[...])) * _bdot(br, wb_ref[m])
        mixed = term if mixed is None else mixed + term
    mixed = mixed.astype(BF16)
    d = o_ref.shape[1]
    for c0 in range(0, d, FFN_DOWN_CHUNK):
        cols = slice(c0, c0 + FFN_DOWN_CHUNK)
        o_ref[:, cols] += _bdot(mixed, wo_ref[:, cols])

    @pl.when(j == pl.num_programs(1) - 1)
    def _():
        o_ref[...] = x_ref[...] + o_ref[...]


def _merge(x, g1, o_gla, y_s5, o_dil, o_swa, gluw, glub, w_all, wb, wo, layer):
    t, d = x.shape
    tm, tn = min(TM_MERGE, t), TN_MERGE
    nj = d // tn
    bw = BRANCH_WIDTH

    def rowblk(w):
        return pl.BlockSpec((tm, w), lambda i, j: (i, 0))

    def gate_spec(m):
        return pl.BlockSpec((None, d, tn), lambda i, j, m=m: (layer, 0, N_SMALL // tn + m * nj + j))

    return pl.pallas_call(
        _merge_kernel,
        grid=(t // tm, nj),
        in_specs=[
            rowblk(d),
            pl.BlockSpec((1, d), lambda i, j: (0, 0)),
            rowblk(bw),
            rowblk(bw),
            rowblk(bw),
            rowblk(bw),
            pl.BlockSpec((None, bw, bw), lambda i, j: (layer, 0, 0)),
            pl.BlockSpec((1, bw), lambda i, j: (0, 0)),
            gate_spec(0),
            gate_spec(1),
            gate_spec(2),
            gate_spec(3),
            pl.BlockSpec((N_BRANCH, bw, tn), lambda i, j: (0, 0, j)),
            pl.BlockSpec((tn, d), lambda i, j: (j, 0)),
        ],
        out_specs=rowblk(d),
        out_shape=jax.ShapeDtypeStruct((t, d), F32),
        scratch_shapes=[pltpu.VMEM((tm, d), BF16), pltpu.VMEM((tm, bw), BF16)],
        compiler_params=_cparams(("parallel", "arbitrary")),
        name="merge",
    )(x, g1, o_gla, y_s5, o_dil, o_swa, gluw, glub, w_all, w_all, w_all, w_all, wb, wo)


def _ffn_kernel(x_ref, g2_ref, wg_ref, wu_ref, wd_ref, gf_ref, o_ref, h_s, *, final_norm):
    j = pl.program_id(1)

    @pl.when(j == 0)
    def _():
        h_s[...] = _rms(x_ref[...], g2_ref[...]).astype(BF16)
        o_ref[...] = jnp.zeros_like(o_ref)

    h = h_s[...]
    gate = _bdot(h, wg_ref[...])
    act = ((gate * jax.nn.sigmoid(gate)) * _bdot(h, wu_ref[...])).astype(BF16)
    d = o_ref.shape[1]
    for c0 in range(0, d, FFN_DOWN_CHUNK):
        cols = slice(c0, c0 + FFN_DOWN_CHUNK)
        o_ref[:, cols] += _bdot(act, wd_ref[:, cols])

    @pl.when(j == pl.num_programs(1) - 1)
    def _():
        y = x_ref[...] + o_ref[...]
        o_ref[...] = _rms(y, gf_ref[...]) if final_norm else y


def _ffn(x, g2, wg, wu, wd, gf, final_norm):
    t, d = x.shape
    fh = wg.shape[-1]
    tm, tf = min(TM_FFN, t), TF_FFN
    return pl.pallas_call(
        functools.partial(_ffn_kernel, final_norm=final_norm),
        grid=(t // tm, fh // tf),
        in_specs=[
            pl.BlockSpec((tm, d), lambda i, j: (i, 0)),
            pl.BlockSpec((1, d), lambda i, j: (0, 0)),
            pl.BlockSpec((d, tf), lambda i, j: (0, j)),
            pl.BlockSpec((d, tf), lambda i, j: (0, j)),
            pl.BlockSpec((tf, d), lambda i, j: (j, 0)),
            pl.BlockSpec((1, d), lambda i, j: (0, 0)),
        ],
        out_specs=pl.BlockSpec((tm, d), lambda i, j: (i, 0)),
        out_shape=jax.ShapeDtypeStruct((t, d), F32),
        scratch_shapes=[pltpu.VMEM((tm, d), BF16)],
        compiler_params=_cparams(("parallel", "arbitrary")),
        name="ffn",
    )(x, g2, wg, wu, wd, gf)


def _pack_kernel(*refs):
    o_ref = refs[-1]
    for piece, w_ref in enumerate(refs[:-1]):
        x = w_ref[...]
        row = lax.broadcasted_iota(jnp.int32, x.shape, 0)
        blk = pl.program_id(1) * PACK_PIECES + piece
        keep = jnp.where(blk == C_GLR // PACK_CB, GLA_LOWRANK, PACK_CB)
        o_ref[:, piece * PACK_CB : (piece + 1) * PACK_CB] = jnp.where(row < keep, x, 0.0).T.astype(BF16)


def _pack_row0(c):
    cb = PACK_CB
    u = GLA_LOWRANK
    front = jnp.where(c < GLR_ORIG // cb, c * (cb // u), c * (cb // u) + 1)
    back = jnp.where(c == C_GLR // cb, GLR_ORIG // u, c * (cb // u) - (N_SMALL - N_ORIG_SMALL) // u)
    return jnp.where(c < C_GLR // cb, front, back) * u


def _pack_w_in(w_in):
    depth, d, d_in = w_in.shape
    n_out = N_SMALL + d_in - N_ORIG_SMALL
    cb = PACK_CB
    assert GLR_ORIG % cb == 0 and C_GLR % cb == 0 and N_SMALL % cb == 0 and n_out % cb == 0
    assert C_GLR == N_ORIG_SMALL - GLA_LOWRANK and C_GLR + cb == N_SMALL
    np_ = PACK_PIECES
    assert n_out % (cb * np_) == 0
    w_t = jnp.swapaxes(w_in, 1, 2)
    return pl.pallas_call(
        _pack_kernel,
        grid=(depth, n_out // (cb * np_)),
        in_specs=[
            pl.BlockSpec((None, pl.Element(cb), pl.Element(d)), lambda i, c, p=p: (i, _pack_row0(c * np_ + p), 0))
            for p in range(np_)
        ],
        out_specs=pl.BlockSpec((None, d, cb * np_), lambda i, c: (i, 0, c)),
        out_shape=jax.ShapeDtypeStruct((depth, d, n_out), BF16),
        compiler_params=_cparams(("parallel", "parallel")),
        name="pack_w_in",
    )(*([w_t] * np_))


def kernel(x, positions, norm1_g, w_in, gla_a2, gla_a_b, gla_norm_g, s5_lambda_re, s5_lambda_im, s5_log_dt, s5_b_re, s5_b_im, s5_c_re, s5_c_im, s5_d, s5_glu_w, s5_glu_b, swa_sinks, w_branch, w_out, norm2_g, w_ffn_gate, w_ffn_up, w_ffn_down, final_norm_g):
    b, l, d = x.shape
    t = b * l
    depth = w_in.shape[0]
    xs = x.reshape(t, d).astype(F32)
    w_all = _pack_w_in(w_in)
    w_br32 = w_branch.reshape(depth, N_BRANCH * BRANCH_WIDTH, d)
    glu_w = s5_glu_w.astype(BF16)
    tabs = _rope_tables(positions)
    gf = final_norm_g.reshape(1, d).astype(F32)
    for i in range(depth):
        g1 = norm1_g[i].reshape(1, d).astype(F32)
        proj = _inproj(xs, g1, w_all, i)
        proj3 = proj.reshape(b, l, N_SMALL)
        o_gla, w_fg, w_fu, w_br, w_o = _gla(proj, b, gla_a2[i], gla_a_b[i], gla_norm_g[i],
                                            (w_ffn_gate, w_ffn_up, w_br32, w_out), i)
        w_br = w_br.reshape(N_BRANCH, BRANCH_WIDTH, d)
        s5p = _s5_params(s5_lambda_re[i], s5_lambda_im[i], s5_log_dt[i], s5_b_re[i], s5_b_im[i], s5_c_re[i],
                         s5_c_im[i], s5_d[i], l // S5_CH)
        y_s5 = _s5_core(proj, b, s5p)
        o_dil = _dilated(proj3, tabs).reshape(t, BRANCH_WIDTH)
        o_swa, w_fd = _swa(proj3, tabs, swa_sinks[i], (w_ffn_down,), i)
        o_swa = o_swa.reshape(t, BRANCH_WIDTH)
        xs = _merge(xs, g1, o_gla, y_s5, o_dil, o_swa, glu_w, s5_glu_b[i].reshape(1, -1).astype(F32),
                    w_all, w_br, w_o, i)
        xs = _ffn(xs, norm2_g[i].reshape(1, d).astype(F32), w_fg, w_fu, w_fd, gf, i == depth - 1)
    return xs.reshape(b, l, d).astype(x.dtype)
```

```python
import functools
import math

import jax
import jax.numpy as jnp
import numpy as np
from jax import lax
from jax.experimental import pallas as pl
from jax.experimental.pallas import tpu as pltpu

F32 = jnp.float32
BF16 = jnp.bfloat16
HIGHEST = lax.Precision.HIGHEST

D_MODEL = 2048
DEPTH = 2
N_BRANCH = 4
BRANCH_WIDTH = 512
HEAD_DIM = 64
ATT_BLOCK = 128
ROPE_THETA = 500000.0
ROPE_DIM = HEAD_DIM // 4
NORM_EPS = 1e-6
GLA_HEADS = 4
GLA_DK = 64
GLA_DV = BRANCH_WIDTH // GLA_HEADS
GLA_LOWRANK = 16
GLA_TAU = 16.0
S5_GROUP = 16
S5_GROUPS = BRANCH_WIDTH // S5_GROUP
S5_STATE = 64
DIL_CONFIGS = ((128, 1), (512, 4), (2048, 16))
DIL_SPAN = ATT_BLOCK * 16
SWA_HEADS = BRANCH_WIDTH // HEAD_DIM
SWA_KV_HEADS = 2
SWA_WINDOW = 128
FFN_HIDDEN = -((-8 * D_MODEL) // (3 * 256)) * 256

LANES = 128
SUBLANES = 8
VMEM_LIMIT = 56 * 1024 * 1024

C_GQ, C_GK, C_GV, C_GR, C_S5 = 0, 256, 512, 1024, 1536
C_CQ, C_CK, C_CV, C_SQ, C_SK, C_SV, C_GLR = 2048, 2560, 3072, 3584, 4096, 4224, 4352
N_SMALL = 4608
N_ORIG_SMALL = 4368
GLR_ORIG = 1536

TM_PROJ, TN_PROJ = 1024, 1536
TM_MERGE, TN_MERGE = 512, 512
TM_FFN, TF_FFN = 1024, 512
FFN_DOWN_CHUNK = 512
GLA_CH = 128
GLA_NCS = 4
GLA_SUB = 8
GLA_LEVELS = (8, 16, 32, 64, 128)
PACK_CB = 256
PACK_PIECES = 2
S5_CH = 16
S5_GPB = LANES // S5_GROUP
S5_RT = 512
S5_TSPLIT = 4
ATT_UNROLL = 8


def _cparams(sem):
    return pltpu.CompilerParams(dimension_semantics=sem, vmem_limit_bytes=VMEM_LIMIT)


def _rms(x, g):
    ms = jnp.mean(x * x, axis=-1, keepdims=True)
    return x * lax.rsqrt(ms + NORM_EPS) * g


def _bdot(a, b):
    return jnp.dot(a, b, preferred_element_type=F32)


def _bdot_nt(a, b):
    return lax.dot_general(a, b, (((1,), (1,)), ((), ())), preferred_element_type=F32)


def _cast_specs(ws, layer, nsteps, step_of):
    ins, outs, shapes = [], [], []
    for w in ws:
        _, r, c = w.shape
        rb = r // nsteps
        assert rb * nsteps == r and rb % (2 * SUBLANES) == 0
        ins.append(pl.BlockSpec((None, rb, c), lambda *g: (layer, step_of(*g), 0)))
        outs.append(pl.BlockSpec((rb, c), lambda *g: (step_of(*g), 0)))
        shapes.append(jax.ShapeDtypeStruct((r, c), BF16))
    return ins, outs, shapes


def _side_cast(in_refs, out_refs):
    for src, dst in zip(in_refs, out_refs):
        dst[...] = src[...].astype(BF16)


def _rope_apply(x, c, sa, sb):
    half = ROPE_DIM // 2
    return x * c + pltpu.roll(x, LANES - half, axis=1) * sa + pltpu.roll(x, half, axis=1) * sb


def _inproj_kernel(x_ref, g_ref, w_ref, o_ref, h_ref):
    @pl.when(pl.program_id(1) == 0)
    def _():
        h_ref[...] = _rms(x_ref[...], g_ref[...]).astype(BF16)

    o_ref[...] = _bdot(h_ref[...], w_ref[...])


def _inproj(x, g, w_all, layer):
    t, d = x.shape
    n = N_SMALL
    tm, tn = min(TM_PROJ, t), TN_PROJ
    return pl.pallas_call(
        _inproj_kernel,
        grid=(t // tm, n // tn),
        in_specs=[
            pl.BlockSpec((tm, d), lambda i, j: (i, 0)),
            pl.BlockSpec((1, d), lambda i, j: (0, 0)),
            pl.BlockSpec((None, d, tn), lambda i, j: (layer, 0, j)),
        ],
        out_specs=pl.BlockSpec((tm, tn), lambda i, j: (i, j)),
        out_shape=jax.ShapeDtypeStruct((t, n), F32),
        scratch_shapes=[pltpu.VMEM((tm, d), BF16)],
        compiler_params=_cparams(("parallel", "arbitrary")),
        name="inproj",
    )(x, g, w_all)


def _rope_kernel(pos_ref, cos_ref, sa_ref, sb_ref):
    pos = pos_ref[0].astype(F32)
    lane = lax.broadcasted_iota(jnp.int32, (SUBLANES, LANES), 1)
    d = lane % HEAD_DIM
    half = ROPE_DIM // 2
    fi = (d % half).astype(F32) / half
    inv = jnp.power(jnp.full((SUBLANES, LANES), ROPE_THETA, F32), -fi)[0:1]
    d1 = d[0:1]
    ang = pos * inv
    c, s = jnp.cos(ang), jnp.sin(ang)
    cos_ref[0] = jnp.where(d1 < ROPE_DIM, c, 1.0)
    sa_ref[0] = jnp.where(d1 < half, -s, 0.0)
    sb_ref[0] = jnp.where((d1 >= half) & (d1 < ROPE_DIM), s, 0.0)


def _rope_tables(positions):
    b, l = positions.shape
    spec = pl.BlockSpec((1, l, LANES), lambda i: (i, 0, 0))
    shp = jax.ShapeDtypeStruct((b, l, LANES), F32)
    return pl.pallas_call(
        _rope_kernel,
        grid=(b,),
        in_specs=[pl.BlockSpec((1, l, 1), lambda i: (i, 0, 0))],
        out_specs=[spec, spec, spec],
        out_shape=[shp, shp, shp],
        compiler_params=_cparams(("parallel",)),
        name="rope_tables",
    )(positions.reshape(b, l, 1))


def _band_bias(strict_prev):
    t = ATT_BLOCK
    qi = np.arange(2 * t)[:, None] % t
    kj = np.arange(2 * t)[None, :]
    own = (kj >= t) & (kj - t <= qi)
    prev = (kj < t) & (kj >= qi + (1 if strict_prev else 0))
    ninf = np.float32(-np.inf)
    return np.stack([np.where(own | prev, 0.0, ninf), np.where(own, 0.0, ninf)]).astype(np.float32)


def _band_block(q, kp, ko, vp, vo, bias):
    t = ATT_BLOCK
    lane = lax.broadcasted_iota(jnp.int32, (t, LANES), 1)
    in_a = lane < HEAD_DIM
    q2 = jnp.concatenate([jnp.where(in_a, q, 0.0), jnp.where(in_a, 0.0, q)], axis=0).astype(BF16)
    s = _bdot_nt(q2, jnp.concatenate([kp, ko], axis=0).astype(BF16)) + bias
    m = jnp.max(s, axis=-1, keepdims=True)
    p = jnp.exp(s - m).astype(BF16)
    ones = jnp.ones((2 * t, LANES), BF16)
    vcat = jnp.concatenate([jnp.concatenate([vp, vo], axis=0).astype(BF16), ones], axis=1)
    od = _bdot(p, vcat)
    o = jnp.where(in_a, od[:t, :LANES], od[t:, :LANES])
    df = jnp.where(in_a, od[:t, LANES:], od[t:, LANES:])
    mf = jnp.where(in_a, m[:t], m[t:])
    return mf, df, o


def _dil_kernel(q_ref, k_ref, v_ref, c_ref, sa_ref, sb_ref, bias_ref, o_ref, qs, ks, ms, ds, os_):
    l = q_ref.shape[1]
    c, sa, sb = c_ref[0], sa_ref[0], sb_ref[0]
    qs[...] = _rope_apply(q_ref[0], c, sa, sb) * (HEAD_DIM**-0.5)
    ks[...] = _rope_apply(k_ref[0], c, sa, sb)
    t = ATT_BLOCK
    nblk = DIL_SPAN // t

    for sblk in range(l // DIL_SPAN):
        base = sblk * DIL_SPAN
        for ci, (_, dil) in enumerate(DIL_CONFIGS):

            def body(idx, carry, dil=dil, ci=ci, base=base):
                r = idx % dil
                n = idx // dil
                loc = r + dil * t * n
                start = base + loc
                has_prev = start >= dil * t
                pstart = jnp.where(has_prev, start - dil * t, start)
                if dil == 1:
                    start = pl.multiple_of(start, t)
                    pstart = pl.multiple_of(pstart, t)
                    loc = pl.multiple_of(loc, t)
                    rows, prow, lrow = pl.ds(start, t), pl.ds(pstart, t), pl.ds(loc, t)
                else:
                    rows = pl.ds(start, t, stride=dil)
                    prow = pl.ds(pstart, t, stride=dil)
                    lrow = pl.ds(loc, t, stride=dil)
                mf, df, o = _band_block(
                    qs[rows, :], ks[prow, :], ks[rows, :], v_ref[0, prow, :], v_ref[0, rows, :],
                    bias_ref[jnp.where(has_prev, 0, 1)],
                )
                ms[ci, lrow, :] = mf
                ds[ci, lrow, :] = df
                os_[ci, lrow, :] = o
                return carry

            lax.fori_loop(0, nblk, body, 0, unroll=ATT_UNROLL)

        m0, m1, m2 = ms[0], ms[1], ms[2]
        mx = jnp.maximum(jnp.maximum(m0, m1), m2)
        w0, w1, w2 = jnp.exp(m0 - mx), jnp.exp(m1 - mx), jnp.exp(m2 - mx)
        den = w0 * ds[0] + w1 * ds[1] + w2 * ds[2]
        num = w0 * os_[0] + w1 * os_[1] + w2 * os_[2]
        o_ref[0, base : base + DIL_SPAN, :] = (num / den).astype(o_ref.dtype)


def _dilated(proj3, tabs):
    b, l, _ = proj3.shape
    assert l % DIL_SPAN == 0
    bias = _band_bias(False)

    def col(c0):
        return pl.BlockSpec((1, l, LANES), lambda i, p: (i, 0, c0 // LANES + p))

    tab = pl.BlockSpec((1, l, LANES), lambda i, p: (i, 0, 0))
    return pl.pallas_call(
        _dil_kernel,
        grid=(b, BRANCH_WIDTH // LANES),
        in_specs=[col(C_CQ), col(C_CK), col(C_CV), tab, tab, tab, pl.BlockSpec(bias.shape, lambda i, p: (0, 0, 0))],
        out_specs=pl.BlockSpec((1, l, LANES), lambda i, p: (i, 0, p)),
        out_shape=jax.ShapeDtypeStruct((b, l, BRANCH_WIDTH), BF16),
        scratch_shapes=[pltpu.VMEM((l, LANES), F32)] * 2
        + [pltpu.VMEM((len(DIL_CONFIGS), DIL_SPAN, LANES), F32)] * 3,
        compiler_params=_cparams(("parallel", "parallel")),
        name="dilated_attn",
    )(proj3, proj3, proj3, *tabs, jnp.asarray(bias))


def _swa_kernel(*refs, n_cast):
    q_ref, k_ref, v_ref, c_ref, sa_ref, sb_ref, bias_ref, sink_ref = refs[:8]
    o_ref = refs[8 + n_cast]
    qs, ks, vs = refs[-3:]
    _side_cast(refs[8 : 8 + n_cast], refs[9 + n_cast : 9 + 2 * n_cast])
    l = q_ref.shape[1]
    c, sa, sb = c_ref[0], sa_ref[0], sb_ref[0]
    qs[...] = _rope_apply(q_ref[0], c, sa, sb) * (HEAD_DIM**-0.5)
    g = pl.program_id(1) // (SWA_HEADS // SWA_KV_HEADS // 2)
    lane = lax.broadcasted_iota(jnp.int32, (l, LANES), 1)
    keep = (lane // HEAD_DIM) == g
    k, v = _rope_apply(k_ref[0], c, sa, sb), v_ref[0]
    ks[...] = jnp.where(keep, k, pltpu.roll(k, HEAD_DIM, axis=1))
    vs[...] = jnp.where(keep, v, pltpu.roll(v, HEAD_DIM, axis=1))
    sink = sink_ref[0]
    t = ATT_BLOCK

    def body(n, carry):
        start = pl.multiple_of(n * t, t)
        has_prev = n > 0
        pstart = pl.multiple_of(jnp.where(has_prev, start - t, start), t)
        rows, prow = pl.ds(start, t), pl.ds(pstart, t)
        mf, df, o = _band_block(qs[rows, :], ks[prow, :], ks[rows, :], vs[prow, :], vs[rows, :],
                                bias_ref[jnp.where(has_prev, 0, 1)])
        lse = mf + jnp.log(df)
        o_ref[0, rows, :] = (o / df * jax.nn.sigmoid(lse - sink)).astype(o_ref.dtype)
        return carry

    lax.fori_loop(0, l // t, body, 0, unroll=ATT_UNROLL)


def _swa(proj3, tabs, sinks, cast=(), layer=0):
    b, l, _ = proj3.shape
    npair = BRANCH_WIDTH // LANES
    sink_l = jnp.repeat(sinks.astype(F32), HEAD_DIM).reshape(npair, 1, LANES)
    bias = _band_bias(True)
    tab = pl.BlockSpec((1, l, LANES), lambda i, p: (i, 0, 0))
    c_in, c_out, c_shape = _cast_specs(cast, layer, b * npair, lambda i, p: i * npair + p)
    return pl.pallas_call(
        functools.partial(_swa_kernel, n_cast=len(cast)),
        grid=(b, npair),
        in_specs=[
            pl.BlockSpec((1, l, LANES), lambda i, p: (i, 0, C_SQ // LANES + p)),
            pl.BlockSpec((1, l, LANES), lambda i, p: (i, 0, C_SK // LANES)),
            pl.BlockSpec((1, l, LANES), lambda i, p: (i, 0, C_SV // LANES)),
            tab,
            tab,
            tab,
            pl.BlockSpec(bias.shape, lambda i, p: (0, 0, 0)),
            pl.BlockSpec((1, 1, LANES), lambda i, p: (p, 0, 0)),
        ]
        + c_in,
        out_specs=[pl.BlockSpec((1, l, LANES), lambda i, p: (i, 0, p))] + c_out,
        out_shape=[jax.ShapeDtypeStruct((b, l, BRANCH_WIDTH), BF16)] + c_shape,
        scratch_shapes=[pltpu.VMEM((l, LANES), F32)] * 3,
        compiler_params=_cparams(("parallel", "parallel")),
        name="swa_attn",
    )(proj3, proj3, proj3, *tabs, jnp.asarray(bias), sink_l, *cast)


@functools.lru_cache(maxsize=None)
def _gla_consts():
    ch = GLA_CH
    r = np.arange(ch)[:, None]
    j = np.arange(ch)[None, :]
    dstack = (j <= r).astype(np.float32)
    lm = []
    for m in GLA_LEVELS[:-1]:
        mk = (((r // m) % 2 == 1) & ((j // m) == (r // m) - 1)).astype(np.float32)
        lm.append(np.tile(mk, (GLA_HEADS, 1)))
    lmask = np.stack(lm)
    dmask = np.tile(((r // GLA_SUB) == (j // GLA_SUB)).astype(np.float32), (1, GLA_HEADS))
    nsub = ch // GLA_SUB
    e = np.zeros((GLA_SUB, GLA_HEADS * GLA_DK, GLA_HEADS * ch), np.float32)
    for u in range(GLA_SUB):
        for h in range(GLA_HEADS):
            for s in range(nsub):
                e[u, h * GLA_DK : (h + 1) * GLA_DK, h * ch + GLA_SUB * s + u] = 1.0
    rr = np.arange(GLA_HEADS * GLA_DV)[:, None]
    cc = np.arange(GLA_HEADS * GLA_DK)[None, :]
    bd = ((rr // GLA_DV) == (cc // GLA_DK)).astype(np.float32)
    return dstack, lmask, dmask, e, bd


def _split3(x):
    hi = x.astype(BF16)
    r1 = x - hi.astype(F32)
    mid = r1.astype(BF16)
    lo = (r1 - mid.astype(F32)).astype(BF16)
    return hi, mid, lo


def _bcast_grp(x, m, u):
    r, w = x.shape
    x3 = x.reshape(r // m, m, w)
    return jnp.broadcast_to(x3[:, u : u + 1, :], x3.shape).reshape(r, w)


def _bcast_sub(x, u):
    return _bcast_grp(x, GLA_SUB, u)


def _gla_kernel(*refs, n_cast):
    (q_ref, k_ref, v_ref, r_ref, glr_ref, a2_ref, ab_ref, ng_ref, dst_ref, lmask_ref, dmask_ref, e_ref,
     bd_ref) = refs[:13]
    o_ref, st_ref = refs[13 + n_cast], refs[-1]
    _side_cast(refs[13 : 13 + n_cast], refs[14 + n_cast : 14 + 2 * n_cast])
    ch = GLA_CH
    rs = q_ref.shape[0]
    ncs = rs // ch
    nlev = len(GLA_LEVELS)
    hk = GLA_HEADS * GLA_DK

    @pl.when(pl.program_id(1) == 0)
    def _():
        st_ref[...] = jnp.zeros_like(st_ref)

    q = q_ref[...] * (GLA_DK**-0.5)
    k = k_ref[...]
    z = _bdot(glr_ref[...].astype(BF16), a2_ref[...]) + ab_ref[...]
    g = (jnp.minimum(z, 0.0) - jnp.log1p(jnp.exp(-jnp.abs(z)))) * (1.0 / GLA_TAU)
    g3 = _split3(g)
    dst = dst_ref[...]

    cums, excl = [], []
    for c in range(ncs):
        rows = slice(c * ch, (c + 1) * ch)
        cums.append(_bdot(dst, g3[0][rows]) + _bdot(dst, g3[1][rows]) + _bdot(dst, g3[2][rows]))
        excl.append(cums[c] - g[rows])

    def eq(c, li):
        m = GLA_LEVELS[li]
        return cums[c] - _bcast_grp(excl[c], m, 0)

    def ek(c, li):
        m = GLA_LEVELS[li]
        return _bcast_grp(cums[c], m, m - 1) - cums[c]

    cs = jnp.concatenate([eq(c, 0) for c in range(ncs)], axis=0)
    tsub = lax.broadcasted_iota(jnp.int32, (rs, hk), 0) % GLA_SUB
    arep = jnp.zeros((rs, GLA_HEADS * ch), F32)
    for u in range(GLA_SUB):
        dec = jnp.exp(jnp.where(tsub >= u, cs - _bcast_sub(cs, u), -jnp.inf))
        p = q * _bcast_sub(k, u) * dec
        arep = arep + _bdot(p.astype(BF16), e_ref[u])
    arep = arep * jnp.concatenate([dmask_ref[...]] * ncs, axis=0)

    lane_k = lax.broadcasted_iota(jnp.int32, (ch, hk), 1) // GLA_DK
    ng = ng_ref[...]
    bd = bd_ref[...]
    for c in range(ncs):
        rows = slice(c * ch, (c + 1) * ch)
        qc, kc = q[rows], k[rows]
        vc = v_ref[rows, :]
        aoff = jnp.zeros((GLA_HEADS * ch, ch), F32)
        for li in range(nlev - 1):
            qe = qc * jnp.exp(eq(c, li))
            ke = (kc * jnp.exp(ek(c, li))).astype(BF16)
            qst = jnp.concatenate([jnp.where(lane_k == h, qe, 0.0) for h in range(GLA_HEADS)], axis=0).astype(BF16)
            aoff = aoff + _bdot_nt(qst, ke) * lmask_ref[li]
        cum = eq(c, nlev - 1)
        st = st_ref[...]
        o_inter = _bdot_nt((qc * jnp.exp(cum)).astype(BF16), st.astype(BF16))
        vb = vc.astype(BF16)
        outs = []
        for h in range(GLA_HEADS):
            a_h = aoff[h * ch : (h + 1) * ch] + arep[rows, h * ch : (h + 1) * ch]
            o_h = _bdot(a_h.astype(BF16), vb[:, h * GLA_DV : (h + 1) * GLA_DV]) + o_inter[:, h * GLA_DV : (h + 1) * GLA_DV]
            ms = jnp.mean(o_h * o_h, axis=-1, keepdims=True)
            outs.append(o_h * lax.rsqrt(ms + NORM_EPS))
        o = jnp.concatenate(outs, axis=1) * ng
        rc = r_ref[rows, :]
        o_ref[rows, :] = (o * (rc * jax.nn.sigmoid(rc))).astype(o_ref.dtype)
        ke_last = (kc * jnp.exp(ek(c, nlev - 1))).astype(BF16)
        kv = _bdot(vc.T.astype(BF16), ke_last)
        st_ref[...] = st * jnp.exp(cum[ch - 1 : ch, :]) + kv * bd


def _gla(proj, b, a2, ab, ng, cast=(), layer=0):
    t = proj.shape[0]
    l = t // b
    rs = GLA_CH * GLA_NCS
    assert l % rs == 0
    ns = l // rs
    dstack, lmask, dmask, e, bd = _gla_consts()
    c_in, c_out, c_shape = _cast_specs(cast, layer, b * ns, lambda i, s: i * ns + s)
    a2p = jnp.zeros((LANES, GLA_HEADS * GLA_DK), BF16).at[:GLA_LOWRANK].set(a2.astype(BF16))

    def rowblk(w, c0):
        return pl.BlockSpec((rs, w), lambda i, s: (i * ns + s, c0 // w))

    def full(shape):
        nd = len(shape)
        return pl.BlockSpec(shape, lambda i, s: (0,) * nd)

    hk, hv = GLA_HEADS * GLA_DK, GLA_HEADS * GLA_DV
    return pl.pallas_call(
        functools.partial(_gla_kernel, n_cast=len(cast)),
        grid=(b, ns),
        in_specs=[
            rowblk(hk, C_GQ),
            rowblk(hk, C_GK),
            rowblk(hv, C_GV),
            rowblk(hv, C_GR),
            rowblk(LANES, C_GLR),
            full((LANES, hk)),
            full((1, hk)),
            full((1, hv)),
            full(dstack.shape),
            full(lmask.shape),
            full(dmask.shape),
            full(e.shape),
            full(bd.shape),
        ]
        + c_in,
        out_specs=[pl.BlockSpec((rs, hv), lambda i, s: (i * ns + s, 0))] + c_out,
        out_shape=[jax.ShapeDtypeStruct((t, hv), BF16)] + c_shape,
        scratch_shapes=[pltpu.VMEM((hv, hk), F32)],
        compiler_params=_cparams(("parallel", "arbitrary")),
        name="gla",
    )(
        proj, proj, proj, proj, proj, a2p, ab.reshape(1, hk).astype(F32), ng.reshape(1, hv).astype(F32),
        jnp.asarray(dstack, BF16), jnp.asarray(lmask), jnp.asarray(dmask), jnp.asarray(e, BF16), jnp.asarray(bd),
        *cast,
    )


def _s5_params(lam_re, lam_im, log_dt, b_re, b_im, c_re, c_im, d, nch):
    f = F32
    cs = S5_CH
    dt = jnp.exp(log_dt.astype(f))[:, None]
    lr, li = lam_re.astype(f), lam_im.astype(f)
    mag = jnp.exp(lr * dt)
    ab_re, ab_im = mag * jnp.cos(li * dt), mag * jnp.sin(li * dt)
    den = lr * lr + li * li
    z_re = ((ab_re - 1.0) * lr + ab_im * li) / den
    z_im = (ab_im * lr - (ab_re - 1.0) * li) / den
    br, bi = b_re.astype(f), b_im.astype(f)
    bb_re = z_re[..., None] * br - z_im[..., None] * bi
    bb_im = z_re[..., None] * bi + z_im[..., None] * br

    def apow(p):
        p = jnp.asarray(p, f)[:, None, None]
        m = jnp.exp(p * (lr * dt))
        return m * jnp.cos(p * (li * dt)), m * jnp.sin(p * (li * dt))

    cr, ci = c_re.astype(f), c_im.astype(f)
    p_re, p_im = apow(np.arange(cs + 1))
    nb, gpb = S5_GROUPS // S5_GPB, S5_GPB
    hw = gpb * S5_STATE
    eye = jnp.eye(gpb, dtype=f)

    def b_blockdiag(x):
        x = x.reshape(nb, gpb, S5_STATE, S5_GROUP).transpose(0, 1, 3, 2)
        x = (x[:, :, :, None, :] * eye[None, :, None, :, None]).reshape(nb, LANES, hw)
        return jnp.concatenate([x, x], axis=-1)

    def c_blockdiag(x):
        x = x.reshape(nb, gpb, S5_GROUP, S5_STATE).transpose(0, 1, 3, 2)
        return (x[:, :, :, None, :] * eye[None, :, None, :, None]).reshape(nb, hw, LANES)

    def lanes(a, b_):
        p = a.shape[0]
        return jnp.concatenate([a.reshape(p, nb, hw), b_.reshape(p, nb, hw)], axis=-1).transpose(1, 0, 2)

    rev = np.arange(cs - 1, -1, -1)
    pr, pi = p_re[rev], p_im[rev]
    pa, pb = lanes(pr, pi), lanes(-pi, pr)
    acol_re = jnp.broadcast_to(ab_re.reshape(nb, hw, 1), (nb, hw, LANES))
    acol_im = jnp.broadcast_to(ab_im.reshape(nb, hw, 1), (nb, hw, LANES))
    nstep = max(1, int(math.log2(nch)))
    s_re, s_im = apow(cs * (2 ** np.arange(nstep)))
    ar = s_re.reshape(nstep, nb, hw).transpose(1, 0, 2)
    ai = s_im.reshape(nstep, nb, hw).transpose(1, 0, 2)
    dd = d.astype(f).reshape(nb, 1, LANES)
    return (b_blockdiag(bb_re), b_blockdiag(bb_im), c_blockdiag(cr), c_blockdiag(ci), acol_re, acol_im,
            pa, pb, ar, ai, dd)


def _s5_kernel(u_ref, br_ref, bi_ref, cr_ref, ci_ref, acr_ref, aci_ref, pa_ref, pb_ref, ar_ref, ai_ref, d_ref,
               y_ref, mb_s, mc_s, kc_s, *, nchb):
    cs = S5_CH
    t = u_ref.shape[0]
    nch = t // cs
    hw = S5_GPB * S5_STATE
    br, bi = br_ref[0], bi_ref[0]
    pa, pb = pa_ref[0], pb_ref[0]
    for j in range(cs):
        mb_s[j * LANES : (j + 1) * LANES, :] = (br * pa[j : j + 1] + bi * pb[j : j + 1]).astype(BF16)
    x_re, x_im = cr_ref[0], ci_ref[0]
    kc = _bdot(mb_s[...], jnp.concatenate([x_re, -x_im], axis=0).astype(BF16)).astype(BF16)
    a_re, a_im = acr_ref[0], aci_ref[0]
    for tt in range(cs):
        x_re, x_im = x_re * a_re - x_im * a_im, x_re * a_im + x_im * a_re
        mc_s[:hw, tt * LANES : (tt + 1) * LANES] = x_re.astype(BF16)
        mc_s[hw:, tt * LANES : (tt + 1) * LANES] = (-x_im).astype(BF16)
    gl = cs // S5_TSPLIT * LANES
    for qq in range(S5_TSPLIT):
        kc_s[:, qq * LANES : (qq + 1) * LANES] = kc[(S5_TSPLIT - 1 - qq) * gl : (S5_TSPLIT - qq) * gl]
    ucat = jnp.concatenate([u_ref[pl.ds(j, nch, stride=cs), :] for j in range(cs)], axis=1).astype(BF16)
    e_all = _bdot(ucat, mb_s[...])
    cidx = lax.broadcasted_iota(jnp.int32, (nch, LANES), 0) % nchb
    ar, ai = ar_ref[0], ai_ref[0]
    hp_re, hp_im = [], []
    for g in range(hw // LANES):
        lo = g * LANES
        e_re, e_im = e_all[:, lo : lo + LANES], e_all[:, hw + lo : hw + lo + LANES]
        for kstep in range(ar.shape[0]):
            s = 2**kstep
            if s >= nchb:
                break
            s_re = jnp.where(cidx >= s, pltpu.roll(e_re, s, axis=0), 0.0)
            s_im = jnp.where(cidx >= s, pltpu.roll(e_im, s, axis=0), 0.0)
            k_re, k_im = ar[kstep : kstep + 1, lo : lo + LANES], ai[kstep : kstep + 1, lo : lo + LANES]
            e_re, e_im = e_re + s_re * k_re - s_im * k_im, e_im + s_re * k_im + s_im * k_re
        hp_re.append(jnp.where(cidx >= 1, pltpu.roll(e_re, 1, axis=0), 0.0).astype(BF16))
        hp_im.append(jnp.where(cidx >= 1, pltpu.roll(e_im, 1, axis=0), 0.0).astype(BF16))
    ycar = _bdot(jnp.concatenate(hp_re + hp_im, axis=1), mc_s[...])
    for tt in range(cs):
        y_ref[pl.ds(tt, nch, stride=cs), :] = ycar[:, tt * LANES : (tt + 1) * LANES]

    rt = min(S5_RT, t)
    rmod = lax.broadcasted_iota(jnp.int32, (rt, LANES), 0) % cs
    dvec = d_ref[0]
    glag = cs // S5_TSPLIT

    def tile(i, carry):
        rows = pl.ds(pl.multiple_of(i * rt, rt), rt)
        ut = u_ref[rows, :]
        ush = [jnp.where(rmod >= tau, pltpu.roll(ut, tau, axis=0), 0.0) for tau in range(glag - 1, 0, -1)] + [ut]
        zz = _bdot(jnp.concatenate(ush, axis=1).astype(BF16), kc_s[...])
        acc = y_ref[rows, :] + dvec * ut + zz[:, :LANES]
        for qq in range(1, S5_TSPLIT):
            part = pltpu.roll(zz[:, qq * LANES : (qq + 1) * LANES], qq * glag, axis=0)
            acc = acc + jnp.where(rmod >= qq * glag, part, 0.0)
        y_ref[rows, :] = acc
        return carry

    lax.fori_loop(0, t // rt, tile, 0)


def _s5_core(proj, b, params):
    t = proj.shape[0]
    nchb = t // b // S5_CH
    sw = S5_GPB * 2 * S5_STATE

    def bspec(a):
        return pl.BlockSpec((1,) + a.shape[1:], lambda p: (p, 0, 0))

    return pl.pallas_call(
        functools.partial(_s5_kernel, nchb=nchb),
        grid=(S5_GROUPS // S5_GPB,),
        in_specs=[pl.BlockSpec((t, LANES), lambda p: (0, C_S5 // LANES + p))] + [bspec(a) for a in params],
        out_specs=pl.BlockSpec((t, LANES), lambda p: (0, p)),
        out_shape=jax.ShapeDtypeStruct((t, BRANCH_WIDTH), F32),
        scratch_shapes=[
            pltpu.VMEM((S5_CH * LANES, sw), BF16),
            pltpu.VMEM((sw, S5_CH * LANES), BF16),
            pltpu.VMEM((S5_CH // S5_TSPLIT * LANES, S5_TSPLIT * LANES), BF16),
        ],
        compiler_params=_cparams(("parallel",)),
        name="s5_core",
    )(proj, *params)


def _merge_kernel(
    x_ref, g1_ref, gla_ref, ys5_ref, dil_ref, swa_ref, gluw_ref, glub_ref, wg0, wg1, wg2, wg3, wb_ref, wo_ref,
    o_ref, h_s, s5_s,
):
    j = pl.program_id(1)

    @pl.when(j == 0)
    def _():
        h_s[...] = _rms(x_ref[...], g1_ref[...]).astype(BF16)
        zz = jax.nn.gelu(ys5_ref[...])
        gate = jax.nn.sigmoid(_bdot(zz.astype(BF16), gluw_ref[...]) + glub_ref[...])
        s5_s[...] = (zz * gate).astype(BF16)
        o_ref[...] = jnp.zeros_like(o_ref)

    h = h_s[...]
    branches = (gla_ref[...], s5_s[...], dil_ref[...], swa_ref[...])
    mixed = None
    for m, (wg, br) in enumerate(zip((wg0, wg1, wg2, wg3), branches)):
        term = jax.nn.sigmoid(_bdot(h, wg[...])) * _bdot(br, wb_ref[m])
        mixed = term if mixed is None else mixed + term
    mixed = mixed.astype(BF16)
    d = o_ref.shape[1]
    for c0 in range(0, d, FFN_DOWN_CHUNK):
        cols = slice(c0, c0 + FFN_DOWN_CHUNK)
        o_ref[:, cols] += _bdot(mixed, wo_ref[:, cols])

    @pl.when(j == pl.num_programs(1) - 1)
    def _():
        o_ref[...] = x_ref[...] + o_ref[...]


def _merge(x, g1, o_gla, y_s5, o_dil, o_swa, gluw, glub, w_all, wb, wo, layer):
    t, d = x.shape
    tm, tn = min(TM_MERGE, t), TN_MERGE
    nj = d // tn
    bw = BRANCH_WIDTH

    def rowblk(w):
        return pl.BlockSpec((tm, w), lambda i, j: (i, 0))

    def gate_spec(m):
        return pl.BlockSpec((None, d, tn), lambda i, j, m=m: (layer, 0, N_SMALL // tn + m * nj + j))

    return pl.pallas_call(
        _merge_kernel,
        grid=(t // tm, nj),
        in_specs=[
            rowblk(d),
            pl.BlockSpec((1, d), lambda i, j: (0, 0)),
            rowblk(bw),
            rowblk(bw),
            rowblk(bw),
            rowblk(bw),
            pl.BlockSpec((None, bw, bw), lambda i, j: (layer, 0, 0)),
            pl.BlockSpec((1, bw), lambda i, j: (0, 0)),
            gate_spec(0),
            gate_spec(1),
            gate_spec(2),
            gate_spec(3),
            pl.BlockSpec((N_BRANCH, bw, tn), lambda i, j: (0, 0, j)),
            pl.BlockSpec((tn, d), lambda i, j: (j, 0)),
        ],
        out_specs=rowblk(d),
        out_shape=jax.ShapeDtypeStruct((t, d), F32),
        scratch_shapes=[pltpu.VMEM((tm, d), BF16), pltpu.VMEM((tm, bw), BF16)],
        compiler_params=_cparams(("parallel", "arbitrary")),
        name="merge",
    )(x, g1, o_gla, y_s5, o_dil, o_swa, gluw, glub, w_all, w_all, w_all, w_all, wb, wo)


def _ffn_kernel(x_ref, g2_ref, wg_ref, wu_ref, wd_ref, gf_ref, o_ref, h_s, *, final_norm):
    j = pl.program_id(1)

    @pl.when(j == 0)
    def _():
        h_s[...] = _rms(x_ref[...], g2_ref[...]).astype(BF16)
        o_ref[...] = jnp.zeros_like(o_ref)

    h = h_s[...]
    gate = _bdot(h, wg_ref[...])
    act = ((gate * jax.nn.sigmoid(gate)) * _bdot(h, wu_ref[...])).astype(BF16)
    d = o_ref.shape[1]
    for c0 in range(0, d, FFN_DOWN_CHUNK):
        cols = slice(c0, c0 + FFN_DOWN_CHUNK)
        o_ref[:, cols] += _bdot(act, wd_ref[:, cols])

    @pl.when(j == pl.num_programs(1) - 1)
    def _():
        y = x_ref[...] + o_ref[...]
        o_ref[...] = _rms(y, gf_ref[...]) if final_norm else y


def _ffn(x, g2, wg, wu, wd, gf, final_norm):
    t, d = x.shape
    fh = wg.shape[-1]
    tm, tf = min(TM_FFN, t), TF_FFN
    return pl.pallas_call(
        functools.partial(_ffn_kernel, final_norm=final_norm),
        grid=(t // tm, fh // tf),
        in_specs=[
            pl.BlockSpec((tm, d), lambda i, j: (i, 0), pipeline_mode=pl.Buffered(1)),
            pl.BlockSpec((1, d), lambda i, j: (0, 0)),
            pl.BlockSpec((d, tf), lambda i, j: (0, j)),
            pl.BlockSpec((d, tf), lambda i, j: (0, j)),
            pl.BlockSpec((tf, d), lambda i, j: (j, 0)),
            pl.BlockSpec((1, d), lambda i, j: (0, 0)),
        ],
        out_specs=pl.BlockSpec((tm, d), lambda i, j: (i, 0)),
        out_shape=jax.ShapeDtypeStruct((t, d), F32),
        scratch_shapes=[pltpu.VMEM((tm, d), BF16)],
        compiler_params=_cparams(("parallel", "arbitrary")),
        name="ffn",
    )(x, g2, wg, wu, wd, gf)


def _pack_kernel(*refs):
    o_ref = refs[-1]
    for piece, w_ref in enumerate(refs[:-1]):
        x = w_ref[...]
        row = lax.broadcasted_iota(jnp.int32, x.shape, 0)
        blk = pl.program_id(1) * PACK_PIECES + piece
        keep = jnp.where(blk == C_GLR // PACK_CB, GLA_LOWRANK, PACK_CB)
        o_ref[:, piece * PACK_CB : (piece + 1) * PACK_CB] = jnp.where(row < keep, x, 0.0).T.astype(BF16)


def _pack_row0(c):
    cb = PACK_CB
    u = GLA_LOWRANK
    front = jnp.where(c < GLR_ORIG // cb, c * (cb // u), c * (cb // u) + 1)
    back = jnp.where(c == C_GLR // cb, GLR_ORIG // u, c * (cb // u) - (N_SMALL - N_ORIG_SMALL) // u)
    return jnp.where(c < C_GLR // cb, front, back) * u


def _pack_w_in(w_in):
    depth, d, d_in = w_in.shape
    n_out = N_SMALL + d_in - N_ORIG_SMALL
    cb = PACK_CB
    assert GLR_ORIG % cb == 0 and C_GLR % cb == 0 and N_SMALL % cb == 0 and n_out % cb == 0
    assert C_GLR == N_ORIG_SMALL - GLA_LOWRANK and C_GLR + cb == N_SMALL
    np_ = PACK_PIECES
    assert n_out % (cb * np_) == 0
    w_t = jnp.swapaxes(w_in, 1, 2)
    return pl.pallas_call(
        _pack_kernel,
        grid=(depth, n_out // (cb * np_)),
        in_specs=[
            pl.BlockSpec((None, pl.Element(cb), pl.Element(d)), lambda i, c, p=p: (i, _pack_row0(c * np_ + p), 0))
            for p in range(np_)
        ],
        out_specs=pl.BlockSpec((None, d, cb * np_), lambda i, c: (i, 0, c)),
        out_shape=jax.ShapeDtypeStruct((depth, d, n_out), BF16),
        compiler_params=_cparams(("parallel", "parallel")),
        name="pack_w_in",
    )(*([w_t] * np_))


def kernel(x, positions, norm1_g, w_in, gla_a2, gla_a_b, gla_norm_g, s5_lambda_re, s5_lambda_im, s5_log_dt, s5_b_re, s5_b_im, s5_c_re, s5_c_im, s5_d, s5_glu_w, s5_glu_b, swa_sinks, w_branch, w_out, norm2_g, w_ffn_gate, w_ffn_up, w_ffn_down, final_norm_g):
    b, l, d = x.shape
    t = b * l
    depth = w_in.shape[0]
    xs = x.reshape(t, d).astype(F32)
    w_all = _pack_w_in(w_in)
    w_br32 = w_branch.reshape(depth, N_BRANCH * BRANCH_WIDTH, d)
    glu_w = s5_glu_w.astype(BF16)
    tabs = _rope_tables(positions)
    gf = final_norm_g.reshape(1, d).astype(F32)
    for i in range(depth):
        g1 = norm1_g[i].reshape(1, d).astype(F32)
        proj = _inproj(xs, g1, w_all, i)
        proj3 = proj.reshape(b, l, N_SMALL)
        o_gla, w_fg, w_fu, w_br, w_o = _gla(proj, b, gla_a2[i], gla_a_b[i], gla_norm_g[i],
                                            (w_ffn_gate, w_ffn_up, w_br32, w_out), i)
        w_br = w_br.reshape(N_BRANCH, BRANCH_WIDTH, d)
        s5p = _s5_params(s5_lambda_re[i], s5_lambda_im[i], s5_log_dt[i], s5_b_re[i], s5_b_im[i], s5_c_re[i],
                         s5_c_im[i], s5_d[i], l // S5_CH)
        y_s5 = _s5_core(proj, b, s5p)
        o_dil = _dilated(proj3, tabs).reshape(t, BRANCH_WIDTH)
        o_swa, w_fd = _swa(proj3, tabs, swa_sinks[i], (w_ffn_down,), i)
        o_swa = o_swa.reshape(t, BRANCH_WIDTH)
        xs = _merge(xs, g1, o_gla, y_s5, o_dil, o_swa, glu_w, s5_glu_b[i].reshape(1, -1).astype(F32),
                    w_all, w_br, w_o, i)
        xs = _ffn(xs, norm2_g[i].reshape(1, d).astype(F32), w_fg, w_fu, w_fd, gf, i == depth - 1)
    return xs.reshape(b, l, d).astype(x.dtype)
```

```python
import functools
import math

import jax
import jax.numpy as jnp
import numpy as np
from jax import lax
from jax.experimental import pallas as pl
from jax.experimental.pallas import tpu as pltpu

F32 = jnp.float32
BF16 = jnp.bfloat16
HIGHEST = lax.Precision.HIGHEST

D_MODEL = 2048
DEPTH = 2
N_BRANCH = 4
BRANCH_WIDTH = 512
HEAD_DIM = 64
ATT_BLOCK = 128
ROPE_THETA = 500000.0
ROPE_DIM = HEAD_DIM // 4
NORM_EPS = 1e-6
GLA_HEADS = 4
GLA_DK = 64
GLA_DV = BRANCH_WIDTH // GLA_HEADS
GLA_LOWRANK = 16
GLA_TAU = 16.0
S5_GROUP = 16
S5_GROUPS = BRANCH_WIDTH // S5_GROUP
S5_STATE = 64
DIL_CONFIGS = ((128, 1), (512, 4), (2048, 16))
DIL_SPAN = ATT_BLOCK * 16
SWA_HEADS = BRANCH_WIDTH // HEAD_DIM
SWA_KV_HEADS = 2
SWA_WINDOW = 128
FFN_HIDDEN = -((-8 * D_MODEL) // (3 * 256)) * 256

LANES = 128
SUBLANES = 8
VMEM_LIMIT = 56 * 1024 * 1024

C_GQ, C_GK, C_GV, C_GR, C_S5 = 0, 256, 512, 1024, 1536
C_CQ, C_CK, C_CV, C_SQ, C_SK, C_SV, C_GLR = 2048, 2560, 3072, 3584, 4096, 4224, 4352
N_SMALL = 4608
N_ORIG_SMALL = 4368
GLR_ORIG = 1536

TM_PROJ, TN_PROJ = 1024, 1536
TM_MERGE, TN_MERGE = 512, 512
TM_FFN, TF_FFN = 512, 512
FFN_DOWN_CHUNK = 512
GLA_CH = 128
GLA_NCS = 4
GLA_SUB = 8
GLA_LEVELS = (8, 16, 32, 64, 128)
PACK_CB = 256
PACK_PIECES = 2
S5_CH = 16
S5_GPB = LANES // S5_GROUP
S5_RT = 512
S5_TSPLIT = 4
ATT_UNROLL = 8


def _cparams(sem):
    return pltpu.CompilerParams(dimension_semantics=sem, vmem_limit_bytes=VMEM_LIMIT)


def _rms(x, g):
    ms = jnp.mean(x * x, axis=-1, keepdims=True)
    return x * lax.rsqrt(ms + NORM_EPS) * g


def _bdot(a, b):
    return jnp.dot(a, b, preferred_element_type=F32)


def _bdot_nt(a, b):
    return lax.dot_general(a, b, (((1,), (1,)), ((), ())), preferred_element_type=F32)


def _cast_specs(ws, layer, nsteps, step_of):
    ins, outs, shapes = [], [], []
    for w in ws:
        _, r, c = w.shape
        rb = r // nsteps
        assert rb * nsteps == r and rb % (2 * SUBLANES) == 0
        ins.append(pl.BlockSpec((None, rb, c), lambda *g: (layer, step_of(*g), 0)))
        outs.append(pl.BlockSpec((rb, c), lambda *g: (step_of(*g), 0)))
        shapes.append(jax.ShapeDtypeStruct((r, c), BF16))
    return ins, outs, shapes


def _side_cast(in_refs, out_refs):
    for src, dst in zip(in_refs, out_refs):
        dst[...] = src[...].astype(BF16)


def _rope_apply(x, c, sa, sb):
    half = ROPE_DIM // 2
    return x * c + pltpu.roll(x, LANES - half, axis=1) * sa + pltpu.roll(x, half, axis=1) * sb


def _inproj_kernel(x_ref, g_ref, w_ref, o_ref, h_ref):
    @pl.when(pl.program_id(1) == 0)
    def _():
        h_ref[...] = _rms(x_ref[...], g_ref[...]).astype(BF16)

    o_ref[...] = _bdot(h_ref[...], w_ref[...])


def _inproj(x, g, w_all, layer):
    t, d = x.shape
    n = N_SMALL
    tm, tn = min(TM_PROJ, t), TN_PROJ
    return pl.pallas_call(
        _inproj_kernel,
        grid=(t // tm, n // tn),
        in_specs=[
            pl.BlockSpec((tm, d), lambda i, j: (i, 0)),
            pl.BlockSpec((1, d), lambda i, j: (0, 0)),
            pl.BlockSpec((None, d, tn), lambda i, j: (layer, 0, j)),
        ],
        out_specs=pl.BlockSpec((tm, tn), lambda i, j: (i, j)),
        out_shape=jax.ShapeDtypeStruct((t, n), F32),
        scratch_shapes=[pltpu.VMEM((tm, d), BF16)],
        compiler_params=_cparams(("parallel", "arbitrary")),
        name="inproj",
    )(x, g, w_all)


def _rope_kernel(pos_ref, cos_ref, sa_ref, sb_ref):
    pos = pos_ref[0].astype(F32)
    lane = lax.broadcasted_iota(jnp.int32, (SUBLANES, LANES), 1)
    d = lane % HEAD_DIM
    half = ROPE_DIM // 2
    fi = (d % half).astype(F32) / half
    inv = jnp.power(jnp.full((SUBLANES, LANES), ROPE_THETA, F32), -fi)[0:1]
    d1 = d[0:1]
    ang = pos * inv
    c, s = jnp.cos(ang), jnp.sin(ang)
    cos_ref[0] = jnp.where(d1 < ROPE_DIM, c, 1.0)
    sa_ref[0] = jnp.where(d1 < half, -s, 0.0)
    sb_ref[0] = jnp.where((d1 >= half) & (d1 < ROPE_DIM), s, 0.0)


def _rope_tables(positions):
    b, l = positions.shape
    spec = pl.BlockSpec((1, l, LANES), lambda i: (i, 0, 0))
    shp = jax.ShapeDtypeStruct((b, l, LANES), F32)
    return pl.pallas_call(
        _rope_kernel,
        grid=(b,),
        in_specs=[pl.BlockSpec((1, l, 1), lambda i: (i, 0, 0))],
        out_specs=[spec, spec, spec],
        out_shape=[shp, shp, shp],
        compiler_params=_cparams(("parallel",)),
        name="rope_tables",
    )(positions.reshape(b, l, 1))


def _band_bias(strict_prev):
    t = ATT_BLOCK
    qi = np.arange(2 * t)[:, None] % t
    kj = np.arange(2 * t)[None, :]
    own = (kj >= t) & (kj - t <= qi)
    prev = (kj < t) & (kj >= qi + (1 if strict_prev else 0))
    ninf = np.float32(-np.inf)
    return np.stack([np.where(own | prev, 0.0, ninf), np.where(own, 0.0, ninf)]).astype(np.float32)


def _band_block(q, kp, ko, vp, vo, bias):
    t = ATT_BLOCK
    lane = lax.broadcasted_iota(jnp.int32, (t, LANES), 1)
    in_a = lane < HEAD_DIM
    q2 = jnp.concatenate([jnp.where(in_a, q, 0.0), jnp.where(in_a, 0.0, q)], axis=0).astype(BF16)
    s = _bdot_nt(q2, jnp.concatenate([kp, ko], axis=0).astype(BF16)) + bias
    m = jnp.max(s, axis=-1, keepdims=True)
    p = jnp.exp(s - m).astype(BF16)
    ones = jnp.ones((2 * t, LANES), BF16)
    vcat = jnp.concatenate([jnp.concatenate([vp, vo], axis=0).astype(BF16), ones], axis=1)
    od = _bdot(p, vcat)
    o = jnp.where(in_a, od[:t, :LANES], od[t:, :LANES])
    df = jnp.where(in_a, od[:t, LANES:], od[t:, LANES:])
    mf = jnp.where(in_a, m[:t], m[t:])
    return mf, df, o


def _dil_kernel(q_ref, k_ref, v_ref, c_ref, sa_ref, sb_ref, bias_ref, o_ref, qs, ks, qd, kd, vd, ms, ds, os_):
    l = q_ref.shape[1]
    c, sa, sb = c_ref[0], sa_ref[0], sb_ref[0]
    qs[...] = _rope_apply(q_ref[0], c, sa, sb) * (HEAD_DIM**-0.5)
    ks[...] = _rope_apply(k_ref[0], c, sa, sb)
    t = ATT_BLOCK
    nblk = DIL_SPAN // t
    dw = DIL_SPAN // t
    per = l // dw
    for r in range(dw):
        src, dst = pl.ds(r, per, stride=dw), slice(r * per, (r + 1) * per)
        qd[dst, :] = qs[src, :]
        kd[dst, :] = ks[src, :]
        vd[dst, :] = v_ref[0, src, :]

    for sblk in range(l // DIL_SPAN):
        base = sblk * DIL_SPAN
        for ci, (_, dil) in enumerate(DIL_CONFIGS):

            def body_wide(r, carry, ci=ci, sblk=sblk):
                rows = pl.ds(pl.multiple_of(r * per + sblk * t, t), t)
                prow = pl.ds(pl.multiple_of(r * per + max(sblk - 1, 0) * t, t), t)
                mf, df, o = _band_block(qd[rows, :], kd[prow, :], kd[rows, :], vd[prow, :], vd[rows, :],
                                        bias_ref[0 if sblk > 0 else 1])
                lrow = pl.ds(r, t, stride=dw)
                ms[ci, lrow, :] = mf
                ds[ci, lrow, :] = df
                os_[ci, lrow, :] = o
                return carry

            if dil == dw:
                lax.fori_loop(0, nblk, body_wide, 0, unroll=ATT_UNROLL)
                continue

            def body(idx, carry, dil=dil, ci=ci, base=base):
                r = idx % dil
                n = idx // dil
                loc = r + dil * t * n
                start = base + loc
                has_prev = start >= dil * t
                pstart = jnp.where(has_prev, start - dil * t, start)
                if dil == 1:
                    start = pl.multiple_of(start, t)
                    pstart = pl.multiple_of(pstart, t)
                    loc = pl.multiple_of(loc, t)
                    rows, prow, lrow = pl.ds(start, t), pl.ds(pstart, t), pl.ds(loc, t)
                else:
                    rows = pl.ds(start, t, stride=dil)
                    prow = pl.ds(pstart, t, stride=dil)
                    lrow = pl.ds(loc, t, stride=dil)
                mf, df, o = _band_block(
                    qs[rows, :], ks[prow, :], ks[rows, :], v_ref[0, prow, :], v_ref[0, rows, :],
                    bias_ref[jnp.where(has_prev, 0, 1)],
                )
                ms[ci, lrow, :] = mf
                ds[ci, lrow, :] = df
                os_[ci, lrow, :] = o
                return carry

            lax.fori_loop(0, nblk, body, 0, unroll=ATT_UNROLL)

        m0, m1, m2 = ms[0], ms[1], ms[2]
        mx = jnp.maximum(jnp.maximum(m0, m1), m2)
        w0, w1, w2 = jnp.exp(m0 - mx), jnp.exp(m1 - mx), jnp.exp(m2 - mx)
        den = w0 * ds[0] + w1 * ds[1] + w2 * ds[2]
        num = w0 * os_[0] + w1 * os_[1] + w2 * os_[2]
        o_ref[0, base : base + DIL_SPAN, :] = (num / den).astype(o_ref.dtype)


def _dilated(proj3, tabs):
    b, l, _ = proj3.shape
    assert l % DIL_SPAN == 0
    bias = _band_bias(False)

    def col(c0):
        return pl.BlockSpec((1, l, LANES), lambda i, p: (i, 0, c0 // LANES + p))

    tab = pl.BlockSpec((1, l, LANES), lambda i, p: (i, 0, 0))
    return pl.pallas_call(
        _dil_kernel,
        grid=(b, BRANCH_WIDTH // LANES),
        in_specs=[col(C_CQ), col(C_CK), col(C_CV), tab, tab, tab, pl.BlockSpec(bias.shape, lambda i, p: (0, 0, 0))],
        out_specs=pl.BlockSpec((1, l, LANES), lambda i, p: (i, 0, p)),
        out_shape=jax.ShapeDtypeStruct((b, l, BRANCH_WIDTH), BF16),
        scratch_shapes=[pltpu.VMEM((l, LANES), F32)] * 5
        + [pltpu.VMEM((len(DIL_CONFIGS), DIL_SPAN, LANES), F32)] * 3,
        compiler_params=_cparams(("parallel", "parallel")),
        name="dilated_attn",
    )(proj3, proj3, proj3, *tabs, jnp.asarray(bias))


def _swa_kernel(*refs, n_cast):
    q_ref, k_ref, v_ref, c_ref, sa_ref, sb_ref, bias_ref, sink_ref = refs[:8]
    o_ref = refs[8 + n_cast]
    qs, ks, vs = refs[-3:]
    _side_cast(refs[8 : 8 + n_cast], refs[9 + n_cast : 9 + 2 * n_cast])
    l = q_ref.shape[1]
    c, sa, sb = c_ref[0], sa_ref[0], sb_ref[0]
    qs[...] = _rope_apply(q_ref[0], c, sa, sb) * (HEAD_DIM**-0.5)
    g = pl.program_id(1) // (SWA_HEADS // SWA_KV_HEADS // 2)
    lane = lax.broadcasted_iota(jnp.int32, (l, LANES), 1)
    keep = (lane // HEAD_DIM) == g
    k, v = _rope_apply(k_ref[0], c, sa, sb), v_ref[0]
    ks[...] = jnp.where(keep, k, pltpu.roll(k, HEAD_DIM, axis=1))
    vs[...] = jnp.where(keep, v, pltpu.roll(v, HEAD_DIM, axis=1))
    sink = sink_ref[0]
    t = ATT_BLOCK

    def body(n, carry):
        start = pl.multiple_of(n * t, t)
        has_prev = n > 0
        pstart = pl.multiple_of(jnp.where(has_prev, start - t, start), t)
        rows, prow = pl.ds(start, t), pl.ds(pstart, t)
        mf, df, o = _band_block(qs[rows, :], ks[prow, :], ks[rows, :], vs[prow, :], vs[rows, :],
                                bias_ref[jnp.where(has_prev, 0, 1)])
        lse = mf + jnp.log(df)
        o_ref[0, rows, :] = (o / df * jax.nn.sigmoid(lse - sink)).astype(o_ref.dtype)
        return carry

    lax.fori_loop(0, l // t, body, 0, unroll=ATT_UNROLL)


def _swa(proj3, tabs, sinks, cast=(), layer=0):
    b, l, _ = proj3.shape
    npair = BRANCH_WIDTH // LANES
    sink_l = jnp.repeat(sinks.astype(F32), HEAD_DIM).reshape(npair, 1, LANES)
    bias = _band_bias(True)
    tab = pl.BlockSpec((1, l, LANES), lambda i, p: (i, 0, 0))
    c_in, c_out, c_shape = _cast_specs(cast, layer, b * npair, lambda i, p: i * npair + p)
    return pl.pallas_call(
        functools.partial(_swa_kernel, n_cast=len(cast)),
        grid=(b, npair),
        in_specs=[
            pl.BlockSpec((1, l, LANES), lambda i, p: (i, 0, C_SQ // LANES + p)),
            pl.BlockSpec((1, l, LANES), lambda i, p: (i, 0, C_SK // LANES)),
            pl.BlockSpec((1, l, LANES), lambda i, p: (i, 0, C_SV // LANES)),
            tab,
            tab,
            tab,
            pl.BlockSpec(bias.shape, lambda i, p: (0, 0, 0)),
            pl.BlockSpec((1, 1, LANES), lambda i, p: (p, 0, 0)),
        ]
        + c_in,
        out_specs=[pl.BlockSpec((1, l, LANES), lambda i, p: (i, 0, p))] + c_out,
        out_shape=[jax.ShapeDtypeStruct((b, l, BRANCH_WIDTH), BF16)] + c_shape,
        scratch_shapes=[pltpu.VMEM((l, LANES), F32)] * 3,
        compiler_params=_cparams(("parallel", "parallel")),
        name="swa_attn",
    )(proj3, proj3, proj3, *tabs, jnp.asarray(bias), sink_l, *cast)


@functools.lru_cache(maxsize=None)
def _gla_consts():
    ch = GLA_CH
    r = np.arange(ch)[:, None]
    j = np.arange(ch)[None, :]
    dstack = (j <= r).astype(np.float32)
    lm = []
    for m in GLA_LEVELS[:-1]:
        mk = (((r // m) % 2 == 1) & ((j // m) == (r // m) - 1)).astype(np.float32)
        lm.append(np.tile(mk, (GLA_HEADS, 1)))
    lmask = np.stack(lm)
    dmask = np.tile(((r // GLA_SUB) == (j // GLA_SUB)).astype(np.float32), (1, GLA_HEADS))
    nsub = ch // GLA_SUB
    e = np.zeros((GLA_SUB, GLA_HEADS * GLA_DK, GLA_HEADS * ch), np.float32)
    for u in range(GLA_SUB):
        for h in range(GLA_HEADS):
            for s in range(nsub):
                e[u, h * GLA_DK : (h + 1) * GLA_DK, h * ch + GLA_SUB * s + u] = 1.0
    rr = np.arange(GLA_HEADS * GLA_DV)[:, None]
    cc = np.arange(GLA_HEADS * GLA_DK)[None, :]
    bd = ((rr // GLA_DV) == (cc // GLA_DK)).astype(np.float32)
    return dstack, lmask, dmask, e, bd


def _split3(x):
    hi = x.astype(BF16)
    r1 = x - hi.astype(F32)
    mid = r1.astype(BF16)
    lo = (r1 - mid.astype(F32)).astype(BF16)
    return hi, mid, lo


def _bcast_grp(x, m, u):
    r, w = x.shape
    x3 = x.reshape(r // m, m, w)
    return jnp.broadcast_to(x3[:, u : u + 1, :], x3.shape).reshape(r, w)


def _bcast_sub(x, u):
    return _bcast_grp(x, GLA_SUB, u)


def _gla_kernel(*refs, n_cast):
    (q_ref, k_ref, v_ref, r_ref, glr_ref, a2_ref, ab_ref, ng_ref, dst_ref, lmask_ref, dmask_ref, e_ref,
     bd_ref) = refs[:13]
    o_ref, st_ref = refs[13 + n_cast], refs[-1]
    _side_cast(refs[13 : 13 + n_cast], refs[14 + n_cast : 14 + 2 * n_cast])
    ch = GLA_CH
    rs = q_ref.shape[0]
    ncs = rs // ch
    nlev = len(GLA_LEVELS)
    hk = GLA_HEADS * GLA_DK

    @pl.when(pl.program_id(1) == 0)
    def _():
        st_ref[...] = jnp.zeros_like(st_ref)

    q = q_ref[...] * (GLA_DK**-0.5)
    k = k_ref[...]
    z = _bdot(glr_ref[...].astype(BF16), a2_ref[...]) + ab_ref[...]
    g = (jnp.minimum(z, 0.0) - jnp.log1p(jnp.exp(-jnp.abs(z)))) * (1.0 / GLA_TAU)
    g3 = _split3(g)
    dst = dst_ref[...]

    cums, excl = [], []
    for c in range(ncs):
        rows = slice(c * ch, (c + 1) * ch)
        cums.append(_bdot(dst, g3[0][rows]) + _bdot(dst, g3[1][rows]) + _bdot(dst, g3[2][rows]))
        excl.append(cums[c] - g[rows])

    def eq(c, li):
        m = GLA_LEVELS[li]
        return cums[c] - _bcast_grp(excl[c], m, 0)

    def ek(c, li):
        m = GLA_LEVELS[li]
        return _bcast_grp(cums[c], m, m - 1) - cums[c]

    cs = jnp.concatenate([eq(c, 0) for c in range(ncs)], axis=0)
    tsub = lax.broadcasted_iota(jnp.int32, (rs, hk), 0) % GLA_SUB
    arep = jnp.zeros((rs, GLA_HEADS * ch), F32)
    for u in range(GLA_SUB):
        dec = jnp.exp(jnp.where(tsub >= u, cs - _bcast_sub(cs, u), -jnp.inf))
        p = q * _bcast_sub(k, u) * dec
        arep = arep + _bdot(p.astype(BF16), e_ref[u])
    arep = arep * jnp.concatenate([dmask_ref[...]] * ncs, axis=0)

    lane_k = lax.broadcasted_iota(jnp.int32, (ch, hk), 1) // GLA_DK
    ng = ng_ref[...]
    bd = bd_ref[...]
    for c in range(ncs):
        rows = slice(c * ch, (c + 1) * ch)
        qc, kc = q[rows], k[rows]
        vc = v_ref[rows, :]
        aoff = jnp.zeros((GLA_HEADS * ch, ch), F32)
        for li in range(nlev - 1):
            qe = qc * jnp.exp(eq(c, li))
            ke = (kc * jnp.exp(ek(c, li))).astype(BF16)
            qst = jnp.concatenate([jnp.where(lane_k == h, qe, 0.0) for h in range(GLA_HEADS)], axis=0).astype(BF16)
            aoff = aoff + _bdot_nt(qst, ke) * lmask_ref[li]
        cum = eq(c, nlev - 1)
        st = st_ref[...]
        o_inter = _bdot_nt((qc * jnp.exp(cum)).astype(BF16), st.astype(BF16))
        vb = vc.astype(BF16)
        outs = []
        for h in range(GLA_HEADS):
            a_h = aoff[h * ch : (h + 1) * ch] + arep[rows, h * ch : (h + 1) * ch]
            o_h = _bdot(a_h.astype(BF16), vb[:, h * GLA_DV : (h + 1) * GLA_DV]) + o_inter[:, h * GLA_DV : (h + 1) * GLA_DV]
            ms = jnp.mean(o_h * o_h, axis=-1, keepdims=True)
            outs.append(o_h * lax.rsqrt(ms + NORM_EPS))
        o = jnp.concatenate(outs, axis=1) * ng
        rc = r_ref[rows, :]
        o_ref[rows, :] = (o * (rc * jax.nn.sigmoid(rc))).astype(o_ref.dtype)
        ke_last = (kc * jnp.exp(ek(c, nlev - 1))).astype(BF16)
        kv = _bdot(vc.T.astype(BF16), ke_last)
        st_ref[...] = st * jnp.exp(cum[ch - 1 : ch, :]) + kv * bd


def _gla(proj, b, a2, ab, ng, cast=(), layer=0):
    t = proj.shape[0]
    l = t // b
    rs = GLA_CH * GLA_NCS
    assert l % rs == 0
    ns = l // rs
    dstack, lmask, dmask, e, bd = _gla_consts()
    c_in, c_out, c_shape = _cast_specs(cast, layer, b * ns, lambda i, s: i * ns + s)
    a2p = jnp.zeros((LANES, GLA_HEADS * GLA_DK), BF16).at[:GLA_LOWRANK].set(a2.astype(BF16))

    def rowblk(w, c0):
        return pl.BlockSpec((rs, w), lambda i, s: (i * ns + s, c0 // w))

    def full(shape):
        nd = len(shape)
        return pl.BlockSpec(shape, lambda i, s: (0,) * nd)

    hk, hv = GLA_HEADS * GLA_DK, GLA_HEADS * GLA_DV
    return pl.pallas_call(
        functools.partial(_gla_kernel, n_cast=len(cast)),
        grid=(b, ns),
        in_specs=[
            rowblk(hk, C_GQ),
            rowblk(hk, C_GK),
            rowblk(hv, C_GV),
            rowblk(hv, C_GR),
            rowblk(LANES, C_GLR),
            full((LANES, hk)),
            full((1, hk)),
            full((1, hv)),
            full(dstack.shape),
            full(lmask.shape),
            full(dmask.shape),
            full(e.shape),
            full(bd.shape),
        ]
        + c_in,
        out_specs=[pl.BlockSpec((rs, hv), lambda i, s: (i * ns + s, 0))] + c_out,
        out_shape=[jax.ShapeDtypeStruct((t, hv), BF16)] + c_shape,
        scratch_shapes=[pltpu.VMEM((hv, hk), F32)],
        compiler_params=_cparams(("parallel", "arbitrary")),
        name="gla",
    )(
        proj, proj, proj, proj, proj, a2p, ab.reshape(1, hk).astype(F32), ng.reshape(1, hv).astype(F32),
        jnp.asarray(dstack, BF16), jnp.asarray(lmask), jnp.asarray(dmask), jnp.asarray(e, BF16), jnp.asarray(bd),
        *cast,
    )


def _s5_params(lam_re, lam_im, log_dt, b_re, b_im, c_re, c_im, d, nch):
    f = F32
    cs = S5_CH
    dt = jnp.exp(log_dt.astype(f))[:, None]
    lr, li = lam_re.astype(f), lam_im.astype(f)
    mag = jnp.exp(lr * dt)
    ab_re, ab_im = mag * jnp.cos(li * dt), mag * jnp.sin(li * dt)
    den = lr * lr + li * li
    z_re = ((ab_re - 1.0) * lr + ab_im * li) / den
    z_im = (ab_im * lr - (ab_re - 1.0) * li) / den
    br, bi = b_re.astype(f), b_im.astype(f)
    bb_re = z_re[..., None] * br - z_im[..., None] * bi
    bb_im = z_re[..., None] * bi + z_im[..., None] * br

    def apow(p):
        p = jnp.asarray(p, f)[:, None, None]
        m = jnp.exp(p * (lr * dt))
        return m * jnp.cos(p * (li * dt)), m * jnp.sin(p * (li * dt))

    cr, ci = c_re.astype(f), c_im.astype(f)
    p_re, p_im = apow(np.arange(cs + 1))
    nb, gpb = S5_GROUPS // S5_GPB, S5_GPB
    hw = gpb * S5_STATE
    eye = jnp.eye(gpb, dtype=f)

    def b_blockdiag(x):
        x = x.reshape(nb, gpb, S5_STATE, S5_GROUP).transpose(0, 1, 3, 2)
        x = (x[:, :, :, None, :] * eye[None, :, None, :, None]).reshape(nb, LANES, hw)
        return jnp.concatenate([x, x], axis=-1)

    def c_blockdiag(x):
        x = x.reshape(nb, gpb, S5_GROUP, S5_STATE).transpose(0, 1, 3, 2)
        return (x[:, :, :, None, :] * eye[None, :, None, :, None]).reshape(nb, hw, LANES)

    def lanes(a, b_):
        p = a.shape[0]
        return jnp.concatenate([a.reshape(p, nb, hw), b_.reshape(p, nb, hw)], axis=-1).transpose(1, 0, 2)

    rev = np.arange(cs - 1, -1, -1)
    pr, pi = p_re[rev], p_im[rev]
    pa, pb = lanes(pr, pi), lanes(-pi, pr)
    acol_re = jnp.broadcast_to(ab_re.reshape(nb, hw, 1), (nb, hw, LANES))
    acol_im = jnp.broadcast_to(ab_im.reshape(nb, hw, 1), (nb, hw, LANES))
    nstep = max(1, int(math.log2(nch)))
    s_re, s_im = apow(cs * (2 ** np.arange(nstep)))
    ar = s_re.reshape(nstep, nb, hw).transpose(1, 0, 2)
    ai = s_im.reshape(nstep, nb, hw).transpose(1, 0, 2)
    dd = d.astype(f).reshape(nb, 1, LANES)
    return (b_blockdiag(bb_re), b_blockdiag(bb_im), c_blockdiag(cr), c_blockdiag(ci), acol_re, acol_im,
            pa, pb, ar, ai, dd)


def _s5_kernel(u_ref, br_ref, bi_ref, cr_ref, ci_ref, acr_ref, aci_ref, pa_ref, pb_ref, ar_ref, ai_ref, d_ref,
               y_ref, mb_s, mc_s, kc_s, *, nchb):
    cs = S5_CH
    t = u_ref.shape[0]
    nch = t // cs
    hw = S5_GPB * S5_STATE
    br, bi = br_ref[0], bi_ref[0]
    pa, pb = pa_ref[0], pb_ref[0]
    for j in range(cs):
        mb_s[j * LANES : (j + 1) * LANES, :] = (br * pa[j : j + 1] + bi * pb[j : j + 1]).astype(BF16)
    x_re, x_im = cr_ref[0], ci_ref[0]
    kc = _bdot(mb_s[...], jnp.concatenate([x_re, -x_im], axis=0).astype(BF16)).astype(BF16)
    a_re, a_im = acr_ref[0], aci_ref[0]
    for tt in range(cs):
        x_re, x_im = x_re * a_re - x_im * a_im, x_re * a_im + x_im * a_re
        mc_s[:hw, tt * LANES : (tt + 1) * LANES] = x_re.astype(BF16)
        mc_s[hw:, tt * LANES : (tt + 1) * LANES] = (-x_im).astype(BF16)
    gl = cs // S5_TSPLIT * LANES
    for qq in range(S5_TSPLIT):
        kc_s[:, qq * LANES : (qq + 1) * LANES] = kc[(S5_TSPLIT - 1 - qq) * gl : (S5_TSPLIT - qq) * gl]
    ucat = jnp.concatenate([u_ref[pl.ds(j, nch, stride=cs), :] for j in range(cs)], axis=1).astype(BF16)
    e_all = _bdot(ucat, mb_s[...])
    cidx = lax.broadcasted_iota(jnp.int32, (nch, LANES), 0) % nchb
    ar, ai = ar_ref[0], ai_ref[0]
    hp_re, hp_im = [], []
    for g in range(hw // LANES):
        lo = g * LANES
        e_re, e_im = e_all[:, lo : lo + LANES], e_all[:, hw + lo : hw + lo + LANES]
        for kstep in range(ar.shape[0]):
            s = 2**kstep
            if s >= nchb:
                break
            s_re = jnp.where(cidx >= s, pltpu.roll(e_re, s, axis=0), 0.0)
            s_im = jnp.where(cidx >= s, pltpu.roll(e_im, s, axis=0), 0.0)
            k_re, k_im = ar[kstep : kstep + 1, lo : lo + LANES], ai[kstep : kstep + 1, lo : lo + LANES]
            e_re, e_im = e_re + s_re * k_re - s_im * k_im, e_im + s_re * k_im + s_im * k_re
        hp_re.append(jnp.where(cidx >= 1, pltpu.roll(e_re, 1, axis=0), 0.0).astype(BF16))
        hp_im.append(jnp.where(cidx >= 1, pltpu.roll(e_im, 1, axis=0), 0.0).astype(BF16))
    ycar = _bdot(jnp.concatenate(hp_re + hp_im, axis=1), mc_s[...])
    for tt in range(cs):
        y_ref[pl.ds(tt, nch, stride=cs), :] = ycar[:, tt * LANES : (tt + 1) * LANES]

    rt = min(S5_RT, t)
    rmod = lax.broadcasted_iota(jnp.int32, (rt, LANES), 0) % cs
    dvec = d_ref[0]
    glag = cs // S5_TSPLIT

    def tile(i, carry):
        rows = pl.ds(pl.multiple_of(i * rt, rt), rt)
        ut = u_ref[rows, :]
        ush = [jnp.where(rmod >= tau, pltpu.roll(ut, tau, axis=0), 0.0) for tau in range(glag - 1, 0, -1)] + [ut]
        zz = _bdot(jnp.concatenate(ush, axis=1).astype(BF16), kc_s[...])
        acc = y_ref[rows, :] + dvec * ut + zz[:, :LANES]
        for qq in range(1, S5_TSPLIT):
            part = pltpu.roll(zz[:, qq * LANES : (qq + 1) * LANES], qq * glag, axis=0)
            acc = acc + jnp.where(rmod >= qq * glag, part, 0.0)
        y_ref[rows, :] = acc
        return carry

    lax.fori_loop(0, t // rt, tile, 0)


def _s5_core(proj, b, params):
    t = proj.shape[0]
    nchb = t // b // S5_CH
    sw = S5_GPB * 2 * S5_STATE

    def bspec(a):
        return pl.BlockSpec((1,) + a.shape[1:], lambda p: (p, 0, 0))

    return pl.pallas_call(
        functools.partial(_s5_kernel, nchb=nchb),
        grid=(S5_GROUPS // S5_GPB,),
        in_specs=[pl.BlockSpec((t, LANES), lambda p: (0, C_S5 // LANES + p))] + [bspec(a) for a in params],
        out_specs=pl.BlockSpec((t, LANES), lambda p: (0, p)),
        out_shape=jax.ShapeDtypeStruct((t, BRANCH_WIDTH), F32),
        scratch_shapes=[
            pltpu.VMEM((S5_CH * LANES, sw), BF16),
            pltpu.VMEM((sw, S5_CH * LANES), BF16),
            pltpu.VMEM((S5_CH // S5_TSPLIT * LANES, S5_TSPLIT * LANES), BF16),
        ],
        compiler_params=_cparams(("parallel",)),
        name="s5_core",
    )(proj, *params)


def _merge_kernel(
    x_ref, g1_ref, gla_ref, ys5_ref, dil_ref, swa_ref, gluw_ref, glub_ref, wg0, wg1, wg2, wg3, wb_ref, wo_ref,
    o_ref, h_s, s5_s,
):
    j = pl.program_id(1)

    @pl.when(j == 0)
    def _():
        h_s[...] = _rms(x_ref[...], g1_ref[...]).astype(BF16)
        zz = jax.nn.gelu(ys5_ref[...])
        gate = jax.nn.sigmoid(_bdot(zz.astype(BF16), gluw_ref[...]) + glub_ref[...])
        s5_s[...] = (zz * gate).astype(BF16)
        o_ref[...] = jnp.zeros_like(o_ref)

    h = h_s[...]
    branches = (gla_ref[...], s5_s[...], dil_ref[...], swa_ref[...])
    mixed = None
    for m, (wg, br) in enumerate(zip((wg0, wg1, wg2, wg3), branches)):
        term = jax.nn.sigmoid(_bdot(h, wg[...])) * _bdot(br, wb_ref[m])
        mixed = term if mixed is None else mixed + term
    mixed = mixed.astype(BF16)
    d = o_ref.shape[1]
    for c0 in range(0, d, FFN_DOWN_CHUNK):
        cols = slice(c0, c0 + FFN_DOWN_CHUNK)
        o_ref[:, cols] += _bdot(mixed, wo_ref[:, cols])

    @pl.when(j == pl.num_programs(1) - 1)
    def _():
        o_ref[...] = x_ref[...] + o_ref[...]


def _merge(x, g1, o_gla, y_s5, o_dil, o_swa, gluw, glub, w_all, wb, wo, layer):
    t, d = x.shape
    tm, tn = min(TM_MERGE, t), TN_MERGE
    nj = d // tn
    bw = BRANCH_WIDTH

    def rowblk(w):
        return pl.BlockSpec((tm, w), lambda i, j: (i, 0))

    def gate_spec(m):
        return pl.BlockSpec((None, d, tn), lambda i, j, m=m: (layer, 0, N_SMALL // tn + m * nj + j))

    return pl.pallas_call(
        _merge_kernel,
        grid=(t // tm, nj),
        in_specs=[
            rowblk(d),
            pl.BlockSpec((1, d), lambda i, j: (0, 0)),
            rowblk(bw),
            rowblk(bw),
            rowblk(bw),
            rowblk(bw),
            pl.BlockSpec((None, bw, bw), lambda i, j: (layer, 0, 0)),
            pl.BlockSpec((1, bw), lambda i, j: (0, 0)),
            gate_spec(0),
            gate_spec(1),
            gate_spec(2),
            gate_spec(3),
            pl.BlockSpec((N_BRANCH, bw, tn), lambda i, j: (0, 0, j)),
            pl.BlockSpec((tn, d), lambda i, j: (j, 0)),
        ],
        out_specs=rowblk(d),
        out_shape=jax.ShapeDtypeStruct((t, d), F32),
        scratch_shapes=[pltpu.VMEM((tm, d), BF16), pltpu.VMEM((tm, bw), BF16)],
        compiler_params=_cparams(("parallel", "arbitrary")),
        name="merge",
    )(x, g1, o_gla, y_s5, o_dil, o_swa, gluw, glub, w_all, w_all, w_all, w_all, wb, wo)


def _ffn_kernel(x_ref, g2_ref, wg_ref, wu_ref, wd_ref, gf_ref, o_ref, h_s, *, final_norm):
    j = pl.program_id(1)

    @pl.when(j == 0)
    def _():
        h_s[...] = _rms(x_ref[...], g2_ref[...]).astype(BF16)
        o_ref[...] = jnp.zeros_like(o_ref)

    h = h_s[...]
    gate = _bdot(h, wg_ref[...])
    act = ((gate * jax.nn.sigmoid(gate)) * _bdot(h, wu_ref[...])).astype(BF16)
    d = o_ref.shape[1]
    for c0 in range(0, d, FFN_DOWN_CHUNK):
        cols = slice(c0, c0 + FFN_DOWN_CHUNK)
        o_ref[:, cols] += _bdot(act, wd_ref[:, cols])

    @pl.when(j == pl.num_programs(1) - 1)
    def _():
        y = x_ref[...] + o_ref[...]
        o_ref[...] = _rms(y, gf_ref[...]) if final_norm else y


def _ffn(x, g2, wg, wu, wd, gf, final_norm):
    t, d = x.shape
    fh = wg.shape[-1]
    tm, tf = min(TM_FFN, t), TF_FFN
    return pl.pallas_call(
        functools.partial(_ffn_kernel, final_norm=final_norm),
        grid=(t // tm, fh // tf),
        in_specs=[
            pl.BlockSpec((tm, d), lambda i, j: (i, 0)),
            pl.BlockSpec((1, d), lambda i, j: (0, 0)),
            pl.BlockSpec((d, tf), lambda i, j: (0, j)),
            pl.BlockSpec((d, tf), lambda i, j: (0, j)),
            pl.BlockSpec((tf, d), lambda i, j: (j, 0)),
            pl.BlockSpec((1, d), lambda i, j: (0, 0)),
        ],
        out_specs=pl.BlockSpec((tm, d), lambda i, j: (i, 0)),
        out_shape=jax.ShapeDtypeStruct((t, d), F32),
        scratch_shapes=[pltpu.VMEM((tm, d), BF16)],
        compiler_params=_cparams(("parallel", "arbitrary")),
        name="ffn",
    )(x, g2, wg, wu, wd, gf)


def _pack_kernel(*refs):
    o_ref = refs[-1]
    for piece, w_ref in enumerate(refs[:-1]):
        x = w_ref[...]
        row = lax.broadcasted_iota(jnp.int32, x.shape, 0)
        blk = pl.program_id(1) * PACK_PIECES + piece
        keep = jnp.where(blk == C_GLR // PACK_CB, GLA_LOWRANK, PACK_CB)
        o_ref[:, piece * PACK_CB : (piece + 1) * PACK_CB] = jnp.where(row < keep, x, 0.0).T.astype(BF16)


def _pack_row0(c):
    cb = PACK_CB
    u = GLA_LOWRANK
    front = jnp.where(c < GLR_ORIG // cb, c * (cb // u), c * (cb // u) + 1)
    back = jnp.where(c == C_GLR // cb, GLR_ORIG // u, c * (cb // u) - (N_SMALL - N_ORIG_SMALL) // u)
    return jnp.where(c < C_GLR // cb, front, back) * u


def _pack_w_in(w_in):
    depth, d, d_in = w_in.shape
    n_out = N_SMALL + d_in - N_ORIG_SMALL
    cb = PACK_CB
    assert GLR_ORIG % cb == 0 and C_GLR % cb == 0 and N_SMALL % cb == 0 and n_out % cb == 0
    assert C_GLR == N_ORIG_SMALL - GLA_LOWRANK and C_GLR + cb == N_SMALL
    np_ = PACK_PIECES
    assert n_out % (cb * np_) == 0
    w_t = jnp.swapaxes(w_in, 1, 2)
    return pl.pallas_call(
        _pack_kernel,
        grid=(depth, n_out // (cb * np_)),
        in_specs=[
            pl.BlockSpec((None, pl.Element(cb), pl.Element(d)), lambda i, c, p=p: (i, _pack_row0(c * np_ + p), 0))
            for p in range(np_)
        ],
        out_specs=pl.BlockSpec((None, d, cb * np_), lambda i, c: (i, 0, c)),
        out_shape=jax.ShapeDtypeStruct((depth, d, n_out), BF16),
        compiler_params=_cparams(("parallel", "parallel")),
        name="pack_w_in",
    )(*([w_t] * np_))


def kernel(x, positions, norm1_g, w_in, gla_a2, gla_a_b, gla_norm_g, s5_lambda_re, s5_lambda_im, s5_log_dt, s5_b_re, s5_b_im, s5_c_re, s5_c_im, s5_d, s5_glu_w, s5_glu_b, swa_sinks, w_branch, w_out, norm2_g, w_ffn_gate, w_ffn_up, w_ffn_down, final_norm_g):
    b, l, d = x.shape
    t = b * l
    depth = w_in.shape[0]
    xs = x.reshape(t, d).astype(F32)
    w_all = _pack_w_in(w_in)
    w_br32 = w_branch.reshape(depth, N_BRANCH * BRANCH_WIDTH, d)
    glu_w = s5_glu_w.astype(BF16)
    tabs = _rope_tables(positions)
    gf = final_norm_g.reshape(1, d).astype(F32)
    for i in range(depth):
        g1 = norm1_g[i].reshape(1, d).astype(F32)
        proj = _inproj(xs, g1, w_all, i)
        proj3 = proj.reshape(b, l, N_SMALL)
        o_gla, w_fg, w_fu, w_br, w_o = _gla(proj, b, gla_a2[i], gla_a_b[i], gla_norm_g[i],
                                            (w_ffn_gate, w_ffn_up, w_br32, w_out), i)
        w_br = w_br.reshape(N_BRANCH, BRANCH_WIDTH, d)
        s5p = _s5_params(s5_lambda_re[i], s5_lambda_im[i], s5_log_dt[i], s5_b_re[i], s5_b_im[i], s5_c_re[i],
                         s5_c_im[i], s5_d[i], l // S5_CH)
        y_s5 = _s5_core(proj, b, s5p)
        o_dil = _dilated(proj3, tabs).reshape(t, BRANCH_WIDTH)
        o_swa, w_fd = _swa(proj3, tabs, swa_sinks[i], (w_ffn_down,), i)
        o_swa = o_swa.reshape(t, BRANCH_WIDTH)
        xs = _merge(xs, g1, o_gla, y_s5, o_dil, o_swa, glu_w, s5_glu_b[i].reshape(1, -1).astype(F32),
                    w_all, w_br, w_o, i)
        xs = _ffn(xs, norm2_g[i].reshape(1, d).astype(F32), w_fg, w_fu, w_fd, gf, i == depth - 1)
    return xs.reshape(b, l, d).astype(x.dtype)
```

```python
import functools
import math

import jax
import jax.numpy as jnp
import numpy as np
from jax import lax
from jax.experimental import pallas as pl
from jax.experimental.pallas import tpu as pltpu

F32 = jnp.float32
BF16 = jnp.bfloat16
HIGHEST = lax.Precision.HIGHEST

D_MODEL = 2048
DEPTH = 2
N_BRANCH = 4
BRANCH_WIDTH = 512
HEAD_DIM = 64
ATT_BLOCK = 128
ROPE_THETA = 500000.0
ROPE_DIM = HEAD_DIM // 4
NORM_EPS = 1e-6
GLA_HEADS = 4
GLA_DK = 64
GLA_DV = BRANCH_WIDTH // GLA_HEADS
GLA_LOWRANK = 16
GLA_TAU = 16.0
S5_GROUP = 16
S5_GROUPS = BRANCH_WIDTH // S5_GROUP
S5_STATE = 64
DIL_CONFIGS = ((128, 1), (512, 4), (2048, 16))
DIL_SPAN = ATT_BLOCK * 16
SWA_HEADS = BRANCH_WIDTH // HEAD_DIM
SWA_KV_HEADS = 2
SWA_WINDOW = 128
FFN_HIDDEN = -((-8 * D_MODEL) // (3 * 256)) * 256

LANES = 128
SUBLANES = 8
VMEM_LIMIT = 56 * 1024 * 1024
VMEM_LIMIT_FFN = 60 * 1024 * 1024

C_GQ, C_GK, C_GV, C_GR, C_S5 = 0, 256, 512, 1024, 1536
C_CQ, C_CK, C_CV, C_SQ, C_SK, C_SV, C_GLR = 2048, 2560, 3072, 3584, 4096, 4224, 4352
N_SMALL = 4608
N_ORIG_SMALL = 4368
GLR_ORIG = 1536

TM_PROJ, TN_PROJ = 1024, 1536
TM_MERGE, TN_MERGE = 512, 512
TM_FFN, TF_FFN = 1024, 512
FFN_ROW_CHUNK = 512
FFN_DOWN_CHUNK = 512
GLA_CH = 128
GLA_NCS = 4
GLA_SUB = 8
GLA_LEVELS = (8, 16, 32, 64, 128)
PACK_CB = 256
PACK_PIECES = 2
S5_CH = 16
S5_GPB = LANES // S5_GROUP
S5_RT = 512
S5_TSPLIT = 4
ATT_UNROLL = 8


def _cparams(sem, vmem_limit=VMEM_LIMIT):
    return pltpu.CompilerParams(dimension_semantics=sem, vmem_limit_bytes=vmem_limit)


def _rms(x, g):
    ms = jnp.mean(x * x, axis=-1, keepdims=True)
    return x * lax.rsqrt(ms + NORM_EPS) * g


def _bdot(a, b):
    return jnp.dot(a, b, preferred_element_type=F32)


def _bdot_nt(a, b):
    return lax.dot_general(a, b, (((1,), (1,)), ((), ())), preferred_element_type=F32)


def _cast_specs(ws, layer, nsteps, step_of):
    ins, outs, shapes = [], [], []
    for w in ws:
        _, r, c = w.shape
        rb = r // nsteps
        assert rb * nsteps == r and rb % (2 * SUBLANES) == 0
        ins.append(pl.BlockSpec((None, rb, c), lambda *g: (layer, step_of(*g), 0)))
        outs.append(pl.BlockSpec((rb, c), lambda *g: (step_of(*g), 0)))
        shapes.append(jax.ShapeDtypeStruct((r, c), BF16))
    return ins, outs, shapes


def _side_cast(in_refs, out_refs):
    for src, dst in zip(in_refs, out_refs):
        dst[...] = src[...].astype(BF16)


def _rope_apply(x, c, sa, sb):
    half = ROPE_DIM // 2
    return x * c + pltpu.roll(x, LANES - half, axis=1) * sa + pltpu.roll(x, half, axis=1) * sb


def _inproj_kernel(x_ref, g_ref, w_ref, o_ref, h_ref):
    @pl.when(pl.program_id(1) == 0)
    def _():
        h_ref[...] = _rms(x_ref[...], g_ref[...]).astype(BF16)

    o_ref[...] = _bdot(h_ref[...], w_ref[...])


def _inproj(x, g, w_all, layer):
    t, d = x.shape
    n = N_SMALL
    tm, tn = min(TM_PROJ, t), TN_PROJ
    return pl.pallas_call(
        _inproj_kernel,
        grid=(t // tm, n // tn),
        in_specs=[
            pl.BlockSpec((tm, d), lambda i, j: (i, 0)),
            pl.BlockSpec((1, d), lambda i, j: (0, 0)),
            pl.BlockSpec((None, d, tn), lambda i, j: (layer, 0, j)),
        ],
        out_specs=pl.BlockSpec((tm, tn), lambda i, j: (i, j)),
        out_shape=jax.ShapeDtypeStruct((t, n), F32),
        scratch_shapes=[pltpu.VMEM((tm, d), BF16)],
        compiler_params=_cparams(("parallel", "arbitrary")),
        name="inproj",
    )(x, g, w_all)


def _rope_kernel(pos_ref, cos_ref, sa_ref, sb_ref):
    pos = pos_ref[0].astype(F32)
    lane = lax.broadcasted_iota(jnp.int32, (SUBLANES, LANES), 1)
    d = lane % HEAD_DIM
    half = ROPE_DIM // 2
    fi = (d % half).astype(F32) / half
    inv = jnp.power(jnp.full((SUBLANES, LANES), ROPE_THETA, F32), -fi)[0:1]
    d1 = d[0:1]
    ang = pos * inv
    c, s = jnp.cos(ang), jnp.sin(ang)
    cos_ref[0] = jnp.where(d1 < ROPE_DIM, c, 1.0)
    sa_ref[0] = jnp.where(d1 < half, -s, 0.0)
    sb_ref[0] = jnp.where((d1 >= half) & (d1 < ROPE_DIM), s, 0.0)


def _rope_tables(positions):
    b, l = positions.shape
    spec = pl.BlockSpec((1, l, LANES), lambda i: (i, 0, 0))
    shp = jax.ShapeDtypeStruct((b, l, LANES), F32)
    return pl.pallas_call(
        _rope_kernel,
        grid=(b,),
        in_specs=[pl.BlockSpec((1, l, 1), lambda i: (i, 0, 0))],
        out_specs=[spec, spec, spec],
        out_shape=[shp, shp, shp],
        compiler_params=_cparams(("parallel",)),
        name="rope_tables",
    )(positions.reshape(b, l, 1))


def _band_bias(strict_prev):
    t = ATT_BLOCK
    qi = np.arange(2 * t)[:, None] % t
    kj = np.arange(2 * t)[None, :]
    own = (kj >= t) & (kj - t <= qi)
    prev = (kj < t) & (kj >= qi + (1 if strict_prev else 0))
    ninf = np.float32(-np.inf)
    return np.stack([np.where(own | prev, 0.0, ninf), np.where(own, 0.0, ninf)]).astype(np.float32)


def _band_block(q, kp, ko, vp, vo, bias):
    t = ATT_BLOCK
    lane = lax.broadcasted_iota(jnp.int32, (t, LANES), 1)
    in_a = lane < HEAD_DIM
    q2 = jnp.concatenate([jnp.where(in_a, q, 0.0), jnp.where(in_a, 0.0, q)], axis=0).astype(BF16)
    s = _bdot_nt(q2, jnp.concatenate([kp, ko], axis=0).astype(BF16)) + bias
    m = jnp.max(s, axis=-1, keepdims=True)
    p = jnp.exp(s - m).astype(BF16)
    ones = jnp.ones((2 * t, LANES), BF16)
    vcat = jnp.concatenate([jnp.concatenate([vp, vo], axis=0).astype(BF16), ones], axis=1)
    od = _bdot(p, vcat)
    o = jnp.where(in_a, od[:t, :LANES], od[t:, :LANES])
    df = jnp.where(in_a, od[:t, LANES:], od[t:, LANES:])
    mf = jnp.where(in_a, m[:t], m[t:])
    return mf, df, o


def _dil_kernel(q_ref, k_ref, v_ref, c_ref, sa_ref, sb_ref, bias_ref, o_ref, qs, ks, qd, kd, vd, ms, ds, os_):
    l = q_ref.shape[1]
    c, sa, sb = c_ref[0], sa_ref[0], sb_ref[0]
    qs[...] = _rope_apply(q_ref[0], c, sa, sb) * (HEAD_DIM**-0.5)
    ks[...] = _rope_apply(k_ref[0], c, sa, sb)
    t = ATT_BLOCK
    nblk = DIL_SPAN // t
    dw = DIL_SPAN // t
    per = l // dw
    for r in range(dw):
        src, dst = pl.ds(r, per, stride=dw), slice(r * per, (r + 1) * per)
        qd[dst, :] = qs[src, :]
        kd[dst, :] = ks[src, :]
        vd[dst, :] = v_ref[0, src, :]

    for sblk in range(l // DIL_SPAN):
        base = sblk * DIL_SPAN
        for ci, (_, dil) in enumerate(DIL_CONFIGS):

            def body_wide(r, carry, ci=ci, sblk=sblk):
                rows = pl.ds(pl.multiple_of(r * per + sblk * t, t), t)
                prow = pl.ds(pl.multiple_of(r * per + max(sblk - 1, 0) * t, t), t)
                mf, df, o = _band_block(qd[rows, :], kd[prow, :], kd[rows, :], vd[prow, :], vd[rows, :],
                                        bias_ref[0 if sblk > 0 else 1])
                lrow = pl.ds(r, t, stride=dw)
                ms[ci, lrow, :] = mf
                ds[ci, lrow, :] = df
                os_[ci, lrow, :] = o
                return carry

            if dil == dw:
                lax.fori_loop(0, nblk, body_wide, 0, unroll=ATT_UNROLL)
                continue

            def body(idx, carry, dil=dil, ci=ci, base=base):
                r = idx % dil
                n = idx // dil
                loc = r + dil * t * n
                start = base + loc
                has_prev = start >= dil * t
                pstart = jnp.where(has_prev, start - dil * t, start)
                if dil == 1:
                    start = pl.multiple_of(start, t)
                    pstart = pl.multiple_of(pstart, t)
                    loc = pl.multiple_of(loc, t)
                    rows, prow, lrow = pl.ds(start, t), pl.ds(pstart, t), pl.ds(loc, t)
                else:
                    rows = pl.ds(start, t, stride=dil)
                    prow = pl.ds(pstart, t, stride=dil)
                    lrow = pl.ds(loc, t, stride=dil)
                mf, df, o = _band_block(
                    qs[rows, :], ks[prow, :], ks[rows, :], v_ref[0, prow, :], v_ref[0, rows, :],
                    bias_ref[jnp.where(has_prev, 0, 1)],
                )
                ms[ci, lrow, :] = mf
                ds[ci, lrow, :] = df
                os_[ci, lrow, :] = o
                return carry

            lax.fori_loop(0, nblk, body, 0, unroll=ATT_UNROLL)

        m0, m1, m2 = ms[0], ms[1], ms[2]
        mx = jnp.maximum(jnp.maximum(m0, m1), m2)
        w0, w1, w2 = jnp.exp(m0 - mx), jnp.exp(m1 - mx), jnp.exp(m2 - mx)
        den = w0 * ds[0] + w1 * ds[1] + w2 * ds[2]
        num = w0 * os_[0] + w1 * os_[1] + w2 * os_[2]
        o_ref[0, base : base + DIL_SPAN, :] = (num / den).astype(o_ref.dtype)


def _dilated(proj3, tabs):
    b, l, _ = proj3.shape
    assert l % DIL_SPAN == 0
    bias = _band_bias(False)

    def col(c0):
        return pl.BlockSpec((1, l, LANES), lambda i, p: (i, 0, c0 // LANES + p))

    tab = pl.BlockSpec((1, l, LANES), lambda i, p: (i, 0, 0))
    return pl.pallas_call(
        _dil_kernel,
        grid=(b, BRANCH_WIDTH // LANES),
        in_specs=[col(C_CQ), col(C_CK), col(C_CV), tab, tab, tab, pl.BlockSpec(bias.shape, lambda i, p: (0, 0, 0))],
        out_specs=pl.BlockSpec((1, l, LANES), lambda i, p: (i, 0, p)),
        out_shape=jax.ShapeDtypeStruct((b, l, BRANCH_WIDTH), BF16),
        scratch_shapes=[pltpu.VMEM((l, LANES), F32)] * 5
        + [pltpu.VMEM((len(DIL_CONFIGS), DIL_SPAN, LANES), F32)] * 3,
        compiler_params=_cparams(("parallel", "parallel")),
        name="dilated_attn",
    )(proj3, proj3, proj3, *tabs, jnp.asarray(bias))


def _swa_kernel(*refs, n_cast):
    q_ref, k_ref, v_ref, c_ref, sa_ref, sb_ref, bias_ref, sink_ref = refs[:8]
    o_ref = refs[8 + n_cast]
    qs, ks, vs = refs[-3:]
    _side_cast(refs[8 : 8 + n_cast], refs[9 + n_cast : 9 + 2 * n_cast])
    l = q_ref.shape[1]
    c, sa, sb = c_ref[0], sa_ref[0], sb_ref[0]
    qs[...] = _rope_apply(q_ref[0], c, sa, sb) * (HEAD_DIM**-0.5)
    g = pl.program_id(1) // (SWA_HEADS // SWA_KV_HEADS // 2)
    lane = lax.broadcasted_iota(jnp.int32, (l, LANES), 1)
    keep = (lane // HEAD_DIM) == g
    k, v = _rope_apply(k_ref[0], c, sa, sb), v_ref[0]
    ks[...] = jnp.where(keep, k, pltpu.roll(k, HEAD_DIM, axis=1))
    vs[...] = jnp.where(keep, v, pltpu.roll(v, HEAD_DIM, axis=1))
    sink = sink_ref[0]
    t = ATT_BLOCK

    def body(n, carry):
        start = pl.multiple_of(n * t, t)
        has_prev = n > 0
        pstart = pl.multiple_of(jnp.where(has_prev, start - t, start), t)
        rows, prow = pl.ds(start, t), pl.ds(pstart, t)
        mf, df, o = _band_block(qs[rows, :], ks[prow, :], ks[rows, :], vs[prow, :], vs[rows, :],
                                bias_ref[jnp.where(has_prev, 0, 1)])
        lse = mf + jnp.log(df)
        o_ref[0, rows, :] = (o / df * jax.nn.sigmoid(lse - sink)).astype(o_ref.dtype)
        return carry

    lax.fori_loop(0, l // t, body, 0, unroll=ATT_UNROLL)


def _swa(proj3, tabs, sinks, cast=(), layer=0):
    b, l, _ = proj3.shape
    npair = BRANCH_WIDTH // LANES
    sink_l = jnp.repeat(sinks.astype(F32), HEAD_DIM).reshape(npair, 1, LANES)
    bias = _band_bias(True)
    tab = pl.BlockSpec((1, l, LANES), lambda i, p: (i, 0, 0))
    c_in, c_out, c_shape = _cast_specs(cast, layer, b * npair, lambda i, p: i * npair + p)
    return pl.pallas_call(
        functools.partial(_swa_kernel, n_cast=len(cast)),
        grid=(b, npair),
        in_specs=[
            pl.BlockSpec((1, l, LANES), lambda i, p: (i, 0, C_SQ // LANES + p)),
            pl.BlockSpec((1, l, LANES), lambda i, p: (i, 0, C_SK // LANES)),
            pl.BlockSpec((1, l, LANES), lambda i, p: (i, 0, C_SV // LANES)),
            tab,
            tab,
            tab,
            pl.BlockSpec(bias.shape, lambda i, p: (0, 0, 0)),
            pl.BlockSpec((1, 1, LANES), lambda i, p: (p, 0, 0)),
        ]
        + c_in,
        out_specs=[pl.BlockSpec((1, l, LANES), lambda i, p: (i, 0, p))] + c_out,
        out_shape=[jax.ShapeDtypeStruct((b, l, BRANCH_WIDTH), BF16)] + c_shape,
        scratch_shapes=[pltpu.VMEM((l, LANES), F32)] * 3,
        compiler_params=_cparams(("parallel", "parallel")),
        name="swa_attn",
    )(proj3, proj3, proj3, *tabs, jnp.asarray(bias), sink_l, *cast)


@functools.lru_cache(maxsize=None)
def _gla_consts():
    ch = GLA_CH
    r = np.arange(ch)[:, None]
    j = np.arange(ch)[None, :]
    dstack = (j <= r).astype(np.float32)
    lm = []
    for m in GLA_LEVELS[:-1]:
        mk = (((r // m) % 2 == 1) & ((j // m) == (r // m) - 1)).astype(np.float32)
        lm.append(np.tile(mk, (GLA_HEADS, 1)))
    lmask = np.stack(lm)
    dmask = np.tile(((r // GLA_SUB) == (j // GLA_SUB)).astype(np.float32), (1, GLA_HEADS))
    nsub = ch // GLA_SUB
    e = np.zeros((GLA_SUB, GLA_HEADS * GLA_DK, GLA_HEADS * ch), np.float32)
    for u in range(GLA_SUB):
        for h in range(GLA_HEADS):
            for s in range(nsub):
                e[u, h * GLA_DK : (h + 1) * GLA_DK, h * ch + GLA_SUB * s + u] = 1.0
    rr = np.arange(GLA_HEADS * GLA_DV)[:, None]
    cc = np.arange(GLA_HEADS * GLA_DK)[None, :]
    bd = ((rr // GLA_DV) == (cc // GLA_DK)).astype(np.float32)
    return dstack, lmask, dmask, e, bd


def _split3(x):
    hi = x.astype(BF16)
    r1 = x - hi.astype(F32)
    mid = r1.astype(BF16)
    lo = (r1 - mid.astype(F32)).astype(BF16)
    return hi, mid, lo


def _bcast_grp(x, m, u):
    r, w = x.shape
    x3 = x.reshape(r // m, m, w)
    return jnp.broadcast_to(x3[:, u : u + 1, :], x3.shape).reshape(r, w)


def _bcast_sub(x, u):
    return _bcast_grp(x, GLA_SUB, u)


def _gla_kernel(*refs, n_cast):
    (q_ref, k_ref, v_ref, r_ref, glr_ref, a2_ref, ab_ref, ng_ref, dst_ref, lmask_ref, dmask_ref, e_ref,
     bd_ref) = refs[:13]
    o_ref, st_ref = refs[13 + n_cast], refs[-1]
    _side_cast(refs[13 : 13 + n_cast], refs[14 + n_cast : 14 + 2 * n_cast])
    ch = GLA_CH
    rs = q_ref.shape[0]
    ncs = rs // ch
    nlev = len(GLA_LEVELS)
    hk = GLA_HEADS * GLA_DK

    @pl.when(pl.program_id(1) == 0)
    def _():
        st_ref[...] = jnp.zeros_like(st_ref)

    q = q_ref[...] * (GLA_DK**-0.5)
    k = k_ref[...]
    z = _bdot(glr_ref[...].astype(BF16), a2_ref[...]) + ab_ref[...]
    g = (jnp.minimum(z, 0.0) - jnp.log(1.0 + jnp.exp(-jnp.abs(z)))) * (1.0 / GLA_TAU)
    g3 = _split3(g)
    dst = dst_ref[...]

    cums, excl = [], []
    for c in range(ncs):
        rows = slice(c * ch, (c + 1) * ch)
        cums.append(_bdot(dst, g3[0][rows]) + _bdot(dst, g3[1][rows]) + _bdot(dst, g3[2][rows]))
        excl.append(cums[c] - g[rows])

    def eq(c, li):
        m = GLA_LEVELS[li]
        return cums[c] - _bcast_grp(excl[c], m, 0)

    def ek(c, li):
        m = GLA_LEVELS[li]
        return _bcast_grp(cums[c], m, m - 1) - cums[c]

    cs = jnp.concatenate([eq(c, 0) for c in range(ncs)], axis=0)
    tsub = lax.broadcasted_iota(jnp.int32, (rs, hk), 0) % GLA_SUB
    arep = jnp.zeros((rs, GLA_HEADS * ch), F32)
    for u in range(GLA_SUB):
        dec = jnp.exp(jnp.where(tsub >= u, cs - _bcast_sub(cs, u), -jnp.inf))
        p = q * _bcast_sub(k, u) * dec
        arep = arep + _bdot(p.astype(BF16), e_ref[u])
    arep = arep * jnp.concatenate([dmask_ref[...]] * ncs, axis=0)

    lane_k = lax.broadcasted_iota(jnp.int32, (ch, hk), 1) // GLA_DK
    ng = ng_ref[...]
    bd = bd_ref[...]
    for c in range(ncs):
        rows = slice(c * ch, (c + 1) * ch)
        qc, kc = q[rows], k[rows]
        vc = v_ref[rows, :]
        aoff = jnp.zeros((GLA_HEADS * ch, ch), F32)
        for li in range(nlev - 1):
            qe = qc * jnp.exp(eq(c, li))
            ke = (kc * jnp.exp(ek(c, li))).astype(BF16)
            qst = jnp.concatenate([jnp.where(lane_k == h, qe, 0.0) for h in range(GLA_HEADS)], axis=0).astype(BF16)
            aoff = aoff + _bdot_nt(qst, ke) * lmask_ref[li]
        cum = eq(c, nlev - 1)
        st = st_ref[...]
        o_inter = _bdot_nt((qc * jnp.exp(cum)).astype(BF16), st.astype(BF16))
        vb = vc.astype(BF16)
        outs = []
        for h in range(GLA_HEADS):
            a_h = aoff[h * ch : (h + 1) * ch] + arep[rows, h * ch : (h + 1) * ch]
            o_h = _bdot(a_h.astype(BF16), vb[:, h * GLA_DV : (h + 1) * GLA_DV]) + o_inter[:, h * GLA_DV : (h + 1) * GLA_DV]
            ms = jnp.mean(o_h * o_h, axis=-1, keepdims=True)
            outs.append(o_h * lax.rsqrt(ms + NORM_EPS))
        o = jnp.concatenate(outs, axis=1) * ng
        rc = r_ref[rows, :]
        o_ref[rows, :] = (o * (rc * jax.nn.sigmoid(rc))).astype(o_ref.dtype)
        ke_last = (kc * jnp.exp(ek(c, nlev - 1))).astype(BF16)
        kv = _bdot(vc.T.astype(BF16), ke_last)
        st_ref[...] = st * jnp.exp(cum[ch - 1 : ch, :]) + kv * bd


def _gla(proj, b, a2, ab, ng, cast=(), layer=0):
    t = proj.shape[0]
    l = t // b
    rs = GLA_CH * GLA_NCS
    assert l % rs == 0
    ns = l // rs
    dstack, lmask, dmask, e, bd = _gla_consts()
    c_in, c_out, c_shape = _cast_specs(cast, layer, b * ns, lambda i, s: i * ns + s)
    a2p = jnp.zeros((LANES, GLA_HEADS * GLA_DK), BF16).at[:GLA_LOWRANK].set(a2.astype(BF16))

    def rowblk(w, c0):
        return pl.BlockSpec((rs, w), lambda i, s: (i * ns + s, c0 // w))

    def full(shape):
        nd = len(shape)
        return pl.BlockSpec(shape, lambda i, s: (0,) * nd)

    hk, hv = GLA_HEADS * GLA_DK, GLA_HEADS * GLA_DV
    return pl.pallas_call(
        functools.partial(_gla_kernel, n_cast=len(cast)),
        grid=(b, ns),
        in_specs=[
            rowblk(hk, C_GQ),
            rowblk(hk, C_GK),
            rowblk(hv, C_GV),
            rowblk(hv, C_GR),
            rowblk(LANES, C_GLR),
            full((LANES, hk)),
            full((1, hk)),
            full((1, hv)),
            full(dstack.shape),
            full(lmask.shape),
            full(dmask.shape),
            full(e.shape),
            full(bd.shape),
        ]
        + c_in,
        out_specs=[pl.BlockSpec((rs, hv), lambda i, s: (i * ns + s, 0))] + c_out,
        out_shape=[jax.ShapeDtypeStruct((t, hv), BF16)] + c_shape,
        scratch_shapes=[pltpu.VMEM((hv, hk), F32)],
        compiler_params=_cparams(("parallel", "arbitrary")),
        name="gla",
    )(
        proj, proj, proj, proj, proj, a2p, ab.reshape(1, hk).astype(F32), ng.reshape(1, hv).astype(F32),
        jnp.asarray(dstack, BF16), jnp.asarray(lmask), jnp.asarray(dmask), jnp.asarray(e, BF16), jnp.asarray(bd),
        *cast,
    )


def _s5_params(lam_re, lam_im, log_dt, b_re, b_im, c_re, c_im, d, nch):
    f = F32
    cs = S5_CH
    dt = jnp.exp(log_dt.astype(f))[:, None]
    lr, li = lam_re.astype(f), lam_im.astype(f)
    mag = jnp.exp(lr * dt)
    ab_re, ab_im = mag * jnp.cos(li * dt), mag * jnp.sin(li * dt)
    den = lr * lr + li * li
    z_re = ((ab_re - 1.0) * lr + ab_im * li) / den
    z_im = (ab_im * lr - (ab_re - 1.0) * li) / den
    br, bi = b_re.astype(f), b_im.astype(f)
    bb_re = z_re[..., None] * br - z_im[..., None] * bi
    bb_im = z_re[..., None] * bi + z_im[..., None] * br

    def apow(p):
        p = jnp.asarray(p, f)[:, None, None]
        m = jnp.exp(p * (lr * dt))
        return m * jnp.cos(p * (li * dt)), m * jnp.sin(p * (li * dt))

    cr, ci = c_re.astype(f), c_im.astype(f)
    p_re, p_im = apow(np.arange(cs + 1))
    nb, gpb = S5_GROUPS // S5_GPB, S5_GPB
    hw = gpb * S5_STATE
    eye = jnp.eye(gpb, dtype=f)

    def b_blockdiag(x):
        x = x.reshape(nb, gpb, S5_STATE, S5_GROUP).transpose(0, 1, 3, 2)
        x = (x[:, :, :, None, :] * eye[None, :, None, :, None]).reshape(nb, LANES, hw)
        return jnp.concatenate([x, x], axis=-1)

    def c_blockdiag(x):
        x = x.reshape(nb, gpb, S5_GROUP, S5_STATE).transpose(0, 1, 3, 2)
        return (x[:, :, :, None, :] * eye[None, :, None, :, None]).reshape(nb, hw, LANES)

    def lanes(a, b_):
        p = a.shape[0]
        return jnp.concatenate([a.reshape(p, nb, hw), b_.reshape(p, nb, hw)], axis=-1).transpose(1, 0, 2)

    rev = np.arange(cs - 1, -1, -1)
    pr, pi = p_re[rev], p_im[rev]
    pa, pb = lanes(pr, pi), lanes(-pi, pr)
    acol_re = jnp.broadcast_to(ab_re.reshape(nb, hw, 1), (nb, hw, LANES))
    acol_im = jnp.broadcast_to(ab_im.reshape(nb, hw, 1), (nb, hw, LANES))
    nstep = max(1, int(math.log2(nch)))
    s_re, s_im = apow(cs * (2 ** np.arange(nstep)))
    ar = s_re.reshape(nstep, nb, hw).transpose(1, 0, 2)
    ai = s_im.reshape(nstep, nb, hw).transpose(1, 0, 2)
    dd = d.astype(f).reshape(nb, 1, LANES)
    return (b_blockdiag(bb_re), b_blockdiag(bb_im), c_blockdiag(cr), c_blockdiag(ci), acol_re, acol_im,
            pa, pb, ar, ai, dd)


def _s5_kernel(u_ref, br_ref, bi_ref, cr_ref, ci_ref, acr_ref, aci_ref, pa_ref, pb_ref, ar_ref, ai_ref, d_ref,
               y_ref, mb_s, mc_s, kc_s, *, nchb):
    cs = S5_CH
    t = u_ref.shape[0]
    nch = t // cs
    hw = S5_GPB * S5_STATE
    br, bi = br_ref[0], bi_ref[0]
    pa, pb = pa_ref[0], pb_ref[0]
    for j in range(cs):
        mb_s[j * LANES : (j + 1) * LANES, :] = (br * pa[j : j + 1] + bi * pb[j : j + 1]).astype(BF16)
    x_re, x_im = cr_ref[0], ci_ref[0]
    kc = _bdot(mb_s[...], jnp.concatenate([x_re, -x_im], axis=0).astype(BF16)).astype(BF16)
    a_re, a_im = acr_ref[0], aci_ref[0]
    for tt in range(cs):
        x_re, x_im = x_re * a_re - x_im * a_im, x_re * a_im + x_im * a_re
        mc_s[:hw, tt * LANES : (tt + 1) * LANES] = x_re.astype(BF16)
        mc_s[hw:, tt * LANES : (tt + 1) * LANES] = (-x_im).astype(BF16)
    gl = cs // S5_TSPLIT * LANES
    for qq in range(S5_TSPLIT):
        kc_s[:, qq * LANES : (qq + 1) * LANES] = kc[(S5_TSPLIT - 1 - qq) * gl : (S5_TSPLIT - qq) * gl]
    ucat = jnp.concatenate([u_ref[pl.ds(j, nch, stride=cs), :] for j in range(cs)], axis=1).astype(BF16)
    e_all = _bdot(ucat, mb_s[...])
    cidx = lax.broadcasted_iota(jnp.int32, (nch, LANES), 0) % nchb
    ar, ai = ar_ref[0], ai_ref[0]
    hp_re, hp_im = [], []
    for g in range(hw // LANES):
        lo = g * LANES
        e_re, e_im = e_all[:, lo : lo + LANES], e_all[:, hw + lo : hw + lo + LANES]
        for kstep in range(ar.shape[0]):
            s = 2**kstep
            if s >= nchb:
                break
            s_re = jnp.where(cidx >= s, pltpu.roll(e_re, s, axis=0), 0.0)
            s_im = jnp.where(cidx >= s, pltpu.roll(e_im, s, axis=0), 0.0)
            k_re, k_im = ar[kstep : kstep + 1, lo : lo + LANES], ai[kstep : kstep + 1, lo : lo + LANES]
            e_re, e_im = e_re + s_re * k_re - s_im * k_im, e_im + s_re * k_im + s_im * k_re
        hp_re.append(jnp.where(cidx >= 1, pltpu.roll(e_re, 1, axis=0), 0.0).astype(BF16))
        hp_im.append(jnp.where(cidx >= 1, pltpu.roll(e_im, 1, axis=0), 0.0).astype(BF16))
    ycar = _bdot(jnp.concatenate(hp_re + hp_im, axis=1), mc_s[...])
    for tt in range(cs):
        y_ref[pl.ds(tt, nch, stride=cs), :] = ycar[:, tt * LANES : (tt + 1) * LANES]

    rt = min(S5_RT, t)
    rmod = lax.broadcasted_iota(jnp.int32, (rt, LANES), 0) % cs
    dvec = d_ref[0]
    glag = cs // S5_TSPLIT

    def tile(i, carry):
        rows = pl.ds(pl.multiple_of(i * rt, rt), rt)
        ut = u_ref[rows, :]
        ush = [jnp.where(rmod >= tau, pltpu.roll(ut, tau, axis=0), 0.0) for tau in range(glag - 1, 0, -1)] + [ut]
        zz = _bdot(jnp.concatenate(ush, axis=1).astype(BF16), kc_s[...])
        acc = y_ref[rows, :] + dvec * ut + zz[:, :LANES]
        for qq in range(1, S5_TSPLIT):
            part = pltpu.roll(zz[:, qq * LANES : (qq + 1) * LANES], qq * glag, axis=0)
            acc = acc + jnp.where(rmod >= qq * glag, part, 0.0)
        y_ref[rows, :] = acc
        return carry

    lax.fori_loop(0, t // rt, tile, 0)


def _s5_core(proj, b, params):
    t = proj.shape[0]
    nchb = t // b // S5_CH
    sw = S5_GPB * 2 * S5_STATE

    def bspec(a):
        return pl.BlockSpec((1,) + a.shape[1:], lambda p: (p, 0, 0))

    return pl.pallas_call(
        functools.partial(_s5_kernel, nchb=nchb),
        grid=(S5_GROUPS // S5_GPB,),
        in_specs=[pl.BlockSpec((t, LANES), lambda p: (0, C_S5 // LANES + p))] + [bspec(a) for a in params],
        out_specs=pl.BlockSpec((t, LANES), lambda p: (0, p)),
        out_shape=jax.ShapeDtypeStruct((t, BRANCH_WIDTH), F32),
        scratch_shapes=[
            pltpu.VMEM((S5_CH * LANES, sw), BF16),
            pltpu.VMEM((sw, S5_CH * LANES), BF16),
            pltpu.VMEM((S5_CH // S5_TSPLIT * LANES, S5_TSPLIT * LANES), BF16),
        ],
        compiler_params=_cparams(("parallel",)),
        name="s5_core",
    )(proj, *params)


def _merge_kernel(
    x_ref, g1_ref, gla_ref, ys5_ref, dil_ref, swa_ref, gluw_ref, glub_ref, wg0, wg1, wg2, wg3, wb_ref, wo_ref,
    o_ref, h_s, s5_s,
):
    j = pl.program_id(1)

    @pl.when(j == 0)
    def _():
        h_s[...] = _rms(x_ref[...], g1_ref[...]).astype(BF16)
        zz = jax.nn.gelu(ys5_ref[...])
        gate = jax.nn.sigmoid(_bdot(zz.astype(BF16), gluw_ref[...]) + glub_ref[...])
        s5_s[...] = (zz * gate).astype(BF16)
        o_ref[...] = jnp.zeros_like(o_ref)

    h = h_s[...]
    branches = (gla_ref[...], s5_s[...], dil_ref[...], swa_ref[...])
    mixed = None
    for m, (wg, br) in enumerate(zip((wg0, wg1, wg2, wg3), branches)):
        term = jax.nn.sigmoid(_bdot(h, wg[...])) * _bdot(br, wb_ref[m])
        mixed = term if mixed is None else mixed + term
    mixed = mixed.astype(BF16)
    d = o_ref.shape[1]
    for c0 in range(0, d, FFN_DOWN_CHUNK):
        cols = slice(c0, c0 + FFN_DOWN_CHUNK)
        o_ref[:, cols] += _bdot(mixed, wo_ref[:, cols])

    @pl.when(j == pl.num_programs(1) - 1)
    def _():
        o_ref[...] = x_ref[...] + o_ref[...]


def _merge(x, g1, o_gla, y_s5, o_dil, o_swa, gluw, glub, w_all, wb, wo, layer):
    t, d = x.shape
    tm, tn = min(TM_MERGE, t), TN_MERGE
    nj = d // tn
    bw = BRANCH_WIDTH

    def rowblk(w):
        return pl.BlockSpec((tm, w), lambda i, j: (i, 0))

    def gate_spec(m):
        return pl.BlockSpec((None, d, tn), lambda i, j, m=m: (layer, 0, N_SMALL // tn + m * nj + j))

    return pl.pallas_call(
        _merge_kernel,
        grid=(t // tm, nj),
        in_specs=[
            rowblk(d),
            pl.BlockSpec((1, d), lambda i, j: (0, 0)),
            rowblk(bw),
            rowblk(bw),
            rowblk(bw),
            rowblk(bw),
            pl.BlockSpec((None, bw, bw), lambda i, j: (layer, 0, 0)),
            pl.BlockSpec((1, bw), lambda i, j: (0, 0)),
            gate_spec(0),
            gate_spec(1),
            gate_spec(2),
            gate_spec(3),
            pl.BlockSpec((N_BRANCH, bw, tn), lambda i, j: (0, 0, j)),
            pl.BlockSpec((tn, d), lambda i, j: (j, 0)),
        ],
        out_specs=rowblk(d),
        out_shape=jax.ShapeDtypeStruct((t, d), F32),
        scratch_shapes=[pltpu.VMEM((tm, d), BF16), pltpu.VMEM((tm, bw), BF16)],
        compiler_params=_cparams(("parallel", "arbitrary")),
        name="merge",
    )(x, g1, o_gla, y_s5, o_dil, o_swa, gluw, glub, w_all, w_all, w_all, w_all, wb, wo)


def _ffn_kernel(x_ref, g2_ref, wg_ref, wu_ref, wd_ref, gf_ref, o_ref, h_s, *, final_norm):
    j = pl.program_id(1)
    tm, d = o_ref.shape
    chunks = [slice(r0, r0 + FFN_ROW_CHUNK) for r0 in range(0, tm, FFN_ROW_CHUNK)]

    @pl.when(j == 0)
    def _():
        for rows in chunks:
            h_s[rows, :] = _rms(x_ref[rows, :], g2_ref[...]).astype(BF16)
        o_ref[...] = jnp.zeros_like(o_ref)

    for rows in chunks:
        h = h_s[rows, :]
        gate = _bdot(h, wg_ref[...])
        act = ((gate * jax.nn.sigmoid(gate)) * _bdot(h, wu_ref[...])).astype(BF16)
        for c0 in range(0, d, FFN_DOWN_CHUNK):
            cols = slice(c0, c0 + FFN_DOWN_CHUNK)
            o_ref[rows, cols] += _bdot(act, wd_ref[:, cols])

    @pl.when(j == pl.num_programs(1) - 1)
    def _():
        for rows in chunks:
            y = x_ref[rows, :] + o_ref[rows, :]
            o_ref[rows, :] = _rms(y, gf_ref[...]) if final_norm else y


def _ffn(x, g2, wg, wu, wd, gf, final_norm):
    t, d = x.shape
    fh = wg.shape[-1]
    tm, tf = min(TM_FFN, t), TF_FFN
    return pl.pallas_call(
        functools.partial(_ffn_kernel, final_norm=final_norm),
        grid=(t // tm, fh // tf),
        in_specs=[
            pl.BlockSpec((tm, d), lambda i, j: (i, 0)),
            pl.BlockSpec((1, d), lambda i, j: (0, 0)),
            pl.BlockSpec((d, tf), lambda i, j: (0, j)),
            pl.BlockSpec((d, tf), lambda i, j: (0, j)),
            pl.BlockSpec((tf, d), lambda i, j: (j, 0)),
            pl.BlockSpec((1, d), lambda i, j: (0, 0)),
        ],
        out_specs=pl.BlockSpec((tm, d), lambda i, j: (i, 0)),
        out_shape=jax.ShapeDtypeStruct((t, d), F32),
        scratch_shapes=[pltpu.VMEM((tm, d), BF16)],
        compiler_params=_cparams(("parallel", "arbitrary"), VMEM_LIMIT_FFN),
        name="ffn",
    )(x, g2, wg, wu, wd, gf)


def _pack_kernel(*refs):
    o_ref = refs[-1]
    for piece, w_ref in enumerate(refs[:-1]):
        x = w_ref[...]
        row = lax.broadcasted_iota(jnp.int32, x.shape, 0)
        blk = pl.program_id(1) * PACK_PIECES + piece
        keep = jnp.where(blk == C_GLR // PACK_CB, GLA_LOWRANK, PACK_CB)
        o_ref[:, piece * PACK_CB : (piece + 1) * PACK_CB] = jnp.where(row < keep, x, 0.0).T.astype(BF16)


def _pack_row0(c):
    cb = PACK_CB
    u = GLA_LOWRANK
    front = jnp.where(c < GLR_ORIG // cb, c * (cb // u), c * (cb // u) + 1)
    back = jnp.where(c == C_GLR // cb, GLR_ORIG // u, c * (cb // u) - (N_SMALL - N_ORIG_SMALL) // u)
    return jnp.where(c < C_GLR // cb, front, back) * u


def _pack_w_in(w_in):
    depth, d, d_in = w_in.shape
    n_out = N_SMALL + d_in - N_ORIG_SMALL
    cb = PACK_CB
    assert GLR_ORIG % cb == 0 and C_GLR % cb == 0 and N_SMALL % cb == 0 and n_out % cb == 0
    assert C_GLR == N_ORIG_SMALL - GLA_LOWRANK and C_GLR + cb == N_SMALL
    np_ = PACK_PIECES
    assert n_out % (cb * np_) == 0
    w_t = jnp.swapaxes(w_in, 1, 2)
    return pl.pallas_call(
        _pack_kernel,
        grid=(depth, n_out // (cb * np_)),
        in_specs=[
            pl.BlockSpec((None, pl.Element(cb), pl.Element(d)), lambda i, c, p=p: (i, _pack_row0(c * np_ + p), 0))
            for p in range(np_)
        ],
        out_specs=pl.BlockSpec((None, d, cb * np_), lambda i, c: (i, 0, c)),
        out_shape=jax.ShapeDtypeStruct((depth, d, n_out), BF16),
        compiler_params=_cparams(("parallel", "parallel")),
        name="pack_w_in",
    )(*([w_t] * np_))


def kernel(x, positions, norm1_g, w_in, gla_a2, gla_a_b, gla_norm_g, s5_lambda_re, s5_lambda_im, s5_log_dt, s5_b_re, s5_b_im, s5_c_re, s5_c_im, s5_d, s5_glu_w, s5_glu_b, swa_sinks, w_branch, w_out, norm2_g, w_ffn_gate, w_ffn_up, w_ffn_down, final_norm_g):
    b, l, d = x.shape
    t = b * l
    depth = w_in.shape[0]
    xs = x.reshape(t, d).astype(F32)
    w_all = _pack_w_in(w_in)
    w_br32 = w_branch.reshape(depth, N_BRANCH * BRANCH_WIDTH, d)
    glu_w = s5_glu_w.astype(BF16)
    tabs = _rope_tables(positions)
    gf = final_norm_g.reshape(1, d).astype(F32)
    for i in range(depth):
        g1 = norm1_g[i].reshape(1, d).astype(F32)
        proj = _inproj(xs, g1, w_all, i)
        proj3 = proj.reshape(b, l, N_SMALL)
        o_gla, w_fg, w_fu, w_br, w_o = _gla(proj, b, gla_a2[i], gla_a_b[i], gla_norm_g[i],
                                            (w_ffn_gate, w_ffn_up, w_br32, w_out), i)
        w_br = w_br.reshape(N_BRANCH, BRANCH_WIDTH, d)
        s5p = _s5_params(s5_lambda_re[i], s5_lambda_im[i], s5_log_dt[i], s5_b_re[i], s5_b_im[i], s5_c_re[i],
                         s5_c_im[i], s5_d[i], l // S5_CH)
        y_s5 = _s5_core(proj, b, s5p)
        o_dil = _dilated(proj3, tabs).reshape(t, BRANCH_WIDTH)
        o_swa, w_fd = _swa(proj3, tabs, swa_sinks[i], (w_ffn_down,), i)
        o_swa = o_swa.reshape(t, BRANCH_WIDTH)
        xs = _merge(xs, g1, o_gla, y_s5, o_dil, o_swa, glu_w, s5_glu_b[i].reshape(1, -1).astype(F32),
                    w_all, w_br, w_o, i)
        xs = _ffn(xs, norm2_g[i].reshape(1, d).astype(F32), w_fg, w_fu, w_fd, gf, i == depth - 1)
    return xs.reshape(b, l, d).astype(x.dtype)
```

```python
import functools
import math

import jax
import jax.numpy as jnp
import numpy as np
from jax import lax
from jax.experimental import pallas as pl
from jax.experimental.pallas import tpu as pltpu

F32 = jnp.float32
BF16 = jnp.bfloat16

D_MODEL = 2048
N_BRANCH = 4
BRANCH_WIDTH = 512
HEAD_DIM = 64
ATT_BLOCK = 128
ROPE_THETA = 500000.0
ROPE_DIM = HEAD_DIM // 4
NORM_EPS = 1e-6
GLA_HEADS = 4
GLA_DK = 64
GLA_DV = BRANCH_WIDTH // GLA_HEADS
GLA_LOWRANK = 16
GLA_TAU = 16.0
S5_GROUP = 16
S5_GROUPS = BRANCH_WIDTH // S5_GROUP
S5_STATE = 64
DIL_CONFIGS = ((128, 1), (512, 4), (2048, 16))
DIL_SPAN = ATT_BLOCK * 16
SWA_HEADS = BRANCH_WIDTH // HEAD_DIM
SWA_KV_HEADS = 2
SWA_WINDOW = 128

LANES = 128
SUBLANES = 8
VMEM_LIMIT = 56 * 1024 * 1024
VMEM_LIMIT_FFN = 60 * 1024 * 1024

C_GQ, C_GK, C_GV, C_GR, C_S5 = 0, 256, 512, 1024, 1536
C_CQ, C_CK, C_CV, C_SQ, C_SK, C_SV, C_GLR = 2048, 2560, 3072, 3584, 4096, 4224, 4352
N_SMALL = 4608
N_ORIG_SMALL = 4368
GLR_ORIG = 1536

TM_PROJ, TN_PROJ = 1024, 1536
TM_MERGE, TN_MERGE = 512, 512
TM_FFN, TF_FFN = 1024, 512
FFN_ROW_CHUNK = 512
FFN_DOWN_CHUNK = 512
GLA_CH = 128
GLA_NCS = 4
GLA_SUB = 8
GLA_LEVELS = (8, 16, 32, 64, 128)
PACK_CB = 256
PACK_PIECES = 2
S5_CH = 16
S5_GPB = LANES // S5_GROUP
S5_RT = 512
S5_TSPLIT = 4
ATT_UNROLL = 8


def _cparams(sem, vmem_limit=VMEM_LIMIT):
    return pltpu.CompilerParams(dimension_semantics=sem, vmem_limit_bytes=vmem_limit)


def _rms(x, g):
    ms = jnp.mean(x * x, axis=-1, keepdims=True)
    return x * lax.rsqrt(ms + NORM_EPS) * g


def _bdot(a, b):
    return jnp.dot(a, b, preferred_element_type=F32)


def _bdot_nt(a, b):
    return lax.dot_general(a, b, (((1,), (1,)), ((), ())), preferred_element_type=F32)


def _cast_specs(ws, layer, nsteps, step_of):
    ins, outs, shapes = [], [], []
    for w in ws:
        _, r, c = w.shape
        rb = r // nsteps
        assert rb * nsteps == r and rb % (2 * SUBLANES) == 0
        ins.append(pl.BlockSpec((None, rb, c), lambda *g: (layer, step_of(*g), 0)))
        outs.append(pl.BlockSpec((rb, c), lambda *g: (step_of(*g), 0)))
        shapes.append(jax.ShapeDtypeStruct((r, c), BF16))
    return ins, outs, shapes


def _side_cast(in_refs, out_refs):
    for src, dst in zip(in_refs, out_refs):
        dst[...] = src[...].astype(BF16)


def _rope_apply(x, c, sa, sb):
    half = ROPE_DIM // 2
    return x * c + pltpu.roll(x, LANES - half, axis=1) * sa + pltpu.roll(x, half, axis=1) * sb


def _inproj_kernel(x_ref, g_ref, w_ref, o_ref, h_ref):
    @pl.when(pl.program_id(1) == 0)
    def _():
        h_ref[...] = _rms(x_ref[...], g_ref[...]).astype(BF16)

    o_ref[...] = _bdot(h_ref[...], w_ref[...])


def _inproj(x, g, w_all, layer):
    t, d = x.shape
    n = N_SMALL
    tm, tn = min(TM_PROJ, t), TN_PROJ
    return pl.pallas_call(
        _inproj_kernel,
        grid=(t // tm, n // tn),
        in_specs=[
            pl.BlockSpec((tm, d), lambda i, j: (i, 0)),
            pl.BlockSpec((1, d), lambda i, j: (0, 0)),
            pl.BlockSpec((None, d, tn), lambda i, j: (layer, 0, j)),
        ],
        out_specs=pl.BlockSpec((tm, tn), lambda i, j: (i, j)),
        out_shape=jax.ShapeDtypeStruct((t, n), F32),
        scratch_shapes=[pltpu.VMEM((tm, d), BF16)],
        compiler_params=_cparams(("parallel", "arbitrary")),
        name="inproj",
    )(x, g, w_all)


def _rope_kernel(pos_ref, cos_ref, sa_ref, sb_ref):
    pos = pos_ref[0].astype(F32)
    lane = lax.broadcasted_iota(jnp.int32, (SUBLANES, LANES), 1)
    d = lane % HEAD_DIM
    half = ROPE_DIM // 2
    fi = (d % half).astype(F32) / half
    inv = jnp.power(jnp.full((SUBLANES, LANES), ROPE_THETA, F32), -fi)[0:1]
    d1 = d[0:1]
    ang = pos * inv
    c, s = jnp.cos(ang), jnp.sin(ang)
    cos_ref[0] = jnp.where(d1 < ROPE_DIM, c, 1.0)
    sa_ref[0] = jnp.where(d1 < half, -s, 0.0)
    sb_ref[0] = jnp.where((d1 >= half) & (d1 < ROPE_DIM), s, 0.0)


def _rope_tables(positions):
    b, l = positions.shape
    spec = pl.BlockSpec((1, l, LANES), lambda i: (i, 0, 0))
    shp = jax.ShapeDtypeStruct((b, l, LANES), F32)
    return pl.pallas_call(
        _rope_kernel,
        grid=(b,),
        in_specs=[pl.BlockSpec((1, l, 1), lambda i: (i, 0, 0))],
        out_specs=[spec, spec, spec],
        out_shape=[shp, shp, shp],
        compiler_params=_cparams(("parallel",)),
        name="rope_tables",
    )(positions.reshape(b, l, 1))


def _band_bias(max_dist):
    t = ATT_BLOCK
    assert 0 < max_dist <= t
    qi = np.arange(2 * t)[:, None] % t
    kj = np.arange(2 * t)[None, :]
    own = (kj >= t) & (kj - t <= qi)
    prev = (kj < t) & (t + qi - kj <= max_dist)
    ninf = np.float32(-np.inf)
    return np.stack([np.where(own | prev, 0.0, ninf), np.where(own, 0.0, ninf)]).astype(np.float32)


def _band_block(q, kp, ko, vp, vo, bias):
    t = ATT_BLOCK
    lane = lax.broadcasted_iota(jnp.int32, (t, LANES), 1)
    in_a = lane < HEAD_DIM
    q2 = jnp.concatenate([jnp.where(in_a, q, 0.0), jnp.where(in_a, 0.0, q)], axis=0).astype(BF16)
    s = _bdot_nt(q2, jnp.concatenate([kp, ko], axis=0).astype(BF16)) + bias
    m = jnp.max(s, axis=-1, keepdims=True)
    p = jnp.exp(s - m).astype(BF16)
    ones = jnp.ones((2 * t, LANES), BF16)
    vcat = jnp.concatenate([jnp.concatenate([vp, vo], axis=0).astype(BF16), ones], axis=1)
    od = _bdot(p, vcat)
    o = jnp.where(in_a, od[:t, :LANES], od[t:, :LANES])
    df = jnp.where(in_a, od[:t, LANES:], od[t:, LANES:])
    mf = jnp.where(in_a, m[:t], m[t:])
    return mf, df, o


def _dil_kernel(q_ref, k_ref, v_ref, c_ref, sa_ref, sb_ref, bias_ref, o_ref, qs, ks, qd, kd, vd, ms, ds, os_):
    l = q_ref.shape[1]
    c, sa, sb = c_ref[0], sa_ref[0], sb_ref[0]
    qs[...] = _rope_apply(q_ref[0], c, sa, sb) * (HEAD_DIM**-0.5)
    ks[...] = _rope_apply(k_ref[0], c, sa, sb)
    t = ATT_BLOCK
    nblk = DIL_SPAN // t
    dw = DIL_SPAN // t
    per = l // dw
    for r in range(dw):
        src, dst = pl.ds(r, per, stride=dw), slice(r * per, (r + 1) * per)
        qd[dst, :] = qs[src, :]
        kd[dst, :] = ks[src, :]
        vd[dst, :] = v_ref[0, src, :]

    for sblk in range(l // DIL_SPAN):
        base = sblk * DIL_SPAN
        for ci, (_, dil) in enumerate(DIL_CONFIGS):

            def body_wide(r, carry, ci=ci, sblk=sblk):
                rows = pl.ds(pl.multiple_of(r * per + sblk * t, t), t)
                prow = pl.ds(pl.multiple_of(r * per + max(sblk - 1, 0) * t, t), t)
                mf, df, o = _band_block(qd[rows, :], kd[prow, :], kd[rows, :], vd[prow, :], vd[rows, :],
                                        bias_ref[0 if sblk > 0 else 1])
                lrow = pl.ds(r, t, stride=dw)
                ms[ci, lrow, :] = mf
                ds[ci, lrow, :] = df
                os_[ci, lrow, :] = o
                return carry

            if dil == dw:
                lax.fori_loop(0, nblk, body_wide, 0, unroll=ATT_UNROLL)
                continue

            def body(idx, carry, dil=dil, ci=ci, base=base):
                r = idx % dil
                n = idx // dil
                loc = r + dil * t * n
                start = base + loc
                has_prev = start >= dil * t
                pstart = jnp.where(has_prev, start - dil * t, start)
                if dil == 1:
                    start = pl.multiple_of(start, t)
                    pstart = pl.multiple_of(pstart, t)
                    loc = pl.multiple_of(loc, t)
                    rows, prow, lrow = pl.ds(start, t), pl.ds(pstart, t), pl.ds(loc, t)
                else:
                    rows = pl.ds(start, t, stride=dil)
                    prow = pl.ds(pstart, t, stride=dil)
                    lrow = pl.ds(loc, t, stride=dil)
                mf, df, o = _band_block(
                    qs[rows, :], ks[prow, :], ks[rows, :], v_ref[0, prow, :], v_ref[0, rows, :],
                    bias_ref[jnp.where(has_prev, 0, 1)],
                )
                ms[ci, lrow, :] = mf
                ds[ci, lrow, :] = df
                os_[ci, lrow, :] = o
                return carry

            lax.fori_loop(0, nblk, body, 0, unroll=ATT_UNROLL)

        m0, m1, m2 = ms[0], ms[1], ms[2]
        mx = jnp.maximum(jnp.maximum(m0, m1), m2)
        w0, w1, w2 = jnp.exp(m0 - mx), jnp.exp(m1 - mx), jnp.exp(m2 - mx)
        den = w0 * ds[0] + w1 * ds[1] + w2 * ds[2]
        num = w0 * os_[0] + w1 * os_[1] + w2 * os_[2]
        o_ref[0, base : base + DIL_SPAN, :] = (num / den).astype(o_ref.dtype)


def _dilated(proj3, tabs):
    b, l, _ = proj3.shape
    assert l % DIL_SPAN == 0
    assert all(window // dil == ATT_BLOCK for window, dil in DIL_CONFIGS)
    bias = _band_bias(ATT_BLOCK)

    def col(c0):
        return pl.BlockSpec((1, l, LANES), lambda i, p: (i, 0, c0 // LANES + p))

    tab = pl.BlockSpec((1, l, LANES), lambda i, p: (i, 0, 0))
    return pl.pallas_call(
        _dil_kernel,
        grid=(b, BRANCH_WIDTH // LANES),
        in_specs=[col(C_CQ), col(C_CK), col(C_CV), tab, tab, tab, pl.BlockSpec(bias.shape, lambda i, p: (0, 0, 0))],
        out_specs=pl.BlockSpec((1, l, LANES), lambda i, p: (i, 0, p)),
        out_shape=jax.ShapeDtypeStruct((b, l, BRANCH_WIDTH), BF16),
        scratch_shapes=[pltpu.VMEM((l, LANES), F32)] * 5
        + [pltpu.VMEM((len(DIL_CONFIGS), DIL_SPAN, LANES), F32)] * 3,
        compiler_params=_cparams(("parallel", "parallel")),
        name="dilated_attn",
    )(proj3, proj3, proj3, *tabs, jnp.asarray(bias))


def _swa_kernel(*refs, n_cast):
    q_ref, k_ref, v_ref, c_ref, sa_ref, sb_ref, bias_ref, sink_ref = refs[:8]
    o_ref = refs[8 + n_cast]
    qs, ks, vs = refs[-3:]
    _side_cast(refs[8 : 8 + n_cast], refs[9 + n_cast : 9 + 2 * n_cast])
    l = q_ref.shape[1]
    c, sa, sb = c_ref[0], sa_ref[0], sb_ref[0]
    qs[...] = _rope_apply(q_ref[0], c, sa, sb) * (HEAD_DIM**-0.5)
    g = pl.program_id(1) // (SWA_HEADS // SWA_KV_HEADS // 2)
    lane = lax.broadcasted_iota(jnp.int32, (l, LANES), 1)
    keep = (lane // HEAD_DIM) == g
    k, v = _rope_apply(k_ref[0], c, sa, sb), v_ref[0]
    ks[...] = jnp.where(keep, k, pltpu.roll(k, HEAD_DIM, axis=1))
    vs[...] = jnp.where(keep, v, pltpu.roll(v, HEAD_DIM, axis=1))

    sink = sink_ref[0]
    t = ATT_BLOCK

    def body(n, carry):
        start = pl.multiple_of(n * t, t)
        has_prev = n > 0
        pstart = pl.multiple_of(jnp.where(has_prev, start - t, start), t)
        rows, prow = pl.ds(start, t), pl.ds(pstart, t)
        mf, df, o = _band_block(qs[rows, :], ks[prow, :], ks[rows, :], vs[prow, :], vs[rows, :],
                                bias_ref[jnp.where(has_prev, 0, 1)])
        lse = mf + jnp.log(df)
        o_ref[0, rows, :] = (o / df * jax.nn.sigmoid(lse - sink)).astype(o_ref.dtype)
        return carry

    lax.fori_loop(0, l // t, body, 0, unroll=ATT_UNROLL)


def _swa(proj3, tabs, sinks, cast=(), layer=0):
    b, l, _ = proj3.shape
    npair = BRANCH_WIDTH // LANES
    sink_l = jnp.repeat(sinks.astype(F32), HEAD_DIM).reshape(npair, 1, LANES)
    bias = _band_bias(SWA_WINDOW - 1)
    tab = pl.BlockSpec((1, l, LANES), lambda i, p: (i, 0, 0))
    c_in, c_out, c_shape = _cast_specs(cast, layer, b * npair, lambda i, p: i * npair + p)
    return pl.pallas_call(
        functools.partial(_swa_kernel, n_cast=len(cast)),
        grid=(b, npair),
        in_specs=[
            pl.BlockSpec((1, l, LANES), lambda i, p: (i, 0, C_SQ // LANES + p)),
            pl.BlockSpec((1, l, LANES), lambda i, p: (i, 0, C_SK // LANES)),
            pl.BlockSpec((1, l, LANES), lambda i, p: (i, 0, C_SV // LANES)),
            tab,
            tab,
            tab,
            pl.BlockSpec(bias.shape, lambda i, p: (0, 0, 0)),
            pl.BlockSpec((1, 1, LANES), lambda i, p: (p, 0, 0)),
        ]
        + c_in,
        out_specs=[pl.BlockSpec((1, l, LANES), lambda i, p: (i, 0, p))] + c_out,
        out_shape=[jax.ShapeDtypeStruct((b, l, BRANCH_WIDTH), BF16)] + c_shape,
        scratch_shapes=[pltpu.VMEM((l, LANES), F32)] * 3,
        compiler_params=_cparams(("parallel", "parallel")),
        name="swa_attn",
    )(proj3, proj3, proj3, *tabs, jnp.asarray(bias), sink_l, *cast)


@functools.lru_cache(maxsize=None)
def _gla_consts():
    ch = GLA_CH
    r = np.arange(ch)[:, None]
    j = np.arange(ch)[None, :]
    dstack = (j <= r).astype(np.float32)
    lm = []
    for m in GLA_LEVELS[:-1]:
        mk = (((r // m) % 2 == 1) & ((j // m) == (r // m) - 1)).astype(np.float32)
        lm.append(np.tile(mk, (GLA_HEADS, 1)))
    lmask = np.stack(lm)
    dmask = np.tile(((r // GLA_SUB) == (j // GLA_SUB)).astype(np.float32), (1, GLA_HEADS))
    nsub = ch // GLA_SUB
    e = np.zeros((GLA_SUB, GLA_HEADS * GLA_DK, GLA_HEADS * ch), np.float32)
    for u in range(GLA_SUB):
        for h in range(GLA_HEADS):
            for s in range(nsub):
                e[u, h * GLA_DK : (h + 1) * GLA_DK, h * ch + GLA_SUB * s + u] = 1.0
    rr = np.arange(GLA_HEADS * GLA_DV)[:, None]
    cc = np.arange(GLA_HEADS * GLA_DK)[None, :]
    bd = ((rr // GLA_DV) == (cc // GLA_DK)).astype(np.float32)
    return dstack, lmask, dmask, e, bd


def _split3(x):
    hi = x.astype(BF16)
    r1 = x - hi.astype(F32)
    mid = r1.astype(BF16)
    lo = (r1 - mid.astype(F32)).astype(BF16)
    return hi, mid, lo


def _bcast_grp(x, m, u):
    r, w = x.shape
    x3 = x.reshape(r // m, m, w)
    return jnp.broadcast_to(x3[:, u : u + 1, :], x3.shape).reshape(r, w)


def _bcast_sub(x, u):
    return _bcast_grp(x, GLA_SUB, u)


def _gla_kernel(*refs, n_cast):
    (q_ref, k_ref, v_ref, r_ref, glr_ref, a2_ref, ab_ref, ng_ref, dst_ref, lmask_ref, dmask_ref, e_ref,
     bd_ref) = refs[:13]
    o_ref, st_ref = refs[13 + n_cast], refs[-1]
    _side_cast(refs[13 : 13 + n_cast], refs[14 + n_cast : 14 + 2 * n_cast])
    ch = GLA_CH
    rs = q_ref.shape[0]
    ncs = rs // ch
    nlev = len(GLA_LEVELS)
    hk = GLA_HEADS * GLA_DK

    @pl.when(pl.program_id(1) == 0)
    def _():
        st_ref[...] = jnp.zeros_like(st_ref)

    q = q_ref[...] * (GLA_DK**-0.5)
    k = k_ref[...]
    z = _bdot(glr_ref[...].astype(BF16), a2_ref[...]) + ab_ref[...]
    g = (jnp.minimum(z, 0.0) - jnp.log(1.0 + jnp.exp(-jnp.abs(z)))) * (1.0 / GLA_TAU)
    g3 = _split3(g)
    dst = dst_ref[...]

    cums, excl = [], []
    for c in range(ncs):
        rows = slice(c * ch, (c + 1) * ch)
        cums.append(_bdot(dst, g3[0][rows]) + _bdot(dst, g3[1][rows]) + _bdot(dst, g3[2][rows]))
        excl.append(cums[c] - g[rows])

    def eq(c, li):
        m = GLA_LEVELS[li]
        return cums[c] - _bcast_grp(excl[c], m, 0)

    def ek(c, li):
        m = GLA_LEVELS[li]
        return _bcast_grp(cums[c], m, m - 1) - cums[c]

    cs = jnp.concatenate([eq(c, 0) for c in range(ncs)], axis=0)
    tsub = lax.broadcasted_iota(jnp.int32, (rs, hk), 0) % GLA_SUB
    arep = jnp.zeros((rs, GLA_HEADS * ch), F32)
    for u in range(GLA_SUB):
        dec = jnp.exp(jnp.where(tsub >= u, cs - _bcast_sub(cs, u), -jnp.inf))
        p = q * _bcast_sub(k, u) * dec
        arep = arep + _bdot(p.astype(BF16), e_ref[u])
    arep = arep * jnp.concatenate([dmask_ref[...]] * ncs, axis=0)

    lane_k = lax.broadcasted_iota(jnp.int32, (ch, hk), 1) // GLA_DK
    ng = ng_ref[...]
    bd = bd_ref[...]
    for c in range(ncs):
        rows = slice(c * ch, (c + 1) * ch)
        qc, kc = q[rows], k[rows]
        vc = v_ref[rows, :]
        aoff = jnp.zeros((GLA_HEADS * ch, ch), F32)
        for li in range(nlev - 1):
            qe = qc * jnp.exp(eq(c, li))
            ke = (kc * jnp.exp(ek(c, li))).astype(BF16)
            qst = jnp.concatenate([jnp.where(lane_k == h, qe, 0.0) for h in range(GLA_HEADS)], axis=0).astype(BF16)
            aoff = aoff + _bdot_nt(qst, ke) * lmask_ref[li]
        cum = eq(c, nlev - 1)
        st = st_ref[...]
        o_inter = _bdot_nt((qc * jnp.exp(cum)).astype(BF16), st.astype(BF16))
        vb = vc.astype(BF16)
        outs = []
        for h in range(GLA_HEADS):
            a_h = aoff[h * ch : (h + 1) * ch] + arep[rows, h * ch : (h + 1) * ch]
            o_h = _bdot(a_h.astype(BF16), vb[:, h * GLA_DV : (h + 1) * GLA_DV]) + o_inter[:, h * GLA_DV : (h + 1) * GLA_DV]
            ms = jnp.mean(o_h * o_h, axis=-1, keepdims=True)
            outs.append(o_h * lax.rsqrt(ms + NORM_EPS))
        o = jnp.concatenate(outs, axis=1) * ng
        rc = r_ref[rows, :]
        o_ref[rows, :] = (o * (rc * jax.nn.sigmoid(rc))).astype(o_ref.dtype)
        ke_last = (kc * jnp.exp(ek(c, nlev - 1))).astype(BF16)
        kv = _bdot(vc.T.astype(BF16), ke_last)
        st_ref[...] = st * jnp.exp(cum[ch - 1 : ch, :]) + kv * bd


def _gla(proj, b, a2, ab, ng, cast=(), layer=0):
    t = proj.shape[0]
    l = t // b
    rs = GLA_CH * GLA_NCS
    assert l % rs == 0
    ns = l // rs
    dstack, lmask, dmask, e, bd = _gla_consts()
    c_in, c_out, c_shape = _cast_specs(cast, layer, b * ns, lambda i, s: i * ns + s)
    a2p = jnp.zeros((LANES, GLA_HEADS * GLA_DK), BF16).at[:GLA_LOWRANK].set(a2.astype(BF16))

    def rowblk(w, c0):
        return pl.BlockSpec((rs, w), lambda i, s: (i * ns + s, c0 // w))

    def full(shape):
        nd = len(shape)
        return pl.BlockSpec(shape, lambda i, s: (0,) * nd)

    hk, hv = GLA_HEADS * GLA_DK, GLA_HEADS * GLA_DV
    return pl.pallas_call(
        functools.partial(_gla_kernel, n_cast=len(cast)),
        grid=(b, ns),
        in_specs=[
            rowblk(hk, C_GQ),
            rowblk(hk, C_GK),
            rowblk(hv, C_GV),
            rowblk(hv, C_GR),
            rowblk(LANES, C_GLR),
            full((LANES, hk)),
            full((1, hk)),
            full((1, hv)),
            full(dstack.shape),
            full(lmask.shape),
            full(dmask.shape),
            full(e.shape),
            full(bd.shape),
        ]
        + c_in,
        out_specs=[pl.BlockSpec((rs, hv), lambda i, s: (i * ns + s, 0))] + c_out,
        out_shape=[jax.ShapeDtypeStruct((t, hv), BF16)] + c_shape,
        scratch_shapes=[pltpu.VMEM((hv, hk), F32)],
        compiler_params=_cparams(("parallel", "arbitrary")),
        name="gla",
    )(
        proj, proj, proj, proj, proj, a2p, ab.reshape(1, hk).astype(F32), ng.reshape(1, hv).astype(F32),
        jnp.asarray(dstack, BF16), jnp.asarray(lmask), jnp.asarray(dmask), jnp.asarray(e, BF16), jnp.asarray(bd),
        *cast,
    )


def _s5_params(lam_re, lam_im, log_dt, b_re, b_im, c_re, c_im, d, nch):
    f = F32
    cs = S5_CH
    dt = jnp.exp(log_dt.astype(f))[:, None]
    lr, li = lam_re.astype(f), lam_im.astype(f)
    mag = jnp.exp(lr * dt)
    ab_re, ab_im = mag * jnp.cos(li * dt), mag * jnp.sin(li * dt)
    den = lr * lr + li * li
    z_re = ((ab_re - 1.0) * lr + ab_im * li) / den
    z_im = (ab_im * lr - (ab_re - 1.0) * li) / den
    br, bi = b_re.astype(f), b_im.astype(f)
    bb_re = z_re[..., None] * br - z_im[..., None] * bi
    bb_im = z_re[..., None] * bi + z_im[..., None] * br

    def apow(p):
        p = jnp.asarray(p, f)[:, None, None]
        m = jnp.exp(p * (lr * dt))
        return m * jnp.cos(p * (li * dt)), m * jnp.sin(p * (li * dt))

    cr, ci = c_re.astype(f), c_im.astype(f)
    p_re, p_im = apow(np.arange(cs + 1))
    nb, gpb = S5_GROUPS // S5_GPB, S5_GPB
    hw = gpb * S5_STATE
    eye = jnp.eye(gpb, dtype=f)

    def b_blockdiag(x):
        x = x.reshape(nb, gpb, S5_STATE, S5_GROUP).transpose(0, 1, 3, 2)
        x = (x[:, :, :, None, :] * eye[None, :, None, :, None]).reshape(nb, LANES, hw)
        return jnp.concatenate([x, x], axis=-1)

    def c_blockdiag(x):
        x = x.reshape(nb, gpb, S5_GROUP, S5_STATE).transpose(0, 1, 3, 2)
        return (x[:, :, :, None, :] * eye[None, :, None, :, None]).reshape(nb, hw, LANES)

    def lanes(a, b_):
        p = a.shape[0]
        return jnp.concatenate([a.reshape(p, nb, hw), b_.reshape(p, nb, hw)], axis=-1).transpose(1, 0, 2)

    rev = np.arange(cs - 1, -1, -1)
    pr, pi = p_re[rev], p_im[rev]
    pa, pb = lanes(pr, pi), lanes(-pi, pr)
    acol_re = jnp.broadcast_to(ab_re.reshape(nb, hw, 1), (nb, hw, LANES))
    acol_im = jnp.broadcast_to(ab_im.reshape(nb, hw, 1), (nb, hw, LANES))
    nstep = max(1, int(math.log2(nch)))
    s_re, s_im = apow(cs * (2 ** np.arange(nstep)))
    ar = s_re.reshape(nstep, nb, hw).transpose(1, 0, 2)
    ai = s_im.reshape(nstep, nb, hw).transpose(1, 0, 2)
    dd = d.astype(f).reshape(nb, 1, LANES)
    return (b_blockdiag(bb_re), b_blockdiag(bb_im), c_blockdiag(cr), c_blockdiag(ci), acol_re, acol_im,
            pa, pb, ar, ai, dd)


def _s5_kernel(u_ref, br_ref, bi_ref, cr_ref, ci_ref, acr_ref, aci_ref, pa_ref, pb_ref, ar_ref, ai_ref, d_ref,
               y_ref, mb_s, mc_s, kc_s, *, nchb):
    cs = S5_CH
    t = u_ref.shape[0]
    nch = t // cs
    hw = S5_GPB * S5_STATE
    br, bi = br_ref[0], bi_ref[0]
    pa, pb = pa_ref[0], pb_ref[0]
    for j in range(cs):
        mb_s[j * LANES : (j + 1) * LANES, :] = (br * pa[j : j + 1] + bi * pb[j : j + 1]).astype(BF16)
    x_re, x_im = cr_ref[0], ci_ref[0]
    kc = _bdot(mb_s[...], jnp.concatenate([x_re, -x_im], axis=0).astype(BF16)).astype(BF16)
    a_re, a_im = acr_ref[0], aci_ref[0]
    for tt in range(cs):
        x_re, x_im = x_re * a_re - x_im * a_im, x_re * a_im + x_im * a_re
        mc_s[:hw, tt * LANES : (tt + 1) * LANES] = x_re.astype(BF16)
        mc_s[hw:, tt * LANES : (tt + 1) * LANES] = (-x_im).astype(BF16)
    gl = cs // S5_TSPLIT * LANES
    for qq in range(S5_TSPLIT):
        kc_s[:, qq * LANES : (qq + 1) * LANES] = kc[(S5_TSPLIT - 1 - qq) * gl : (S5_TSPLIT - qq) * gl]
    ucat = jnp.concatenate([u_ref[pl.ds(j, nch, stride=cs), :] for j in range(cs)], axis=1).astype(BF16)
    e_all = _bdot(ucat, mb_s[...])
    cidx = lax.broadcasted_iota(jnp.int32, (nch, LANES), 0) % nchb
    ar, ai = ar_ref[0], ai_ref[0]
    hp_re, hp_im = [], []
    for g in range(hw // LANES):
        lo = g * LANES
        e_re, e_im = e_all[:, lo : lo + LANES], e_all[:, hw + lo : hw + lo + LANES]
        for kstep in range(ar.shape[0]):
            s = 2**kstep
            if s >= nchb:
                break
            s_re = jnp.where(cidx >= s, pltpu.roll(e_re, s, axis=0), 0.0)
            s_im = jnp.where(cidx >= s, pltpu.roll(e_im, s, axis=0), 0.0)
            k_re, k_im = ar[kstep : kstep + 1, lo : lo + LANES], ai[kstep : kstep + 1, lo : lo + LANES]
            e_re, e_im = e_re + s_re * k_re - s_im * k_im, e_im + s_re * k_im + s_im * k_re
        hp_re.append(jnp.where(cidx >= 1, pltpu.roll(e_re, 1, axis=0), 0.0).astype(BF16))
        hp_im.append(jnp.where(cidx >= 1, pltpu.roll(e_im, 1, axis=0), 0.0).astype(BF16))
    ycar = _bdot(jnp.concatenate(hp_re + hp_im, axis=1), mc_s[...])
    for tt in range(cs):
        y_ref[pl.ds(tt, nch, stride=cs), :] = ycar[:, tt * LANES : (tt + 1) * LANES]

    rt = min(S5_RT, t)
    rmod = lax.broadcasted_iota(jnp.int32, (rt, LANES), 0) % cs
    dvec = d_ref[0]
    glag = cs // S5_TSPLIT

    def tile(i, carry):
        rows = pl.ds(pl.multiple_of(i * rt, rt), rt)
        ut = u_ref[rows, :]
        ush = [jnp.where(rmod >= tau, pltpu.roll(ut, tau, axis=0), 0.0) for tau in range(glag - 1, 0, -1)] + [ut]
        zz = _bdot(jnp.concatenate(ush, axis=1).astype(BF16), kc_s[...])
        acc = y_ref[rows, :] + dvec * ut + zz[:, :LANES]
        for qq in range(1, S5_TSPLIT):
            part = pltpu.roll(zz[:, qq * LANES : (qq + 1) * LANES], qq * glag, axis=0)
            acc = acc + jnp.where(rmod >= qq * glag, part, 0.0)
        y_ref[rows, :] = acc
        return carry

    lax.fori_loop(0, t // rt, tile, 0)


def _s5_core(proj, b, params):
    t = proj.shape[0]
    nchb = t // b // S5_CH
    sw = S5_GPB * 2 * S5_STATE

    def bspec(a):
        return pl.BlockSpec((1,) + a.shape[1:], lambda p: (p, 0, 0))

    return pl.pallas_call(
        functools.partial(_s5_kernel, nchb=nchb),
        grid=(S5_GROUPS // S5_GPB,),
        in_specs=[pl.BlockSpec((t, LANES), lambda p: (0, C_S5 // LANES + p))] + [bspec(a) for a in params],
        out_specs=pl.BlockSpec((t, LANES), lambda p: (0, p)),
        out_shape=jax.ShapeDtypeStruct((t, BRANCH_WIDTH), F32),
        scratch_shapes=[
            pltpu.VMEM((S5_CH * LANES, sw), BF16),
            pltpu.VMEM((sw, S5_CH * LANES), BF16),
            pltpu.VMEM((S5_CH // S5_TSPLIT * LANES, S5_TSPLIT * LANES), BF16),
        ],
        compiler_params=_cparams(("parallel",)),
        name="s5_core",
    )(proj, *params)


def _merge_kernel(
    x_ref, g1_ref, gla_ref, ys5_ref, dil_ref, swa_ref, gluw_ref, glub_ref, wg0, wg1, wg2, wg3, wb_ref, wo_ref,
    o_ref, h_s, s5_s,
):
    j = pl.program_id(1)

    @pl.when(j == 0)
    def _():
        h_s[...] = _rms(x_ref[...], g1_ref[...]).astype(BF16)
        zz = jax.nn.gelu(ys5_ref[...])
        gate = jax.nn.sigmoid(_bdot(zz.astype(BF16), gluw_ref[...]) + glub_ref[...])
        s5_s[...] = (zz * gate).astype(BF16)
        o_ref[...] = jnp.zeros_like(o_ref)

    h = h_s[...]
    branches = (gla_ref[...], s5_s[...], dil_ref[...], swa_ref[...])
    mixed = None
    for m, (wg, br) in enumerate(zip((wg0, wg1, wg2, wg3), branches)):
        term = jax.nn.sigmoid(_bdot(h, wg[...])) * _bdot(br, wb_ref[m])
        mixed = term if mixed is None else mixed + term
    mixed = mixed.astype(BF16)
    d = o_ref.shape[1]
    for c0 in range(0, d, FFN_DOWN_CHUNK):
        cols = slice(c0, c0 + FFN_DOWN_CHUNK)
        o_ref[:, cols] += _bdot(mixed, wo_ref[:, cols])

    @pl.when(j == pl.num_programs(1) - 1)
    def _():
        o_ref[...] = x_ref[...] + o_ref[...]


def _merge(x, g1, o_gla, y_s5, o_dil, o_swa, gluw, glub, w_all, wb, wo, layer):
    t, d = x.shape
    tm, tn = min(TM_MERGE, t), TN_MERGE
    nj = d // tn
    bw = BRANCH_WIDTH

    def rowblk(w):
        return pl.BlockSpec((tm, w), lambda i, j: (i, 0))

    def gate_spec(m):
        return pl.BlockSpec((None, d, tn), lambda i, j, m=m: (layer, 0, N_SMALL // tn + m * nj + j))

    return pl.pallas_call(
        _merge_kernel,
        grid=(t // tm, nj),
        in_specs=[
            rowblk(d),
            pl.BlockSpec((1, d), lambda i, j: (0, 0)),
            rowblk(bw),
            rowblk(bw),
            rowblk(bw),
            rowblk(bw),
            pl.BlockSpec((None, bw, bw), lambda i, j: (layer, 0, 0)),
            pl.BlockSpec((1, bw), lambda i, j: (0, 0)),
            gate_spec(0),
            gate_spec(1),
            gate_spec(2),
            gate_spec(3),
            pl.BlockSpec((N_BRANCH, bw, tn), lambda i, j: (0, 0, j)),
            pl.BlockSpec((tn, d), lambda i, j: (j, 0)),
        ],
        out_specs=rowblk(d),
        out_shape=jax.ShapeDtypeStruct((t, d), F32),
        scratch_shapes=[pltpu.VMEM((tm, d), BF16), pltpu.VMEM((tm, bw), BF16)],
        compiler_params=_cparams(("parallel", "arbitrary")),
        name="merge",
    )(x, g1, o_gla, y_s5, o_dil, o_swa, gluw, glub, w_all, w_all, w_all, w_all, wb, wo)


def _ffn_kernel(x_ref, g2_ref, wg_ref, wu_ref, wd_ref, gf_ref, o_ref, h_s, *, final_norm):
    j = pl.program_id(1)
    tm, d = o_ref.shape
    chunks = [slice(r0, r0 + FFN_ROW_CHUNK) for r0 in range(0, tm, FFN_ROW_CHUNK)]

    @pl.when(j == 0)
    def _():
        for rows in chunks:
            h_s[rows, :] = _rms(x_ref[rows, :], g2_ref[...]).astype(BF16)
        o_ref[...] = jnp.zeros_like(o_ref)

    for rows in chunks:
        h = h_s[rows, :]
        gate = _bdot(h, wg_ref[...])
        act = ((gate * jax.nn.sigmoid(gate)) * _bdot(h, wu_ref[...])).astype(BF16)
        for c0 in range(0, d, FFN_DOWN_CHUNK):
            cols = slice(c0, c0 + FFN_DOWN_CHUNK)
            o_ref[rows, cols] += _bdot(act, wd_ref[:, cols])

    @pl.when(j == pl.num_programs(1) - 1)
    def _():
        for rows in chunks:
            y = x_ref[rows, :] + o_ref[rows, :]
            o_ref[rows, :] = _rms(y, gf_ref[...]) if final_norm else y


def _ffn(x, g2, wg, wu, wd, gf, final_norm):
    t, d = x.shape
    fh = wg.shape[-1]
    tm, tf = min(TM_FFN, t), TF_FFN
    return pl.pallas_call(
        functools.partial(_ffn_kernel, final_norm=final_norm),
        grid=(t // tm, fh // tf),
        in_specs=[
            pl.BlockSpec((tm, d), lambda i, j: (i, 0)),
            pl.BlockSpec((1, d), lambda i, j: (0, 0)),
            pl.BlockSpec((d, tf), lambda i, j: (0, j)),
            pl.BlockSpec((d, tf), lambda i, j: (0, j)),
            pl.BlockSpec((tf, d), lambda i, j: (j, 0)),
            pl.BlockSpec((1, d), lambda i, j: (0, 0)),
        ],
        out_specs=pl.BlockSpec((tm, d), lambda i, j: (i, 0)),
        out_shape=jax.ShapeDtypeStruct((t, d), F32),
        scratch_shapes=[pltpu.VMEM((tm, d), BF16)],
        compiler_params=_cparams(("parallel", "arbitrary"), VMEM_LIMIT_FFN),
        name="ffn",
    )(x, g2, wg, wu, wd, gf)


def _pack_kernel(*refs):
    o_ref = refs[-1]
    for piece, w_ref in enumerate(refs[:-1]):
        x = w_ref[...]
        row = lax.broadcasted_iota(jnp.int32, x.shape, 0)
        blk = pl.program_id(1) * PACK_PIECES + piece
        keep = jnp.where(blk == C_GLR // PACK_CB, GLA_LOWRANK, PACK_CB)
        o_ref[:, piece * PACK_CB : (piece + 1) * PACK_CB] = jnp.where(row < keep, x, 0.0).T.astype(BF16)


def _pack_row0(c):
    cb = PACK_CB
    u = GLA_LOWRANK
    front = jnp.where(c < GLR_ORIG // cb, c * (cb // u), c * (cb // u) + 1)
    back = jnp.where(c == C_GLR // cb, GLR_ORIG // u, c * (cb // u) - (N_SMALL - N_ORIG_SMALL) // u)
    return jnp.where(c < C_GLR // cb, front, back) * u


def _pack_w_in(w_in):
    depth, d, d_in = w_in.shape
    n_out = N_SMALL + d_in - N_ORIG_SMALL
    cb = PACK_CB
    assert GLR_ORIG % cb == 0 and C_GLR % cb == 0 and N_SMALL % cb == 0 and n_out % cb == 0
    assert C_GLR == N_ORIG_SMALL - GLA_LOWRANK and C_GLR + cb == N_SMALL
    np_ = PACK_PIECES
    assert n_out % (cb * np_) == 0
    w_t = jnp.swapaxes(w_in, 1, 2)
    return pl.pallas_call(
        _pack_kernel,
        grid=(depth, n_out // (cb * np_)),
        in_specs=[
            pl.BlockSpec((None, pl.Element(cb), pl.Element(d)), lambda i, c, p=p: (i, _pack_row0(c * np_ + p), 0))
            for p in range(np_)
        ],
        out_specs=pl.BlockSpec((None, d, cb * np_), lambda i, c: (i, 0, c)),
        out_shape=jax.ShapeDtypeStruct((depth, d, n_out), BF16),
        compiler_params=_cparams(("parallel", "parallel")),
        name="pack_w_in",
    )(*([w_t] * np_))


def kernel(x, positions, norm1_g, w_in, gla_a2, gla_a_b, gla_norm_g, s5_lambda_re, s5_lambda_im, s5_log_dt, s5_b_re, s5_b_im, s5_c_re, s5_c_im, s5_d, s5_glu_w, s5_glu_b, swa_sinks, w_branch, w_out, norm2_g, w_ffn_gate, w_ffn_up, w_ffn_down, final_norm_g):
    b, l, d = x.shape
    t = b * l
    depth = w_in.shape[0]
    xs = x.reshape(t, d).astype(F32)
    w_all = _pack_w_in(w_in)
    w_br32 = w_branch.reshape(depth, N_BRANCH * BRANCH_WIDTH, d)
    glu_w = s5_glu_w.astype(BF16)
    tabs = _rope_tables(positions)
    gf = final_norm_g.reshape(1, d).astype(F32)
    s5_all = jax.vmap(functools.partial(_s5_params, nch=l // S5_CH))(
        s5_lambda_re, s5_lambda_im, s5_log_dt, s5_b_re, s5_b_im, s5_c_re, s5_c_im, s5_d)
    for i in range(depth):
        g1 = norm1_g[i].reshape(1, d).astype(F32)
        proj = _inproj(xs, g1, w_all, i)
        proj3 = proj.reshape(b, l, N_SMALL)
        o_gla, w_fg, w_fu, w_br, w_o = _gla(proj, b, gla_a2[i], gla_a_b[i], gla_norm_g[i],
                                            (w_ffn_gate, w_ffn_up, w_br32, w_out), i)
        w_br = w_br.reshape(N_BRANCH, BRANCH_WIDTH, d)
        y_s5 = _s5_core(proj, b, [a[i] for a in s5_all])
        o_dil = _dilated(proj3, tabs).reshape(t, BRANCH_WIDTH)
        o_swa, w_fd = _swa(proj3, tabs, swa_sinks[i], (w_ffn_down,), i)
        o_swa = o_swa.reshape(t, BRANCH_WIDTH)
        xs = _merge(xs, g1, o_gla, y_s5, o_dil, o_swa, glu_w, s5_glu_b[i].reshape(1, -1).astype(F32),
                    w_all, w_br, w_o, i)
        xs = _ffn(xs, norm2_g[i].reshape(1, d).astype(F32), w_fg, w_fu, w_fd, gf, i == depth - 1)
    return xs.reshape(b, l, d).astype(x.dtype)
```

```python
import functools
import math

import jax
import jax.numpy as jnp
import numpy as np
from jax import lax
from jax.experimental import pallas as pl
from jax.experimental.pallas import tpu as pltpu

F32 = jnp.float32
BF16 = jnp.bfloat16

D_MODEL = 2048
N_BRANCH = 4
BRANCH_WIDTH = 512
HEAD_DIM = 64
ATT_BLOCK = 128
ROPE_THETA = 500000.0
ROPE_DIM = HEAD_DIM // 4
NORM_EPS = 1e-6
GLA_HEADS = 4
GLA_DK = 64
GLA_DV = BRANCH_WIDTH // GLA_HEADS
GLA_LOWRANK = 16
GLA_TAU = 16.0
S5_GROUP = 16
S5_GROUPS = BRANCH_WIDTH // S5_GROUP
S5_STATE = 64
DIL_CONFIGS = ((128, 1), (512, 4), (2048, 16))
DIL_SPAN = ATT_BLOCK * 16
SWA_HEADS = BRANCH_WIDTH // HEAD_DIM
SWA_KV_HEADS = 2
SWA_WINDOW = 128

LANES = 128
SUBLANES = 8
VMEM_LIMIT = 56 * 1024 * 1024
VMEM_LIMIT_FFN = 60 * 1024 * 1024

C_GQ, C_GK, C_GV, C_GR, C_S5 = 0, 256, 512, 1024, 1536
C_CQ, C_CK, C_CV, C_SQ, C_SK, C_SV, C_GLR = 2048, 2560, 3072, 3584, 4096, 4224, 4352
N_SMALL = 4608
N_ORIG_SMALL = 4368
GLR_ORIG = 1536

TM_PROJ, TN_PROJ = 1024, 1536
TM_MERGE, TN_MERGE = 512, 512
TM_FFN, TF_FFN = 1024, 512
FFN_ROW_CHUNK = 512
FFN_DOWN_CHUNK = 512
GLA_CH = 128
GLA_NCS = 4
GLA_SUB = 8
GLA_LEVELS = (8, 16, 32, 64, 128)
PACK_CB = 256
PACK_PIECES = 5
S5_CH = 16
S5_GPB = LANES // S5_GROUP
S5_RT = 512
S5_TSPLIT = 4
ATT_UNROLL = 8
SWA_UNROLL = 16


def _cparams(sem, vmem_limit=VMEM_LIMIT):
    return pltpu.CompilerParams(dimension_semantics=sem, vmem_limit_bytes=vmem_limit)


def _rms(x, g):
    ms = jnp.mean(x * x, axis=-1, keepdims=True)
    return x * lax.rsqrt(ms + NORM_EPS) * g


def _bdot(a, b):
    return jnp.dot(a, b, preferred_element_type=F32)


def _bdot_nt(a, b):
    return lax.dot_general(a, b, (((1,), (1,)), ((), ())), preferred_element_type=F32)


def _cast_specs(ws, layer, nsteps, step_of):
    ins, outs, shapes = [], [], []
    for w in ws:
        _, r, c = w.shape
        rb = r // nsteps
        assert rb * nsteps == r and rb % (2 * SUBLANES) == 0
        ins.append(pl.BlockSpec((None, rb, c), lambda *g: (layer, step_of(*g), 0)))
        outs.append(pl.BlockSpec((rb, c), lambda *g: (step_of(*g), 0)))
        shapes.append(jax.ShapeDtypeStruct((r, c), BF16))
    return ins, outs, shapes


def _side_cast(in_refs, out_refs):
    for src, dst in zip(in_refs, out_refs):
        dst[...] = src[...].astype(BF16)


def _rope_apply(x, c, sa, sb):
    half = ROPE_DIM // 2
    return x * c + pltpu.roll(x, LANES - half, axis=1) * sa + pltpu.roll(x, half, axis=1) * sb


def _inproj_kernel(x_ref, g_ref, w_ref, o_ref, h_ref):
    @pl.when(pl.program_id(1) == 0)
    def _():
        h_ref[...] = _rms(x_ref[...], g_ref[...]).astype(BF16)

    o_ref[...] = _bdot(h_ref[...], w_ref[...])


def _inproj(x, g, w_all, layer):
    t, d = x.shape
    n = N_SMALL
    tm, tn = min(TM_PROJ, t), TN_PROJ
    return pl.pallas_call(
        _inproj_kernel,
        grid=(t // tm, n // tn),
        in_specs=[
            pl.BlockSpec((tm, d), lambda i, j: (i, 0)),
            pl.BlockSpec((1, d), lambda i, j: (0, 0)),
            pl.BlockSpec((None, d, tn), lambda i, j: (layer, 0, j)),
        ],
        out_specs=pl.BlockSpec((tm, tn), lambda i, j: (i, j)),
        out_shape=jax.ShapeDtypeStruct((t, n), F32),
        scratch_shapes=[pltpu.VMEM((tm, d), BF16)],
        compiler_params=_cparams(("parallel", "arbitrary")),
        name="inproj",
    )(x, g, w_all)


def _rope_kernel(pos_ref, cos_ref, sa_ref, sb_ref):
    pos = pos_ref[0].astype(F32)
    lane = lax.broadcasted_iota(jnp.int32, (SUBLANES, LANES), 1)
    d = lane % HEAD_DIM
    half = ROPE_DIM // 2
    fi = (d % half).astype(F32) / half
    inv = jnp.power(jnp.full((SUBLANES, LANES), ROPE_THETA, F32), -fi)[0:1]
    d1 = d[0:1]
    ang = pos * inv
    c, s = jnp.cos(ang), jnp.sin(ang)
    cos_ref[0] = jnp.where(d1 < ROPE_DIM, c, 1.0)
    sa_ref[0] = jnp.where(d1 < half, -s, 0.0)
    sb_ref[0] = jnp.where((d1 >= half) & (d1 < ROPE_DIM), s, 0.0)


def _rope_tables(positions):
    b, l = positions.shape
    spec = pl.BlockSpec((1, l, LANES), lambda i: (i, 0, 0))
    shp = jax.ShapeDtypeStruct((b, l, LANES), F32)
    return pl.pallas_call(
        _rope_kernel,
        grid=(b,),
        in_specs=[pl.BlockSpec((1, l, 1), lambda i: (i, 0, 0))],
        out_specs=[spec, spec, spec],
        out_shape=[shp, shp, shp],
        compiler_params=_cparams(("parallel",)),
        name="rope_tables",
    )(positions.reshape(b, l, 1))


def _band_bias(max_dist):
    t = ATT_BLOCK
    assert 0 < max_dist <= t
    qi = np.arange(2 * t)[:, None] % t
    kj = np.arange(2 * t)[None, :]
    own = (kj >= t) & (kj - t <= qi)
    prev = (kj < t) & (t + qi - kj <= max_dist)
    ninf = np.float32(-np.inf)
    return np.stack([np.where(own | prev, 0.0, ninf), np.where(own, 0.0, ninf)]).astype(np.float32)


def _band_block(q, kp, ko, vp, vo, bias):
    t = ATT_BLOCK
    lane = lax.broadcasted_iota(jnp.int32, (t, LANES), 1)
    in_a = lane < HEAD_DIM
    q2 = jnp.concatenate([jnp.where(in_a, q, 0.0), jnp.where(in_a, 0.0, q)], axis=0).astype(BF16)
    s = _bdot_nt(q2, jnp.concatenate([kp, ko], axis=0).astype(BF16)) + bias
    m = jnp.max(s, axis=-1, keepdims=True)
    p = jnp.exp(s - m).astype(BF16)
    ones = jnp.ones((2 * t, LANES), BF16)
    vcat = jnp.concatenate([jnp.concatenate([vp, vo], axis=0).astype(BF16), ones], axis=1)
    od = _bdot(p, vcat)
    o = jnp.where(in_a, od[:t, :LANES], od[t:, :LANES])
    df = jnp.where(in_a, od[:t, LANES:], od[t:, LANES:])
    mf = jnp.where(in_a, m[:t], m[t:])
    return mf, df, o


def _dil_kernel(q_ref, k_ref, v_ref, c_ref, sa_ref, sb_ref, bias_ref, o_ref, qs, ks, qd, kd, vd, ms, ds, os_):
    l = q_ref.shape[1]
    c, sa, sb = c_ref[0], sa_ref[0], sb_ref[0]
    qs[...] = _rope_apply(q_ref[0], c, sa, sb) * (HEAD_DIM**-0.5)
    ks[...] = _rope_apply(k_ref[0], c, sa, sb)
    t = ATT_BLOCK
    nblk = DIL_SPAN // t
    dw = DIL_SPAN // t
    per = l // dw
    for r in range(dw):
        src, dst = pl.ds(r, per, stride=dw), slice(r * per, (r + 1) * per)
        qd[dst, :] = qs[src, :]
        kd[dst, :] = ks[src, :]
        vd[dst, :] = v_ref[0, src, :]

    for sblk in range(l // DIL_SPAN):
        base = sblk * DIL_SPAN
        for ci, (_, dil) in enumerate(DIL_CONFIGS):

            def body_wide(r, carry, ci=ci, sblk=sblk):
                rows = pl.ds(pl.multiple_of(r * per + sblk * t, t), t)
                prow = pl.ds(pl.multiple_of(r * per + max(sblk - 1, 0) * t, t), t)
                mf, df, o = _band_block(qd[rows, :], kd[prow, :], kd[rows, :], vd[prow, :], vd[rows, :],
                                        bias_ref[0 if sblk > 0 else 1])
                lrow = pl.ds(r, t, stride=dw)
                ms[ci, lrow, :] = mf
                ds[ci, lrow, :] = df
                os_[ci, lrow, :] = o
                return carry

            if dil == dw:
                lax.fori_loop(0, nblk, body_wide, 0, unroll=ATT_UNROLL)
                continue

            def body(idx, carry, dil=dil, ci=ci, base=base):
                r = idx % dil
                n = idx // dil
                loc = r + dil * t * n
                start = base + loc
                has_prev = start >= dil * t
                pstart = jnp.where(has_prev, start - dil * t, start)
                if dil == 1:
                    start = pl.multiple_of(start, t)
                    pstart = pl.multiple_of(pstart, t)
                    loc = pl.multiple_of(loc, t)
                    rows, prow, lrow = pl.ds(start, t), pl.ds(pstart, t), pl.ds(loc, t)
                else:
                    rows = pl.ds(start, t, stride=dil)
                    prow = pl.ds(pstart, t, stride=dil)
                    lrow = pl.ds(loc, t, stride=dil)
                mf, df, o = _band_block(
                    qs[rows, :], ks[prow, :], ks[rows, :], v_ref[0, prow, :], v_ref[0, rows, :],
                    bias_ref[jnp.where(has_prev, 0, 1)],
                )
                ms[ci, lrow, :] = mf
                ds[ci, lrow, :] = df
                os_[ci, lrow, :] = o
                return carry

            lax.fori_loop(0, nblk, body, 0, unroll=ATT_UNROLL)

        m0, m1, m2 = ms[0], ms[1], ms[2]
        mx = jnp.maximum(jnp.maximum(m0, m1), m2)
        w0, w1, w2 = jnp.exp(m0 - mx), jnp.exp(m1 - mx), jnp.exp(m2 - mx)
        den = w0 * ds[0] + w1 * ds[1] + w2 * ds[2]
        num = w0 * os_[0] + w1 * os_[1] + w2 * os_[2]
        o_ref[0, base : base + DIL_SPAN, :] = (num / den).astype(o_ref.dtype)


def _dilated(proj3, tabs):
    b, l, _ = proj3.shape
    assert l % DIL_SPAN == 0
    assert all(window // dil == ATT_BLOCK for window, dil in DIL_CONFIGS)
    bias = _band_bias(ATT_BLOCK)

    def col(c0):
        return pl.BlockSpec((1, l, LANES), lambda i, p: (i, 0, c0 // LANES + p))

    tab = pl.BlockSpec((1, l, LANES), lambda i, p: (i, 0, 0))
    return pl.pallas_call(
        _dil_kernel,
        grid=(b, BRANCH_WIDTH // LANES),
        in_specs=[col(C_CQ), col(C_CK), col(C_CV), tab, tab, tab, pl.BlockSpec(bias.shape, lambda i, p: (0, 0, 0))],
        out_specs=pl.BlockSpec((1, l, LANES), lambda i, p: (i, 0, p)),
        out_shape=jax.ShapeDtypeStruct((b, l, BRANCH_WIDTH), BF16),
        scratch_shapes=[pltpu.VMEM((l, LANES), F32)] * 5
        + [pltpu.VMEM((len(DIL_CONFIGS), DIL_SPAN, LANES), F32)] * 3,
        compiler_params=_cparams(("parallel", "parallel")),
        name="dilated_attn",
    )(proj3, proj3, proj3, *tabs, jnp.asarray(bias))


def _swa_kernel(*refs, n_cast):
    q_ref, k_ref, v_ref, c_ref, sa_ref, sb_ref, bias_ref, sink_ref = refs[:8]
    o_ref = refs[8 + n_cast]
    qs, ks, vs = refs[-3:]
    _side_cast(refs[8 : 8 + n_cast], refs[9 + n_cast : 9 + 2 * n_cast])
    l = q_ref.shape[1]
    c, sa, sb = c_ref[0], sa_ref[0], sb_ref[0]
    qs[...] = _rope_apply(q_ref[0], c, sa, sb) * (HEAD_DIM**-0.5)
    g = pl.program_id(1) // (SWA_HEADS // SWA_KV_HEADS // 2)
    lane = lax.broadcasted_iota(jnp.int32, (l, LANES), 1)
    keep = (lane // HEAD_DIM) == g
    k, v = _rope_apply(k_ref[0], c, sa, sb), v_ref[0]
    ks[...] = jnp.where(keep, k, pltpu.roll(k, HEAD_DIM, axis=1))
    vs[...] = jnp.where(keep, v, pltpu.roll(v, HEAD_DIM, axis=1))

    sink = sink_ref[0]
    t = ATT_BLOCK

    def body(n, carry):
        start = pl.multiple_of(n * t, t)
        has_prev = n > 0
        pstart = pl.multiple_of(jnp.where(has_prev, start - t, start), t)
        rows, prow = pl.ds(start, t), pl.ds(pstart, t)
        mf, df, o = _band_block(qs[rows, :], ks[prow, :], ks[rows, :], vs[prow, :], vs[rows, :],
                                bias_ref[jnp.where(has_prev, 0, 1)])
        lse = mf + jnp.log(df)
        o_ref[0, rows, :] = (o / df * jax.nn.sigmoid(lse - sink)).astype(o_ref.dtype)
        return carry

    lax.fori_loop(0, l // t, body, 0, unroll=SWA_UNROLL)


def _swa(proj3, tabs, sinks, cast=(), layer=0):
    b, l, _ = proj3.shape
    npair = BRANCH_WIDTH // LANES
    sink_l = jnp.repeat(sinks.astype(F32), HEAD_DIM).reshape(npair, 1, LANES)
    bias = _band_bias(SWA_WINDOW - 1)
    tab = pl.BlockSpec((1, l, LANES), lambda i, p: (i, 0, 0))
    c_in, c_out, c_shape = _cast_specs(cast, layer, b * npair, lambda i, p: i * npair + p)
    return pl.pallas_call(
        functools.partial(_swa_kernel, n_cast=len(cast)),
        grid=(b, npair),
        in_specs=[
            pl.BlockSpec((1, l, LANES), lambda i, p: (i, 0, C_SQ // LANES + p)),
            pl.BlockSpec((1, l, LANES), lambda i, p: (i, 0, C_SK // LANES)),
            pl.BlockSpec((1, l, LANES), lambda i, p: (i, 0, C_SV // LANES)),
            tab,
            tab,
            tab,
            pl.BlockSpec(bias.shape, lambda i, p: (0, 0, 0)),
            pl.BlockSpec((1, 1, LANES), lambda i, p: (p, 0, 0)),
        ]
        + c_in,
        out_specs=[pl.BlockSpec((1, l, LANES), lambda i, p: (i, 0, p))] + c_out,
        out_shape=[jax.ShapeDtypeStruct((b, l, BRANCH_WIDTH), BF16)] + c_shape,
        scratch_shapes=[pltpu.VMEM((l, LANES), F32)] * 3,
        compiler_params=_cparams(("parallel", "parallel")),
        name="swa_attn",
    )(proj3, proj3, proj3, *tabs, jnp.asarray(bias), sink_l, *cast)


@functools.lru_cache(maxsize=None)
def _gla_consts():
    ch = GLA_CH
    r = np.arange(ch)[:, None]
    j = np.arange(ch)[None, :]
    dstack = (j <= r).astype(np.float32)
    lm = []
    for m in GLA_LEVELS[:-1]:
        mk = (((r // m) % 2 == 1) & ((j // m) == (r // m) - 1)).astype(np.float32)
        lm.append(np.tile(mk, (GLA_HEADS, 1)))
    lmask = np.stack(lm)
    dmask = np.tile(((r // GLA_SUB) == (j // GLA_SUB)).astype(np.float32), (1, GLA_HEADS))
    nsub = ch // GLA_SUB
    e = np.zeros((GLA_SUB, GLA_HEADS * GLA_DK, GLA_HEADS * ch), np.float32)
    for u in range(GLA_SUB):
        for h in range(GLA_HEADS):
            for s in range(nsub):
                e[u, h * GLA_DK : (h + 1) * GLA_DK, h * ch + GLA_SUB * s + u] = 1.0
    rr = np.arange(GLA_HEADS * GLA_DV)[:, None]
    cc = np.arange(GLA_HEADS * GLA_DK)[None, :]
    bd = ((rr // GLA_DV) == (cc // GLA_DK)).astype(np.float32)
    return dstack, lmask, dmask, e, bd


def _split3(x):
    hi = x.astype(BF16)
    r1 = x - hi.astype(F32)
    mid = r1.astype(BF16)
    lo = (r1 - mid.astype(F32)).astype(BF16)
    return hi, mid, lo


def _bcast_grp(x, m, u):
    r, w = x.shape
    x3 = x.reshape(r // m, m, w)
    return jnp.broadcast_to(x3[:, u : u + 1, :], x3.shape).reshape(r, w)


def _bcast_sub(x, u):
    return _bcast_grp(x, GLA_SUB, u)


def _gla_kernel(*refs, n_cast):
    (q_ref, k_ref, v_ref, r_ref, glr_ref, a2_ref, ab_ref, ng_ref, dst_ref, lmask_ref, dmask_ref, e_ref,
     bd_ref) = refs[:13]
    o_ref, st_ref = refs[13 + n_cast], refs[-1]
    _side_cast(refs[13 : 13 + n_cast], refs[14 + n_cast : 14 + 2 * n_cast])
    ch = GLA_CH
    rs = q_ref.shape[0]
    ncs = rs // ch
    nlev = len(GLA_LEVELS)
    hk = GLA_HEADS * GLA_DK

    @pl.when(pl.program_id(1) == 0)
    def _():
        st_ref[...] = jnp.zeros_like(st_ref)

    q = q_ref[...] * (GLA_DK**-0.5)
    k = k_ref[...]
    z = _bdot(glr_ref[...].astype(BF16), a2_ref[...]) + ab_ref[...]
    g = (jnp.minimum(z, 0.0) - jnp.log(1.0 + jnp.exp(-jnp.abs(z)))) * (1.0 / GLA_TAU)
    g3 = _split3(g)
    dst = dst_ref[...]

    cums, excl = [], []
    for c in range(ncs):
        rows = slice(c * ch, (c + 1) * ch)
        cums.append(_bdot(dst, g3[0][rows]) + _bdot(dst, g3[1][rows]) + _bdot(dst, g3[2][rows]))
        excl.append(cums[c] - g[rows])

    def eq(c, li):
        m = GLA_LEVELS[li]
        return cums[c] - _bcast_grp(excl[c], m, 0)

    def ek(c, li):
        m = GLA_LEVELS[li]
        return _bcast_grp(cums[c], m, m - 1) - cums[c]

    cs = jnp.concatenate([eq(c, 0) for c in range(ncs)], axis=0)
    tsub = lax.broadcasted_iota(jnp.int32, (rs, hk), 0) % GLA_SUB
    arep = jnp.zeros((rs, GLA_HEADS * ch), F32)
    for u in range(GLA_SUB):
        dec = jnp.exp(jnp.where(tsub >= u, cs - _bcast_sub(cs, u), -jnp.inf))
        p = q * _bcast_sub(k, u) * dec
        arep = arep + _bdot(p.astype(BF16), e_ref[u])
    arep = arep * jnp.concatenate([dmask_ref[...]] * ncs, axis=0)

    lane_k = lax.broadcasted_iota(jnp.int32, (ch, hk), 1) // GLA_DK
    ng = ng_ref[...]
    bd = bd_ref[...]
    for c in range(ncs):
        rows = slice(c * ch, (c + 1) * ch)
        qc, kc = q[rows], k[rows]
        vc = v_ref[rows, :]
        aoff = jnp.zeros((GLA_HEADS * ch, ch), F32)
        for li in range(nlev - 1):
            qe = qc * jnp.exp(eq(c, li))
            ke = (kc * jnp.exp(ek(c, li))).astype(BF16)
            qst = jnp.concatenate([jnp.where(lane_k == h, qe, 0.0) for h in range(GLA_HEADS)], axis=0).astype(BF16)
            aoff = aoff + _bdot_nt(qst, ke) * lmask_ref[li]
        cum = eq(c, nlev - 1)
        st = st_ref[...]
        o_inter = _bdot_nt((qc * jnp.exp(cum)).astype(BF16), st.astype(BF16))
        vb = vc.astype(BF16)
        outs = []
        for h in range(GLA_HEADS):
            a_h = aoff[h * ch : (h + 1) * ch] + arep[rows, h * ch : (h + 1) * ch]
            o_h = _bdot(a_h.astype(BF16), vb[:, h * GLA_DV : (h + 1) * GLA_DV]) + o_inter[:, h * GLA_DV : (h + 1) * GLA_DV]
            ms = jnp.mean(o_h * o_h, axis=-1, keepdims=True)
            outs.append(o_h * lax.rsqrt(ms + NORM_EPS))
        o = jnp.concatenate(outs, axis=1) * ng
        rc = r_ref[rows, :]
        o_ref[rows, :] = (o * (rc * jax.nn.sigmoid(rc))).astype(o_ref.dtype)
        ke_last = (kc * jnp.exp(ek(c, nlev - 1))).astype(BF16)
        kv = _bdot(vc.T.astype(BF16), ke_last)
        st_ref[...] = st * jnp.exp(cum[ch - 1 : ch, :]) + kv * bd


def _gla(proj, b, a2, ab, ng, cast=(), layer=0):
    t = proj.shape[0]
    l = t // b
    rs = GLA_CH * GLA_NCS
    assert l % rs == 0
    ns = l // rs
    dstack, lmask, dmask, e, bd = _gla_consts()
    c_in, c_out, c_shape = _cast_specs(cast, layer, b * ns, lambda i, s: i * ns + s)
    a2p = jnp.zeros((LANES, GLA_HEADS * GLA_DK), BF16).at[:GLA_LOWRANK].set(a2.astype(BF16))

    def rowblk(w, c0):
        return pl.BlockSpec((rs, w), lambda i, s: (i * ns + s, c0 // w))

    def full(shape):
        nd = len(shape)
        return pl.BlockSpec(shape, lambda i, s: (0,) * nd)

    hk, hv = GLA_HEADS * GLA_DK, GLA_HEADS * GLA_DV
    return pl.pallas_call(
        functools.partial(_gla_kernel, n_cast=len(cast)),
        grid=(b, ns),
        in_specs=[
            rowblk(hk, C_GQ),
            rowblk(hk, C_GK),
            rowblk(hv, C_GV),
            rowblk(hv, C_GR),
            rowblk(LANES, C_GLR),
            full((LANES, hk)),
            full((1, hk)),
            full((1, hv)),
            full(dstack.shape),
            full(lmask.shape),
            full(dmask.shape),
            full(e.shape),
            full(bd.shape),
        ]
        + c_in,
        out_specs=[pl.BlockSpec((rs, hv), lambda i, s: (i * ns + s, 0))] + c_out,
        out_shape=[jax.ShapeDtypeStruct((t, hv), BF16)] + c_shape,
        scratch_shapes=[pltpu.VMEM((hv, hk), F32)],
        compiler_params=_cparams(("parallel", "arbitrary")),
        name="gla",
    )(
        proj, proj, proj, proj, proj, a2p, ab.reshape(1, hk).astype(F32), ng.reshape(1, hv).astype(F32),
        jnp.asarray(dstack, BF16), jnp.asarray(lmask), jnp.asarray(dmask), jnp.asarray(e, BF16), jnp.asarray(bd),
        *cast,
    )


def _s5_params(lam_re, lam_im, log_dt, b_re, b_im, c_re, c_im, d, nch):
    f = F32
    cs = S5_CH
    dt = jnp.exp(log_dt.astype(f))[:, None]
    lr, li = lam_re.astype(f), lam_im.astype(f)
    mag = jnp.exp(lr * dt)
    ab_re, ab_im = mag * jnp.cos(li * dt), mag * jnp.sin(li * dt)
    den = lr * lr + li * li
    z_re = ((ab_re - 1.0) * lr + ab_im * li) / den
    z_im = (ab_im * lr - (ab_re - 1.0) * li) / den
    br, bi = b_re.astype(f), b_im.astype(f)
    bb_re = z_re[..., None] * br - z_im[..., None] * bi
    bb_im = z_re[..., None] * bi + z_im[..., None] * br

    def apow(p):
        p = jnp.asarray(p, f)[:, None, None]
        m = jnp.exp(p * (lr * dt))
        return m * jnp.cos(p * (li * dt)), m * jnp.sin(p * (li * dt))

    cr, ci = c_re.astype(f), c_im.astype(f)
    p_re, p_im = apow(np.arange(cs + 1))
    nb, gpb = S5_GROUPS // S5_GPB, S5_GPB
    hw = gpb * S5_STATE
    eye = jnp.eye(gpb, dtype=f)

    def b_blockdiag(x):
        x = x.reshape(nb, gpb, S5_STATE, S5_GROUP).transpose(0, 1, 3, 2)
        x = (x[:, :, :, None, :] * eye[None, :, None, :, None]).reshape(nb, LANES, hw)
        return jnp.concatenate([x, x], axis=-1)

    def c_blockdiag(x):
        x = x.reshape(nb, gpb, S5_GROUP, S5_STATE).transpose(0, 1, 3, 2)
        return (x[:, :, :, None, :] * eye[None, :, None, :, None]).reshape(nb, hw, LANES)

    def lanes(a, b_):
        p = a.shape[0]
        return jnp.concatenate([a.reshape(p, nb, hw), b_.reshape(p, nb, hw)], axis=-1).transpose(1, 0, 2)

    rev = np.arange(cs - 1, -1, -1)
    pr, pi = p_re[rev], p_im[rev]
    pa, pb = lanes(pr, pi), lanes(-pi, pr)
    acol_re = jnp.broadcast_to(ab_re.reshape(nb, hw, 1), (nb, hw, LANES))
    acol_im = jnp.broadcast_to(ab_im.reshape(nb, hw, 1), (nb, hw, LANES))
    nstep = max(1, int(math.log2(nch)))
    s_re, s_im = apow(cs * (2 ** np.arange(nstep)))
    ar = s_re.reshape(nstep, nb, hw).transpose(1, 0, 2)
    ai = s_im.reshape(nstep, nb, hw).transpose(1, 0, 2)
    dd = d.astype(f).reshape(nb, 1, LANES)
    return (b_blockdiag(bb_re), b_blockdiag(bb_im), c_blockdiag(cr), c_blockdiag(ci), acol_re, acol_im,
            pa, pb, ar, ai, dd)


def _s5_kernel(u_ref, br_ref, bi_ref, cr_ref, ci_ref, acr_ref, aci_ref, pa_ref, pb_ref, ar_ref, ai_ref, d_ref,
               y_ref, mb_s, mc_s, kc_s, *, nchb):
    cs = S5_CH
    t = u_ref.shape[0]
    nch = t // cs
    hw = S5_GPB * S5_STATE
    br, bi = br_ref[0], bi_ref[0]
    pa, pb = pa_ref[0], pb_ref[0]
    for j in range(cs):
        mb_s[j * LANES : (j + 1) * LANES, :] = (br * pa[j : j + 1] + bi * pb[j : j + 1]).astype(BF16)
    x_re, x_im = cr_ref[0], ci_ref[0]
    kc = _bdot(mb_s[...], jnp.concatenate([x_re, -x_im], axis=0).astype(BF16)).astype(BF16)
    a_re, a_im = acr_ref[0], aci_ref[0]
    for tt in range(cs):
        x_re, x_im = x_re * a_re - x_im * a_im, x_re * a_im + x_im * a_re
        mc_s[:hw, tt * LANES : (tt + 1) * LANES] = x_re.astype(BF16)
        mc_s[hw:, tt * LANES : (tt + 1) * LANES] = (-x_im).astype(BF16)
    gl = cs // S5_TSPLIT * LANES
    for qq in range(S5_TSPLIT):
        kc_s[:, qq * LANES : (qq + 1) * LANES] = kc[(S5_TSPLIT - 1 - qq) * gl : (S5_TSPLIT - qq) * gl]
    ucat = jnp.concatenate([u_ref[pl.ds(j, nch, stride=cs), :] for j in range(cs)], axis=1).astype(BF16)
    e_all = _bdot(ucat, mb_s[...])
    cidx = lax.broadcasted_iota(jnp.int32, (nch, LANES), 0) % nchb
    ar, ai = ar_ref[0], ai_ref[0]
    hp_re, hp_im = [], []
    for g in range(hw // LANES):
        lo = g * LANES
        e_re, e_im = e_all[:, lo : lo + LANES], e_all[:, hw + lo : hw + lo + LANES]
        for kstep in range(ar.shape[0]):
            s = 2**kstep
            if s >= nchb:
                break
            s_re = jnp.where(cidx >= s, pltpu.roll(e_re, s, axis=0), 0.0)
            s_im = jnp.where(cidx >= s, pltpu.roll(e_im, s, axis=0), 0.0)
            k_re, k_im = ar[kstep : kstep + 1, lo : lo + LANES], ai[kstep : kstep + 1, lo : lo + LANES]
            e_re, e_im = e_re + s_re * k_re - s_im * k_im, e_im + s_re * k_im + s_im * k_re
        hp_re.append(jnp.where(cidx >= 1, pltpu.roll(e_re, 1, axis=0), 0.0).astype(BF16))
        hp_im.append(jnp.where(cidx >= 1, pltpu.roll(e_im, 1, axis=0), 0.0).astype(BF16))
    ycar = _bdot(jnp.concatenate(hp_re + hp_im, axis=1), mc_s[...])
    for tt in range(cs):
        y_ref[pl.ds(tt, nch, stride=cs), :] = ycar[:, tt * LANES : (tt + 1) * LANES]

    rt = min(S5_RT, t)
    rmod = lax.broadcasted_iota(jnp.int32, (rt, LANES), 0) % cs
    dvec = d_ref[0]
    glag = cs // S5_TSPLIT

    def tile(i, carry):
        rows = pl.ds(pl.multiple_of(i * rt, rt), rt)
        ut = u_ref[rows, :]
        ush = [jnp.where(rmod >= tau, pltpu.roll(ut, tau, axis=0), 0.0) for tau in range(glag - 1, 0, -1)] + [ut]
        zz = _bdot(jnp.concatenate(ush, axis=1).astype(BF16), kc_s[...])
        acc = y_ref[rows, :] + dvec * ut + zz[:, :LANES]
        for qq in range(1, S5_TSPLIT):
            part = pltpu.roll(zz[:, qq * LANES : (qq + 1) * LANES], qq * glag, axis=0)
            acc = acc + jnp.where(rmod >= qq * glag, part, 0.0)
        y_ref[rows, :] = acc
        return carry

    lax.fori_loop(0, t // rt, tile, 0)


def _s5_core(proj, b, params):
    t = proj.shape[0]
    nchb = t // b // S5_CH
    sw = S5_GPB * 2 * S5_STATE

    def bspec(a):
        return pl.BlockSpec((1,) + a.shape[1:], lambda p: (p, 0, 0))

    return pl.pallas_call(
        functools.partial(_s5_kernel, nchb=nchb),
        grid=(S5_GROUPS // S5_GPB,),
        in_specs=[pl.BlockSpec((t, LANES), lambda p: (0, C_S5 // LANES + p))] + [bspec(a) for a in params],
        out_specs=pl.BlockSpec((t, LANES), lambda p: (0, p)),
        out_shape=jax.ShapeDtypeStruct((t, BRANCH_WIDTH), F32),
        scratch_shapes=[
            pltpu.VMEM((S5_CH * LANES, sw), BF16),
            pltpu.VMEM((sw, S5_CH * LANES), BF16),
            pltpu.VMEM((S5_CH // S5_TSPLIT * LANES, S5_TSPLIT * LANES), BF16),
        ],
        compiler_params=_cparams(("parallel",)),
        name="s5_core",
    )(proj, *params)


def _merge_kernel(
    x_ref, g1_ref, gla_ref, ys5_ref, dil_ref, swa_ref, gluw_ref, glub_ref, wg0, wg1, wg2, wg3, wb_ref, wo_ref,
    o_ref, h_s, s5_s,
):
    j = pl.program_id(1)

    @pl.when(j == 0)
    def _():
        h_s[...] = _rms(x_ref[...], g1_ref[...]).astype(BF16)
        zz = jax.nn.gelu(ys5_ref[...])
        gate = jax.nn.sigmoid(_bdot(zz.astype(BF16), gluw_ref[...]) + glub_ref[...])
        s5_s[...] = (zz * gate).astype(BF16)
        o_ref[...] = jnp.zeros_like(o_ref)

    h = h_s[...]
    branches = (gla_ref[...], s5_s[...], dil_ref[...], swa_ref[...])
    mixed = None
    for m, (wg, br) in enumerate(zip((wg0, wg1, wg2, wg3), branches)):
        term = jax.nn.sigmoid(_bdot(h, wg[...])) * _bdot(br, wb_ref[m])
        mixed = term if mixed is None else mixed + term
    mixed = mixed.astype(BF16)
    d = o_ref.shape[1]
    for c0 in range(0, d, FFN_DOWN_CHUNK):
        cols = slice(c0, c0 + FFN_DOWN_CHUNK)
        o_ref[:, cols] += _bdot(mixed, wo_ref[:, cols])

    @pl.when(j == pl.num_programs(1) - 1)
    def _():
        o_ref[...] = x_ref[...] + o_ref[...]


def _merge(x, g1, o_gla, y_s5, o_dil, o_swa, gluw, glub, w_all, wb, wo, layer):
    t, d = x.shape
    tm, tn = min(TM_MERGE, t), TN_MERGE
    nj = d // tn
    bw = BRANCH_WIDTH

    def rowblk(w):
        return pl.BlockSpec((tm, w), lambda i, j: (i, 0))

    def gate_spec(m):
        return pl.BlockSpec((None, d, tn), lambda i, j, m=m: (layer, 0, N_SMALL // tn + m * nj + j))

    return pl.pallas_call(
        _merge_kernel,
        grid=(t // tm, nj),
        in_specs=[
            rowblk(d),
            pl.BlockSpec((1, d), lambda i, j: (0, 0)),
            rowblk(bw),
            rowblk(bw),
            rowblk(bw),
            rowblk(bw),
            pl.BlockSpec((None, bw, bw), lambda i, j: (layer, 0, 0)),
            pl.BlockSpec((1, bw), lambda i, j: (0, 0)),
            gate_spec(0),
            gate_spec(1),
            gate_spec(2),
            gate_spec(3),
            pl.BlockSpec((N_BRANCH, bw, tn), lambda i, j: (0, 0, j)),
            pl.BlockSpec((tn, d), lambda i, j: (j, 0)),
        ],
        out_specs=rowblk(d),
        out_shape=jax.ShapeDtypeStruct((t, d), F32),
        scratch_shapes=[pltpu.VMEM((tm, d), BF16), pltpu.VMEM((tm, bw), BF16)],
        compiler_params=_cparams(("parallel", "arbitrary")),
        name="merge",
    )(x, g1, o_gla, y_s5, o_dil, o_swa, gluw, glub, w_all, w_all, w_all, w_all, wb, wo)


def _ffn_kernel(x_ref, g2_ref, wg_ref, wu_ref, wd_ref, gf_ref, o_ref, h_s, *, final_norm):
    j = pl.program_id(1)
    tm, d = o_ref.shape
    chunks = [slice(r0, r0 + FFN_ROW_CHUNK) for r0 in range(0, tm, FFN_ROW_CHUNK)]

    @pl.when(j == 0)
    def _():
        for rows in chunks:
            h_s[rows, :] = _rms(x_ref[rows, :], g2_ref[...]).astype(BF16)
        o_ref[...] = jnp.zeros_like(o_ref)

    for rows in chunks:
        h = h_s[rows, :]
        gate = _bdot(h, wg_ref[...])
        act = ((gate * jax.nn.sigmoid(gate)) * _bdot(h, wu_ref[...])).astype(BF16)
        for c0 in range(0, d, FFN_DOWN_CHUNK):
            cols = slice(c0, c0 + FFN_DOWN_CHUNK)
            o_ref[rows, cols] += _bdot(act, wd_ref[:, cols])

    @pl.when(j == pl.num_programs(1) - 1)
    def _():
        for rows in chunks:
            y = x_ref[rows, :] + o_ref[rows, :]
            o_ref[rows, :] = _rms(y, gf_ref[...]) if final_norm else y


def _ffn(x, g2, wg, wu, wd, gf, final_norm):
    t, d = x.shape
    fh = wg.shape[-1]
    tm, tf = min(TM_FFN, t), TF_FFN
    return pl.pallas_call(
        functools.partial(_ffn_kernel, final_norm=final_norm),
        grid=(t // tm, fh // tf),
        in_specs=[
            pl.BlockSpec((tm, d), lambda i, j: (i, 0)),
            pl.BlockSpec((1, d), lambda i, j: (0, 0)),
            pl.BlockSpec((d, tf), lambda i, j: (0, j)),
            pl.BlockSpec((d, tf), lambda i, j: (0, j)),
            pl.BlockSpec((tf, d), lambda i, j: (j, 0)),
            pl.BlockSpec((1, d), lambda i, j: (0, 0)),
        ],
        out_specs=pl.BlockSpec((tm, d), lambda i, j: (i, 0)),
        out_shape=jax.ShapeDtypeStruct((t, d), F32),
        scratch_shapes=[pltpu.VMEM((tm, d), BF16)],
        compiler_params=_cparams(("parallel", "arbitrary"), VMEM_LIMIT_FFN),
        name="ffn",
    )(x, g2, wg, wu, wd, gf)


def _pack_kernel(*refs):
    o_ref = refs[-1]
    for piece, w_ref in enumerate(refs[:-1]):
        x = w_ref[...]
        row = lax.broadcasted_iota(jnp.int32, x.shape, 0)
        blk = pl.program_id(1) * PACK_PIECES + piece
        keep = jnp.where(blk == C_GLR // PACK_CB, GLA_LOWRANK, PACK_CB)
        o_ref[:, piece * PACK_CB : (piece + 1) * PACK_CB] = jnp.where(row < keep, x, 0.0).T.astype(BF16)


def _pack_row0(c):
    cb = PACK_CB
    u = GLA_LOWRANK
    front = jnp.where(c < GLR_ORIG // cb, c * (cb // u), c * (cb // u) + 1)
    back = jnp.where(c == C_GLR // cb, GLR_ORIG // u, c * (cb // u) - (N_SMALL - N_ORIG_SMALL) // u)
    return jnp.where(c < C_GLR // cb, front, back) * u


def _pack_w_in(w_in):
    depth, d, d_in = w_in.shape
    n_out = N_SMALL + d_in - N_ORIG_SMALL
    cb = PACK_CB
    assert GLR_ORIG % cb == 0 and C_GLR % cb == 0 and N_SMALL % cb == 0 and n_out % cb == 0
    assert C_GLR == N_ORIG_SMALL - GLA_LOWRANK and C_GLR + cb == N_SMALL
    np_ = PACK_PIECES
    assert n_out % (cb * np_) == 0
    w_t = jnp.swapaxes(w_in, 1, 2)
    return pl.pallas_call(
        _pack_kernel,
        grid=(depth, n_out // (cb * np_)),
        in_specs=[
            pl.BlockSpec((None, pl.Element(cb), pl.Element(d)), lambda i, c, p=p: (i, _pack_row0(c * np_ + p), 0))
            for p in range(np_)
        ],
        out_specs=pl.BlockSpec((None, d, cb * np_), lambda i, c: (i, 0, c)),
        out_shape=jax.ShapeDtypeStruct((depth, d, n_out), BF16),
        compiler_params=_cparams(("parallel", "parallel")),
        name="pack_w_in",
    )(*([w_t] * np_))


def kernel(x, positions, norm1_g, w_in, gla_a2, gla_a_b, gla_norm_g, s5_lambda_re, s5_lambda_im, s5_log_dt, s5_b_re, s5_b_im, s5_c_re, s5_c_im, s5_d, s5_glu_w, s5_glu_b, swa_sinks, w_branch, w_out, norm2_g, w_ffn_gate, w_ffn_up, w_ffn_down, final_norm_g):
    b, l, d = x.shape
    t = b * l
    depth = w_in.shape[0]
    xs = x.reshape(t, d).astype(F32)
    w_all = _pack_w_in(w_in)
    w_br32 = w_branch.reshape(depth, N_BRANCH * BRANCH_WIDTH, d)
    glu_w = s5_glu_w.astype(BF16)
    tabs = _rope_tables(positions)
    gf = final_norm_g.reshape(1, d).astype(F32)
    s5_all = jax.vmap(functools.partial(_s5_params, nch=l // S5_CH))(
        s5_lambda_re, s5_lambda_im, s5_log_dt, s5_b_re, s5_b_im, s5_c_re, s5_c_im, s5_d)
    for i in range(depth):
        g1 = norm1_g[i].reshape(1, d).astype(F32)
        proj = _inproj(xs, g1, w_all, i)
        proj3 = proj.reshape(b, l, N_SMALL)
        o_gla, w_fg, w_fu, w_br, w_o = _gla(proj, b, gla_a2[i], gla_a_b[i], gla_norm_g[i],
                                            (w_ffn_gate, w_ffn_up, w_br32, w_out), i)
        w_br = w_br.reshape(N_BRANCH, BRANCH_WIDTH, d)
        y_s5 = _s5_core(proj, b, [a[i] for a in s5_all])
        o_dil = _dilated(proj3, tabs).reshape(t, BRANCH_WIDTH)
        o_swa, w_fd = _swa(proj3, tabs, swa_sinks[i], (w_ffn_down,), i)
        o_swa = o_swa.reshape(t, BRANCH_WIDTH)
        xs = _merge(xs, g1, o_gla, y_s5, o_dil, o_swa, glu_w, s5_glu_b[i].reshape(1, -1).astype(F32),
                    w_all, w_br, w_o, i)
        xs = _ffn(xs, norm2_g[i].reshape(1, d).astype(F32), w_fg, w_fu, w_fd, gf, i == depth - 1)
    return xs.reshape(b, l, d).astype(x.dtype)
```

```python
import functools
import math

import jax
import jax.numpy as jnp
import numpy as np
from jax import lax
from jax.experimental import pallas as pl
from jax.experimental.pallas import tpu as pltpu

F32 = jnp.float32
BF16 = jnp.bfloat16

D_MODEL = 2048
N_BRANCH = 4
BRANCH_WIDTH = 512
HEAD_DIM = 64
ATT_BLOCK = 128
ROPE_THETA = 500000.0
ROPE_DIM = HEAD_DIM // 4
NORM_EPS = 1e-6
GLA_HEADS = 4
GLA_DK = 64
GLA_DV = BRANCH_WIDTH // GLA_HEADS
GLA_LOWRANK = 16
GLA_TAU = 16.0
S5_GROUP = 16
S5_GROUPS = BRANCH_WIDTH // S5_GROUP
S5_STATE = 64
DIL_CONFIGS = ((128, 1), (512, 4), (2048, 16))
DIL_SPAN = ATT_BLOCK * 16
SWA_HEADS = BRANCH_WIDTH // HEAD_DIM
SWA_KV_HEADS = 2
SWA_WINDOW = 128

LANES = 128
SUBLANES = 8
VMEM_LIMIT = 56 * 1024 * 1024
VMEM_LIMIT_LARGE = 60 * 1024 * 1024

C_GQ, C_GK, C_GV, C_GR, C_S5 = 0, 256, 512, 1024, 1536
C_CQ, C_CK, C_CV, C_SQ, C_SK, C_SV, C_GLR = 2048, 2560, 3072, 3584, 4096, 4224, 4352
N_SMALL = 4608
N_ORIG_SMALL = 4368
GLR_ORIG = 1536

TM_PROJ, TN_PROJ = 1024, 1536
TM_MERGE, TN_MERGE = 512, 512
TM_FFN, TF_FFN = 1024, 512
FFN_ROW_CHUNK = 512
FFN_DOWN_CHUNK = 512
GLA_CH = 128
GLA_NCS = 4
GLA_SUB = 8
GLA_LEVELS = (8, 16, 32, 64, 128)
PACK_CB = 256
PACK_PIECES = 5
S5_CH = 16
S5_GPB = LANES // S5_GROUP
S5_RT = 512
S5_TSPLIT = 4
ATT_UNROLL = 8
SWA_UNROLL = 16


def _cparams(sem, vmem_limit=VMEM_LIMIT):
    return pltpu.CompilerParams(dimension_semantics=sem, vmem_limit_bytes=vmem_limit)


def _rms(x, g):
    ms = jnp.mean(x * x, axis=-1, keepdims=True)
    return x * lax.rsqrt(ms + NORM_EPS) * g


def _bdot(a, b):
    return jnp.dot(a, b, preferred_element_type=F32)


def _bdot_nt(a, b):
    return lax.dot_general(a, b, (((1,), (1,)), ((), ())), preferred_element_type=F32)


def _cast_specs(ws, layer, nsteps, step_of):
    ins, outs, shapes = [], [], []
    for w in ws:
        _, r, c = w.shape
        rb = r // nsteps
        assert rb * nsteps == r and rb % (2 * SUBLANES) == 0
        ins.append(pl.BlockSpec((None, rb, c), lambda *g: (layer, step_of(*g), 0)))
        outs.append(pl.BlockSpec((rb, c), lambda *g: (step_of(*g), 0)))
        shapes.append(jax.ShapeDtypeStruct((r, c), BF16))
    return ins, outs, shapes


def _side_cast(in_refs, out_refs):
    for src, dst in zip(in_refs, out_refs):
        dst[...] = src[...].astype(BF16)


def _rope_apply(x, c, sa, sb):
    half = ROPE_DIM // 2
    return x * c + pltpu.roll(x, LANES - half, axis=1) * sa + pltpu.roll(x, half, axis=1) * sb


def _inproj_kernel(x_ref, g_ref, w_ref, o_ref, h_ref):
    @pl.when(pl.program_id(1) == 0)
    def _():
        h_ref[...] = _rms(x_ref[...], g_ref[...]).astype(BF16)

    o_ref[...] = _bdot(h_ref[...], w_ref[...])


def _inproj(x, g, w_all, layer):
    t, d = x.shape
    n = N_SMALL
    tm, tn = min(TM_PROJ, t), TN_PROJ
    return pl.pallas_call(
        _inproj_kernel,
        grid=(t // tm, n // tn),
        in_specs=[
            pl.BlockSpec((tm, d), lambda i, j: (i, 0)),
            pl.BlockSpec((1, d), lambda i, j: (0, 0)),
            pl.BlockSpec((None, d, tn), lambda i, j: (layer, 0, j)),
        ],
        out_specs=[pl.BlockSpec((tm, tn), lambda i, j: (i, j)), pl.BlockSpec((tm, d), lambda i, j: (i, 0))],
        out_shape=[jax.ShapeDtypeStruct((t, n), F32), jax.ShapeDtypeStruct((t, d), BF16)],
        compiler_params=_cparams(("parallel", "arbitrary")),
        name="inproj",
    )(x, g, w_all)


def _rope_kernel(pos_ref, cos_ref, sa_ref, sb_ref):
    pos = pos_ref[0].astype(F32)
    lane = lax.broadcasted_iota(jnp.int32, (SUBLANES, LANES), 1)
    d = lane % HEAD_DIM
    half = ROPE_DIM // 2
    fi = (d % half).astype(F32) / half
    inv = jnp.power(jnp.full((SUBLANES, LANES), ROPE_THETA, F32), -fi)[0:1]
    d1 = d[0:1]
    ang = pos * inv
    c, s = jnp.cos(ang), jnp.sin(ang)
    cos_ref[0] = jnp.where(d1 < ROPE_DIM, c, 1.0)
    sa_ref[0] = jnp.where(d1 < half, -s, 0.0)
    sb_ref[0] = jnp.where((d1 >= half) & (d1 < ROPE_DIM), s, 0.0)


def _rope_tables(positions):
    b, l = positions.shape
    spec = pl.BlockSpec((1, l, LANES), lambda i: (i, 0, 0))
    shp = jax.ShapeDtypeStruct((b, l, LANES), F32)
    return pl.pallas_call(
        _rope_kernel,
        grid=(b,),
        in_specs=[pl.BlockSpec((1, l, 1), lambda i: (i, 0, 0))],
        out_specs=[spec, spec, spec],
        out_shape=[shp, shp, shp],
        compiler_params=_cparams(("parallel",)),
        name="rope_tables",
    )(positions.reshape(b, l, 1))


def _band_bias(max_dist):
    t = ATT_BLOCK
    assert 0 < max_dist <= t
    qi = np.arange(2 * t)[:, None] % t
    kj = np.arange(2 * t)[None, :]
    own = (kj >= t) & (kj - t <= qi)
    prev = (kj < t) & (t + qi - kj <= max_dist)
    ninf = np.float32(-np.inf)
    return np.stack([np.where(own | prev, 0.0, ninf), np.where(own, 0.0, ninf)]).astype(np.float32)


def _band_block(q, kp, ko, vp, vo, bias):
    t = ATT_BLOCK
    lane = lax.broadcasted_iota(jnp.int32, (t, LANES), 1)
    in_a = lane < HEAD_DIM
    q2 = jnp.concatenate([jnp.where(in_a, q, 0.0), jnp.where(in_a, 0.0, q)], axis=0).astype(BF16)
    s = _bdot_nt(q2, jnp.concatenate([kp, ko], axis=0).astype(BF16)) + bias
    m = jnp.max(s, axis=-1, keepdims=True)
    p = jnp.exp(s - m).astype(BF16)
    ones = jnp.ones((2 * t, LANES), BF16)
    vcat = jnp.concatenate([jnp.concatenate([vp, vo], axis=0).astype(BF16), ones], axis=1)
    od = _bdot(p, vcat)
    o = jnp.where(in_a, od[:t, :LANES], od[t:, :LANES])
    df = jnp.where(in_a, od[:t, LANES:], od[t:, LANES:])
    mf = jnp.where(in_a, m[:t], m[t:])
    return mf, df, o


def _dil_kernel(q_ref, k_ref, v_ref, c_ref, sa_ref, sb_ref, bias_ref, o_ref, qs, ks, qd, kd, vd, ms, ds, os_):
    l = q_ref.shape[1]
    c, sa, sb = c_ref[0], sa_ref[0], sb_ref[0]
    qs[...] = _rope_apply(q_ref[0], c, sa, sb) * (HEAD_DIM**-0.5)
    ks[...] = _rope_apply(k_ref[0], c, sa, sb)
    t = ATT_BLOCK
    nblk = DIL_SPAN // t
    dw = DIL_SPAN // t
    per = l // dw
    for r in range(dw):
        src, dst = pl.ds(r, per, stride=dw), slice(r * per, (r + 1) * per)
        qd[dst, :] = qs[src, :]
        kd[dst, :] = ks[src, :]
        vd[dst, :] = v_ref[0, src, :]

    for sblk in range(l // DIL_SPAN):
        base = sblk * DIL_SPAN
        for ci, (_, dil) in enumerate(DIL_CONFIGS):

            def body_wide(r, carry, ci=ci, sblk=sblk):
                rows = pl.ds(pl.multiple_of(r * per + sblk * t, t), t)
                prow = pl.ds(pl.multiple_of(r * per + max(sblk - 1, 0) * t, t), t)
                mf, df, o = _band_block(qd[rows, :], kd[prow, :], kd[rows, :], vd[prow, :], vd[rows, :],
                                        bias_ref[0 if sblk > 0 else 1])
                lrow = pl.ds(r, t, stride=dw)
                ms[ci, lrow, :] = mf
                ds[ci, lrow, :] = df
                os_[ci, lrow, :] = o
                return carry

            if dil == dw:
                lax.fori_loop(0, nblk, body_wide, 0, unroll=ATT_UNROLL)
                continue

            def body(idx, carry, dil=dil, ci=ci, base=base):
                r = idx % dil
                n = idx // dil
                loc = r + dil * t * n
                start = base + loc
                has_prev = start >= dil * t
                pstart = jnp.where(has_prev, start - dil * t, start)
                if dil == 1:
                    start = pl.multiple_of(start, t)
                    pstart = pl.multiple_of(pstart, t)
                    loc = pl.multiple_of(loc, t)
                    rows, prow, lrow = pl.ds(start, t), pl.ds(pstart, t), pl.ds(loc, t)
                else:
                    rows = pl.ds(start, t, stride=dil)
                    prow = pl.ds(pstart, t, stride=dil)
                    lrow = pl.ds(loc, t, stride=dil)
                mf, df, o = _band_block(
                    qs[rows, :], ks[prow, :], ks[rows, :], v_ref[0, prow, :], v_ref[0, rows, :],
                    bias_ref[jnp.where(has_prev, 0, 1)],
                )
                ms[ci, lrow, :] = mf
                ds[ci, lrow, :] = df
                os_[ci, lrow, :] = o
                return carry

            lax.fori_loop(0, nblk, body, 0, unroll=ATT_UNROLL)

        m0, m1, m2 = ms[0], ms[1], ms[2]
        mx = jnp.maximum(jnp.maximum(m0, m1), m2)
        w0, w1, w2 = jnp.exp(m0 - mx), jnp.exp(m1 - mx), jnp.exp(m2 - mx)
        den = w0 * ds[0] + w1 * ds[1] + w2 * ds[2]
        num = w0 * os_[0] + w1 * os_[1] + w2 * os_[2]
        o_ref[0, base : base + DIL_SPAN, :] = (num / den).astype(o_ref.dtype)


def _dilated(proj3, tabs):
    b, l, _ = proj3.shape
    assert l % DIL_SPAN == 0
    assert all(window // dil == ATT_BLOCK for window, dil in DIL_CONFIGS)
    bias = _band_bias(ATT_BLOCK)

    def col(c0):
        return pl.BlockSpec((1, l, LANES), lambda i, p: (i, 0, c0 // LANES + p))

    tab = pl.BlockSpec((1, l, LANES), lambda i, p: (i, 0, 0))
    return pl.pallas_call(
        _dil_kernel,
        grid=(b, BRANCH_WIDTH // LANES),
        in_specs=[col(C_CQ), col(C_CK), col(C_CV), tab, tab, tab, pl.BlockSpec(bias.shape, lambda i, p: (0, 0, 0))],
        out_specs=pl.BlockSpec((1, l, LANES), lambda i, p: (i, 0, p)),
        out_shape=jax.ShapeDtypeStruct((b, l, BRANCH_WIDTH), BF16),
        scratch_shapes=[pltpu.VMEM((l, LANES), F32)] * 5
        + [pltpu.VMEM((len(DIL_CONFIGS), DIL_SPAN, LANES), F32)] * 3,
        compiler_params=_cparams(("parallel", "parallel")),
        name="dilated_attn",
    )(proj3, proj3, proj3, *tabs, jnp.asarray(bias))


def _swa_kernel(*refs, n_cast):
    q_ref, k_ref, v_ref, c_ref, sa_ref, sb_ref, bias_ref, sink_ref = refs[:8]
    o_ref = refs[8 + n_cast]
    qs, ks, vs = refs[-3:]
    _side_cast(refs[8 : 8 + n_cast], refs[9 + n_cast : 9 + 2 * n_cast])
    l = q_ref.shape[1]
    c, sa, sb = c_ref[0], sa_ref[0], sb_ref[0]
    qs[...] = _rope_apply(q_ref[0], c, sa, sb) * (HEAD_DIM**-0.5)
    g = pl.program_id(1) // (SWA_HEADS // SWA_KV_HEADS // 2)
    lane = lax.broadcasted_iota(jnp.int32, (l, LANES), 1)
    keep = (lane // HEAD_DIM) == g
    k, v = _rope_apply(k_ref[0], c, sa, sb), v_ref[0]
    ks[...] = jnp.where(keep, k, pltpu.roll(k, HEAD_DIM, axis=1))
    vs[...] = jnp.where(keep, v, pltpu.roll(v, HEAD_DIM, axis=1))

    sink = sink_ref[0]
    t = ATT_BLOCK

    def body(n, carry):
        start = pl.multiple_of(n * t, t)
        has_prev = n > 0
        pstart = pl.multiple_of(jnp.where(has_prev, start - t, start), t)
        rows, prow = pl.ds(start, t), pl.ds(pstart, t)
        mf, df, o = _band_block(qs[rows, :], ks[prow, :], ks[rows, :], vs[prow, :], vs[rows, :],
                                bias_ref[jnp.where(has_prev, 0, 1)])
        lse = mf + jnp.log(df)
        o_ref[0, rows, :] = (o / df * jax.nn.sigmoid(lse - sink)).astype(o_ref.dtype)
        return carry

    lax.fori_loop(0, l // t, body, 0, unroll=SWA_UNROLL)


def _swa(proj3, tabs, sinks, cast=(), layer=0):
    b, l, _ = proj3.shape
    npair = BRANCH_WIDTH // LANES
    sink_l = jnp.repeat(sinks.astype(F32), HEAD_DIM).reshape(npair, 1, LANES)
    bias = _band_bias(SWA_WINDOW - 1)
    tab = pl.BlockSpec((1, l, LANES), lambda i, p: (i, 0, 0))
    c_in, c_out, c_shape = _cast_specs(cast, layer, b * npair, lambda i, p: i * npair + p)
    return pl.pallas_call(
        functools.partial(_swa_kernel, n_cast=len(cast)),
        grid=(b, npair),
        in_specs=[
            pl.BlockSpec((1, l, LANES), lambda i, p: (i, 0, C_SQ // LANES + p)),
            pl.BlockSpec((1, l, LANES), lambda i, p: (i, 0, C_SK // LANES)),
            pl.BlockSpec((1, l, LANES), lambda i, p: (i, 0, C_SV // LANES)),
            tab,
            tab,
            tab,
            pl.BlockSpec(bias.shape, lambda i, p: (0, 0, 0)),
            pl.BlockSpec((1, 1, LANES), lambda i, p: (p, 0, 0)),
        ]
        + c_in,
        out_specs=[pl.BlockSpec((1, l, LANES), lambda i, p: (i, 0, p))] + c_out,
        out_shape=[jax.ShapeDtypeStruct((b, l, BRANCH_WIDTH), BF16)] + c_shape,
        scratch_shapes=[pltpu.VMEM((l, LANES), F32)] * 3,
        compiler_params=_cparams(("parallel", "parallel")),
        name="swa_attn",
    )(proj3, proj3, proj3, *tabs, jnp.asarray(bias), sink_l, *cast)


@functools.lru_cache(maxsize=None)
def _gla_consts():
    ch = GLA_CH
    r = np.arange(ch)[:, None]
    j = np.arange(ch)[None, :]
    dstack = (j <= r).astype(np.float32)
    lm = []
    for m in GLA_LEVELS[:-1]:
        mk = (((r // m) % 2 == 1) & ((j // m) == (r // m) - 1)).astype(np.float32)
        lm.append(np.tile(mk, (GLA_HEADS, 1)))
    lmask = np.stack(lm)
    dmask = np.tile(((r // GLA_SUB) == (j // GLA_SUB)).astype(np.float32), (1, GLA_HEADS))
    nsub = ch // GLA_SUB
    e = np.zeros((GLA_SUB, GLA_HEADS * GLA_DK, GLA_HEADS * ch), np.float32)
    for u in range(GLA_SUB):
        for h in range(GLA_HEADS):
            for s in range(nsub):
                e[u, h * GLA_DK : (h + 1) * GLA_DK, h * ch + GLA_SUB * s + u] = 1.0
    rr = np.arange(GLA_HEADS * GLA_DV)[:, None]
    cc = np.arange(GLA_HEADS * GLA_DK)[None, :]
    bd = ((rr // GLA_DV) == (cc // GLA_DK)).astype(np.float32)
    return dstack, lmask, dmask, e, bd


def _split3(x):
    hi = x.astype(BF16)
    r1 = x - hi.astype(F32)
    mid = r1.astype(BF16)
    lo = (r1 - mid.astype(F32)).astype(BF16)
    return hi, mid, lo


def _bcast_grp(x, m, u):
    r, w = x.shape
    x3 = x.reshape(r // m, m, w)
    return jnp.broadcast_to(x3[:, u : u + 1, :], x3.shape).reshape(r, w)


def _bcast_sub(x, u):
    return _bcast_grp(x, GLA_SUB, u)


def _gla_kernel(*refs, n_cast):
    (q_ref, k_ref, v_ref, r_ref, glr_ref, a2_ref, ab_ref, ng_ref, dst_ref, lmask_ref, dmask_ref, e_ref,
     bd_ref) = refs[:13]
    o_ref, st_ref = refs[13 + n_cast], refs[-1]
    _side_cast(refs[13 : 13 + n_cast], refs[14 + n_cast : 14 + 2 * n_cast])
    ch = GLA_CH
    rs = q_ref.shape[0]
    ncs = rs // ch
    nlev = len(GLA_LEVELS)
    hk = GLA_HEADS * GLA_DK

    @pl.when(pl.program_id(1) == 0)
    def _():
        st_ref[...] = jnp.zeros_like(st_ref)

    q = q_ref[...] * (GLA_DK**-0.5)
    k = k_ref[...]
    z = _bdot(glr_ref[...].astype(BF16), a2_ref[...]) + ab_ref[...]
    g = (jnp.minimum(z, 0.0) - jnp.log(1.0 + jnp.exp(-jnp.abs(z)))) * (1.0 / GLA_TAU)
    g3 = _split3(g)
    dst = dst_ref[...]

    cums, excl = [], []
    for c in range(ncs):
        rows = slice(c * ch, (c + 1) * ch)
        cums.append(_bdot(dst, g3[0][rows]) + _bdot(dst, g3[1][rows]) + _bdot(dst, g3[2][rows]))
        excl.append(cums[c] - g[rows])

    def eq(c, li):
        m = GLA_LEVELS[li]
        return cums[c] - _bcast_grp(excl[c], m, 0)

    def ek(c, li):
        m = GLA_LEVELS[li]
        return _bcast_grp(cums[c], m, m - 1) - cums[c]

    cs = jnp.concatenate([eq(c, 0) for c in range(ncs)], axis=0)
    tsub = lax.broadcasted_iota(jnp.int32, (rs, hk), 0) % GLA_SUB
    arep = jnp.zeros((rs, GLA_HEADS * ch), F32)
    for u in range(GLA_SUB):
        dec = jnp.exp(jnp.where(tsub >= u, cs - _bcast_sub(cs, u), -jnp.inf))
        p = q * _bcast_sub(k, u) * dec
        arep = arep + _bdot(p.astype(BF16), e_ref[u])
    arep = arep * jnp.concatenate([dmask_ref[...]] * ncs, axis=0)

    lane_k = lax.broadcasted_iota(jnp.int32, (ch, hk), 1) // GLA_DK
    ng = ng_ref[...]
    bd = bd_ref[...]
    for c in range(ncs):
        rows = slice(c * ch, (c + 1) * ch)
        qc, kc = q[rows], k[rows]
        vc = v_ref[rows, :]
        aoff = jnp.zeros((GLA_HEADS * ch, ch), F32)
        for li in range(nlev - 1):
            qe = qc * jnp.exp(eq(c, li))
            ke = (kc * jnp.exp(ek(c, li))).astype(BF16)
            qst = jnp.concatenate([jnp.where(lane_k == h, qe, 0.0) for h in range(GLA_HEADS)], axis=0).astype(BF16)
            aoff = aoff + _bdot_nt(qst, ke) * lmask_ref[li]
        cum = eq(c, nlev - 1)
        st = st_ref[...]
        o_inter = _bdot_nt((qc * jnp.exp(cum)).astype(BF16), st.astype(BF16))
        vb = vc.astype(BF16)
        outs = []
        for h in range(GLA_HEADS):
            a_h = aoff[h * ch : (h + 1) * ch] + arep[rows, h * ch : (h + 1) * ch]
            o_h = _bdot(a_h.astype(BF16), vb[:, h * GLA_DV : (h + 1) * GLA_DV]) + o_inter[:, h * GLA_DV : (h + 1) * GLA_DV]
            ms = jnp.mean(o_h * o_h, axis=-1, keepdims=True)
            outs.append(o_h * lax.rsqrt(ms + NORM_EPS))
        o = jnp.concatenate(outs, axis=1) * ng
        rc = r_ref[rows, :]
        o_ref[rows, :] = (o * (rc * jax.nn.sigmoid(rc))).astype(o_ref.dtype)
        ke_last = (kc * jnp.exp(ek(c, nlev - 1))).astype(BF16)
        kv = _bdot(vc.T.astype(BF16), ke_last)
        st_ref[...] = st * jnp.exp(cum[ch - 1 : ch, :]) + kv * bd


def _gla(proj, b, a2, ab, ng, cast=(), layer=0):
    t = proj.shape[0]
    l = t // b
    rs = GLA_CH * GLA_NCS
    assert l % rs == 0
    ns = l // rs
    dstack, lmask, dmask, e, bd = _gla_consts()
    c_in, c_out, c_shape = _cast_specs(cast, layer, b * ns, lambda i, s: i * ns + s)
    a2p = jnp.zeros((LANES, GLA_HEADS * GLA_DK), BF16).at[:GLA_LOWRANK].set(a2.astype(BF16))

    def rowblk(w, c0):
        return pl.BlockSpec((rs, w), lambda i, s: (i * ns + s, c0 // w))

    def full(shape):
        nd = len(shape)
        return pl.BlockSpec(shape, lambda i, s: (0,) * nd)

    hk, hv = GLA_HEADS * GLA_DK, GLA_HEADS * GLA_DV
    return pl.pallas_call(
        functools.partial(_gla_kernel, n_cast=len(cast)),
        grid=(b, ns),
        in_specs=[
            rowblk(hk, C_GQ),
            rowblk(hk, C_GK),
            rowblk(hv, C_GV),
            rowblk(hv, C_GR),
            rowblk(LANES, C_GLR),
            full((LANES, hk)),
            full((1, hk)),
            full((1, hv)),
            full(dstack.shape),
            full(lmask.shape),
            full(dmask.shape),
            full(e.shape),
            full(bd.shape),
        ]
        + c_in,
        out_specs=[pl.BlockSpec((rs, hv), lambda i, s: (i * ns + s, 0))] + c_out,
        out_shape=[jax.ShapeDtypeStruct((t, hv), BF16)] + c_shape,
        scratch_shapes=[pltpu.VMEM((hv, hk), F32)],
        compiler_params=_cparams(("parallel", "arbitrary")),
        name="gla",
    )(
        proj, proj, proj, proj, proj, a2p, ab.reshape(1, hk).astype(F32), ng.reshape(1, hv).astype(F32),
        jnp.asarray(dstack, BF16), jnp.asarray(lmask), jnp.asarray(dmask), jnp.asarray(e, BF16), jnp.asarray(bd),
        *cast,
    )


def _s5_params(lam_re, lam_im, log_dt, b_re, b_im, c_re, c_im, d, nch):
    f = F32
    cs = S5_CH
    dt = jnp.exp(log_dt.astype(f))[:, None]
    lr, li = lam_re.astype(f), lam_im.astype(f)
    mag = jnp.exp(lr * dt)
    ab_re, ab_im = mag * jnp.cos(li * dt), mag * jnp.sin(li * dt)
    den = lr * lr + li * li
    z_re = ((ab_re - 1.0) * lr + ab_im * li) / den
    z_im = (ab_im * lr - (ab_re - 1.0) * li) / den
    br, bi = b_re.astype(f), b_im.astype(f)
    bb_re = z_re[..., None] * br - z_im[..., None] * bi
    bb_im = z_re[..., None] * bi + z_im[..., None] * br

    def apow(p):
        p = jnp.asarray(p, f)[:, None, None]
        m = jnp.exp(p * (lr * dt))
        return m * jnp.cos(p * (li * dt)), m * jnp.sin(p * (li * dt))

    cr, ci = c_re.astype(f), c_im.astype(f)
    p_re, p_im = apow(np.arange(cs + 1))
    nb, gpb = S5_GROUPS // S5_GPB, S5_GPB
    hw = gpb * S5_STATE
    eye = jnp.eye(gpb, dtype=f)

    def b_blockdiag(x):
        x = x.reshape(nb, gpb, S5_STATE, S5_GROUP).transpose(0, 1, 3, 2)
        x = (x[:, :, :, None, :] * eye[None, :, None, :, None]).reshape(nb, LANES, hw)
        return jnp.concatenate([x, x], axis=-1)

    def c_blockdiag(x):
        x = x.reshape(nb, gpb, S5_GROUP, S5_STATE).transpose(0, 1, 3, 2)
        return (x[:, :, :, None, :] * eye[None, :, None, :, None]).reshape(nb, hw, LANES)

    def lanes(a, b_):
        p = a.shape[0]
        return jnp.concatenate([a.reshape(p, nb, hw), b_.reshape(p, nb, hw)], axis=-1).transpose(1, 0, 2)

    rev = np.arange(cs - 1, -1, -1)
    pr, pi = p_re[rev], p_im[rev]
    pa, pb = lanes(pr, pi), lanes(-pi, pr)
    acol_re = jnp.broadcast_to(ab_re.reshape(nb, hw, 1), (nb, hw, LANES))
    acol_im = jnp.broadcast_to(ab_im.reshape(nb, hw, 1), (nb, hw, LANES))
    nstep = max(1, int(math.log2(nch)))
    s_re, s_im = apow(cs * (2 ** np.arange(nstep)))
    ar = s_re.reshape(nstep, nb, hw).transpose(1, 0, 2)
    ai = s_im.reshape(nstep, nb, hw).transpose(1, 0, 2)
    dd = d.astype(f).reshape(nb, 1, LANES)
    return (b_blockdiag(bb_re), b_blockdiag(bb_im), c_blockdiag(cr), c_blockdiag(ci), acol_re, acol_im,
            pa, pb, ar, ai, dd)


def _s5_kernel(u_ref, br_ref, bi_ref, cr_ref, ci_ref, acr_ref, aci_ref, pa_ref, pb_ref, ar_ref, ai_ref, d_ref,
               y_ref, mb_s, mc_s, kc_s, *, nchb):
    cs = S5_CH
    t = u_ref.shape[0]
    nch = t // cs
    hw = S5_GPB * S5_STATE
    br, bi = br_ref[0], bi_ref[0]
    pa, pb = pa_ref[0], pb_ref[0]
    for j in range(cs):
        mb_s[j * LANES : (j + 1) * LANES, :] = (br * pa[j : j + 1] + bi * pb[j : j + 1]).astype(BF16)
    x_re, x_im = cr_ref[0], ci_ref[0]
    kc = _bdot(mb_s[...], jnp.concatenate([x_re, -x_im], axis=0).astype(BF16)).astype(BF16)
    a_re, a_im = acr_ref[0], aci_ref[0]
    for tt in range(cs):
        x_re, x_im = x_re * a_re - x_im * a_im, x_re * a_im + x_im * a_re
        mc_s[:hw, tt * LANES : (tt + 1) * LANES] = x_re.astype(BF16)
        mc_s[hw:, tt * LANES : (tt + 1) * LANES] = (-x_im).astype(BF16)
    gl = cs // S5_TSPLIT * LANES
    for qq in range(S5_TSPLIT):
        kc_s[:, qq * LANES : (qq + 1) * LANES] = kc[(S5_TSPLIT - 1 - qq) * gl : (S5_TSPLIT - qq) * gl]
    ucat = jnp.concatenate([u_ref[pl.ds(j, nch, stride=cs), :] for j in range(cs)], axis=1).astype(BF16)
    e_all = _bdot(ucat, mb_s[...])
    cidx = lax.broadcasted_iota(jnp.int32, (nch, LANES), 0) % nchb
    ar, ai = ar_ref[0], ai_ref[0]
    hp_re, hp_im = [], []
    for g in range(hw // LANES):
        lo = g * LANES
        e_re, e_im = e_all[:, lo : lo + LANES], e_all[:, hw + lo : hw + lo + LANES]
        for kstep in range(ar.shape[0]):
            s = 2**kstep
            if s >= nchb:
                break
            s_re = jnp.where(cidx >= s, pltpu.roll(e_re, s, axis=0), 0.0)
            s_im = jnp.where(cidx >= s, pltpu.roll(e_im, s, axis=0), 0.0)
            k_re, k_im = ar[kstep : kstep + 1, lo : lo + LANES], ai[kstep : kstep + 1, lo : lo + LANES]
            e_re, e_im = e_re + s_re * k_re - s_im * k_im, e_im + s_re * k_im + s_im * k_re
        hp_re.append(jnp.where(cidx >= 1, pltpu.roll(e_re, 1, axis=0), 0.0).astype(BF16))
        hp_im.append(jnp.where(cidx >= 1, pltpu.roll(e_im, 1, axis=0), 0.0).astype(BF16))
    ycar = _bdot(jnp.concatenate(hp_re + hp_im, axis=1), mc_s[...])
    for tt in range(cs):
        y_ref[pl.ds(tt, nch, stride=cs), :] = ycar[:, tt * LANES : (tt + 1) * LANES]

    rt = min(S5_RT, t)
    rmod = lax.broadcasted_iota(jnp.int32, (rt, LANES), 0) % cs
    dvec = d_ref[0]
    glag = cs // S5_TSPLIT

    def tile(i, carry):
        rows = pl.ds(pl.multiple_of(i * rt, rt), rt)
        ut = u_ref[rows, :]
        ush = [jnp.where(rmod >= tau, pltpu.roll(ut, tau, axis=0), 0.0) for tau in range(glag - 1, 0, -1)] + [ut]
        zz = _bdot(jnp.concatenate(ush, axis=1).astype(BF16), kc_s[...])
        acc = y_ref[rows, :] + dvec * ut + zz[:, :LANES]
        for qq in range(1, S5_TSPLIT):
            part = pltpu.roll(zz[:, qq * LANES : (qq + 1) * LANES], qq * glag, axis=0)
            acc = acc + jnp.where(rmod >= qq * glag, part, 0.0)
        y_ref[rows, :] = acc
        return carry

    lax.fori_loop(0, t // rt, tile, 0)


def _s5_core(proj, b, params):
    t = proj.shape[0]
    nchb = t // b // S5_CH
    sw = S5_GPB * 2 * S5_STATE

    def bspec(a):
        return pl.BlockSpec((1,) + a.shape[1:], lambda p: (p, 0, 0))

    return pl.pallas_call(
        functools.partial(_s5_kernel, nchb=nchb),
        grid=(S5_GROUPS // S5_GPB,),
        in_specs=[pl.BlockSpec((t, LANES), lambda p: (0, C_S5 // LANES + p))] + [bspec(a) for a in params],
        out_specs=pl.BlockSpec((t, LANES), lambda p: (0, p)),
        out_shape=jax.ShapeDtypeStruct((t, BRANCH_WIDTH), F32),
        scratch_shapes=[
            pltpu.VMEM((S5_CH * LANES, sw), BF16),
            pltpu.VMEM((sw, S5_CH * LANES), BF16),
            pltpu.VMEM((S5_CH // S5_TSPLIT * LANES, S5_TSPLIT * LANES), BF16),
        ],
        compiler_params=_cparams(("parallel",)),
        name="s5_core",
    )(proj, *params)


def _merge_kernel(
    x_ref, h_ref, gla_ref, ys5_ref, dil_ref, swa_ref, gluw_ref, glub_ref, wg0, wg1, wg2, wg3, wb_ref, wo_ref,
    o_ref, s5_s,
):
    j = pl.program_id(1)

    @pl.when(j == 0)
    def _():
        zz = jax.nn.gelu(ys5_ref[...])
        gate = jax.nn.sigmoid(_bdot(zz.astype(BF16), gluw_ref[...]) + glub_ref[...])
        s5_s[...] = (zz * gate).astype(BF16)
        o_ref[...] = jnp.zeros_like(o_ref)

    h = h_ref[...]
    branches = (gla_ref[...], s5_s[...], dil_ref[...], swa_ref[...])
    mixed = None
    for m, (wg, br) in enumerate(zip((wg0, wg1, wg2, wg3), branches)):
        term = jax.nn.sigmoid(_bdot(h, wg[...])) * _bdot(br, wb_ref[m])
        mixed = term if mixed is None else mixed + term
    mixed = mixed.astype(BF16)
    d = o_ref.shape[1]
    for c0 in range(0, d, FFN_DOWN_CHUNK):
        cols = slice(c0, c0 + FFN_DOWN_CHUNK)
        o_ref[:, cols] += _bdot(mixed, wo_ref[:, cols])

    @pl.when(j == pl.num_programs(1) - 1)
    def _():
        o_ref[...] = x_ref[...] + o_ref[...]


def _merge(x, h, o_gla, y_s5, o_dil, o_swa, gluw, glub, w_all, wb, wo, layer):
    t, d = x.shape
    tm, tn = min(TM_MERGE, t), TN_MERGE
    nj = d // tn
    bw = BRANCH_WIDTH

    def rowblk(w):
        return pl.BlockSpec((tm, w), lambda i, j: (i, 0))

    def gate_spec(m):
        return pl.BlockSpec((None, d, tn), lambda i, j, m=m: (layer, 0, N_SMALL // tn + m * nj + j))

    return pl.pallas_call(
        _merge_kernel,
        grid=(t // tm, nj),
        in_specs=[
            rowblk(d),
            rowblk(d),
            rowblk(bw),
            rowblk(bw),
            rowblk(bw),
            rowblk(bw),
            pl.BlockSpec((None, bw, bw), lambda i, j: (layer, 0, 0)),
            pl.BlockSpec((1, bw), lambda i, j: (0, 0)),
            gate_spec(0),
            gate_spec(1),
            gate_spec(2),
            gate_spec(3),
            pl.BlockSpec((N_BRANCH, bw, tn), lambda i, j: (0, 0, j)),
            pl.BlockSpec((tn, d), lambda i, j: (j, 0)),
        ],
        out_specs=rowblk(d),
        out_shape=jax.ShapeDtypeStruct((t, d), F32),
        scratch_shapes=[pltpu.VMEM((tm, bw), BF16)],
        compiler_params=_cparams(("parallel", "arbitrary"), VMEM_LIMIT_LARGE),
        name="merge",
    )(x, h, o_gla, y_s5, o_dil, o_swa, gluw, glub, w_all, w_all, w_all, w_all, wb, wo)


def _ffn_kernel(x_ref, g2_ref, wg_ref, wu_ref, wd_ref, gf_ref, o_ref, h_s, *, final_norm):
    j = pl.program_id(1)
    tm, d = o_ref.shape
    chunks = [slice(r0, r0 + FFN_ROW_CHUNK) for r0 in range(0, tm, FFN_ROW_CHUNK)]

    @pl.when(j == 0)
    def _():
        for rows in chunks:
            h_s[rows, :] = _rms(x_ref[rows, :], g2_ref[...]).astype(BF16)
        o_ref[...] = jnp.zeros_like(o_ref)

    for rows in chunks:
        h = h_s[rows, :]
        gate = _bdot(h, wg_ref[...])
        act = ((gate * jax.nn.sigmoid(gate)) * _bdot(h, wu_ref[...])).astype(BF16)
        for c0 in range(0, d, FFN_DOWN_CHUNK):
            cols = slice(c0, c0 + FFN_DOWN_CHUNK)
            o_ref[rows, cols] += _bdot(act, wd_ref[:, cols])

    @pl.when(j == pl.num_programs(1) - 1)
    def _():
        for rows in chunks:
            y = x_ref[rows, :] + o_ref[rows, :]
            o_ref[rows, :] = _rms(y, gf_ref[...]) if final_norm else y


def _ffn(x, g2, wg, wu, wd, gf, final_norm):
    t, d = x.shape
    fh = wg.shape[-1]
    tm, tf = min(TM_FFN, t), TF_FFN
    return pl.pallas_call(
        functools.partial(_ffn_kernel, final_norm=final_norm),
        grid=(t // tm, fh // tf),
        in_specs=[
            pl.BlockSpec((tm, d), lambda i, j: (i, 0)),
            pl.BlockSpec((1, d), lambda i, j: (0, 0)),
            pl.BlockSpec((d, tf), lambda i, j: (0, j)),
            pl.BlockSpec((d, tf), lambda i, j: (0, j)),
            pl.BlockSpec((tf, d), lambda i, j: (j, 0)),
            pl.BlockSpec((1, d), lambda i, j: (0, 0)),
        ],
        out_specs=pl.BlockSpec((tm, d), lambda i, j: (i, 0)),
        out_shape=jax.ShapeDtypeStruct((t, d), F32),
        scratch_shapes=[pltpu.VMEM((tm, d), BF16)],
        compiler_params=_cparams(("parallel", "arbitrary"), VMEM_LIMIT_LARGE),
        name="ffn",
    )(x, g2, wg, wu, wd, gf)


def _pack_kernel(*refs):
    o_ref = refs[-1]
    for piece, w_ref in enumerate(refs[:-1]):
        x = w_ref[...]
        row = lax.broadcasted_iota(jnp.int32, x.shape, 0)
        blk = pl.program_id(1) * PACK_PIECES + piece
        keep = jnp.where(blk == C_GLR // PACK_CB, GLA_LOWRANK, PACK_CB)
        o_ref[:, piece * PACK_CB : (piece + 1) * PACK_CB] = jnp.where(row < keep, x, 0.0).T.astype(BF16)


def _pack_row0(c):
    cb = PACK_CB
    u = GLA_LOWRANK
    front = jnp.where(c < GLR_ORIG // cb, c * (cb // u), c * (cb // u) + 1)
    back = jnp.where(c == C_GLR // cb, GLR_ORIG // u, c * (cb // u) - (N_SMALL - N_ORIG_SMALL) // u)
    return jnp.where(c < C_GLR // cb, front, back) * u


def _pack_w_in(w_in):
    depth, d, d_in = w_in.shape
    n_out = N_SMALL + d_in - N_ORIG_SMALL
    cb = PACK_CB
    assert GLR_ORIG % cb == 0 and C_GLR % cb == 0 and N_SMALL % cb == 0 and n_out % cb == 0
    assert C_GLR == N_ORIG_SMALL - GLA_LOWRANK and C_GLR + cb == N_SMALL
    np_ = PACK_PIECES
    assert n_out % (cb * np_) == 0
    w_t = jnp.swapaxes(w_in, 1, 2)
    return pl.pallas_call(
        _pack_kernel,
        grid=(depth, n_out // (cb * np_)),
        in_specs=[
            pl.BlockSpec((None, pl.Element(cb), pl.Element(d)), lambda i, c, p=p: (i, _pack_row0(c * np_ + p), 0))
            for p in range(np_)
        ],
        out_specs=pl.BlockSpec((None, d, cb * np_), lambda i, c: (i, 0, c)),
        out_shape=jax.ShapeDtypeStruct((depth, d, n_out), BF16),
        compiler_params=_cparams(("parallel", "parallel")),
        name="pack_w_in",
    )(*([w_t] * np_))


def kernel(x, positions, norm1_g, w_in, gla_a2, gla_a_b, gla_norm_g, s5_lambda_re, s5_lambda_im, s5_log_dt, s5_b_re, s5_b_im, s5_c_re, s5_c_im, s5_d, s5_glu_w, s5_glu_b, swa_sinks, w_branch, w_out, norm2_g, w_ffn_gate, w_ffn_up, w_ffn_down, final_norm_g):
    b, l, d = x.shape
    t = b * l
    depth = w_in.shape[0]
    xs = x.reshape(t, d).astype(F32)
    w_all = _pack_w_in(w_in)
    w_br32 = w_branch.reshape(depth, N_BRANCH * BRANCH_WIDTH, d)
    glu_w = s5_glu_w.astype(BF16)
    tabs = _rope_tables(positions)
    gf = final_norm_g.reshape(1, d).astype(F32)
    s5_all = jax.vmap(functools.partial(_s5_params, nch=l // S5_CH))(
        s5_lambda_re, s5_lambda_im, s5_log_dt, s5_b_re, s5_b_im, s5_c_re, s5_c_im, s5_d)
    for i in range(depth):
        g1 = norm1_g[i].reshape(1, d).astype(F32)
        proj, h1 = _inproj(xs, g1, w_all, i)
        proj3 = proj.reshape(b, l, N_SMALL)
        o_gla, w_fg, w_fu, w_br, w_o = _gla(proj, b, gla_a2[i], gla_a_b[i], gla_norm_g[i],
                                            (w_ffn_gate, w_ffn_up, w_br32, w_out), i)
        w_br = w_br.reshape(N_BRANCH, BRANCH_WIDTH, d)
        y_s5 = _s5_core(proj, b, [a[i] for a in s5_all])
        o_dil = _dilated(proj3, tabs).reshape(t, BRANCH_WIDTH)
        o_swa, w_fd = _swa(proj3, tabs, swa_sinks[i], (w_ffn_down,), i)
        o_swa = o_swa.reshape(t, BRANCH_WIDTH)
        xs = _merge(xs, h1, o_gla, y_s5, o_dil, o_swa, glu_w, s5_glu_b[i].reshape(1, -1).astype(F32),
                    w_all, w_br, w_o, i)
        xs = _ffn(xs, norm2_g[i].reshape(1, d).astype(F32), w_fg, w_fu, w_fd, gf, i == depth - 1)
    return xs.reshape(b, l, d).astype(x.dtype)
```

```python
import functools
import math

import jax
import jax.numpy as jnp
import numpy as np
from jax import lax
from jax.experimental import pallas as pl
from jax.experimental.pallas import tpu as pltpu

F32 = jnp.float32
BF16 = jnp.bfloat16

D_MODEL = 2048
N_BRANCH = 4
BRANCH_WIDTH = 512
HEAD_DIM = 64
ATT_BLOCK = 128
ROPE_THETA = 500000.0
ROPE_DIM = HEAD_DIM // 4
NORM_EPS = 1e-6
GLA_HEADS = 4
GLA_DK = 64
GLA_DV = BRANCH_WIDTH // GLA_HEADS
GLA_LOWRANK = 16
GLA_TAU = 16.0
S5_GROUP = 16
S5_GROUPS = BRANCH_WIDTH // S5_GROUP
S5_STATE = 64
DIL_CONFIGS = ((128, 1), (512, 4), (2048, 16))
DIL_SPAN = ATT_BLOCK * 16
SWA_HEADS = BRANCH_WIDTH // HEAD_DIM
SWA_KV_HEADS = 2
SWA_WINDOW = 128

LANES = 128
SUBLANES = 8
VMEM_LIMIT = 56 * 1024 * 1024
VMEM_LIMIT_FFN = 60 * 1024 * 1024

C_GQ, C_GK, C_GV, C_GR, C_S5 = 0, 256, 512, 1024, 1536
C_CQ, C_CK, C_CV, C_SQ, C_SK, C_SV, C_GLR = 2048, 2560, 3072, 3584, 4096, 4224, 4352
N_SMALL = 4608
N_ORIG_SMALL = 4368
GLR_ORIG = 1536

TM_PROJ, TN_PROJ = 512, 1536
TM_MERGE, TN_MERGE = 512, 512
TM_FFN, TF_FFN = 1024, 512
FFN_ROW_CHUNK = 512
FFN_DOWN_CHUNK = 512
GLA_CH = 128
GLA_NCS = 4
GLA_SUB = 8
GLA_LEVELS = (8, 16, 32, 64, 128)
PACK_CB = 256
PACK_PIECES = 5
S5_CH = 16
S5_GPB = LANES // S5_GROUP
S5_RT = 512
S5_TSPLIT = 4
ATT_UNROLL = 8
SWA_UNROLL = 16


def _cparams(sem, vmem_limit=VMEM_LIMIT):
    return pltpu.CompilerParams(dimension_semantics=sem, vmem_limit_bytes=vmem_limit)


def _rms(x, g):
    ms = jnp.mean(x * x, axis=-1, keepdims=True)
    return x * lax.rsqrt(ms + NORM_EPS) * g


def _bdot(a, b):
    return jnp.dot(a, b, preferred_element_type=F32)


def _bdot_nt(a, b):
    return lax.dot_general(a, b, (((1,), (1,)), ((), ())), preferred_element_type=F32)


def _cast_specs(ws, layer, nsteps, step_of):
    ins, outs, shapes = [], [], []
    for w in ws:
        _, r, c = w.shape
        rb = r // nsteps
        assert rb * nsteps == r and rb % (2 * SUBLANES) == 0
        ins.append(pl.BlockSpec((None, rb, c), lambda *g: (layer, step_of(*g), 0)))
        outs.append(pl.BlockSpec((rb, c), lambda *g: (step_of(*g), 0)))
        shapes.append(jax.ShapeDtypeStruct((r, c), BF16))
    return ins, outs, shapes


def _side_cast(in_refs, out_refs):
    for src, dst in zip(in_refs, out_refs):
        dst[...] = src[...].astype(BF16)


def _rope_apply(x, c, sa, sb):
    half = ROPE_DIM // 2
    return x * c + pltpu.roll(x, LANES - half, axis=1) * sa + pltpu.roll(x, half, axis=1) * sb


def _inproj_kernel(x_ref, g_ref, w_ref, o_ref):
    h = _rms(x_ref[...], g_ref[...]).astype(BF16)
    for c0 in range(0, o_ref.shape[1], TN_PROJ):
        cols = slice(c0, c0 + TN_PROJ)
        o_ref[:, cols] = _bdot(h, w_ref[:, cols])


def _inproj(x, g, w_all, layer):
    t, d = x.shape
    n = N_SMALL
    tm = min(TM_PROJ, t)
    assert n % TN_PROJ == 0
    return pl.pallas_call(
        _inproj_kernel,
        grid=(t // tm,),
        in_specs=[
            pl.BlockSpec((tm, d), lambda i: (i, 0)),
            pl.BlockSpec((1, d), lambda i: (0, 0)),
            pl.BlockSpec((None, d, n), lambda i: (layer, 0, 0), pipeline_mode=pl.Buffered(1)),
        ],
        out_specs=pl.BlockSpec((tm, n), lambda i: (i, 0)),
        out_shape=jax.ShapeDtypeStruct((t, n), F32),
        compiler_params=_cparams(("parallel",)),
        name="inproj",
    )(x, g, w_all)


def _rope_kernel(pos_ref, cos_ref, sa_ref, sb_ref):
    pos = pos_ref[0].astype(F32)
    lane = lax.broadcasted_iota(jnp.int32, (SUBLANES, LANES), 1)
    d = lane % HEAD_DIM
    half = ROPE_DIM // 2
    fi = (d % half).astype(F32) / half
    inv = jnp.power(jnp.full((SUBLANES, LANES), ROPE_THETA, F32), -fi)[0:1]
    d1 = d[0:1]
    ang = pos * inv
    c, s = jnp.cos(ang), jnp.sin(ang)
    cos_ref[0] = jnp.where(d1 < ROPE_DIM, c, 1.0)
    sa_ref[0] = jnp.where(d1 < half, -s, 0.0)
    sb_ref[0] = jnp.where((d1 >= half) & (d1 < ROPE_DIM), s, 0.0)


def _rope_tables(positions):
    b, l = positions.shape
    spec = pl.BlockSpec((1, l, LANES), lambda i: (i, 0, 0))
    shp = jax.ShapeDtypeStruct((b, l, LANES), F32)
    return pl.pallas_call(
        _rope_kernel,
        grid=(b,),
        in_specs=[pl.BlockSpec((1, l, 1), lambda i: (i, 0, 0))],
        out_specs=[spec, spec, spec],
        out_shape=[shp, shp, shp],
        compiler_params=_cparams(("parallel",)),
        name="rope_tables",
    )(positions.reshape(b, l, 1))


def _band_bias(max_dist):
    t = ATT_BLOCK
    assert 0 < max_dist <= t
    qi = np.arange(2 * t)[:, None] % t
    kj = np.arange(2 * t)[None, :]
    own = (kj >= t) & (kj - t <= qi)
    prev = (kj < t) & (t + qi - kj <= max_dist)
    ninf = np.float32(-np.inf)
    return np.stack([np.where(own | prev, 0.0, ninf), np.where(own, 0.0, ninf)]).astype(np.float32)


def _band_block(q, kp, ko, vp, vo, bias):
    t = ATT_BLOCK
    lane = lax.broadcasted_iota(jnp.int32, (t, LANES), 1)
    in_a = lane < HEAD_DIM
    q2 = jnp.concatenate([jnp.where(in_a, q, 0.0), jnp.where(in_a, 0.0, q)], axis=0).astype(BF16)
    s = _bdot_nt(q2, jnp.concatenate([kp, ko], axis=0).astype(BF16)) + bias
    m = jnp.max(s, axis=-1, keepdims=True)
    p = jnp.exp(s - m).astype(BF16)
    ones = jnp.ones((2 * t, LANES), BF16)
    vcat = jnp.concatenate([jnp.concatenate([vp, vo], axis=0).astype(BF16), ones], axis=1)
    od = _bdot(p, vcat)
    o = jnp.where(in_a, od[:t, :LANES], od[t:, :LANES])
    df = jnp.where(in_a, od[:t, LANES:], od[t:, LANES:])
    mf = jnp.where(in_a, m[:t], m[t:])
    return mf, df, o


def _dil_kernel(q_ref, k_ref, v_ref, c_ref, sa_ref, sb_ref, bias_ref, o_ref, qs, ks, qd, kd, vd, ms, ds, os_):
    l = q_ref.shape[1]
    c, sa, sb = c_ref[0], sa_ref[0], sb_ref[0]
    qs[...] = _rope_apply(q_ref[0], c, sa, sb) * (HEAD_DIM**-0.5)
    ks[...] = _rope_apply(k_ref[0], c, sa, sb)
    t = ATT_BLOCK
    nblk = DIL_SPAN // t
    dw = DIL_SPAN // t
    per = l // dw
    for r in range(dw):
        src, dst = pl.ds(r, per, stride=dw), slice(r * per, (r + 1) * per)
        qd[dst, :] = qs[src, :]
        kd[dst, :] = ks[src, :]
        vd[dst, :] = v_ref[0, src, :]

    for sblk in range(l // DIL_SPAN):
        base = sblk * DIL_SPAN
        for ci, (_, dil) in enumerate(DIL_CONFIGS):

            def body_wide(r, carry, ci=ci, sblk=sblk):
                rows = pl.ds(pl.multiple_of(r * per + sblk * t, t), t)
                prow = pl.ds(pl.multiple_of(r * per + max(sblk - 1, 0) * t, t), t)
                mf, df, o = _band_block(qd[rows, :], kd[prow, :], kd[rows, :], vd[prow, :], vd[rows, :],
                                        bias_ref[0 if sblk > 0 else 1])
                lrow = pl.ds(r, t, stride=dw)
                ms[ci, lrow, :] = mf
                ds[ci, lrow, :] = df
                os_[ci, lrow, :] = o
                return carry

            if dil == dw:
                lax.fori_loop(0, nblk, body_wide, 0, unroll=ATT_UNROLL)
                continue

            def body(idx, carry, dil=dil, ci=ci, base=base):
                r = idx % dil
                n = idx // dil
                loc = r + dil * t * n
                start = base + loc
                has_prev = start >= dil * t
                pstart = jnp.where(has_prev, start - dil * t, start)
                if dil == 1:
                    start = pl.multiple_of(start, t)
                    pstart = pl.multiple_of(pstart, t)
                    loc = pl.multiple_of(loc, t)
                    rows, prow, lrow = pl.ds(start, t), pl.ds(pstart, t), pl.ds(loc, t)
                else:
                    rows = pl.ds(start, t, stride=dil)
                    prow = pl.ds(pstart, t, stride=dil)
                    lrow = pl.ds(loc, t, stride=dil)
                mf, df, o = _band_block(
                    qs[rows, :], ks[prow, :], ks[rows, :], v_ref[0, prow, :], v_ref[0, rows, :],
                    bias_ref[jnp.where(has_prev, 0, 1)],
                )
                ms[ci, lrow, :] = mf
                ds[ci, lrow, :] = df
                os_[ci, lrow, :] = o
                return carry

            lax.fori_loop(0, nblk, body, 0, unroll=ATT_UNROLL)

        m0, m1, m2 = ms[0], ms[1], ms[2]
        mx = jnp.maximum(jnp.maximum(m0, m1), m2)
        w0, w1, w2 = jnp.exp(m0 - mx), jnp.exp(m1 - mx), jnp.exp(m2 - mx)
        den = w0 * ds[0] + w1 * ds[1] + w2 * ds[2]
        num = w0 * os_[0] + w1 * os_[1] + w2 * os_[2]
        o_ref[0, base : base + DIL_SPAN, :] = (num / den).astype(o_ref.dtype)


def _dilated(proj3, tabs):
    b, l, _ = proj3.shape
    assert l % DIL_SPAN == 0
    assert all(window // dil == ATT_BLOCK for window, dil in DIL_CONFIGS)
    bias = _band_bias(ATT_BLOCK)

    def col(c0):
        return pl.BlockSpec((1, l, LANES), lambda i, p: (i, 0, c0 // LANES + p))

    tab = pl.BlockSpec((1, l, LANES), lambda i, p: (i, 0, 0))
    return pl.pallas_call(
        _dil_kernel,
        grid=(b, BRANCH_WIDTH // LANES),
        in_specs=[col(C_CQ), col(C_CK), col(C_CV), tab, tab, tab, pl.BlockSpec(bias.shape, lambda i, p: (0, 0, 0))],
        out_specs=pl.BlockSpec((1, l, LANES), lambda i, p: (i, 0, p)),
        out_shape=jax.ShapeDtypeStruct((b, l, BRANCH_WIDTH), BF16),
        scratch_shapes=[pltpu.VMEM((l, LANES), F32)] * 5
        + [pltpu.VMEM((len(DIL_CONFIGS), DIL_SPAN, LANES), F32)] * 3,
        compiler_params=_cparams(("parallel", "parallel")),
        name="dilated_attn",
    )(proj3, proj3, proj3, *tabs, jnp.asarray(bias))


def _swa_kernel(*refs, n_cast):
    q_ref, k_ref, v_ref, c_ref, sa_ref, sb_ref, bias_ref, sink_ref = refs[:8]
    o_ref = refs[8 + n_cast]
    qs, ks, vs = refs[-3:]
    _side_cast(refs[8 : 8 + n_cast], refs[9 + n_cast : 9 + 2 * n_cast])
    l = q_ref.shape[1]
    c, sa, sb = c_ref[0], sa_ref[0], sb_ref[0]
    qs[...] = _rope_apply(q_ref[0], c, sa, sb) * (HEAD_DIM**-0.5)
    g = pl.program_id(1) // (SWA_HEADS // SWA_KV_HEADS // 2)
    lane = lax.broadcasted_iota(jnp.int32, (l, LANES), 1)
    keep = (lane // HEAD_DIM) == g
    k, v = _rope_apply(k_ref[0], c, sa, sb), v_ref[0]
    ks[...] = jnp.where(keep, k, pltpu.roll(k, HEAD_DIM, axis=1))
    vs[...] = jnp.where(keep, v, pltpu.roll(v, HEAD_DIM, axis=1))

    sink = sink_ref[0]
    t = ATT_BLOCK

    def body(n, carry):
        start = pl.multiple_of(n * t, t)
        has_prev = n > 0
        pstart = pl.multiple_of(jnp.where(has_prev, start - t, start), t)
        rows, prow = pl.ds(start, t), pl.ds(pstart, t)
        mf, df, o = _band_block(qs[rows, :], ks[prow, :], ks[rows, :], vs[prow, :], vs[rows, :],
                                bias_ref[jnp.where(has_prev, 0, 1)])
        lse = mf + jnp.log(df)
        o_ref[0, rows, :] = (o / df * jax.nn.sigmoid(lse - sink)).astype(o_ref.dtype)
        return carry

    lax.fori_loop(0, l // t, body, 0, unroll=SWA_UNROLL)


def _swa(proj3, tabs, sinks, cast=(), layer=0):
    b, l, _ = proj3.shape
    npair = BRANCH_WIDTH // LANES
    sink_l = jnp.repeat(sinks.astype(F32), HEAD_DIM).reshape(npair, 1, LANES)
    bias = _band_bias(SWA_WINDOW - 1)
    tab = pl.BlockSpec((1, l, LANES), lambda i, p: (i, 0, 0))
    c_in, c_out, c_shape = _cast_specs(cast, layer, b * npair, lambda i, p: i * npair + p)
    return pl.pallas_call(
        functools.partial(_swa_kernel, n_cast=len(cast)),
        grid=(b, npair),
        in_specs=[
            pl.BlockSpec((1, l, LANES), lambda i, p: (i, 0, C_SQ // LANES + p)),
            pl.BlockSpec((1, l, LANES), lambda i, p: (i, 0, C_SK // LANES)),
            pl.BlockSpec((1, l, LANES), lambda i, p: (i, 0, C_SV // LANES)),
            tab,
            tab,
            tab,
            pl.BlockSpec(bias.shape, lambda i, p: (0, 0, 0)),
            pl.BlockSpec((1, 1, LANES), lambda i, p: (p, 0, 0)),
        ]
        + c_in,
        out_specs=[pl.BlockSpec((1, l, LANES), lambda i, p: (i, 0, p))] + c_out,
        out_shape=[jax.ShapeDtypeStruct((b, l, BRANCH_WIDTH), BF16)] + c_shape,
        scratch_shapes=[pltpu.VMEM((l, LANES), F32)] * 3,
        compiler_params=_cparams(("parallel", "parallel")),
        name="swa_attn",
    )(proj3, proj3, proj3, *tabs, jnp.asarray(bias), sink_l, *cast)


@functools.lru_cache(maxsize=None)
def _gla_consts():
    ch = GLA_CH
    r = np.arange(ch)[:, None]
    j = np.arange(ch)[None, :]
    dstack = (j <= r).astype(np.float32)
    lm = []
    for m in GLA_LEVELS[:-1]:
        mk = (((r // m) % 2 == 1) & ((j // m) == (r // m) - 1)).astype(np.float32)
        lm.append(np.tile(mk, (GLA_HEADS, 1)))
    lmask = np.stack(lm)
    dmask = np.tile(((r // GLA_SUB) == (j // GLA_SUB)).astype(np.float32), (1, GLA_HEADS))
    nsub = ch // GLA_SUB
    e = np.zeros((GLA_SUB, GLA_HEADS * GLA_DK, GLA_HEADS * ch), np.float32)
    for u in range(GLA_SUB):
        for h in range(GLA_HEADS):
            for s in range(nsub):
                e[u, h * GLA_DK : (h + 1) * GLA_DK, h * ch + GLA_SUB * s + u] = 1.0
    rr = np.arange(GLA_HEADS * GLA_DV)[:, None]
    cc = np.arange(GLA_HEADS * GLA_DK)[None, :]
    bd = ((rr // GLA_DV) == (cc // GLA_DK)).astype(np.float32)
    return dstack, lmask, dmask, e, bd


def _split3(x):
    hi = x.astype(BF16)
    r1 = x - hi.astype(F32)
    mid = r1.astype(BF16)
    lo = (r1 - mid.astype(F32)).astype(BF16)
    return hi, mid, lo


def _bcast_grp(x, m, u):
    r, w = x.shape
    x3 = x.reshape(r // m, m, w)
    return jnp.broadcast_to(x3[:, u : u + 1, :], x3.shape).reshape(r, w)


def _bcast_sub(x, u):
    return _bcast_grp(x, GLA_SUB, u)


def _gla_kernel(*refs, n_cast):
    (q_ref, k_ref, v_ref, r_ref, glr_ref, a2_ref, ab_ref, ng_ref, dst_ref, lmask_ref, dmask_ref, e_ref,
     bd_ref) = refs[:13]
    o_ref, st_ref = refs[13 + n_cast], refs[-1]
    _side_cast(refs[13 : 13 + n_cast], refs[14 + n_cast : 14 + 2 * n_cast])
    ch = GLA_CH
    rs = q_ref.shape[0]
    ncs = rs // ch
    nlev = len(GLA_LEVELS)
    hk = GLA_HEADS * GLA_DK

    @pl.when(pl.program_id(1) == 0)
    def _():
        st_ref[...] = jnp.zeros_like(st_ref)

    q = q_ref[...] * (GLA_DK**-0.5)
    k = k_ref[...]
    z = _bdot(glr_ref[...].astype(BF16), a2_ref[...]) + ab_ref[...]
    g = (jnp.minimum(z, 0.0) - jnp.log(1.0 + jnp.exp(-jnp.abs(z)))) * (1.0 / GLA_TAU)
    g3 = _split3(g)
    dst = dst_ref[...]

    cums, excl = [], []
    for c in range(ncs):
        rows = slice(c * ch, (c + 1) * ch)
        cums.append(_bdot(dst, g3[0][rows]) + _bdot(dst, g3[1][rows]) + _bdot(dst, g3[2][rows]))
        excl.append(cums[c] - g[rows])

    def eq(c, li):
        m = GLA_LEVELS[li]
        return cums[c] - _bcast_grp(excl[c], m, 0)

    def ek(c, li):
        m = GLA_LEVELS[li]
        return _bcast_grp(cums[c], m, m - 1) - cums[c]

    cs = jnp.concatenate([eq(c, 0) for c in range(ncs)], axis=0)
    tsub = lax.broadcasted_iota(jnp.int32, (rs, hk), 0) % GLA_SUB
    arep = jnp.zeros((rs, GLA_HEADS * ch), F32)
    for u in range(GLA_SUB):
        dec = jnp.exp(jnp.where(tsub >= u, cs - _bcast_sub(cs, u), -jnp.inf))
        p = q * _bcast_sub(k, u) * dec
        arep = arep + _bdot(p.astype(BF16), e_ref[u])
    arep = arep * jnp.concatenate([dmask_ref[...]] * ncs, axis=0)

    lane_k = lax.broadcasted_iota(jnp.int32, (ch, hk), 1) // GLA_DK
    ng = ng_ref[...]
    bd = bd_ref[...]
    for c in range(ncs):
        rows = slice(c * ch, (c + 1) * ch)
        qc, kc = q[rows], k[rows]
        vc = v_ref[rows, :]
        aoff = jnp.zeros((GLA_HEADS * ch, ch), F32)
        for li in range(nlev - 1):
            qe = qc * jnp.exp(eq(c, li))
            ke = (kc * jnp.exp(ek(c, li))).astype(BF16)
            qst = jnp.concatenate([jnp.where(lane_k == h, qe, 0.0) for h in range(GLA_HEADS)], axis=0).astype(BF16)
            aoff = aoff + _bdot_nt(qst, ke) * lmask_ref[li]
        cum = eq(c, nlev - 1)
        st = st_ref[...]
        o_inter = _bdot_nt((qc * jnp.exp(cum)).astype(BF16), st.astype(BF16))
        vb = vc.astype(BF16)
        outs = []
        for h in range(GLA_HEADS):
            a_h = aoff[h * ch : (h + 1) * ch] + arep[rows, h * ch : (h + 1) * ch]
            o_h = _bdot(a_h.astype(BF16), vb[:, h * GLA_DV : (h + 1) * GLA_DV]) + o_inter[:, h * GLA_DV : (h + 1) * GLA_DV]
            ms = jnp.mean(o_h * o_h, axis=-1, keepdims=True)
            outs.append(o_h * lax.rsqrt(ms + NORM_EPS))
        o = jnp.concatenate(outs, axis=1) * ng
        rc = r_ref[rows, :]
        o_ref[rows, :] = (o * (rc * jax.nn.sigmoid(rc))).astype(o_ref.dtype)
        ke_last = (kc * jnp.exp(ek(c, nlev - 1))).astype(BF16)
        kv = _bdot(vc.T.astype(BF16), ke_last)
        st_ref[...] = st * jnp.exp(cum[ch - 1 : ch, :]) + kv * bd


def _gla(proj, b, a2, ab, ng, cast=(), layer=0):
    t = proj.shape[0]
    l = t // b
    rs = GLA_CH * GLA_NCS
    assert l % rs == 0
    ns = l // rs
    dstack, lmask, dmask, e, bd = _gla_consts()
    c_in, c_out, c_shape = _cast_specs(cast, layer, b * ns, lambda i, s: i * ns + s)
    a2p = jnp.zeros((LANES, GLA_HEADS * GLA_DK), BF16).at[:GLA_LOWRANK].set(a2.astype(BF16))

    def rowblk(w, c0):
        return pl.BlockSpec((rs, w), lambda i, s: (i * ns + s, c0 // w))

    def full(shape):
        nd = len(shape)
        return pl.BlockSpec(shape, lambda i, s: (0,) * nd)

    hk, hv = GLA_HEADS * GLA_DK, GLA_HEADS * GLA_DV
    return pl.pallas_call(
        functools.partial(_gla_kernel, n_cast=len(cast)),
        grid=(b, ns),
        in_specs=[
            rowblk(hk, C_GQ),
            rowblk(hk, C_GK),
            rowblk(hv, C_GV),
            rowblk(hv, C_GR),
            rowblk(LANES, C_GLR),
            full((LANES, hk)),
            full((1, hk)),
            full((1, hv)),
            full(dstack.shape),
            full(lmask.shape),
            full(dmask.shape),
            full(e.shape),
            full(bd.shape),
        ]
        + c_in,
        out_specs=[pl.BlockSpec((rs, hv), lambda i, s: (i * ns + s, 0))] + c_out,
        out_shape=[jax.ShapeDtypeStruct((t, hv), BF16)] + c_shape,
        scratch_shapes=[pltpu.VMEM((hv, hk), F32)],
        compiler_params=_cparams(("parallel", "arbitrary")),
        name="gla",
    )(
        proj, proj, proj, proj, proj, a2p, ab.reshape(1, hk).astype(F32), ng.reshape(1, hv).astype(F32),
        jnp.asarray(dstack, BF16), jnp.asarray(lmask), jnp.asarray(dmask), jnp.asarray(e, BF16), jnp.asarray(bd),
        *cast,
    )


def _s5_params(lam_re, lam_im, log_dt, b_re, b_im, c_re, c_im, d, nch):
    f = F32
    cs = S5_CH
    dt = jnp.exp(log_dt.astype(f))[:, None]
    lr, li = lam_re.astype(f), lam_im.astype(f)
    mag = jnp.exp(lr * dt)
    ab_re, ab_im = mag * jnp.cos(li * dt), mag * jnp.sin(li * dt)
    den = lr * lr + li * li
    z_re = ((ab_re - 1.0) * lr + ab_im * li) / den
    z_im = (ab_im * lr - (ab_re - 1.0) * li) / den
    br, bi = b_re.astype(f), b_im.astype(f)
    bb_re = z_re[..., None] * br - z_im[..., None] * bi
    bb_im = z_re[..., None] * bi + z_im[..., None] * br

    def apow(p):
        p = jnp.asarray(p, f)[:, None, None]
        m = jnp.exp(p * (lr * dt))
        return m * jnp.cos(p * (li * dt)), m * jnp.sin(p * (li * dt))

    cr, ci = c_re.astype(f), c_im.astype(f)
    p_re, p_im = apow(np.arange(cs + 1))
    nb, gpb = S5_GROUPS // S5_GPB, S5_GPB
    hw = gpb * S5_STATE
    eye = jnp.eye(gpb, dtype=f)

    def b_blockdiag(x):
        x = x.reshape(nb, gpb, S5_STATE, S5_GROUP).transpose(0, 1, 3, 2)
        x = (x[:, :, :, None, :] * eye[None, :, None, :, None]).reshape(nb, LANES, hw)
        return jnp.concatenate([x, x], axis=-1)

    def c_blockdiag(x):
        x = x.reshape(nb, gpb, S5_GROUP, S5_STATE).transpose(0, 1, 3, 2)
        return (x[:, :, :, None, :] * eye[None, :, None, :, None]).reshape(nb, hw, LANES)

    def lanes(a, b_):
        p = a.shape[0]
        return jnp.concatenate([a.reshape(p, nb, hw), b_.reshape(p, nb, hw)], axis=-1).transpose(1, 0, 2)

    rev = np.arange(cs - 1, -1, -1)
    pr, pi = p_re[rev], p_im[rev]
    pa, pb = lanes(pr, pi), lanes(-pi, pr)
    acol_re = jnp.broadcast_to(ab_re.reshape(nb, hw, 1), (nb, hw, LANES))
    acol_im = jnp.broadcast_to(ab_im.reshape(nb, hw, 1), (nb, hw, LANES))
    nstep = max(1, int(math.log2(nch)))
    s_re, s_im = apow(cs * (2 ** np.arange(nstep)))
    ar = s_re.reshape(nstep, nb, hw).transpose(1, 0, 2)
    ai = s_im.reshape(nstep, nb, hw).transpose(1, 0, 2)
    dd = d.astype(f).reshape(nb, 1, LANES)
    return (b_blockdiag(bb_re), b_blockdiag(bb_im), c_blockdiag(cr), c_blockdiag(ci), acol_re, acol_im,
            pa, pb, ar, ai, dd)


def _s5_kernel(u_ref, br_ref, bi_ref, cr_ref, ci_ref, acr_ref, aci_ref, pa_ref, pb_ref, ar_ref, ai_ref, d_ref,
               y_ref, mb_s, mc_s, kc_s, *, nchb):
    cs = S5_CH
    t = u_ref.shape[0]
    nch = t // cs
    hw = S5_GPB * S5_STATE
    br, bi = br_ref[0], bi_ref[0]
    pa, pb = pa_ref[0], pb_ref[0]
    for j in range(cs):
        mb_s[j * LANES : (j + 1) * LANES, :] = (br * pa[j : j + 1] + bi * pb[j : j + 1]).astype(BF16)
    x_re, x_im = cr_ref[0], ci_ref[0]
    kc = _bdot(mb_s[...], jnp.concatenate([x_re, -x_im], axis=0).astype(BF16)).astype(BF16)
    a_re, a_im = acr_ref[0], aci_ref[0]
    for tt in range(cs):
        x_re, x_im = x_re * a_re - x_im * a_im, x_re * a_im + x_im * a_re
        mc_s[:hw, tt * LANES : (tt + 1) * LANES] = x_re.astype(BF16)
        mc_s[hw:, tt * LANES : (tt + 1) * LANES] = (-x_im).astype(BF16)
    gl = cs // S5_TSPLIT * LANES
    for qq in range(S5_TSPLIT):
        kc_s[:, qq * LANES : (qq + 1) * LANES] = kc[(S5_TSPLIT - 1 - qq) * gl : (S5_TSPLIT - qq) * gl]
    ucat = jnp.concatenate([u_ref[pl.ds(j, nch, stride=cs), :] for j in range(cs)], axis=1).astype(BF16)
    e_all = _bdot(ucat, mb_s[...])
    cidx = lax.broadcasted_iota(jnp.int32, (nch, LANES), 0) % nchb
    ar, ai = ar_ref[0], ai_ref[0]
    hp_re, hp_im = [], []
    for g in range(hw // LANES):
        lo = g * LANES
        e_re, e_im = e_all[:, lo : lo + LANES], e_all[:, hw + lo : hw + lo + LANES]
        for kstep in range(ar.shape[0]):
            s = 2**kstep
            if s >= nchb:
                break
            s_re = jnp.where(cidx >= s, pltpu.roll(e_re, s, axis=0), 0.0)
            s_im = jnp.where(cidx >= s, pltpu.roll(e_im, s, axis=0), 0.0)
            k_re, k_im = ar[kstep : kstep + 1, lo : lo + LANES], ai[kstep : kstep + 1, lo : lo + LANES]
            e_re, e_im = e_re + s_re * k_re - s_im * k_im, e_im + s_re * k_im + s_im * k_re
        hp_re.append(jnp.where(cidx >= 1, pltpu.roll(e_re, 1, axis=0), 0.0).astype(BF16))
        hp_im.append(jnp.where(cidx >= 1, pltpu.roll(e_im, 1, axis=0), 0.0).astype(BF16))
    ycar = _bdot(jnp.concatenate(hp_re + hp_im, axis=1), mc_s[...])
    for tt in range(cs):
        y_ref[pl.ds(tt, nch, stride=cs), :] = ycar[:, tt * LANES : (tt + 1) * LANES]

    rt = min(S5_RT, t)
    rmod = lax.broadcasted_iota(jnp.int32, (rt, LANES), 0) % cs
    dvec = d_ref[0]
    glag = cs // S5_TSPLIT

    def tile(i, carry):
        rows = pl.ds(pl.multiple_of(i * rt, rt), rt)
        ut = u_ref[rows, :]
        ush = [jnp.where(rmod >= tau, pltpu.roll(ut, tau, axis=0), 0.0) for tau in range(glag - 1, 0, -1)] + [ut]
        zz = _bdot(jnp.concatenate(ush, axis=1).astype(BF16), kc_s[...])
        acc = y_ref[rows, :] + dvec * ut + zz[:, :LANES]
        for qq in range(1, S5_TSPLIT):
            part = pltpu.roll(zz[:, qq * LANES : (qq + 1) * LANES], qq * glag, axis=0)
            acc = acc + jnp.where(rmod >= qq * glag, part, 0.0)
        y_ref[rows, :] = acc
        return carry

    lax.fori_loop(0, t // rt, tile, 0)


def _s5_core(proj, b, params):
    t = proj.shape[0]
    nchb = t // b // S5_CH
    sw = S5_GPB * 2 * S5_STATE

    def bspec(a):
        return pl.BlockSpec((1,) + a.shape[1:], lambda p: (p, 0, 0))

    return pl.pallas_call(
        functools.partial(_s5_kernel, nchb=nchb),
        grid=(S5_GROUPS // S5_GPB,),
        in_specs=[pl.BlockSpec((t, LANES), lambda p: (0, C_S5 // LANES + p))] + [bspec(a) for a in params],
        out_specs=pl.BlockSpec((t, LANES), lambda p: (0, p)),
        out_shape=jax.ShapeDtypeStruct((t, BRANCH_WIDTH), F32),
        scratch_shapes=[
            pltpu.VMEM((S5_CH * LANES, sw), BF16),
            pltpu.VMEM((sw, S5_CH * LANES), BF16),
            pltpu.VMEM((S5_CH // S5_TSPLIT * LANES, S5_TSPLIT * LANES), BF16),
        ],
        compiler_params=_cparams(("parallel",)),
        name="s5_core",
    )(proj, *params)


def _merge_kernel(
    x_ref, g1_ref, gla_ref, ys5_ref, dil_ref, swa_ref, gluw_ref, glub_ref, wg0, wg1, wg2, wg3, wb_ref, wo_ref,
    o_ref, h_s, s5_s,
):
    j = pl.program_id(1)

    @pl.when(j == 0)
    def _():
        h_s[...] = _rms(x_ref[...], g1_ref[...]).astype(BF16)
        zz = jax.nn.gelu(ys5_ref[...])
        gate = jax.nn.sigmoid(_bdot(zz.astype(BF16), gluw_ref[...]) + glub_ref[...])
        s5_s[...] = (zz * gate).astype(BF16)
        o_ref[...] = jnp.zeros_like(o_ref)

    h = h_s[...]
    branches = (gla_ref[...], s5_s[...], dil_ref[...], swa_ref[...])
    mixed = None
    for m, (wg, br) in enumerate(zip((wg0, wg1, wg2, wg3), branches)):
        term = jax.nn.sigmoid(_bdot(h, wg[...])) * _bdot(br, wb_ref[m])
        mixed = term if mixed is None else mixed + term
    mixed = mixed.astype(BF16)
    d = o_ref.shape[1]
    for c0 in range(0, d, FFN_DOWN_CHUNK):
        cols = slice(c0, c0 + FFN_DOWN_CHUNK)
        o_ref[:, cols] += _bdot(mixed, wo_ref[:, cols])

    @pl.when(j == pl.num_programs(1) - 1)
    def _():
        o_ref[...] = x_ref[...] + o_ref[...]


def _merge(x, g1, o_gla, y_s5, o_dil, o_swa, gluw, glub, w_all, wb, wo, layer):
    t, d = x.shape
    tm, tn = min(TM_MERGE, t), TN_MERGE
    nj = d // tn
    bw = BRANCH_WIDTH

    def rowblk(w):
        return pl.BlockSpec((tm, w), lambda i, j: (i, 0))

    def gate_spec(m):
        return pl.BlockSpec((None, d, tn), lambda i, j, m=m: (layer, 0, N_SMALL // tn + m * nj + j))

    return pl.pallas_call(
        _merge_kernel,
        grid=(t // tm, nj),
        in_specs=[
            rowblk(d),
            pl.BlockSpec((1, d), lambda i, j: (0, 0)),
            rowblk(bw),
            rowblk(bw),
            rowblk(bw),
            rowblk(bw),
            pl.BlockSpec((None, bw, bw), lambda i, j: (layer, 0, 0)),
            pl.BlockSpec((1, bw), lambda i, j: (0, 0)),
            gate_spec(0),
            gate_spec(1),
            gate_spec(2),
            gate_spec(3),
            pl.BlockSpec((N_BRANCH, bw, tn), lambda i, j: (0, 0, j)),
            pl.BlockSpec((tn, d), lambda i, j: (j, 0)),
        ],
        out_specs=rowblk(d),
        out_shape=jax.ShapeDtypeStruct((t, d), F32),
        scratch_shapes=[pltpu.VMEM((tm, d), BF16), pltpu.VMEM((tm, bw), BF16)],
        compiler_params=_cparams(("parallel", "arbitrary")),
        name="merge",
    )(x, g1, o_gla, y_s5, o_dil, o_swa, gluw, glub, w_all, w_all, w_all, w_all, wb, wo)


def _ffn_kernel(x_ref, g2_ref, wg_ref, wu_ref, wd_ref, gf_ref, o_ref, h_s, *, final_norm):
    j = pl.program_id(1)
    tm, d = o_ref.shape
    chunks = [slice(r0, r0 + FFN_ROW_CHUNK) for r0 in range(0, tm, FFN_ROW_CHUNK)]

    @pl.when(j == 0)
    def _():
        for rows in chunks:
            h_s[rows, :] = _rms(x_ref[rows, :], g2_ref[...]).astype(BF16)
        o_ref[...] = jnp.zeros_like(o_ref)

    for rows in chunks:
        h = h_s[rows, :]
        gate = _bdot(h, wg_ref[...])
        act = ((gate * jax.nn.sigmoid(gate)) * _bdot(h, wu_ref[...])).astype(BF16)
        for c0 in range(0, d, FFN_DOWN_CHUNK):
            cols = slice(c0, c0 + FFN_DOWN_CHUNK)
            o_ref[rows, cols] += _bdot(act, wd_ref[:, cols])

    @pl.when(j == pl.num_programs(1) - 1)
    def _():
        for rows in chunks:
            y = x_ref[rows, :] + o_ref[rows, :]
            o_ref[rows, :] = _rms(y, gf_ref[...]) if final_norm else y


def _ffn(x, g2, wg, wu, wd, gf, final_norm):
    t, d = x.shape
    fh = wg.shape[-1]
    tm, tf = min(TM_FFN, t), TF_FFN
    return pl.pallas_call(
        functools.partial(_ffn_kernel, final_norm=final_norm),
        grid=(t // tm, fh // tf),
        in_specs=[
            pl.BlockSpec((tm, d), lambda i, j: (i, 0)),
            pl.BlockSpec((1, d), lambda i, j: (0, 0)),
            pl.BlockSpec((d, tf), lambda i, j: (0, j)),
            pl.BlockSpec((d, tf), lambda i, j: (0, j)),
            pl.BlockSpec((tf, d), lambda i, j: (j, 0)),
            pl.BlockSpec((1, d), lambda i, j: (0, 0)),
        ],
        out_specs=pl.BlockSpec((tm, d), lambda i, j: (i, 0)),
        out_shape=jax.ShapeDtypeStruct((t, d), F32),
        scratch_shapes=[pltpu.VMEM((tm, d), BF16)],
        compiler_params=_cparams(("parallel", "arbitrary"), VMEM_LIMIT_FFN),
        name="ffn",
    )(x, g2, wg, wu, wd, gf)


def _pack_kernel(*refs):
    o_ref = refs[-1]
    for piece, w_ref in enumerate(refs[:-1]):
        x = w_ref[...]
        row = lax.broadcasted_iota(jnp.int32, x.shape, 0)
        blk = pl.program_id(1) * PACK_PIECES + piece
        keep = jnp.where(blk == C_GLR // PACK_CB, GLA_LOWRANK, PACK_CB)
        o_ref[:, piece * PACK_CB : (piece + 1) * PACK_CB] = jnp.where(row < keep, x, 0.0).T.astype(BF16)


def _pack_row0(c):
    cb = PACK_CB
    u = GLA_LOWRANK
    front = jnp.where(c < GLR_ORIG // cb, c * (cb // u), c * (cb // u) + 1)
    back = jnp.where(c == C_GLR // cb, GLR_ORIG // u, c * (cb // u) - (N_SMALL - N_ORIG_SMALL) // u)
    return jnp.where(c < C_GLR // cb, front, back) * u


def _pack_w_in(w_in):
    depth, d, d_in = w_in.shape
    n_out = N_SMALL + d_in - N_ORIG_SMALL
    cb = PACK_CB
    assert GLR_ORIG % cb == 0 and C_GLR % cb == 0 and N_SMALL % cb == 0 and n_out % cb == 0
    assert C_GLR == N_ORIG_SMALL - GLA_LOWRANK and C_GLR + cb == N_SMALL
    np_ = PACK_PIECES
    assert n_out % (cb * np_) == 0
    w_t = jnp.swapaxes(w_in, 1, 2)
    return pl.pallas_call(
        _pack_kernel,
        grid=(depth, n_out // (cb * np_)),
        in_specs=[
            pl.BlockSpec((None, pl.Element(cb), pl.Element(d)), lambda i, c, p=p: (i, _pack_row0(c * np_ + p), 0))
            for p in range(np_)
        ],
        out_specs=pl.BlockSpec((None, d, cb * np_), lambda i, c: (i, 0, c)),
        out_shape=jax.ShapeDtypeStruct((depth, d, n_out), BF16),
        compiler_params=_cparams(("parallel", "parallel")),
        name="pack_w_in",
    )(*([w_t] * np_))


def kernel(x, positions, norm1_g, w_in, gla_a2, gla_a_b, gla_norm_g, s5_lambda_re, s5_lambda_im, s5_log_dt, s5_b_re, s5_b_im, s5_c_re, s5_c_im, s5_d, s5_glu_w, s5_glu_b, swa_sinks, w_branch, w_out, norm2_g, w_ffn_gate, w_ffn_up, w_ffn_down, final_norm_g):
    b, l, d = x.shape
    t = b * l
    depth = w_in.shape[0]
    xs = x.reshape(t, d).astype(F32)
    w_all = _pack_w_in(w_in)
    w_br32 = w_branch.reshape(depth, N_BRANCH * BRANCH_WIDTH, d)
    glu_w = s5_glu_w.astype(BF16)
    tabs = _rope_tables(positions)
    gf = final_norm_g.reshape(1, d).astype(F32)
    s5_all = jax.vmap(functools.partial(_s5_params, nch=l // S5_CH))(
        s5_lambda_re, s5_lambda_im, s5_log_dt, s5_b_re, s5_b_im, s5_c_re, s5_c_im, s5_d)
    for i in range(depth):
        g1 = norm1_g[i].reshape(1, d).astype(F32)
        proj = _inproj(xs, g1, w_all, i)
        proj3 = proj.reshape(b, l, N_SMALL)
        o_gla, w_fg, w_fu, w_br, w_o = _gla(proj, b, gla_a2[i], gla_a_b[i], gla_norm_g[i],
                                            (w_ffn_gate, w_ffn_up, w_br32, w_out), i)
        w_br = w_br.reshape(N_BRANCH, BRANCH_WIDTH, d)
        y_s5 = _s5_core(proj, b, [a[i] for a in s5_all])
        o_dil = _dilated(proj3, tabs).reshape(t, BRANCH_WIDTH)
        o_swa, w_fd = _swa(proj3, tabs, swa_sinks[i], (w_ffn_down,), i)
        o_swa = o_swa.reshape(t, BRANCH_WIDTH)
        xs = _merge(xs, g1, o_gla, y_s5, o_dil, o_swa, glu_w, s5_glu_b[i].reshape(1, -1).astype(F32),
                    w_all, w_br, w_o, i)
        xs = _ffn(xs, norm2_g[i].reshape(1, d).astype(F32), w_fg, w_fu, w_fd, gf, i == depth - 1)
    return xs.reshape(b, l, d).astype(x.dtype)
```

```python
import functools
import math

import jax
import jax.numpy as jnp
import numpy as np
from jax import lax
from jax.experimental import pallas as pl
from jax.experimental.pallas import tpu as pltpu

F32 = jnp.float32
BF16 = jnp.bfloat16

D_MODEL = 2048
N_BRANCH = 4
BRANCH_WIDTH = 512
HEAD_DIM = 64
ATT_BLOCK = 128
ROPE_THETA = 500000.0
ROPE_DIM = HEAD_DIM // 4
NORM_EPS = 1e-6
GLA_HEADS = 4
GLA_DK = 64
GLA_DV = BRANCH_WIDTH // GLA_HEADS
GLA_LOWRANK = 16
GLA_TAU = 16.0
S5_GROUP = 16
S5_GROUPS = BRANCH_WIDTH // S5_GROUP
S5_STATE = 64
DIL_CONFIGS = ((128, 1), (512, 4), (2048, 16))
DIL_SPAN = ATT_BLOCK * 16
SWA_HEADS = BRANCH_WIDTH // HEAD_DIM
SWA_KV_HEADS = 2
SWA_WINDOW = 128

LANES = 128
SUBLANES = 8
VMEM_LIMIT = 56 * 1024 * 1024
VMEM_LIMIT_FFN = 60 * 1024 * 1024

C_GQ, C_GK, C_GV, C_GR, C_S5 = 0, 256, 512, 1024, 1536
C_CQ, C_CK, C_CV, C_SQ, C_SK, C_SV, C_GLR = 2048, 2560, 3072, 3584, 4096, 4224, 4352
N_SMALL = 4608
N_ORIG_SMALL = 4368
GLR_ORIG = 1536

TM_PROJ, TN_PROJ = 512, 1536
TM_MERGE, TN_MERGE = 512, 512
TM_FFN, TF_FFN = 1024, 512
FFN_ROW_CHUNK = 512
FFN_DOWN_CHUNK = 512
GLA_CH = 128
GLA_NCS = 4
GLA_SUB = 8
GLA_LEVELS = (8, 16, 32, 64, 128)
PACK_CB = 256
PACK_PIECES = 5
S5_CH = 16
S5_GPB = LANES // S5_GROUP
S5_RT = 512
S5_TSPLIT = 4
ATT_UNROLL = 8
SWA_UNROLL = 16


def _cparams(sem, vmem_limit=VMEM_LIMIT):
    return pltpu.CompilerParams(dimension_semantics=sem, vmem_limit_bytes=vmem_limit)


def _rms(x, g):
    ms = jnp.mean(x * x, axis=-1, keepdims=True)
    return x * lax.rsqrt(ms + NORM_EPS) * g


def _bdot(a, b):
    return jnp.dot(a, b, preferred_element_type=F32)


def _bdot_nt(a, b):
    return lax.dot_general(a, b, (((1,), (1,)), ((), ())), preferred_element_type=F32)


def _cast_specs(ws, layer, nsteps, step_of):
    ins, outs, shapes = [], [], []
    for w in ws:
        _, r, c = w.shape
        rb = r // nsteps
        assert rb * nsteps == r and rb % (2 * SUBLANES) == 0
        ins.append(pl.BlockSpec((None, rb, c), lambda *g: (layer, step_of(*g), 0)))
        outs.append(pl.BlockSpec((rb, c), lambda *g: (step_of(*g), 0)))
        shapes.append(jax.ShapeDtypeStruct((r, c), BF16))
    return ins, outs, shapes


def _side_cast(in_refs, out_refs):
    for src, dst in zip(in_refs, out_refs):
        dst[...] = src[...].astype(BF16)


def _rope_apply(x, c, sa, sb):
    half = ROPE_DIM // 2
    return x * c + pltpu.roll(x, LANES - half, axis=1) * sa + pltpu.roll(x, half, axis=1) * sb


def _inproj_kernel(x_ref, g_ref, w_ref, o_ref):
    h = _rms(x_ref[...], g_ref[...]).astype(BF16)
    for c0 in range(0, o_ref.shape[1], TN_PROJ):
        cols = slice(c0, c0 + TN_PROJ)
        o_ref[:, cols] = _bdot(h, w_ref[:, cols])


def _inproj(x, g, w_all, layer):
    t, d = x.shape
    n = N_SMALL
    tm = min(TM_PROJ, t)
    assert n % TN_PROJ == 0
    return pl.pallas_call(
        _inproj_kernel,
        grid=(t // tm,),
        in_specs=[
            pl.BlockSpec((tm, d), lambda i: (i, 0)),
            pl.BlockSpec((1, d), lambda i: (0, 0)),
            pl.BlockSpec((None, d, n), lambda i: (layer, 0, 0), pipeline_mode=pl.Buffered(1)),
        ],
        out_specs=pl.BlockSpec((tm, n), lambda i: (i, 0)),
        out_shape=jax.ShapeDtypeStruct((t, n), F32),
        compiler_params=_cparams(("parallel",)),
        name="inproj",
    )(x, g, w_all)


def _rope_kernel(pos_ref, cos_ref, sa_ref, sb_ref):
    pos = pos_ref[0].astype(F32)
    lane = lax.broadcasted_iota(jnp.int32, (SUBLANES, LANES), 1)
    d = lane % HEAD_DIM
    half = ROPE_DIM // 2
    fi = (d % half).astype(F32) / half
    inv = jnp.power(jnp.full((SUBLANES, LANES), ROPE_THETA, F32), -fi)[0:1]
    d1 = d[0:1]
    ang = pos * inv
    c, s = jnp.cos(ang), jnp.sin(ang)
    cos_ref[0] = jnp.where(d1 < ROPE_DIM, c, 1.0)
    sa_ref[0] = jnp.where(d1 < half, -s, 0.0)
    sb_ref[0] = jnp.where((d1 >= half) & (d1 < ROPE_DIM), s, 0.0)


def _rope_tables(positions):
    b, l = positions.shape
    spec = pl.BlockSpec((1, l, LANES), lambda i: (i, 0, 0))
    shp = jax.ShapeDtypeStruct((b, l, LANES), F32)
    return pl.pallas_call(
        _rope_kernel,
        grid=(b,),
        in_specs=[pl.BlockSpec((1, l, 1), lambda i: (i, 0, 0))],
        out_specs=[spec, spec, spec],
        out_shape=[shp, shp, shp],
        compiler_params=_cparams(("parallel",)),
        name="rope_tables",
    )(positions.reshape(b, l, 1))


def _band_bias(max_dist):
    t = ATT_BLOCK
    assert 0 < max_dist <= t
    qi = np.arange(2 * t)[:, None] % t
    kj = np.arange(2 * t)[None, :]
    own = (kj >= t) & (kj - t <= qi)
    prev = (kj < t) & (t + qi - kj <= max_dist)
    ninf = np.float32(-np.inf)
    return np.stack([np.where(own | prev, 0.0, ninf), np.where(own, 0.0, ninf)]).astype(np.float32)


def _band_block(q, kp, ko, vp, vo, bias):
    t = ATT_BLOCK
    lane = lax.broadcasted_iota(jnp.int32, (t, LANES), 1)
    in_a = lane < HEAD_DIM
    q2 = jnp.concatenate([jnp.where(in_a, q, 0.0), jnp.where(in_a, 0.0, q)], axis=0).astype(BF16)
    s = _bdot_nt(q2, jnp.concatenate([kp, ko], axis=0).astype(BF16)) + bias
    m = jnp.max(s, axis=-1, keepdims=True)
    p = jnp.exp(s - m).astype(BF16)
    ones = jnp.ones((2 * t, LANES), BF16)
    vcat = jnp.concatenate([jnp.concatenate([vp, vo], axis=0).astype(BF16), ones], axis=1)
    od = _bdot(p, vcat)
    o = jnp.where(in_a, od[:t, :LANES], od[t:, :LANES])
    df = jnp.where(in_a, od[:t, LANES:], od[t:, LANES:])
    mf = jnp.where(in_a, m[:t], m[t:])
    return mf, df, o


def _dil_kernel(q_ref, k_ref, v_ref, c_ref, sa_ref, sb_ref, bias_ref, o_ref, qs, ks, qd, kd, vd, ms, ds, os_):
    l = q_ref.shape[1]
    c, sa, sb = c_ref[0], sa_ref[0], sb_ref[0]
    qs[...] = _rope_apply(q_ref[0], c, sa, sb) * (HEAD_DIM**-0.5)
    ks[...] = _rope_apply(k_ref[0], c, sa, sb)
    t = ATT_BLOCK
    nblk = DIL_SPAN // t
    dw = DIL_SPAN // t
    per = l // dw
    for r in range(dw):
        src, dst = pl.ds(r, per, stride=dw), slice(r * per, (r + 1) * per)
        qd[dst, :] = qs[src, :]
        kd[dst, :] = ks[src, :]
        vd[dst, :] = v_ref[0, src, :]

    for sblk in range(l // DIL_SPAN):
        base = sblk * DIL_SPAN
        for ci, (_, dil) in enumerate(DIL_CONFIGS):

            def body_wide(r, carry, ci=ci, sblk=sblk):
                rows = pl.ds(pl.multiple_of(r * per + sblk * t, t), t)
                prow = pl.ds(pl.multiple_of(r * per + max(sblk - 1, 0) * t, t), t)
                mf, df, o = _band_block(qd[rows, :], kd[prow, :], kd[rows, :], vd[prow, :], vd[rows, :],
                                        bias_ref[0 if sblk > 0 else 1])
                lrow = pl.ds(r, t, stride=dw)
                ms[ci, lrow, :] = mf
                ds[ci, lrow, :] = df
                os_[ci, lrow, :] = o
                return carry

            if dil == dw:
                lax.fori_loop(0, nblk, body_wide, 0, unroll=ATT_UNROLL)
                continue

            def body(idx, carry, dil=dil, ci=ci, base=base):
                r = idx % dil
                n = idx // dil
                loc = r + dil * t * n
                start = base + loc
                has_prev = start >= dil * t
                pstart = jnp.where(has_prev, start - dil * t, start)
                if dil == 1:
                    start = pl.multiple_of(start, t)
                    pstart = pl.multiple_of(pstart, t)
                    loc = pl.multiple_of(loc, t)
                    rows, prow, lrow = pl.ds(start, t), pl.ds(pstart, t), pl.ds(loc, t)
                else:
                    rows = pl.ds(start, t, stride=dil)
                    prow = pl.ds(pstart, t, stride=dil)
                    lrow = pl.ds(loc, t, stride=dil)
                mf, df, o = _band_block(
                    qs[rows, :], ks[prow, :], ks[rows, :], v_ref[0, prow, :], v_ref[0, rows, :],
                    bias_ref[jnp.where(has_prev, 0, 1)],
                )
                ms[ci, lrow, :] = mf
                ds[ci, lrow, :] = df
                os_[ci, lrow, :] = o
                return carry

            lax.fori_loop(0, nblk, body, 0, unroll=ATT_UNROLL)

        m0, m1, m2 = ms[0], ms[1], ms[2]
        mx = jnp.maximum(jnp.maximum(m0, m1), m2)
        w0, w1, w2 = jnp.exp(m0 - mx), jnp.exp(m1 - mx), jnp.exp(m2 - mx)
        den = w0 * ds[0] + w1 * ds[1] + w2 * ds[2]
        num = w0 * os_[0] + w1 * os_[1] + w2 * os_[2]
        o_ref[0, base : base + DIL_SPAN, :] = (num / den).astype(o_ref.dtype)


def _dilated(proj3, tabs):
    b, l, _ = proj3.shape
    assert l % DIL_SPAN == 0
    assert all(window // dil == ATT_BLOCK for window, dil in DIL_CONFIGS)
    bias = _band_bias(ATT_BLOCK)

    def col(c0):
        return pl.BlockSpec((1, l, LANES), lambda i, p: (i, 0, c0 // LANES + p))

    tab = pl.BlockSpec((1, l, LANES), lambda i, p: (i, 0, 0))
    return pl.pallas_call(
        _dil_kernel,
        grid=(b, BRANCH_WIDTH // LANES),
        in_specs=[col(C_CQ), col(C_CK), col(C_CV), tab, tab, tab, pl.BlockSpec(bias.shape, lambda i, p: (0, 0, 0))],
        out_specs=pl.BlockSpec((1, l, LANES), lambda i, p: (i, 0, p)),
        out_shape=jax.ShapeDtypeStruct((b, l, BRANCH_WIDTH), BF16),
        scratch_shapes=[pltpu.VMEM((l, LANES), F32)] * 5
        + [pltpu.VMEM((len(DIL_CONFIGS), DIL_SPAN, LANES), F32)] * 3,
        compiler_params=_cparams(("parallel", "parallel")),
        name="dilated_attn",
    )(proj3, proj3, proj3, *tabs, jnp.asarray(bias))


def _swa_kv_kernel(k_ref, v_ref, c_ref, sa_ref, sb_ref, kd_ref, vd_ref):
    k, v = _rope_apply(k_ref[0], c_ref[0], sa_ref[0], sb_ref[0]), v_ref[0]
    kr, vr = pltpu.roll(k, HEAD_DIM, axis=1), pltpu.roll(v, HEAD_DIM, axis=1)
    lane = lax.broadcasted_iota(jnp.int32, k.shape, 1)
    for g in range(SWA_KV_HEADS):
        keep = (lane // HEAD_DIM) == g
        kd_ref[0, g] = jnp.where(keep, k, kr)
        vd_ref[0, g] = jnp.where(keep, v, vr)


def _swa_kv(proj3, tabs):
    b, l, _ = proj3.shape
    assert SWA_KV_HEADS * HEAD_DIM == LANES
    tab = pl.BlockSpec((1, l, LANES), lambda i: (i, 0, 0))
    out = pl.BlockSpec((1, SWA_KV_HEADS, l, LANES), lambda i: (i, 0, 0, 0))
    shp = jax.ShapeDtypeStruct((b, SWA_KV_HEADS, l, LANES), F32)
    return pl.pallas_call(
        _swa_kv_kernel,
        grid=(b,),
        in_specs=[
            pl.BlockSpec((1, l, LANES), lambda i: (i, 0, C_SK // LANES)),
            pl.BlockSpec((1, l, LANES), lambda i: (i, 0, C_SV // LANES)),
            tab,
            tab,
            tab,
        ],
        out_specs=[out, out],
        out_shape=[shp, shp],
        compiler_params=_cparams(("parallel",)),
        name="swa_kv",
    )(proj3, proj3, *tabs)


def _swa_kernel(*refs, n_cast):
    q_ref, k_ref, v_ref, c_ref, sa_ref, sb_ref, bias_ref, sink_ref = refs[:8]
    o_ref = refs[8 + n_cast]
    qs = refs[-1]
    _side_cast(refs[8 : 8 + n_cast], refs[9 + n_cast : 9 + 2 * n_cast])
    l = q_ref.shape[1]
    qs[...] = _rope_apply(q_ref[0], c_ref[0], sa_ref[0], sb_ref[0]) * (HEAD_DIM**-0.5)
    sink = sink_ref[0]
    t = ATT_BLOCK

    def body(n, carry):
        start = pl.multiple_of(n * t, t)
        has_prev = n > 0
        pstart = pl.multiple_of(jnp.where(has_prev, start - t, start), t)
        rows, prow = pl.ds(start, t), pl.ds(pstart, t)
        mf, df, o = _band_block(qs[rows, :], k_ref[prow, :], k_ref[rows, :], v_ref[prow, :], v_ref[rows, :],
                                bias_ref[jnp.where(has_prev, 0, 1)])
        lse = mf + jnp.log(df)
        o_ref[0, rows, :] = (o / df * jax.nn.sigmoid(lse - sink)).astype(o_ref.dtype)
        return carry

    lax.fori_loop(0, l // t, body, 0, unroll=SWA_UNROLL)


def _swa(proj3, tabs, sinks, cast=(), layer=0):
    b, l, _ = proj3.shape
    npair = BRANCH_WIDTH // LANES
    sink_l = jnp.repeat(sinks.astype(F32), HEAD_DIM).reshape(npair, 1, LANES)
    bias = _band_bias(SWA_WINDOW - 1)
    tab = pl.BlockSpec((1, l, LANES), lambda i, p: (i, 0, 0))
    c_in, c_out, c_shape = _cast_specs(cast, layer, b * npair, lambda i, p: i * npair + p)
    kd, vd = _swa_kv(proj3, tabs)
    pairs_per_kv = npair // SWA_KV_HEADS
    kv_spec = pl.BlockSpec((None, None, l, LANES), lambda i, p: (i, p // pairs_per_kv, 0, 0))
    return pl.pallas_call(
        functools.partial(_swa_kernel, n_cast=len(cast)),
        grid=(b, npair),
        in_specs=[
            pl.BlockSpec((1, l, LANES), lambda i, p: (i, 0, C_SQ // LANES + p)),
            kv_spec,
            kv_spec,
            tab,
            tab,
            tab,
            pl.BlockSpec(bias.shape, lambda i, p: (0, 0, 0)),
            pl.BlockSpec((1, 1, LANES), lambda i, p: (p, 0, 0)),
        ]
        + c_in,
        out_specs=[pl.BlockSpec((1, l, LANES), lambda i, p: (i, 0, p))] + c_out,
        out_shape=[jax.ShapeDtypeStruct((b, l, BRANCH_WIDTH), BF16)] + c_shape,
        scratch_shapes=[pltpu.VMEM((l, LANES), F32)],
        compiler_params=_cparams(("parallel", "parallel")),
        name="swa_attn",
    )(proj3, kd, vd, *tabs, jnp.asarray(bias), sink_l, *cast)


@functools.lru_cache(maxsize=None)
def _gla_consts():
    ch = GLA_CH
    r = np.arange(ch)[:, None]
    j = np.arange(ch)[None, :]
    dstack = (j <= r).astype(np.float32)
    lm = []
    for m in GLA_LEVELS[:-1]:
        mk = (((r // m) % 2 == 1) & ((j // m) == (r // m) - 1)).astype(np.float32)
        lm.append(np.tile(mk, (GLA_HEADS, 1)))
    lmask = np.stack(lm)
    dmask = np.tile(((r // GLA_SUB) == (j // GLA_SUB)).astype(np.float32), (1, GLA_HEADS))
    nsub = ch // GLA_SUB
    e = np.zeros((GLA_SUB, GLA_HEADS * GLA_DK, GLA_HEADS * ch), np.float32)
    for u in range(GLA_SUB):
        for h in range(GLA_HEADS):
            for s in range(nsub):
                e[u, h * GLA_DK : (h + 1) * GLA_DK, h * ch + GLA_SUB * s + u] = 1.0
    rr = np.arange(GLA_HEADS * GLA_DV)[:, None]
    cc = np.arange(GLA_HEADS * GLA_DK)[None, :]
    bd = ((rr // GLA_DV) == (cc // GLA_DK)).astype(np.float32)
    return dstack, lmask, dmask, e, bd


def _split3(x):
    hi = x.astype(BF16)
    r1 = x - hi.astype(F32)
    mid = r1.astype(BF16)
    lo = (r1 - mid.astype(F32)).astype(BF16)
    return hi, mid, lo


def _bcast_grp(x, m, u):
    r, w = x.shape
    x3 = x.reshape(r // m, m, w)
    return jnp.broadcast_to(x3[:, u : u + 1, :], x3.shape).reshape(r, w)


def _bcast_sub(x, u):
    return _bcast_grp(x, GLA_SUB, u)


def _gla_kernel(*refs, n_cast):
    (q_ref, k_ref, v_ref, r_ref, glr_ref, a2_ref, ab_ref, ng_ref, dst_ref, lmask_ref, dmask_ref, e_ref,
     bd_ref) = refs[:13]
    o_ref, st_ref = refs[13 + n_cast], refs[-1]
    _side_cast(refs[13 : 13 + n_cast], refs[14 + n_cast : 14 + 2 * n_cast])
    ch = GLA_CH
    rs = q_ref.shape[0]
    ncs = rs // ch
    nlev = len(GLA_LEVELS)
    hk = GLA_HEADS * GLA_DK

    @pl.when(pl.program_id(1) == 0)
    def _():
        st_ref[...] = jnp.zeros_like(st_ref)

    q = q_ref[...] * (GLA_DK**-0.5)
    k = k_ref[...]
    z = _bdot(glr_ref[...].astype(BF16), a2_ref[...]) + ab_ref[...]
    g = (jnp.minimum(z, 0.0) - jnp.log(1.0 + jnp.exp(-jnp.abs(z)))) * (1.0 / GLA_TAU)
    g3 = _split3(g)
    dst = dst_ref[...]

    cums, excl = [], []
    for c in range(ncs):
        rows = slice(c * ch, (c + 1) * ch)
        cums.append(_bdot(dst, g3[0][rows]) + _bdot(dst, g3[1][rows]) + _bdot(dst, g3[2][rows]))
        excl.append(cums[c] - g[rows])

    def eq(c, li):
        m = GLA_LEVELS[li]
        return cums[c] - _bcast_grp(excl[c], m, 0)

    def ek(c, li):
        m = GLA_LEVELS[li]
        return _bcast_grp(cums[c], m, m - 1) - cums[c]

    cs = jnp.concatenate([eq(c, 0) for c in range(ncs)], axis=0)
    tsub = lax.broadcasted_iota(jnp.int32, (rs, hk), 0) % GLA_SUB
    arep = jnp.zeros((rs, GLA_HEADS * ch), F32)
    for u in range(GLA_SUB):
        dec = jnp.exp(jnp.where(tsub >= u, cs - _bcast_sub(cs, u), -jnp.inf))
        p = q * _bcast_sub(k, u) * dec
        arep = arep + _bdot(p.astype(BF16), e_ref[u])
    arep = arep * jnp.concatenate([dmask_ref[...]] * ncs, axis=0)

    lane_k = lax.broadcasted_iota(jnp.int32, (ch, hk), 1) // GLA_DK
    ng = ng_ref[...]
    bd = bd_ref[...]
    for c in range(ncs):
        rows = slice(c * ch, (c + 1) * ch)
        qc, kc = q[rows], k[rows]
        vc = v_ref[rows, :]
        aoff = jnp.zeros((GLA_HEADS * ch, ch), F32)
        for li in range(nlev - 1):
            qe = qc * jnp.exp(eq(c, li))
            ke = (kc * jnp.exp(ek(c, li))).astype(BF16)
            qst = jnp.concatenate([jnp.where(lane_k == h, qe, 0.0) for h in range(GLA_HEADS)], axis=0).astype(BF16)
            aoff = aoff + _bdot_nt(qst, ke) * lmask_ref[li]
        cum = eq(c, nlev - 1)
        st = st_ref[...]
        o_inter = _bdot_nt((qc * jnp.exp(cum)).astype(BF16), st.astype(BF16))
        vb = vc.astype(BF16)
        outs = []
        for h in range(GLA_HEADS):
            a_h = aoff[h * ch : (h + 1) * ch] + arep[rows, h * ch : (h + 1) * ch]
            o_h = _bdot(a_h.astype(BF16), vb[:, h * GLA_DV : (h + 1) * GLA_DV]) + o_inter[:, h * GLA_DV : (h + 1) * GLA_DV]
            ms = jnp.mean(o_h * o_h, axis=-1, keepdims=True)
            outs.append(o_h * lax.rsqrt(ms + NORM_EPS))
        o = jnp.concatenate(outs, axis=1) * ng
        rc = r_ref[rows, :]
        o_ref[rows, :] = (o * (rc * jax.nn.sigmoid(rc))).astype(o_ref.dtype)
        ke_last = (kc * jnp.exp(ek(c, nlev - 1))).astype(BF16)
        kv = _bdot(vc.T.astype(BF16), ke_last)
        st_ref[...] = st * jnp.exp(cum[ch - 1 : ch, :]) + kv * bd


def _gla(proj, b, a2, ab, ng, cast=(), layer=0):
    t = proj.shape[0]
    l = t // b
    rs = GLA_CH * GLA_NCS
    assert l % rs == 0
    ns = l // rs
    dstack, lmask, dmask, e, bd = _gla_consts()
    c_in, c_out, c_shape = _cast_specs(cast, layer, b * ns, lambda i, s: i * ns + s)
    a2p = jnp.zeros((LANES, GLA_HEADS * GLA_DK), BF16).at[:GLA_LOWRANK].set(a2.astype(BF16))

    def rowblk(w, c0):
        return pl.BlockSpec((rs, w), lambda i, s: (i * ns + s, c0 // w))

    def full(shape):
        nd = len(shape)
        return pl.BlockSpec(shape, lambda i, s: (0,) * nd)

    hk, hv = GLA_HEADS * GLA_DK, GLA_HEADS * GLA_DV
    return pl.pallas_call(
        functools.partial(_gla_kernel, n_cast=len(cast)),
        grid=(b, ns),
        in_specs=[
            rowblk(hk, C_GQ),
            rowblk(hk, C_GK),
            rowblk(hv, C_GV),
            rowblk(hv, C_GR),
            rowblk(LANES, C_GLR),
            full((LANES, hk)),
            full((1, hk)),
            full((1, hv)),
            full(dstack.shape),
            full(lmask.shape),
            full(dmask.shape),
            full(e.shape),
            full(bd.shape),
        ]
        + c_in,
        out_specs=[pl.BlockSpec((rs, hv), lambda i, s: (i * ns + s, 0))] + c_out,
        out_shape=[jax.ShapeDtypeStruct((t, hv), BF16)] + c_shape,
        scratch_shapes=[pltpu.VMEM((hv, hk), F32)],
        compiler_params=_cparams(("parallel", "arbitrary")),
        name="gla",
    )(
        proj, proj, proj, proj, proj, a2p, ab.reshape(1, hk).astype(F32), ng.reshape(1, hv).astype(F32),
        jnp.asarray(dstack, BF16), jnp.asarray(lmask), jnp.asarray(dmask), jnp.asarray(e, BF16), jnp.asarray(bd),
        *cast,
    )


def _s5_params(lam_re, lam_im, log_dt, b_re, b_im, c_re, c_im, d, nch):
    f = F32
    cs = S5_CH
    dt = jnp.exp(log_dt.astype(f))[:, None]
    lr, li = lam_re.astype(f), lam_im.astype(f)
    mag = jnp.exp(lr * dt)
    ab_re, ab_im = mag * jnp.cos(li * dt), mag * jnp.sin(li * dt)
    den = lr * lr + li * li
    z_re = ((ab_re - 1.0) * lr + ab_im * li) / den
    z_im = (ab_im * lr - (ab_re - 1.0) * li) / den
    br, bi = b_re.astype(f), b_im.astype(f)
    bb_re = z_re[..., None] * br - z_im[..., None] * bi
    bb_im = z_re[..., None] * bi + z_im[..., None] * br

    def apow(p):
        p = jnp.asarray(p, f)[:, None, None]
        m = jnp.exp(p * (lr * dt))
        return m * jnp.cos(p * (li * dt)), m * jnp.sin(p * (li * dt))

    cr, ci = c_re.astype(f), c_im.astype(f)
    p_re, p_im = apow(np.arange(cs + 1))
    nb, gpb = S5_GROUPS // S5_GPB, S5_GPB
    hw = gpb * S5_STATE
    eye = jnp.eye(gpb, dtype=f)

    def b_blockdiag(x):
        x = x.reshape(nb, gpb, S5_STATE, S5_GROUP).transpose(0, 1, 3, 2)
        x = (x[:, :, :, None, :] * eye[None, :, None, :, None]).reshape(nb, LANES, hw)
        return jnp.concatenate([x, x], axis=-1)

    def c_blockdiag(x):
        x = x.reshape(nb, gpb, S5_GROUP, S5_STATE).transpose(0, 1, 3, 2)
        return (x[:, :, :, None, :] * eye[None, :, None, :, None]).reshape(nb, hw, LANES)

    def lanes(a, b_):
        p = a.shape[0]
        return jnp.concatenate([a.reshape(p, nb, hw), b_.reshape(p, nb, hw)], axis=-1).transpose(1, 0, 2)

    rev = np.arange(cs - 1, -1, -1)
    pr, pi = p_re[rev], p_im[rev]
    pa, pb = lanes(pr, pi), lanes(-pi, pr)
    acol_re = jnp.broadcast_to(ab_re.reshape(nb, hw, 1), (nb, hw, LANES))
    acol_im = jnp.broadcast_to(ab_im.reshape(nb, hw, 1), (nb, hw, LANES))
    nstep = max(1, int(math.log2(nch)))
    s_re, s_im = apow(cs * (2 ** np.arange(nstep)))
    ar = s_re.reshape(nstep, nb, hw).transpose(1, 0, 2)
    ai = s_im.reshape(nstep, nb, hw).transpose(1, 0, 2)
    dd = d.astype(f).reshape(nb, 1, LANES)
    return (b_blockdiag(bb_re), b_blockdiag(bb_im), c_blockdiag(cr), c_blockdiag(ci), acol_re, acol_im,
            pa, pb, ar, ai, dd)


def _s5_kernel(u_ref, br_ref, bi_ref, cr_ref, ci_ref, acr_ref, aci_ref, pa_ref, pb_ref, ar_ref, ai_ref, d_ref,
               y_ref, mb_s, mc_s, kc_s, *, nchb):
    cs = S5_CH
    t = u_ref.shape[0]
    nch = t // cs
    hw = S5_GPB * S5_STATE
    br, bi = br_ref[0], bi_ref[0]
    pa, pb = pa_ref[0], pb_ref[0]
    for j in range(cs):
        mb_s[j * LANES : (j + 1) * LANES, :] = (br * pa[j : j + 1] + bi * pb[j : j + 1]).astype(BF16)
    x_re, x_im = cr_ref[0], ci_ref[0]
    kc = _bdot(mb_s[...], jnp.concatenate([x_re, -x_im], axis=0).astype(BF16)).astype(BF16)
    a_re, a_im = acr_ref[0], aci_ref[0]
    for tt in range(cs):
        x_re, x_im = x_re * a_re - x_im * a_im, x_re * a_im + x_im * a_re
        mc_s[:hw, tt * LANES : (tt + 1) * LANES] = x_re.astype(BF16)
        mc_s[hw:, tt * LANES : (tt + 1) * LANES] = (-x_im).astype(BF16)
    gl = cs // S5_TSPLIT * LANES
    for qq in range(S5_TSPLIT):
        kc_s[:, qq * LANES : (qq + 1) * LANES] = kc[(S5_TSPLIT - 1 - qq) * gl : (S5_TSPLIT - qq) * gl]
    ucat = jnp.concatenate([u_ref[pl.ds(j, nch, stride=cs), :] for j in range(cs)], axis=1).astype(BF16)
    e_all = _bdot(ucat, mb_s[...])
    cidx = lax.broadcasted_iota(jnp.int32, (nch, LANES), 0) % nchb
    ar, ai = ar_ref[0], ai_ref[0]
    hp_re, hp_im = [], []
    for g in range(hw // LANES):
        lo = g * LANES
        e_re, e_im = e_all[:, lo : lo + LANES], e_all[:, hw + lo : hw + lo + LANES]
        for kstep in range(ar.shape[0]):
            s = 2**kstep
            if s >= nchb:
                break
            s_re = jnp.where(cidx >= s, pltpu.roll(e_re, s, axis=0), 0.0)
            s_im = jnp.where(cidx >= s, pltpu.roll(e_im, s, axis=0), 0.0)
            k_re, k_im = ar[kstep : kstep + 1, lo : lo + LANES], ai[kstep : kstep + 1, lo : lo + LANES]
            e_re, e_im = e_re + s_re * k_re - s_im * k_im, e_im + s_re * k_im + s_im * k_re
        hp_re.append(jnp.where(cidx >= 1, pltpu.roll(e_re, 1, axis=0), 0.0).astype(BF16))
        hp_im.append(jnp.where(cidx >= 1, pltpu.roll(e_im, 1, axis=0), 0.0).astype(BF16))
    ycar = _bdot(jnp.concatenate(hp_re + hp_im, axis=1), mc_s[...])
    for tt in range(cs):
        y_ref[pl.ds(tt, nch, stride=cs), :] = ycar[:, tt * LANES : (tt + 1) * LANES]

    rt = min(S5_RT, t)
    rmod = lax.broadcasted_iota(jnp.int32, (rt, LANES), 0) % cs
    dvec = d_ref[0]
    glag = cs // S5_TSPLIT

    def tile(i, carry):
        rows = pl.ds(pl.multiple_of(i * rt, rt), rt)
        ut = u_ref[rows, :]
        ush = [jnp.where(rmod >= tau, pltpu.roll(ut, tau, axis=0), 0.0) for tau in range(glag - 1, 0, -1)] + [ut]
        zz = _bdot(jnp.concatenate(ush, axis=1).astype(BF16), kc_s[...])
        acc = y_ref[rows, :] + dvec * ut + zz[:, :LANES]
        for qq in range(1, S5_TSPLIT):
            part = pltpu.roll(zz[:, qq * LANES : (qq + 1) * LANES], qq * glag, axis=0)
            acc = acc + jnp.where(rmod >= qq * glag, part, 0.0)
        y_ref[rows, :] = acc
        return carry

    lax.fori_loop(0, t // rt, tile, 0)


def _s5_core(proj, b, params):
    t = proj.shape[0]
    nchb = t // b // S5_CH
    sw = S5_GPB * 2 * S5_STATE

    def bspec(a):
        return pl.BlockSpec((1,) + a.shape[1:], lambda p: (p, 0, 0))

    return pl.pallas_call(
        functools.partial(_s5_kernel, nchb=nchb),
        grid=(S5_GROUPS // S5_GPB,),
        in_specs=[pl.BlockSpec((t, LANES), lambda p: (0, C_S5 // LANES + p))] + [bspec(a) for a in params],
        out_specs=pl.BlockSpec((t, LANES), lambda p: (0, p)),
        out_shape=jax.ShapeDtypeStruct((t, BRANCH_WIDTH), F32),
        scratch_shapes=[
            pltpu.VMEM((S5_CH * LANES, sw), BF16),
            pltpu.VMEM((sw, S5_CH * LANES), BF16),
            pltpu.VMEM((S5_CH // S5_TSPLIT * LANES, S5_TSPLIT * LANES), BF16),
        ],
        compiler_params=_cparams(("parallel",)),
        name="s5_core",
    )(proj, *params)


def _merge_kernel(
    x_ref, g1_ref, gla_ref, ys5_ref, dil_ref, swa_ref, gluw_ref, glub_ref, wg0, wg1, wg2, wg3, wb_ref, wo_ref,
    o_ref, h_s, s5_s,
):
    j = pl.program_id(1)

    @pl.when(j == 0)
    def _():
        h_s[...] = _rms(x_ref[...], g1_ref[...]).astype(BF16)
        zz = jax.nn.gelu(ys5_ref[...])
        gate = jax.nn.sigmoid(_bdot(zz.astype(BF16), gluw_ref[...]) + glub_ref[...])
        s5_s[...] = (zz * gate).astype(BF16)
        o_ref[...] = jnp.zeros_like(o_ref)

    h = h_s[...]
    branches = (gla_ref[...], s5_s[...], dil_ref[...], swa_ref[...])
    mixed = None
    for m, (wg, br) in enumerate(zip((wg0, wg1, wg2, wg3), branches)):
        term = jax.nn.sigmoid(_bdot(h, wg[...])) * _bdot(br, wb_ref[m])
        mixed = term if mixed is None else mixed + term
    mixed = mixed.astype(BF16)
    d = o_ref.shape[1]
    for c0 in range(0, d, FFN_DOWN_CHUNK):
        cols = slice(c0, c0 + FFN_DOWN_CHUNK)
        o_ref[:, cols] += _bdot(mixed, wo_ref[:, cols])

    @pl.when(j == pl.num_programs(1) - 1)
    def _():
        o_ref[...] = x_ref[...] + o_ref[...]


def _merge(x, g1, o_gla, y_s5, o_dil, o_swa, gluw, glub, w_all, wb, wo, layer):
    t, d = x.shape
    tm, tn = min(TM_MERGE, t), TN_MERGE
    nj = d // tn
    bw = BRANCH_WIDTH

    def rowblk(w):
        return pl.BlockSpec((tm, w), lambda i, j: (i, 0))

    def gate_spec(m):
        return pl.BlockSpec((None, d, tn), lambda i, j, m=m: (layer, 0, N_SMALL // tn + m * nj + j))

    return pl.pallas_call(
        _merge_kernel,
        grid=(t // tm, nj),
        in_specs=[
            rowblk(d),
            pl.BlockSpec((1, d), lambda i, j: (0, 0)),
            rowblk(bw),
            rowblk(bw),
            rowblk(bw),
            rowblk(bw),
            pl.BlockSpec((None, bw, bw), lambda i, j: (layer, 0, 0)),
            pl.BlockSpec((1, bw), lambda i, j: (0, 0)),
            gate_spec(0),
            gate_spec(1),
            gate_spec(2),
            gate_spec(3),
            pl.BlockSpec((N_BRANCH, bw, tn), lambda i, j: (0, 0, j)),
            pl.BlockSpec((tn, d), lambda i, j: (j, 0)),
        ],
        out_specs=rowblk(d),
        out_shape=jax.ShapeDtypeStruct((t, d), F32),
        scratch_shapes=[pltpu.VMEM((tm, d), BF16), pltpu.VMEM((tm, bw), BF16)],
        compiler_params=_cparams(("parallel", "arbitrary")),
        name="merge",
    )(x, g1, o_gla, y_s5, o_dil, o_swa, gluw, glub, w_all, w_all, w_all, w_all, wb, wo)


def _ffn_kernel(x_ref, g2_ref, wg_ref, wu_ref, wd_ref, gf_ref, o_ref, h_s, *, final_norm):
    j = pl.program_id(1)
    tm, d = o_ref.shape
    chunks = [slice(r0, r0 + FFN_ROW_CHUNK) for r0 in range(0, tm, FFN_ROW_CHUNK)]

    @pl.when(j == 0)
    def _():
        for rows in chunks:
            h_s[rows, :] = _rms(x_ref[rows, :], g2_ref[...]).astype(BF16)
        o_ref[...] = jnp.zeros_like(o_ref)

    for rows in chunks:
        h = h_s[rows, :]
        gate = _bdot(h, wg_ref[...])
        act = ((gate * jax.nn.sigmoid(gate)) * _bdot(h, wu_ref[...])).astype(BF16)
        for c0 in range(0, d, FFN_DOWN_CHUNK):
            cols = slice(c0, c0 + FFN_DOWN_CHUNK)
            o_ref[rows, cols] += _bdot(act, wd_ref[:, cols])

    @pl.when(j == pl.num_programs(1) - 1)
    def _():
        for rows in chunks:
            y = x_ref[rows, :] + o_ref[rows, :]
            o_ref[rows, :] = _rms(y, gf_ref[...]) if final_norm else y


def _ffn(x, g2, wg, wu, wd, gf, final_norm):
    t, d = x.shape
    fh = wg.shape[-1]
    tm, tf = min(TM_FFN, t), TF_FFN
    return pl.pallas_call(
        functools.partial(_ffn_kernel, final_norm=final_norm),
        grid=(t // tm, fh // tf),
        in_specs=[
            pl.BlockSpec((tm, d), lambda i, j: (i, 0)),
            pl.BlockSpec((1, d), lambda i, j: (0, 0)),
            pl.BlockSpec((d, tf), lambda i, j: (0, j)),
            pl.BlockSpec((d, tf), lambda i, j: (0, j)),
            pl.BlockSpec((tf, d), lambda i, j: (j, 0)),
            pl.BlockSpec((1, d), lambda i, j: (0, 0)),
        ],
        out_specs=pl.BlockSpec((tm, d), lambda i, j: (i, 0)),
        out_shape=jax.ShapeDtypeStruct((t, d), F32),
        scratch_shapes=[pltpu.VMEM((tm, d), BF16)],
        compiler_params=_cparams(("parallel", "arbitrary"), VMEM_LIMIT_FFN),
        name="ffn",
    )(x, g2, wg, wu, wd, gf)


def _pack_kernel(*refs):
    o_ref = refs[-1]
    for piece, w_ref in enumerate(refs[:-1]):
        x = w_ref[...]
        row = lax.broadcasted_iota(jnp.int32, x.shape, 0)
        blk = pl.program_id(1) * PACK_PIECES + piece
        keep = jnp.where(blk == C_GLR // PACK_CB, GLA_LOWRANK, PACK_CB)
        o_ref[:, piece * PACK_CB : (piece + 1) * PACK_CB] = jnp.where(row < keep, x, 0.0).T.astype(BF16)


def _pack_row0(c):
    cb = PACK_CB
    u = GLA_LOWRANK
    front = jnp.where(c < GLR_ORIG // cb, c * (cb // u), c * (cb // u) + 1)
    back = jnp.where(c == C_GLR // cb, GLR_ORIG // u, c * (cb // u) - (N_SMALL - N_ORIG_SMALL) // u)
    return jnp.where(c < C_GLR // cb, front, back) * u


def _pack_w_in(w_in):
    depth, d, d_in = w_in.shape
    n_out = N_SMALL + d_in - N_ORIG_SMALL
    cb = PACK_CB
    assert GLR_ORIG % cb == 0 and C_GLR % cb == 0 and N_SMALL % cb == 0 and n_out % cb == 0
    assert C_GLR == N_ORIG_SMALL - GLA_LOWRANK and C_GLR + cb == N_SMALL
    np_ = PACK_PIECES
    assert n_out % (cb * np_) == 0
    w_t = jnp.swapaxes(w_in, 1, 2)
    return pl.pallas_call(
        _pack_kernel,
        grid=(depth, n_out // (cb * np_)),
        in_specs=[
            pl.BlockSpec((None, pl.Element(cb), pl.Element(d)), lambda i, c, p=p: (i, _pack_row0(c * np_ + p), 0))
            for p in range(np_)
        ],
        out_specs=pl.BlockSpec((None, d, cb * np_), lambda i, c: (i, 0, c)),
        out_shape=jax.ShapeDtypeStruct((depth, d, n_out), BF16),
        compiler_params=_cparams(("parallel", "parallel")),
        name="pack_w_in",
    )(*([w_t] * np_))


def kernel(x, positions, norm1_g, w_in, gla_a2, gla_a_b, gla_norm_g, s5_lambda_re, s5_lambda_im, s5_log_dt, s5_b_re, s5_b_im, s5_c_re, s5_c_im, s5_d, s5_glu_w, s5_glu_b, swa_sinks, w_branch, w_out, norm2_g, w_ffn_gate, w_ffn_up, w_ffn_down, final_norm_g):
    b, l, d = x.shape
    t = b * l
    depth = w_in.shape[0]
    xs = x.reshape(t, d).astype(F32)
    w_all = _pack_w_in(w_in)
    w_br32 = w_branch.reshape(depth, N_BRANCH * BRANCH_WIDTH, d)
    glu_w = s5_glu_w.astype(BF16)
    tabs = _rope_tables(positions)
    gf = final_norm_g.reshape(1, d).astype(F32)
    s5_all = jax.vmap(functools.partial(_s5_params, nch=l // S5_CH))(
        s5_lambda_re, s5_lambda_im, s5_log_dt, s5_b_re, s5_b_im, s5_c_re, s5_c_im, s5_d)
    for i in range(depth):
        g1 = norm1_g[i].reshape(1, d).astype(F32)
        proj = _inproj(xs, g1, w_all, i)
        proj3 = proj.reshape(b, l, N_SMALL)
        o_gla, w_fg, w_fu, w_br, w_o = _gla(proj, b, gla_a2[i], gla_a_b[i], gla_norm_g[i],
                                            (w_ffn_gate, w_ffn_up, w_br32, w_out), i)
        w_br = w_br.reshape(N_BRANCH, BRANCH_WIDTH, d)
        y_s5 = _s5_core(proj, b, [a[i] for a in s5_all])
        o_dil = _dilated(proj3, tabs).reshape(t, BRANCH_WIDTH)
        o_swa, w_fd = _swa(proj3, tabs, swa_sinks[i], (w_ffn_down,), i)
        o_swa = o_swa.reshape(t, BRANCH_WIDTH)
        xs = _merge(xs, g1, o_gla, y_s5, o_dil, o_swa, glu_w, s5_glu_b[i].reshape(1, -1).astype(F32),
                    w_all, w_br, w_o, i)
        xs = _ffn(xs, norm2_g[i].reshape(1, d).astype(F32), w_fg, w_fu, w_fd, gf, i == depth - 1)
    return xs.reshape(b, l, d).astype(x.dtype)
```

```python
import functools
import math

import jax
import jax.numpy as jnp
import numpy as np
from jax import lax
from jax.experimental import pallas as pl
from jax.experimental.pallas import tpu as pltpu

F32 = jnp.float32
BF16 = jnp.bfloat16

D_MODEL = 2048
N_BRANCH = 4
BRANCH_WIDTH = 512
HEAD_DIM = 64
ATT_BLOCK = 128
ROPE_THETA = 500000.0
ROPE_DIM = HEAD_DIM // 4
NORM_EPS = 1e-6
GLA_HEADS = 4
GLA_DK = 64
GLA_DV = BRANCH_WIDTH // GLA_HEADS
GLA_LOWRANK = 16
GLA_TAU = 16.0
S5_GROUP = 16
S5_GROUPS = BRANCH_WIDTH // S5_GROUP
S5_STATE = 64
DIL_CONFIGS = ((128, 1), (512, 4), (2048, 16))
DIL_SPAN = ATT_BLOCK * 16
SWA_HEADS = BRANCH_WIDTH // HEAD_DIM
SWA_KV_HEADS = 2
SWA_WINDOW = 128

LANES = 128
SUBLANES = 8
VMEM_LIMIT = 56 * 1024 * 1024
VMEM_LIMIT_FFN = 60 * 1024 * 1024

C_GQ, C_GK, C_GV, C_GR, C_S5 = 0, 256, 512, 1024, 1536
C_CQ, C_CK, C_CV, C_SQ, C_SK, C_SV, C_GLR = 2048, 2560, 3072, 3584, 4096, 4224, 4352
N_SMALL = 4608
N_ORIG_SMALL = 4368
GLR_ORIG = 1536

TM_PROJ, TN_PROJ = 512, 1536
TM_MERGE, TN_MERGE = 512, 512
TM_FFN, TF_FFN = 1024, 512
FFN_ROW_CHUNK = 512
FFN_DOWN_CHUNK = 512
GLA_CH = 128
GLA_NCS = 4
GLA_SUB = 8
GLA_LEVELS = (8, 16, 32, 64, 128)
PACK_CB = 256
PACK_PIECES = 5
S5_CH = 16
S5_GPB = LANES // S5_GROUP
S5_RT = 2048
S5_TSPLIT = 4
ATT_UNROLL = 8
SWA_UNROLL = 16


def _cparams(sem, vmem_limit=VMEM_LIMIT):
    return pltpu.CompilerParams(dimension_semantics=sem, vmem_limit_bytes=vmem_limit)


def _rms(x, g):
    ms = jnp.mean(x * x, axis=-1, keepdims=True)
    return x * lax.rsqrt(ms + NORM_EPS) * g


def _bdot(a, b):
    return jnp.dot(a, b, preferred_element_type=F32)


def _bdot_nt(a, b):
    return lax.dot_general(a, b, (((1,), (1,)), ((), ())), preferred_element_type=F32)


def _cast_specs(ws, layer, nsteps, step_of):
    ins, outs, shapes = [], [], []
    for w in ws:
        _, r, c = w.shape
        rb = r // nsteps
        assert rb * nsteps == r and rb % (2 * SUBLANES) == 0
        ins.append(pl.BlockSpec((None, rb, c), lambda *g: (layer, step_of(*g), 0)))
        outs.append(pl.BlockSpec((rb, c), lambda *g: (step_of(*g), 0)))
        shapes.append(jax.ShapeDtypeStruct((r, c), BF16))
    return ins, outs, shapes


def _side_cast(in_refs, out_refs):
    for src, dst in zip(in_refs, out_refs):
        dst[...] = src[...].astype(BF16)


def _rope_apply(x, c, sa, sb):
    half = ROPE_DIM // 2
    return x * c + pltpu.roll(x, LANES - half, axis=1) * sa + pltpu.roll(x, half, axis=1) * sb


def _inproj_kernel(x_ref, g_ref, w_ref, o_ref):
    h = _rms(x_ref[...], g_ref[...]).astype(BF16)
    for c0 in range(0, o_ref.shape[1], TN_PROJ):
        cols = slice(c0, c0 + TN_PROJ)
        o_ref[:, cols] = _bdot(h, w_ref[:, cols])


def _inproj(x, g, w_all, layer):
    t, d = x.shape
    n = N_SMALL
    tm = min(TM_PROJ, t)
    assert n % TN_PROJ == 0
    return pl.pallas_call(
        _inproj_kernel,
        grid=(t // tm,),
        in_specs=[
            pl.BlockSpec((tm, d), lambda i: (i, 0)),
            pl.BlockSpec((1, d), lambda i: (0, 0)),
            pl.BlockSpec((None, d, n), lambda i: (layer, 0, 0), pipeline_mode=pl.Buffered(1)),
        ],
        out_specs=pl.BlockSpec((tm, n), lambda i: (i, 0)),
        out_shape=jax.ShapeDtypeStruct((t, n), F32),
        compiler_params=_cparams(("parallel",)),
        name="inproj",
    )(x, g, w_all)


def _rope_kernel(pos_ref, cos_ref, sa_ref, sb_ref):
    pos = pos_ref[0].astype(F32)
    lane = lax.broadcasted_iota(jnp.int32, (SUBLANES, LANES), 1)
    d = lane % HEAD_DIM
    half = ROPE_DIM // 2
    fi = (d % half).astype(F32) / half
    inv = jnp.power(jnp.full((SUBLANES, LANES), ROPE_THETA, F32), -fi)[0:1]
    d1 = d[0:1]
    ang = pos * inv
    c, s = jnp.cos(ang), jnp.sin(ang)
    cos_ref[0] = jnp.where(d1 < ROPE_DIM, c, 1.0)
    sa_ref[0] = jnp.where(d1 < half, -s, 0.0)
    sb_ref[0] = jnp.where((d1 >= half) & (d1 < ROPE_DIM), s, 0.0)


def _rope_tables(positions):
    b, l = positions.shape
    spec = pl.BlockSpec((1, l, LANES), lambda i: (i, 0, 0))
    shp = jax.ShapeDtypeStruct((b, l, LANES), F32)
    return pl.pallas_call(
        _rope_kernel,
        grid=(b,),
        in_specs=[pl.BlockSpec((1, l, 1), lambda i: (i, 0, 0))],
        out_specs=[spec, spec, spec],
        out_shape=[shp, shp, shp],
        compiler_params=_cparams(("parallel",)),
        name="rope_tables",
    )(positions.reshape(b, l, 1))


def _band_bias(max_dist):
    t = ATT_BLOCK
    assert 0 < max_dist <= t
    qi = np.arange(2 * t)[:, None] % t
    kj = np.arange(2 * t)[None, :]
    own = (kj >= t) & (kj - t <= qi)
    prev = (kj < t) & (t + qi - kj <= max_dist)
    ninf = np.float32(-np.inf)
    return np.stack([np.where(own | prev, 0.0, ninf), np.where(own, 0.0, ninf)]).astype(np.float32)


def _band_block(q, kp, ko, vp, vo, bias):
    t = ATT_BLOCK
    lane = lax.broadcasted_iota(jnp.int32, (t, LANES), 1)
    in_a = lane < HEAD_DIM
    q2 = jnp.concatenate([jnp.where(in_a, q, 0.0), jnp.where(in_a, 0.0, q)], axis=0).astype(BF16)
    s = _bdot_nt(q2, jnp.concatenate([kp, ko], axis=0).astype(BF16)) + bias
    m = jnp.max(s, axis=-1, keepdims=True)
    p = jnp.exp(s - m).astype(BF16)
    ones = jnp.ones((2 * t, LANES), BF16)
    vcat = jnp.concatenate([jnp.concatenate([vp, vo], axis=0).astype(BF16), ones], axis=1)
    od = _bdot(p, vcat)
    o = jnp.where(in_a, od[:t, :LANES], od[t:, :LANES])
    df = jnp.where(in_a, od[:t, LANES:], od[t:, LANES:])
    mf = jnp.where(in_a, m[:t], m[t:])
    return mf, df, o


def _dil_kernel(q_ref, k_ref, v_ref, c_ref, sa_ref, sb_ref, bias_ref, o_ref, qs, ks, qd, kd, vd, ms, ds, os_):
    l = q_ref.shape[1]
    c, sa, sb = c_ref[0], sa_ref[0], sb_ref[0]
    qs[...] = _rope_apply(q_ref[0], c, sa, sb) * (HEAD_DIM**-0.5)
    ks[...] = _rope_apply(k_ref[0], c, sa, sb)
    t = ATT_BLOCK
    nblk = DIL_SPAN // t
    dw = DIL_SPAN // t
    per = l // dw
    for r in range(dw):
        src, dst = pl.ds(r, per, stride=dw), slice(r * per, (r + 1) * per)
        qd[dst, :] = qs[src, :]
        kd[dst, :] = ks[src, :]
        vd[dst, :] = v_ref[0, src, :]

    for sblk in range(l // DIL_SPAN):
        base = sblk * DIL_SPAN
        for ci, (_, dil) in enumerate(DIL_CONFIGS):

            def body_wide(r, carry, ci=ci, sblk=sblk):
                rows = pl.ds(pl.multiple_of(r * per + sblk * t, t), t)
                prow = pl.ds(pl.multiple_of(r * per + max(sblk - 1, 0) * t, t), t)
                mf, df, o = _band_block(qd[rows, :], kd[prow, :], kd[rows, :], vd[prow, :], vd[rows, :],
                                        bias_ref[0 if sblk > 0 else 1])
                lrow = pl.ds(r, t, stride=dw)
                ms[ci, lrow, :] = mf
                ds[ci, lrow, :] = df
                os_[ci, lrow, :] = o
                return carry

            if dil == dw:
                lax.fori_loop(0, nblk, body_wide, 0, unroll=ATT_UNROLL)
                continue

            def body(idx, carry, dil=dil, ci=ci, base=base):
                r = idx % dil
                n = idx // dil
                loc = r + dil * t * n
                start = base + loc
                has_prev = start >= dil * t
                pstart = jnp.where(has_prev, start - dil * t, start)
                if dil == 1:
                    start = pl.multiple_of(start, t)
                    pstart = pl.multiple_of(pstart, t)
                    loc = pl.multiple_of(loc, t)
                    rows, prow, lrow = pl.ds(start, t), pl.ds(pstart, t), pl.ds(loc, t)
                else:
                    rows = pl.ds(start, t, stride=dil)
                    prow = pl.ds(pstart, t, stride=dil)
                    lrow = pl.ds(loc, t, stride=dil)
                mf, df, o = _band_block(
                    qs[rows, :], ks[prow, :], ks[rows, :], v_ref[0, prow, :], v_ref[0, rows, :],
                    bias_ref[jnp.where(has_prev, 0, 1)],
                )
                ms[ci, lrow, :] = mf
                ds[ci, lrow, :] = df
                os_[ci, lrow, :] = o
                return carry

            lax.fori_loop(0, nblk, body, 0, unroll=ATT_UNROLL)

        m0, m1, m2 = ms[0], ms[1], ms[2]
        mx = jnp.maximum(jnp.maximum(m0, m1), m2)
        w0, w1, w2 = jnp.exp(m0 - mx), jnp.exp(m1 - mx), jnp.exp(m2 - mx)
        den = w0 * ds[0] + w1 * ds[1] + w2 * ds[2]
        num = w0 * os_[0] + w1 * os_[1] + w2 * os_[2]
        o_ref[0, base : base + DIL_SPAN, :] = (num / den).astype(o_ref.dtype)


def _dilated(proj3, tabs):
    b, l, _ = proj3.shape
    assert l % DIL_SPAN == 0
    assert all(window // dil == ATT_BLOCK for window, dil in DIL_CONFIGS)
    bias = _band_bias(ATT_BLOCK)

    def col(c0):
        return pl.BlockSpec((1, l, LANES), lambda i, p: (i, 0, c0 // LANES + p))

    tab = pl.BlockSpec((1, l, LANES), lambda i, p: (i, 0, 0))
    return pl.pallas_call(
        _dil_kernel,
        grid=(b, BRANCH_WIDTH // LANES),
        in_specs=[col(C_CQ), col(C_CK), col(C_CV), tab, tab, tab, pl.BlockSpec(bias.shape, lambda i, p: (0, 0, 0))],
        out_specs=pl.BlockSpec((1, l, LANES), lambda i, p: (i, 0, p)),
        out_shape=jax.ShapeDtypeStruct((b, l, BRANCH_WIDTH), BF16),
        scratch_shapes=[pltpu.VMEM((l, LANES), F32)] * 5
        + [pltpu.VMEM((len(DIL_CONFIGS), DIL_SPAN, LANES), F32)] * 3,
        compiler_params=_cparams(("parallel", "parallel")),
        name="dilated_attn",
    )(proj3, proj3, proj3, *tabs, jnp.asarray(bias))


def _swa_kernel(*refs, n_cast):
    q_ref, k_ref, v_ref, c_ref, sa_ref, sb_ref, bias_ref, sink_ref = refs[:8]
    o_ref = refs[8 + n_cast]
    qs, ks, vs = refs[-3:]
    _side_cast(refs[8 : 8 + n_cast], refs[9 + n_cast : 9 + 2 * n_cast])
    l = q_ref.shape[1]
    c, sa, sb = c_ref[0], sa_ref[0], sb_ref[0]
    qs[...] = _rope_apply(q_ref[0], c, sa, sb) * (HEAD_DIM**-0.5)
    g = pl.program_id(1) // (SWA_HEADS // SWA_KV_HEADS // 2)
    lane = lax.broadcasted_iota(jnp.int32, (l, LANES), 1)
    keep = (lane // HEAD_DIM) == g
    k, v = _rope_apply(k_ref[0], c, sa, sb), v_ref[0]
    ks[...] = jnp.where(keep, k, pltpu.roll(k, HEAD_DIM, axis=1))
    vs[...] = jnp.where(keep, v, pltpu.roll(v, HEAD_DIM, axis=1))

    sink = sink_ref[0]
    t = ATT_BLOCK

    def body(n, carry):
        start = pl.multiple_of(n * t, t)
        has_prev = n > 0
        pstart = pl.multiple_of(jnp.where(has_prev, start - t, start), t)
        rows, prow = pl.ds(start, t), pl.ds(pstart, t)
        mf, df, o = _band_block(qs[rows, :], ks[prow, :], ks[rows, :], vs[prow, :], vs[rows, :],
                                bias_ref[jnp.where(has_prev, 0, 1)])
        lse = mf + jnp.log(df)
        o_ref[0, rows, :] = (o / df * jax.nn.sigmoid(lse - sink)).astype(o_ref.dtype)
        return carry

    lax.fori_loop(0, l // t, body, 0, unroll=SWA_UNROLL)


def _swa(proj3, tabs, sinks, cast=(), layer=0):
    b, l, _ = proj3.shape
    npair = BRANCH_WIDTH // LANES
    sink_l = jnp.repeat(sinks.astype(F32), HEAD_DIM).reshape(npair, 1, LANES)
    bias = _band_bias(SWA_WINDOW - 1)
    tab = pl.BlockSpec((1, l, LANES), lambda i, p: (i, 0, 0))
    c_in, c_out, c_shape = _cast_specs(cast, layer, b * npair, lambda i, p: i * npair + p)
    return pl.pallas_call(
        functools.partial(_swa_kernel, n_cast=len(cast)),
        grid=(b, npair),
        in_specs=[
            pl.BlockSpec((1, l, LANES), lambda i, p: (i, 0, C_SQ // LANES + p)),
            pl.BlockSpec((1, l, LANES), lambda i, p: (i, 0, C_SK // LANES)),
            pl.BlockSpec((1, l, LANES), lambda i, p: (i, 0, C_SV // LANES)),
            tab,
            tab,
            tab,
            pl.BlockSpec(bias.shape, lambda i, p: (0, 0, 0)),
            pl.BlockSpec((1, 1, LANES), lambda i, p: (p, 0, 0)),
        ]
        + c_in,
        out_specs=[pl.BlockSpec((1, l, LANES), lambda i, p: (i, 0, p))] + c_out,
        out_shape=[jax.ShapeDtypeStruct((b, l, BRANCH_WIDTH), BF16)] + c_shape,
        scratch_shapes=[pltpu.VMEM((l, LANES), F32)] * 3,
        compiler_params=_cparams(("parallel", "parallel")),
        name="swa_attn",
    )(proj3, proj3, proj3, *tabs, jnp.asarray(bias), sink_l, *cast)


@functools.lru_cache(maxsize=None)
def _gla_consts():
    ch = GLA_CH
    r = np.arange(ch)[:, None]
    j = np.arange(ch)[None, :]
    dstack = (j <= r).astype(np.float32)
    lm = []
    for m in GLA_LEVELS[:-1]:
        mk = (((r // m) % 2 == 1) & ((j // m) == (r // m) - 1)).astype(np.float32)
        lm.append(np.tile(mk, (GLA_HEADS, 1)))
    lmask = np.stack(lm)
    dmask = np.tile(((r // GLA_SUB) == (j // GLA_SUB)).astype(np.float32), (1, GLA_HEADS))
    nsub = ch // GLA_SUB
    e = np.zeros((GLA_SUB, GLA_HEADS * GLA_DK, GLA_HEADS * ch), np.float32)
    for u in range(GLA_SUB):
        for h in range(GLA_HEADS):
            for s in range(nsub):
                e[u, h * GLA_DK : (h + 1) * GLA_DK, h * ch + GLA_SUB * s + u] = 1.0
    rr = np.arange(GLA_HEADS * GLA_DV)[:, None]
    cc = np.arange(GLA_HEADS * GLA_DK)[None, :]
    bd = ((rr // GLA_DV) == (cc // GLA_DK)).astype(np.float32)
    return dstack, lmask, dmask, e, bd


def _split3(x):
    hi = x.astype(BF16)
    r1 = x - hi.astype(F32)
    mid = r1.astype(BF16)
    lo = (r1 - mid.astype(F32)).astype(BF16)
    return hi, mid, lo


def _bcast_grp(x, m, u):
    r, w = x.shape
    x3 = x.reshape(r // m, m, w)
    return jnp.broadcast_to(x3[:, u : u + 1, :], x3.shape).reshape(r, w)


def _bcast_sub(x, u):
    return _bcast_grp(x, GLA_SUB, u)


def _gla_kernel(*refs, n_cast):
    (q_ref, k_ref, v_ref, r_ref, glr_ref, a2_ref, ab_ref, ng_ref, dst_ref, lmask_ref, dmask_ref, e_ref,
     bd_ref) = refs[:13]
    o_ref, st_ref = refs[13 + n_cast], refs[-1]
    _side_cast(refs[13 : 13 + n_cast], refs[14 + n_cast : 14 + 2 * n_cast])
    ch = GLA_CH
    rs = q_ref.shape[0]
    ncs = rs // ch
    nlev = len(GLA_LEVELS)
    hk = GLA_HEADS * GLA_DK

    @pl.when(pl.program_id(1) == 0)
    def _():
        st_ref[...] = jnp.zeros_like(st_ref)

    q = q_ref[...] * (GLA_DK**-0.5)
    k = k_ref[...]
    z = _bdot(glr_ref[...].astype(BF16), a2_ref[...]) + ab_ref[...]
    g = (jnp.minimum(z, 0.0) - jnp.log(1.0 + jnp.exp(-jnp.abs(z)))) * (1.0 / GLA_TAU)
    g3 = _split3(g)
    dst = dst_ref[...]

    cums, excl = [], []
    for c in range(ncs):
        rows = slice(c * ch, (c + 1) * ch)
        cums.append(_bdot(dst, g3[0][rows]) + _bdot(dst, g3[1][rows]) + _bdot(dst, g3[2][rows]))
        excl.append(cums[c] - g[rows])

    def eq(c, li):
        m = GLA_LEVELS[li]
        return cums[c] - _bcast_grp(excl[c], m, 0)

    def ek(c, li):
        m = GLA_LEVELS[li]
        return _bcast_grp(cums[c], m, m - 1) - cums[c]

    cs = jnp.concatenate([eq(c, 0) for c in range(ncs)], axis=0)
    tsub = lax.broadcasted_iota(jnp.int32, (rs, hk), 0) % GLA_SUB
    arep = jnp.zeros((rs, GLA_HEADS * ch), F32)
    for u in range(GLA_SUB):
        dec = jnp.exp(jnp.where(tsub >= u, cs - _bcast_sub(cs, u), -jnp.inf))
        p = q * _bcast_sub(k, u) * dec
        arep = arep + _bdot(p.astype(BF16), e_ref[u])
    arep = arep * jnp.concatenate([dmask_ref[...]] * ncs, axis=0)

    lane_k = lax.broadcasted_iota(jnp.int32, (ch, hk), 1) // GLA_DK
    ng = ng_ref[...]
    bd = bd_ref[...]
    for c in range(ncs):
        rows = slice(c * ch, (c + 1) * ch)
        qc, kc = q[rows], k[rows]
        vc = v_ref[rows, :]
        aoff = jnp.zeros((GLA_HEADS * ch, ch), F32)
        for li in range(nlev - 1):
            qe = qc * jnp.exp(eq(c, li))
            ke = (kc * jnp.exp(ek(c, li))).astype(BF16)
            qst = jnp.concatenate([jnp.where(lane_k == h, qe, 0.0) for h in range(GLA_HEADS)], axis=0).astype(BF16)
            aoff = aoff + _bdot_nt(qst, ke) * lmask_ref[li]
        cum = eq(c, nlev - 1)
        st = st_ref[...]
        o_inter = _bdot_nt((qc * jnp.exp(cum)).astype(BF16), st.astype(BF16))
        vb = vc.astype(BF16)
        outs = []
        for h in range(GLA_HEADS):
            a_h = aoff[h * ch : (h + 1) * ch] + arep[rows, h * ch : (h + 1) * ch]
            o_h = _bdot(a_h.astype(BF16), vb[:, h * GLA_DV : (h + 1) * GLA_DV]) + o_inter[:, h * GLA_DV : (h + 1) * GLA_DV]
            ms = jnp.mean(o_h * o_h, axis=-1, keepdims=True)
            outs.append(o_h * lax.rsqrt(ms + NORM_EPS))
        o = jnp.concatenate(outs, axis=1) * ng
        rc = r_ref[rows, :]
        o_ref[rows, :] = (o * (rc * jax.nn.sigmoid(rc))).astype(o_ref.dtype)
        ke_last = (kc * jnp.exp(ek(c, nlev - 1))).astype(BF16)
        kv = _bdot(vc.T.astype(BF16), ke_last)
        st_ref[...] = st * jnp.exp(cum[ch - 1 : ch, :]) + kv * bd


def _gla(proj, b, a2, ab, ng, cast=(), layer=0):
    t = proj.shape[0]
    l = t // b
    rs = GLA_CH * GLA_NCS
    assert l % rs == 0
    ns = l // rs
    dstack, lmask, dmask, e, bd = _gla_consts()
    c_in, c_out, c_shape = _cast_specs(cast, layer, b * ns, lambda i, s: i * ns + s)
    a2p = jnp.zeros((LANES, GLA_HEADS * GLA_DK), BF16).at[:GLA_LOWRANK].set(a2.astype(BF16))

    def rowblk(w, c0):
        return pl.BlockSpec((rs, w), lambda i, s: (i * ns + s, c0 // w))

    def full(shape):
        nd = len(shape)
        return pl.BlockSpec(shape, lambda i, s: (0,) * nd)

    hk, hv = GLA_HEADS * GLA_DK, GLA_HEADS * GLA_DV
    return pl.pallas_call(
        functools.partial(_gla_kernel, n_cast=len(cast)),
        grid=(b, ns),
        in_specs=[
            rowblk(hk, C_GQ),
            rowblk(hk, C_GK),
            rowblk(hv, C_GV),
            rowblk(hv, C_GR),
            rowblk(LANES, C_GLR),
            full((LANES, hk)),
            full((1, hk)),
            full((1, hv)),
            full(dstack.shape),
            full(lmask.shape),
            full(dmask.shape),
            full(e.shape),
            full(bd.shape),
        ]
        + c_in,
        out_specs=[pl.BlockSpec((rs, hv), lambda i, s: (i * ns + s, 0))] + c_out,
        out_shape=[jax.ShapeDtypeStruct((t, hv), BF16)] + c_shape,
        scratch_shapes=[pltpu.VMEM((hv, hk), F32)],
        compiler_params=_cparams(("parallel", "arbitrary")),
        name="gla",
    )(
        proj, proj, proj, proj, proj, a2p, ab.reshape(1, hk).astype(F32), ng.reshape(1, hv).astype(F32),
        jnp.asarray(dstack, BF16), jnp.asarray(lmask), jnp.asarray(dmask), jnp.asarray(e, BF16), jnp.asarray(bd),
        *cast,
    )


def _s5_params(lam_re, lam_im, log_dt, b_re, b_im, c_re, c_im, d, nch):
    f = F32
    cs = S5_CH
    dt = jnp.exp(log_dt.astype(f))[:, None]
    lr, li = lam_re.astype(f), lam_im.astype(f)
    mag = jnp.exp(lr * dt)
    ab_re, ab_im = mag * jnp.cos(li * dt), mag * jnp.sin(li * dt)
    den = lr * lr + li * li
    z_re = ((ab_re - 1.0) * lr + ab_im * li) / den
    z_im = (ab_im * lr - (ab_re - 1.0) * li) / den
    br, bi = b_re.astype(f), b_im.astype(f)
    bb_re = z_re[..., None] * br - z_im[..., None] * bi
    bb_im = z_re[..., None] * bi + z_im[..., None] * br

    def apow(p):
        p = jnp.asarray(p, f)[:, None, None]
        m = jnp.exp(p * (lr * dt))
        return m * jnp.cos(p * (li * dt)), m * jnp.sin(p * (li * dt))

    cr, ci = c_re.astype(f), c_im.astype(f)
    p_re, p_im = apow(np.arange(cs + 1))
    nb, gpb = S5_GROUPS // S5_GPB, S5_GPB
    hw = gpb * S5_STATE
    eye = jnp.eye(gpb, dtype=f)

    def b_blockdiag(x):
        x = x.reshape(nb, gpb, S5_STATE, S5_GROUP).transpose(0, 1, 3, 2)
        x = (x[:, :, :, None, :] * eye[None, :, None, :, None]).reshape(nb, LANES, hw)
        return jnp.concatenate([x, x], axis=-1)

    def c_blockdiag(x):
        x = x.reshape(nb, gpb, S5_GROUP, S5_STATE).transpose(0, 1, 3, 2)
        return (x[:, :, :, None, :] * eye[None, :, None, :, None]).reshape(nb, hw, LANES)

    def lanes(a, b_):
        p = a.shape[0]
        return jnp.concatenate([a.reshape(p, nb, hw), b_.reshape(p, nb, hw)], axis=-1).transpose(1, 0, 2)

    rev = np.arange(cs - 1, -1, -1)
    pr, pi = p_re[rev], p_im[rev]
    pa, pb = lanes(pr, pi), lanes(-pi, pr)
    acol_re = jnp.broadcast_to(ab_re.reshape(nb, hw, 1), (nb, hw, LANES))
    acol_im = jnp.broadcast_to(ab_im.reshape(nb, hw, 1), (nb, hw, LANES))
    nstep = max(1, int(math.log2(nch)))
    s_re, s_im = apow(cs * (2 ** np.arange(nstep)))
    ar = s_re.reshape(nstep, nb, hw).transpose(1, 0, 2)
    ai = s_im.reshape(nstep, nb, hw).transpose(1, 0, 2)
    dd = d.astype(f).reshape(nb, 1, LANES)
    return (b_blockdiag(bb_re), b_blockdiag(bb_im), c_blockdiag(cr), c_blockdiag(ci), acol_re, acol_im,
            pa, pb, ar, ai, dd)


def _s5_kernel(u_ref, br_ref, bi_ref, cr_ref, ci_ref, acr_ref, aci_ref, pa_ref, pb_ref, ar_ref, ai_ref, d_ref,
               y_ref, mb_s, mc_s, kc_s, *, nchb):
    cs = S5_CH
    t = u_ref.shape[0]
    nch = t // cs
    hw = S5_GPB * S5_STATE
    br, bi = br_ref[0], bi_ref[0]
    pa, pb = pa_ref[0], pb_ref[0]
    for j in range(cs):
        mb_s[j * LANES : (j + 1) * LANES, :] = (br * pa[j : j + 1] + bi * pb[j : j + 1]).astype(BF16)
    x_re, x_im = cr_ref[0], ci_ref[0]
    kc = _bdot(mb_s[...], jnp.concatenate([x_re, -x_im], axis=0).astype(BF16)).astype(BF16)
    a_re, a_im = acr_ref[0], aci_ref[0]
    for tt in range(cs):
        x_re, x_im = x_re * a_re - x_im * a_im, x_re * a_im + x_im * a_re
        mc_s[:hw, tt * LANES : (tt + 1) * LANES] = x_re.astype(BF16)
        mc_s[hw:, tt * LANES : (tt + 1) * LANES] = (-x_im).astype(BF16)
    gl = cs // S5_TSPLIT * LANES
    for qq in range(S5_TSPLIT):
        kc_s[:, qq * LANES : (qq + 1) * LANES] = kc[(S5_TSPLIT - 1 - qq) * gl : (S5_TSPLIT - qq) * gl]
    ucat = jnp.concatenate([u_ref[pl.ds(j, nch, stride=cs), :] for j in range(cs)], axis=1).astype(BF16)
    e_all = _bdot(ucat, mb_s[...])
    cidx = lax.broadcasted_iota(jnp.int32, (nch, LANES), 0) % nchb
    ar, ai = ar_ref[0], ai_ref[0]
    hp_re, hp_im = [], []
    for g in range(hw // LANES):
        lo = g * LANES
        e_re, e_im = e_all[:, lo : lo + LANES], e_all[:, hw + lo : hw + lo + LANES]
        for kstep in range(ar.shape[0]):
            s = 2**kstep
            if s >= nchb:
                break
            s_re = jnp.where(cidx >= s, pltpu.roll(e_re, s, axis=0), 0.0)
            s_im = jnp.where(cidx >= s, pltpu.roll(e_im, s, axis=0), 0.0)
            k_re, k_im = ar[kstep : kstep + 1, lo : lo + LANES], ai[kstep : kstep + 1, lo : lo + LANES]
            e_re, e_im = e_re + s_re * k_re - s_im * k_im, e_im + s_re * k_im + s_im * k_re
        hp_re.append(jnp.where(cidx >= 1, pltpu.roll(e_re, 1, axis=0), 0.0).astype(BF16))
        hp_im.append(jnp.where(cidx >= 1, pltpu.roll(e_im, 1, axis=0), 0.0).astype(BF16))
    ycar = _bdot(jnp.concatenate(hp_re + hp_im, axis=1), mc_s[...])
    for tt in range(cs):
        y_ref[pl.ds(tt, nch, stride=cs), :] = ycar[:, tt * LANES : (tt + 1) * LANES]

    rt = min(S5_RT, t)
    rmod = lax.broadcasted_iota(jnp.int32, (rt, LANES), 0) % cs
    dvec = d_ref[0]
    glag = cs // S5_TSPLIT

    def tile(i, carry):
        rows = pl.ds(pl.multiple_of(i * rt, rt), rt)
        ut = u_ref[rows, :]
        ush = [jnp.where(rmod >= tau, pltpu.roll(ut, tau, axis=0), 0.0) for tau in range(glag - 1, 0, -1)] + [ut]
        zz = _bdot(jnp.concatenate(ush, axis=1).astype(BF16), kc_s[...])
        acc = y_ref[rows, :] + dvec * ut + zz[:, :LANES]
        for qq in range(1, S5_TSPLIT):
            part = pltpu.roll(zz[:, qq * LANES : (qq + 1) * LANES], qq * glag, axis=0)
            acc = acc + jnp.where(rmod >= qq * glag, part, 0.0)
        y_ref[rows, :] = acc
        return carry

    lax.fori_loop(0, t // rt, tile, 0)


def _s5_core(proj, b, params):
    t = proj.shape[0]
    nchb = t // b // S5_CH
    sw = S5_GPB * 2 * S5_STATE

    def bspec(a):
        return pl.BlockSpec((1,) + a.shape[1:], lambda p: (p, 0, 0))

    return pl.pallas_call(
        functools.partial(_s5_kernel, nchb=nchb),
        grid=(S5_GROUPS // S5_GPB,),
        in_specs=[pl.BlockSpec((t, LANES), lambda p: (0, C_S5 // LANES + p))] + [bspec(a) for a in params],
        out_specs=pl.BlockSpec((t, LANES), lambda p: (0, p)),
        out_shape=jax.ShapeDtypeStruct((t, BRANCH_WIDTH), F32),
        scratch_shapes=[
            pltpu.VMEM((S5_CH * LANES, sw), BF16),
            pltpu.VMEM((sw, S5_CH * LANES), BF16),
            pltpu.VMEM((S5_CH // S5_TSPLIT * LANES, S5_TSPLIT * LANES), BF16),
        ],
        compiler_params=_cparams(("parallel",)),
        name="s5_core",
    )(proj, *params)


def _merge_kernel(
    x_ref, g1_ref, gla_ref, ys5_ref, dil_ref, swa_ref, gluw_ref, glub_ref, wg0, wg1, wg2, wg3, wb_ref, wo_ref,
    o_ref, h_s, s5_s,
):
    j = pl.program_id(1)

    @pl.when(j == 0)
    def _():
        h_s[...] = _rms(x_ref[...], g1_ref[...]).astype(BF16)
        zz = jax.nn.gelu(ys5_ref[...])
        gate = jax.nn.sigmoid(_bdot(zz.astype(BF16), gluw_ref[...]) + glub_ref[...])
        s5_s[...] = (zz * gate).astype(BF16)
        o_ref[...] = jnp.zeros_like(o_ref)

    h = h_s[...]
    branches = (gla_ref[...], s5_s[...], dil_ref[...], swa_ref[...])
    mixed = None
    for m, (wg, br) in enumerate(zip((wg0, wg1, wg2, wg3), branches)):
        term = jax.nn.sigmoid(_bdot(h, wg[...])) * _bdot(br, wb_ref[m])
        mixed = term if mixed is None else mixed + term
    mixed = mixed.astype(BF16)
    d = o_ref.shape[1]
    for c0 in range(0, d, FFN_DOWN_CHUNK):
        cols = slice(c0, c0 + FFN_DOWN_CHUNK)
        o_ref[:, cols] += _bdot(mixed, wo_ref[:, cols])

    @pl.when(j == pl.num_programs(1) - 1)
    def _():
        o_ref[...] = x_ref[...] + o_ref[...]


def _merge(x, g1, o_gla, y_s5, o_dil, o_swa, gluw, glub, w_all, wb, wo, layer):
    t, d = x.shape
    tm, tn = min(TM_MERGE, t), TN_MERGE
    nj = d // tn
    bw = BRANCH_WIDTH

    def rowblk(w):
        return pl.BlockSpec((tm, w), lambda i, j: (i, 0))

    def gate_spec(m):
        return pl.BlockSpec((None, d, tn), lambda i, j, m=m: (layer, 0, N_SMALL // tn + m * nj + j))

    return pl.pallas_call(
        _merge_kernel,
        grid=(t // tm, nj),
        in_specs=[
            rowblk(d),
            pl.BlockSpec((1, d), lambda i, j: (0, 0)),
            rowblk(bw),
            rowblk(bw),
            rowblk(bw),
            rowblk(bw),
            pl.BlockSpec((None, bw, bw), lambda i, j: (layer, 0, 0)),
            pl.BlockSpec((1, bw), lambda i, j: (0, 0)),
            gate_spec(0),
            gate_spec(1),
            gate_spec(2),
            gate_spec(3),
            pl.BlockSpec((N_BRANCH, bw, tn), lambda i, j: (0, 0, j)),
            pl.BlockSpec((tn, d), lambda i, j: (j, 0)),
        ],
        out_specs=rowblk(d),
        out_shape=jax.ShapeDtypeStruct((t, d), F32),
        scratch_shapes=[pltpu.VMEM((tm, d), BF16), pltpu.VMEM((tm, bw), BF16)],
        compiler_params=_cparams(("parallel", "arbitrary")),
        name="merge",
    )(x, g1, o_gla, y_s5, o_dil, o_swa, gluw, glub, w_all, w_all, w_all, w_all, wb, wo)


def _ffn_kernel(x_ref, g2_ref, wg_ref, wu_ref, wd_ref, gf_ref, o_ref, h_s, *, final_norm):
    j = pl.program_id(1)
    tm, d = o_ref.shape
    chunks = [slice(r0, r0 + FFN_ROW_CHUNK) for r0 in range(0, tm, FFN_ROW_CHUNK)]

    @pl.when(j == 0)
    def _():
        for rows in chunks:
            h_s[rows, :] = _rms(x_ref[rows, :], g2_ref[...]).astype(BF16)
        o_ref[...] = jnp.zeros_like(o_ref)

    for rows in chunks:
        h = h_s[rows, :]
        gate = _bdot(h, wg_ref[...])
        act = ((gate * jax.nn.sigmoid(gate)) * _bdot(h, wu_ref[...])).astype(BF16)
        for c0 in range(0, d, FFN_DOWN_CHUNK):
            cols = slice(c0, c0 + FFN_DOWN_CHUNK)
            o_ref[rows, cols] += _bdot(act, wd_ref[:, cols])

    @pl.when(j == pl.num_programs(1) - 1)
    def _():
        for rows in chunks:
            y = x_ref[rows, :] + o_ref[rows, :]
            o_ref[rows, :] = _rms(y, gf_ref[...]) if final_norm else y


def _ffn(x, g2, wg, wu, wd, gf, final_norm):
    t, d = x.shape
    fh = wg.shape[-1]
    tm, tf = min(TM_FFN, t), TF_FFN
    return pl.pallas_call(
        functools.partial(_ffn_kernel, final_norm=final_norm),
        grid=(t // tm, fh // tf),
        in_specs=[
            pl.BlockSpec((tm, d), lambda i, j: (i, 0)),
            pl.BlockSpec((1, d), lambda i, j: (0, 0)),
            pl.BlockSpec((d, tf), lambda i, j: (0, j)),
            pl.BlockSpec((d, tf), lambda i, j: (0, j)),
            pl.BlockSpec((tf, d), lambda i, j: (j, 0)),
            pl.BlockSpec((1, d), lambda i, j: (0, 0)),
        ],
        out_specs=pl.BlockSpec((tm, d), lambda i, j: (i, 0)),
        out_shape=jax.ShapeDtypeStruct((t, d), F32),
        scratch_shapes=[pltpu.VMEM((tm, d), BF16)],
        compiler_params=_cparams(("parallel", "arbitrary"), VMEM_LIMIT_FFN),
        name="ffn",
    )(x, g2, wg, wu, wd, gf)


def _pack_kernel(*refs):
    o_ref = refs[-1]
    for piece, w_ref in enumerate(refs[:-1]):
        x = w_ref[...]
        row = lax.broadcasted_iota(jnp.int32, x.shape, 0)
        blk = pl.program_id(1) * PACK_PIECES + piece
        keep = jnp.where(blk == C_GLR // PACK_CB, GLA_LOWRANK, PACK_CB)
        o_ref[:, piece * PACK_CB : (piece + 1) * PACK_CB] = jnp.where(row < keep, x, 0.0).T.astype(BF16)


def _pack_row0(c):
    cb = PACK_CB
    u = GLA_LOWRANK
    front = jnp.where(c < GLR_ORIG // cb, c * (cb // u), c * (cb // u) + 1)
    back = jnp.where(c == C_GLR // cb, GLR_ORIG // u, c * (cb // u) - (N_SMALL - N_ORIG_SMALL) // u)
    return jnp.where(c < C_GLR // cb, front, back) * u


def _pack_w_in(w_in):
    depth, d, d_in = w_in.shape
    n_out = N_SMALL + d_in - N_ORIG_SMALL
    cb = PACK_CB
    assert GLR_ORIG % cb == 0 and C_GLR % cb == 0 and N_SMALL % cb == 0 and n_out % cb == 0
    assert C_GLR == N_ORIG_SMALL - GLA_LOWRANK and C_GLR + cb == N_SMALL
    np_ = PACK_PIECES
    assert n_out % (cb * np_) == 0
    w_t = jnp.swapaxes(w_in, 1, 2)
    return pl.pallas_call(
        _pack_kernel,
        grid=(depth, n_out // (cb * np_)),
        in_specs=[
            pl.BlockSpec((None, pl.Element(cb), pl.Element(d)), lambda i, c, p=p: (i, _pack_row0(c * np_ + p), 0))
            for p in range(np_)
        ],
        out_specs=pl.BlockSpec((None, d, cb * np_), lambda i, c: (i, 0, c)),
        out_shape=jax.ShapeDtypeStruct((depth, d, n_out), BF16),
        compiler_params=_cparams(("parallel", "parallel")),
        name="pack_w_in",
    )(*([w_t] * np_))


def kernel(x, positions, norm1_g, w_in, gla_a2, gla_a_b, gla_norm_g, s5_lambda_re, s5_lambda_im, s5_log_dt, s5_b_re, s5_b_im, s5_c_re, s5_c_im, s5_d, s5_glu_w, s5_glu_b, swa_sinks, w_branch, w_out, norm2_g, w_ffn_gate, w_ffn_up, w_ffn_down, final_norm_g):
    b, l, d = x.shape
    t = b * l
    depth = w_in.shape[0]
    xs = x.reshape(t, d).astype(F32)
    w_all = _pack_w_in(w_in)
    w_br32 = w_branch.reshape(depth, N_BRANCH * BRANCH_WIDTH, d)
    glu_w = s5_glu_w.astype(BF16)
    tabs = _rope_tables(positions)
    gf = final_norm_g.reshape(1, d).astype(F32)
    s5_all = jax.vmap(functools.partial(_s5_params, nch=l // S5_CH))(
        s5_lambda_re, s5_lambda_im, s5_log_dt, s5_b_re, s5_b_im, s5_c_re, s5_c_im, s5_d)
    for i in range(depth):
        g1 = norm1_g[i].reshape(1, d).astype(F32)
        proj = _inproj(xs, g1, w_all, i)
        proj3 = proj.reshape(b, l, N_SMALL)
        o_gla, w_fg, w_fu, w_br, w_o = _gla(proj, b, gla_a2[i], gla_a_b[i], gla_norm_g[i],
                                            (w_ffn_gate, w_ffn_up, w_br32, w_out), i)
        w_br = w_br.reshape(N_BRANCH, BRANCH_WIDTH, d)
        y_s5 = _s5_core(proj, b, [a[i] for a in s5_all])
        o_dil = _dilated(proj3, tabs).reshape(t, BRANCH_WIDTH)
        o_swa, w_fd = _swa(proj3, tabs, swa_sinks[i], (w_ffn_down,), i)
        o_swa = o_swa.reshape(t, BRANCH_WIDTH)
        xs = _merge(xs, g1, o_gla, y_s5, o_dil, o_swa, glu_w, s5_glu_b[i].reshape(1, -1).astype(F32),
                    w_all, w_br, w_o, i)
        xs = _ffn(xs, norm2_g[i].reshape(1, d).astype(F32), w_fg, w_fu, w_fd, gf, i == depth - 1)
    return xs.reshape(b, l, d).astype(x.dtype)
```

```python
import functools
import math

import jax
import jax.numpy as jnp
import numpy as np
from jax import lax
from jax.experimental import pallas as pl
from jax.experimental.pallas import tpu as pltpu

F32 = jnp.float32
BF16 = jnp.bfloat16

D_MODEL = 2048
N_BRANCH = 4
BRANCH_WIDTH = 512
HEAD_DIM = 64
ATT_BLOCK = 128
ROPE_THETA = 500000.0
ROPE_DIM = HEAD_DIM // 4
NORM_EPS = 1e-6
GLA_HEADS = 4
GLA_DK = 64
GLA_DV = BRANCH_WIDTH // GLA_HEADS
GLA_LOWRANK = 16
GLA_TAU = 16.0
S5_GROUP = 16
S5_GROUPS = BRANCH_WIDTH // S5_GROUP
S5_STATE = 64
DIL_CONFIGS = ((128, 1), (512, 4), (2048, 16))
DIL_SPAN = ATT_BLOCK * 16
SWA_HEADS = BRANCH_WIDTH // HEAD_DIM
SWA_KV_HEADS = 2
SWA_WINDOW = 128

LANES = 128
SUBLANES = 8
VMEM_LIMIT = 56 * 1024 * 1024
VMEM_LIMIT_FFN = 60 * 1024 * 1024

C_GQ, C_GK, C_GV, C_GR, C_S5 = 0, 256, 512, 1024, 1536
C_CQ, C_CK, C_CV, C_SQ, C_SK, C_SV, C_GLR = 2048, 2560, 3072, 3584, 4096, 4224, 4352
N_SMALL = 4608
N_ORIG_SMALL = 4368
GLR_ORIG = 1536

TM_PROJ, TN_PROJ = 512, 1536
TM_MERGE, TN_MERGE = 512, 512
TM_FFN, TF_FFN = 1024, 512
FFN_ROW_CHUNK = 512
FFN_DOWN_CHUNK = 512
GLA_CH = 128
GLA_NCS = 4
GLA_SUB = 8
GLA_LEVELS = (8, 16, 32, 64, 128)
PACK_CB = 256
PACK_PIECES = 5
S5_CH = 16
S5_GPB = LANES // S5_GROUP
S5_RT = 2048
S5_TSPLIT = 4
ATT_UNROLL = 8
SWA_UNROLL = 16


def _cparams(sem, vmem_limit=VMEM_LIMIT):
    return pltpu.CompilerParams(dimension_semantics=sem, vmem_limit_bytes=vmem_limit)


def _rms(x, g):
    ms = jnp.mean(x * x, axis=-1, keepdims=True)
    return x * lax.rsqrt(ms + NORM_EPS) * g


def _bdot(a, b):
    return jnp.dot(a, b, preferred_element_type=F32)


def _bdot_nt(a, b):
    return lax.dot_general(a, b, (((1,), (1,)), ((), ())), preferred_element_type=F32)


def _cast_specs(ws, layer, nsteps, step_of):
    ins, outs, shapes = [], [], []
    for w in ws:
        _, r, c = w.shape
        rb = r // nsteps
        assert rb * nsteps == r and rb % (2 * SUBLANES) == 0
        ins.append(pl.BlockSpec((None, rb, c), lambda *g: (layer, step_of(*g), 0)))
        outs.append(pl.BlockSpec((rb, c), lambda *g: (step_of(*g), 0)))
        shapes.append(jax.ShapeDtypeStruct((r, c), BF16))
    return ins, outs, shapes


def _side_cast(in_refs, out_refs):
    for src, dst in zip(in_refs, out_refs):
        dst[...] = src[...].astype(BF16)


def _rope_apply(x, c, sa, sb):
    half = ROPE_DIM // 2
    return x * c + pltpu.roll(x, LANES - half, axis=1) * sa + pltpu.roll(x, half, axis=1) * sb


def _inproj_kernel(x_ref, g_ref, w_ref, o_ref):
    h = _rms(x_ref[...], g_ref[...]).astype(BF16)
    for c0 in range(0, o_ref.shape[1], TN_PROJ):
        cols = slice(c0, c0 + TN_PROJ)
        o_ref[:, cols] = _bdot(h, w_ref[:, cols])


def _inproj(x, g, w_all, layer):
    t, d = x.shape
    n = N_SMALL
    tm = min(TM_PROJ, t)
    assert n % TN_PROJ == 0
    return pl.pallas_call(
        _inproj_kernel,
        grid=(t // tm,),
        in_specs=[
            pl.BlockSpec((tm, d), lambda i: (i, 0)),
            pl.BlockSpec((1, d), lambda i: (0, 0)),
            pl.BlockSpec((None, d, n), lambda i: (layer, 0, 0), pipeline_mode=pl.Buffered(1)),
        ],
        out_specs=pl.BlockSpec((tm, n), lambda i: (i, 0)),
        out_shape=jax.ShapeDtypeStruct((t, n), F32),
        compiler_params=_cparams(("parallel",)),
        name="inproj",
    )(x, g, w_all)


def _rope_kernel(pos_ref, cos_ref, sa_ref, sb_ref):
    pos = pos_ref[0].astype(F32)
    lane = lax.broadcasted_iota(jnp.int32, (SUBLANES, LANES), 1)
    d = lane % HEAD_DIM
    half = ROPE_DIM // 2
    fi = (d % half).astype(F32) / half
    inv = jnp.power(jnp.full((SUBLANES, LANES), ROPE_THETA, F32), -fi)[0:1]
    d1 = d[0:1]
    ang = pos * inv
    c, s = jnp.cos(ang), jnp.sin(ang)
    cos_ref[0] = jnp.where(d1 < ROPE_DIM, c, 1.0)
    sa_ref[0] = jnp.where(d1 < half, -s, 0.0)
    sb_ref[0] = jnp.where((d1 >= half) & (d1 < ROPE_DIM), s, 0.0)


def _rope_tables(positions):
    b, l = positions.shape
    spec = pl.BlockSpec((1, l, LANES), lambda i: (i, 0, 0))
    shp = jax.ShapeDtypeStruct((b, l, LANES), F32)
    return pl.pallas_call(
        _rope_kernel,
        grid=(b,),
        in_specs=[pl.BlockSpec((1, l, 1), lambda i: (i, 0, 0))],
        out_specs=[spec, spec, spec],
        out_shape=[shp, shp, shp],
        compiler_params=_cparams(("parallel",)),
        name="rope_tables",
    )(positions.reshape(b, l, 1))


def _band_bias(max_dist):
    t = ATT_BLOCK
    assert 0 < max_dist <= t
    qi = np.arange(2 * t)[:, None] % t
    kj = np.arange(2 * t)[None, :]
    own = (kj >= t) & (kj - t <= qi)
    prev = (kj < t) & (t + qi - kj <= max_dist)
    ninf = np.float32(-np.inf)
    return np.stack([np.where(own | prev, 0.0, ninf), np.where(own, 0.0, ninf)]).astype(np.float32)


def _band_block(q, kp, ko, vp, vo, bias):
    t = ATT_BLOCK
    lane = lax.broadcasted_iota(jnp.int32, (t, LANES), 1)
    in_a = lane < HEAD_DIM
    q2 = jnp.concatenate([jnp.where(in_a, q, 0.0), jnp.where(in_a, 0.0, q)], axis=0).astype(BF16)
    s = _bdot_nt(q2, jnp.concatenate([kp, ko], axis=0).astype(BF16)) + bias
    m = jnp.max(s, axis=-1, keepdims=True)
    p = jnp.exp(s - m).astype(BF16)
    ones = jnp.ones((2 * t, LANES), BF16)
    vcat = jnp.concatenate([jnp.concatenate([vp, vo], axis=0).astype(BF16), ones], axis=1)
    od = _bdot(p, vcat)
    o = jnp.where(in_a, od[:t, :LANES], od[t:, :LANES])
    df = jnp.where(in_a, od[:t, LANES:], od[t:, LANES:])
    mf = jnp.where(in_a, m[:t], m[t:])
    return mf, df, o


def _dil_kernel(q_ref, k_ref, v_ref, c_ref, sa_ref, sb_ref, bias_ref, o_ref, qs, ks, qd, kd, vd, ms, ds, os_):
    l = q_ref.shape[1]
    c, sa, sb = c_ref[0], sa_ref[0], sb_ref[0]
    qs[...] = _rope_apply(q_ref[0], c, sa, sb) * (HEAD_DIM**-0.5)
    ks[...] = _rope_apply(k_ref[0], c, sa, sb)
    t = ATT_BLOCK
    nblk = DIL_SPAN // t
    dw = DIL_SPAN // t
    per = l // dw
    for r in range(dw):
        src, dst = pl.ds(r, per, stride=dw), slice(r * per, (r + 1) * per)
        qd[dst, :] = qs[src, :]
        kd[dst, :] = ks[src, :]
        vd[dst, :] = v_ref[0, src, :]

    for sblk in range(l // DIL_SPAN):
        base = sblk * DIL_SPAN
        for ci, (_, dil) in enumerate(DIL_CONFIGS):

            def body_wide(r, carry, ci=ci, sblk=sblk):
                rows = pl.ds(pl.multiple_of(r * per + sblk * t, t), t)
                prow = pl.ds(pl.multiple_of(r * per + max(sblk - 1, 0) * t, t), t)
                mf, df, o = _band_block(qd[rows, :], kd[prow, :], kd[rows, :], vd[prow, :], vd[rows, :],
                                        bias_ref[0 if sblk > 0 else 1])
                lrow = pl.ds(r, t, stride=dw)
                ms[ci, lrow, :] = mf
                ds[ci, lrow, :] = df
                os_[ci, lrow, :] = o
                return carry

            if dil == dw:
                lax.fori_loop(0, nblk, body_wide, 0, unroll=ATT_UNROLL)
                continue

            def body(idx, carry, dil=dil, ci=ci, base=base):
                r = idx % dil
                n = idx // dil
                loc = r + dil * t * n
                start = base + loc
                has_prev = start >= dil * t
                pstart = jnp.where(has_prev, start - dil * t, start)
                if dil == 1:
                    start = pl.multiple_of(start, t)
                    pstart = pl.multiple_of(pstart, t)
                    loc = pl.multiple_of(loc, t)
                    rows, prow, lrow = pl.ds(start, t), pl.ds(pstart, t), pl.ds(loc, t)
                else:
                    rows = pl.ds(start, t, stride=dil)
                    prow = pl.ds(pstart, t, stride=dil)
                    lrow = pl.ds(loc, t, stride=dil)
                mf, df, o = _band_block(
                    qs[rows, :], ks[prow, :], ks[rows, :], v_ref[0, prow, :], v_ref[0, rows, :],
                    bias_ref[jnp.where(has_prev, 0, 1)],
                )
                ms[ci, lrow, :] = mf
                ds[ci, lrow, :] = df
                os_[ci, lrow, :] = o
                return carry

            lax.fori_loop(0, nblk, body, 0, unroll=ATT_UNROLL)

        m0, m1, m2 = ms[0], ms[1], ms[2]
        mx = jnp.maximum(jnp.maximum(m0, m1), m2)
        w0, w1, w2 = jnp.exp(m0 - mx), jnp.exp(m1 - mx), jnp.exp(m2 - mx)
        den = w0 * ds[0] + w1 * ds[1] + w2 * ds[2]
        num = w0 * os_[0] + w1 * os_[1] + w2 * os_[2]
        o_ref[0, base : base + DIL_SPAN, :] = (num / den).astype(o_ref.dtype)


def _dilated(proj3, tabs):
    b, l, _ = proj3.shape
    assert l % DIL_SPAN == 0
    assert all(window // dil == ATT_BLOCK for window, dil in DIL_CONFIGS)
    bias = _band_bias(ATT_BLOCK)

    def col(c0):
        return pl.BlockSpec((1, l, LANES), lambda i, p: (i, 0, c0 // LANES + p))

    tab = pl.BlockSpec((1, l, LANES), lambda i, p: (i, 0, 0))
    return pl.pallas_call(
        _dil_kernel,
        grid=(b, BRANCH_WIDTH // LANES),
        in_specs=[col(C_CQ), col(C_CK), col(C_CV), tab, tab, tab, pl.BlockSpec(bias.shape, lambda i, p: (0, 0, 0))],
        out_specs=pl.BlockSpec((1, l, LANES), lambda i, p: (i, 0, p)),
        out_shape=jax.ShapeDtypeStruct((b, l, BRANCH_WIDTH), BF16),
        scratch_shapes=[pltpu.VMEM((l, LANES), F32)] * 5
        + [pltpu.VMEM((len(DIL_CONFIGS), DIL_SPAN, LANES), F32)] * 3,
        compiler_params=_cparams(("parallel", "parallel")),
        name="dilated_attn",
    )(proj3, proj3, proj3, *tabs, jnp.asarray(bias))


def _swa_kernel(*refs, n_cast):
    q_ref, k_ref, v_ref, c_ref, sa_ref, sb_ref, bias_ref, sink_ref = refs[:8]
    o_ref = refs[8 + n_cast]
    qs, ks, vs = refs[-3:]
    _side_cast(refs[8 : 8 + n_cast], refs[9 + n_cast : 9 + 2 * n_cast])
    l = q_ref.shape[1]
    c, sa, sb = c_ref[0], sa_ref[0], sb_ref[0]
    qs[...] = _rope_apply(q_ref[0], c, sa, sb) * (HEAD_DIM**-0.5)
    g = pl.program_id(1) // (SWA_HEADS // SWA_KV_HEADS // 2)
    lane = lax.broadcasted_iota(jnp.int32, (l, LANES), 1)
    keep = (lane // HEAD_DIM) == g
    k, v = _rope_apply(k_ref[0], c, sa, sb), v_ref[0]
    ks[...] = jnp.where(keep, k, pltpu.roll(k, HEAD_DIM, axis=1))
    vs[...] = jnp.where(keep, v, pltpu.roll(v, HEAD_DIM, axis=1))

    sink = sink_ref[0]
    t = ATT_BLOCK

    def body(n, carry):
        start = pl.multiple_of(n * t, t)
        has_prev = n > 0
        pstart = pl.multiple_of(jnp.where(has_prev, start - t, start), t)
        rows, prow = pl.ds(start, t), pl.ds(pstart, t)
        mf, df, o = _band_block(qs[rows, :], ks[prow, :], ks[rows, :], vs[prow, :], vs[rows, :],
                                bias_ref[jnp.where(has_prev, 0, 1)])
        lse = mf + jnp.log(df)
        o_ref[0, rows, :] = (o / df * jax.nn.sigmoid(lse - sink)).astype(o_ref.dtype)
        return carry

    lax.fori_loop(0, l // t, body, 0, unroll=SWA_UNROLL)


def _swa(proj3, tabs, sinks, cast=(), layer=0):
    b, l, _ = proj3.shape
    npair = BRANCH_WIDTH // LANES
    sink_l = jnp.repeat(sinks.astype(F32), HEAD_DIM).reshape(npair, 1, LANES)
    bias = _band_bias(SWA_WINDOW - 1)
    tab = pl.BlockSpec((1, l, LANES), lambda i, p: (i, 0, 0))
    c_in, c_out, c_shape = _cast_specs(cast, layer, b * npair, lambda i, p: i * npair + p)
    return pl.pallas_call(
        functools.partial(_swa_kernel, n_cast=len(cast)),
        grid=(b, npair),
        in_specs=[
            pl.BlockSpec((1, l, LANES), lambda i, p: (i, 0, C_SQ // LANES + p)),
            pl.BlockSpec((1, l, LANES), lambda i, p: (i, 0, C_SK // LANES)),
            pl.BlockSpec((1, l, LANES), lambda i, p: (i, 0, C_SV // LANES)),
            tab,
            tab,
            tab,
            pl.BlockSpec(bias.shape, lambda i, p: (0, 0, 0)),
            pl.BlockSpec((1, 1, LANES), lambda i, p: (p, 0, 0)),
        ]
        + c_in,
        out_specs=[pl.BlockSpec((1, l, LANES), lambda i, p: (i, 0, p))] + c_out,
        out_shape=[jax.ShapeDtypeStruct((b, l, BRANCH_WIDTH), BF16)] + c_shape,
        scratch_shapes=[pltpu.VMEM((l, LANES), F32)] * 3,
        compiler_params=_cparams(("parallel", "parallel")),
        name="swa_attn",
    )(proj3, proj3, proj3, *tabs, jnp.asarray(bias), sink_l, *cast)


@functools.lru_cache(maxsize=None)
def _gla_consts():
    ch = GLA_CH
    r = np.arange(ch)[:, None]
    j = np.arange(ch)[None, :]
    dstack = (j <= r).astype(np.float32)
    lm = []
    for m in GLA_LEVELS[:-1]:
        mk = (((r // m) % 2 == 1) & ((j // m) == (r // m) - 1)).astype(np.float32)
        lm.append(np.tile(mk, (GLA_HEADS, 1)))
    lmask = np.stack(lm)
    dmask = np.tile(((r // GLA_SUB) == (j // GLA_SUB)).astype(np.float32), (1, GLA_HEADS))
    nsub = ch // GLA_SUB
    e = np.zeros((GLA_SUB, GLA_HEADS * GLA_DK, GLA_HEADS * ch), np.float32)
    for u in range(GLA_SUB):
        for h in range(GLA_HEADS):
            for s in range(nsub):
                e[u, h * GLA_DK : (h + 1) * GLA_DK, h * ch + GLA_SUB * s + u] = 1.0
    rr = np.arange(GLA_HEADS * GLA_DV)[:, None]
    cc = np.arange(GLA_HEADS * GLA_DK)[None, :]
    bd = ((rr // GLA_DV) == (cc // GLA_DK)).astype(np.float32)
    return dstack, lmask, dmask, e, bd


def _split3(x):
    hi = x.astype(BF16)
    r1 = x - hi.astype(F32)
    mid = r1.astype(BF16)
    lo = (r1 - mid.astype(F32)).astype(BF16)
    return hi, mid, lo


def _bcast_grp(x, m, u):
    r, w = x.shape
    x3 = x.reshape(r // m, m, w)
    return jnp.broadcast_to(x3[:, u : u + 1, :], x3.shape).reshape(r, w)


def _bcast_sub(x, u):
    return _bcast_grp(x, GLA_SUB, u)


def _gla_kernel(*refs, n_cast):
    (q_ref, k_ref, v_ref, r_ref, glr_ref, a2_ref, ab_ref, ng_ref, dst_ref, lmask_ref, dmask_ref, e_ref,
     bd_ref) = refs[:13]
    o_ref = refs[13 + n_cast]
    st_ref, arep_s, cum_s, excl_s = refs[-4:]
    _side_cast(refs[13 : 13 + n_cast], refs[14 + n_cast : 14 + 2 * n_cast])
    ch = GLA_CH
    rs = q_ref.shape[0]
    ncs = rs // ch
    nlev = len(GLA_LEVELS)
    hk = GLA_HEADS * GLA_DK

    @pl.when(pl.program_id(1) == 0)
    def _():
        st_ref[...] = jnp.zeros_like(st_ref)

    q = q_ref[...] * (GLA_DK**-0.5)
    k = k_ref[...]
    z = _bdot(glr_ref[...].astype(BF16), a2_ref[...]) + ab_ref[...]
    g = (jnp.minimum(z, 0.0) - jnp.log(1.0 + jnp.exp(-jnp.abs(z)))) * (1.0 / GLA_TAU)
    g3 = _split3(g)
    dst = dst_ref[...]

    cums, excl = [], []
    for c in range(ncs):
        rows = slice(c * ch, (c + 1) * ch)
        cums.append(_bdot(dst, g3[0][rows]) + _bdot(dst, g3[1][rows]) + _bdot(dst, g3[2][rows]))
        excl.append(cums[c] - g[rows])

    def eq(cum_c, excl_c, li):
        return cum_c - _bcast_grp(excl_c, GLA_LEVELS[li], 0)

    def ek(cum_c, li):
        m = GLA_LEVELS[li]
        return _bcast_grp(cum_c, m, m - 1) - cum_c

    cs = jnp.concatenate([eq(cums[c], excl[c], 0) for c in range(ncs)], axis=0)
    tsub = lax.broadcasted_iota(jnp.int32, (rs, hk), 0) % GLA_SUB
    arep = jnp.zeros((rs, GLA_HEADS * ch), F32)
    for u in range(GLA_SUB):
        dec = jnp.exp(jnp.where(tsub >= u, cs - _bcast_sub(cs, u), -jnp.inf))
        p = q * _bcast_sub(k, u) * dec
        arep = arep + _bdot(p.astype(BF16), e_ref[u])
    arep_s[...] = arep * jnp.concatenate([dmask_ref[...]] * ncs, axis=0)
    cum_s[...] = jnp.concatenate(cums, axis=0)
    excl_s[...] = jnp.concatenate(excl, axis=0)

    lane_k = lax.broadcasted_iota(jnp.int32, (ch, hk), 1) // GLA_DK

    def chunk(c, carry):
        rows = pl.ds(pl.multiple_of(c * ch, ch), ch)
        qc, kc = q_ref[rows, :] * (GLA_DK**-0.5), k_ref[rows, :]
        vc = v_ref[rows, :]
        cum_c, excl_c, arep_c = cum_s[rows, :], excl_s[rows, :], arep_s[rows, :]
        aoff = jnp.zeros((GLA_HEADS * ch, ch), F32)
        for li in range(nlev - 1):
            qe = qc * jnp.exp(eq(cum_c, excl_c, li))
            ke = (kc * jnp.exp(ek(cum_c, li))).astype(BF16)
            qst = jnp.concatenate([jnp.where(lane_k == h, qe, 0.0) for h in range(GLA_HEADS)], axis=0).astype(BF16)
            aoff = aoff + _bdot_nt(qst, ke) * lmask_ref[li]
        cum = eq(cum_c, excl_c, nlev - 1)
        st = st_ref[...]
        o_inter = _bdot_nt((qc * jnp.exp(cum)).astype(BF16), st.astype(BF16))
        vb = vc.astype(BF16)
        outs = []
        for h in range(GLA_HEADS):
            a_h = aoff[h * ch : (h + 1) * ch] + arep_c[:, h * ch : (h + 1) * ch]
            o_h = _bdot(a_h.astype(BF16), vb[:, h * GLA_DV : (h + 1) * GLA_DV]) + o_inter[:, h * GLA_DV : (h + 1) * GLA_DV]
            ms = jnp.mean(o_h * o_h, axis=-1, keepdims=True)
            outs.append(o_h * lax.rsqrt(ms + NORM_EPS))
        o = jnp.concatenate(outs, axis=1) * ng_ref[...]
        rc = r_ref[rows, :]
        o_ref[rows, :] = (o * (rc * jax.nn.sigmoid(rc))).astype(o_ref.dtype)
        ke_last = (kc * jnp.exp(ek(cum_c, nlev - 1))).astype(BF16)
        kv = _bdot(vc.T.astype(BF16), ke_last)
        st_ref[...] = st * jnp.exp(cum[ch - 1 : ch, :]) + kv * bd_ref[...]
        return carry

    lax.fori_loop(0, ncs, chunk, 0)


def _gla(proj, b, a2, ab, ng, cast=(), layer=0):
    t = proj.shape[0]
    l = t // b
    rs = GLA_CH * GLA_NCS
    assert l % rs == 0
    ns = l // rs
    dstack, lmask, dmask, e, bd = _gla_consts()
    c_in, c_out, c_shape = _cast_specs(cast, layer, b * ns, lambda i, s: i * ns + s)
    a2p = jnp.zeros((LANES, GLA_HEADS * GLA_DK), BF16).at[:GLA_LOWRANK].set(a2.astype(BF16))

    def rowblk(w, c0):
        return pl.BlockSpec((rs, w), lambda i, s: (i * ns + s, c0 // w))

    def full(shape):
        nd = len(shape)
        return pl.BlockSpec(shape, lambda i, s: (0,) * nd)

    hk, hv = GLA_HEADS * GLA_DK, GLA_HEADS * GLA_DV
    return pl.pallas_call(
        functools.partial(_gla_kernel, n_cast=len(cast)),
        grid=(b, ns),
        in_specs=[
            rowblk(hk, C_GQ),
            rowblk(hk, C_GK),
            rowblk(hv, C_GV),
            rowblk(hv, C_GR),
            rowblk(LANES, C_GLR),
            full((LANES, hk)),
            full((1, hk)),
            full((1, hv)),
            full(dstack.shape),
            full(lmask.shape),
            full(dmask.shape),
            full(e.shape),
            full(bd.shape),
        ]
        + c_in,
        out_specs=[pl.BlockSpec((rs, hv), lambda i, s: (i * ns + s, 0))] + c_out,
        out_shape=[jax.ShapeDtypeStruct((t, hv), BF16)] + c_shape,
        scratch_shapes=[pltpu.VMEM((hv, hk), F32), pltpu.VMEM((rs, GLA_HEADS * GLA_CH), F32),
                        pltpu.VMEM((rs, hk), F32), pltpu.VMEM((rs, hk), F32)],
        compiler_params=_cparams(("parallel", "arbitrary")),
        name="gla",
    )(
        proj, proj, proj, proj, proj, a2p, ab.reshape(1, hk).astype(F32), ng.reshape(1, hv).astype(F32),
        jnp.asarray(dstack, BF16), jnp.asarray(lmask), jnp.asarray(dmask), jnp.asarray(e, BF16), jnp.asarray(bd),
        *cast,
    )


def _s5_params(lam_re, lam_im, log_dt, b_re, b_im, c_re, c_im, d, nch):
    f = F32
    cs = S5_CH
    dt = jnp.exp(log_dt.astype(f))[:, None]
    lr, li = lam_re.astype(f), lam_im.astype(f)
    mag = jnp.exp(lr * dt)
    ab_re, ab_im = mag * jnp.cos(li * dt), mag * jnp.sin(li * dt)
    den = lr * lr + li * li
    z_re = ((ab_re - 1.0) * lr + ab_im * li) / den
    z_im = (ab_im * lr - (ab_re - 1.0) * li) / den
    br, bi = b_re.astype(f), b_im.astype(f)
    bb_re = z_re[..., None] * br - z_im[..., None] * bi
    bb_im = z_re[..., None] * bi + z_im[..., None] * br

    def apow(p):
        p = jnp.asarray(p, f)[:, None, None]
        m = jnp.exp(p * (lr * dt))
        return m * jnp.cos(p * (li * dt)), m * jnp.sin(p * (li * dt))

    cr, ci = c_re.astype(f), c_im.astype(f)
    p_re, p_im = apow(np.arange(cs + 1))
    nb, gpb = S5_GROUPS // S5_GPB, S5_GPB
    hw = gpb * S5_STATE
    eye = jnp.eye(gpb, dtype=f)

    def b_blockdiag(x):
        x = x.reshape(nb, gpb, S5_STATE, S5_GROUP).transpose(0, 1, 3, 2)
        x = (x[:, :, :, None, :] * eye[None, :, None, :, None]).reshape(nb, LANES, hw)
        return jnp.concatenate([x, x], axis=-1)

    def c_blockdiag(x):
        x = x.reshape(nb, gpb, S5_GROUP, S5_STATE).transpose(0, 1, 3, 2)
        return (x[:, :, :, None, :] * eye[None, :, None, :, None]).reshape(nb, hw, LANES)

    def lanes(a, b_):
        p = a.shape[0]
        return jnp.concatenate([a.reshape(p, nb, hw), b_.reshape(p, nb, hw)], axis=-1).transpose(1, 0, 2)

    rev = np.arange(cs - 1, -1, -1)
    pr, pi = p_re[rev], p_im[rev]
    pa, pb = lanes(pr, pi), lanes(-pi, pr)
    acol_re = jnp.broadcast_to(ab_re.reshape(nb, hw, 1), (nb, hw, LANES))
    acol_im = jnp.broadcast_to(ab_im.reshape(nb, hw, 1), (nb, hw, LANES))
    nstep = max(1, int(math.log2(nch)))
    s_re, s_im = apow(cs * (2 ** np.arange(nstep)))
    ar = s_re.reshape(nstep, nb, hw).transpose(1, 0, 2)
    ai = s_im.reshape(nstep, nb, hw).transpose(1, 0, 2)
    dd = d.astype(f).reshape(nb, 1, LANES)
    return (b_blockdiag(bb_re), b_blockdiag(bb_im), c_blockdiag(cr), c_blockdiag(ci), acol_re, acol_im,
            pa, pb, ar, ai, dd)


def _s5_kernel(u_ref, br_ref, bi_ref, cr_ref, ci_ref, acr_ref, aci_ref, pa_ref, pb_ref, ar_ref, ai_ref, d_ref,
               y_ref, mb_s, mc_s, kc_s, *, nchb):
    cs = S5_CH
    t = u_ref.shape[0]
    nch = t // cs
    hw = S5_GPB * S5_STATE
    br, bi = br_ref[0], bi_ref[0]
    pa, pb = pa_ref[0], pb_ref[0]
    for j in range(cs):
        mb_s[j * LANES : (j + 1) * LANES, :] = (br * pa[j : j + 1] + bi * pb[j : j + 1]).astype(BF16)
    x_re, x_im = cr_ref[0], ci_ref[0]
    kc = _bdot(mb_s[...], jnp.concatenate([x_re, -x_im], axis=0).astype(BF16)).astype(BF16)
    a_re, a_im = acr_ref[0], aci_ref[0]
    for tt in range(cs):
        x_re, x_im = x_re * a_re - x_im * a_im, x_re * a_im + x_im * a_re
        mc_s[:hw, tt * LANES : (tt + 1) * LANES] = x_re.astype(BF16)
        mc_s[hw:, tt * LANES : (tt + 1) * LANES] = (-x_im).astype(BF16)
    gl = cs // S5_TSPLIT * LANES
    for qq in range(S5_TSPLIT):
        kc_s[:, qq * LANES : (qq + 1) * LANES] = kc[(S5_TSPLIT - 1 - qq) * gl : (S5_TSPLIT - qq) * gl]
    ucat = jnp.concatenate([u_ref[pl.ds(j, nch, stride=cs), :] for j in range(cs)], axis=1).astype(BF16)
    e_all = _bdot(ucat, mb_s[...])
    cidx = lax.broadcasted_iota(jnp.int32, (nch, LANES), 0) % nchb
    ar, ai = ar_ref[0], ai_ref[0]
    hp_re, hp_im = [], []
    for g in range(hw // LANES):
        lo = g * LANES
        e_re, e_im = e_all[:, lo : lo + LANES], e_all[:, hw + lo : hw + lo + LANES]
        for kstep in range(ar.shape[0]):
            s = 2**kstep
            if s >= nchb:
                break
            s_re = jnp.where(cidx >= s, pltpu.roll(e_re, s, axis=0), 0.0)
            s_im = jnp.where(cidx >= s, pltpu.roll(e_im, s, axis=0), 0.0)
            k_re, k_im = ar[kstep : kstep + 1, lo : lo + LANES], ai[kstep : kstep + 1, lo : lo + LANES]
            e_re, e_im = e_re + s_re * k_re - s_im * k_im, e_im + s_re * k_im + s_im * k_re
        hp_re.append(jnp.where(cidx >= 1, pltpu.roll(e_re, 1, axis=0), 0.0).astype(BF16))
        hp_im.append(jnp.where(cidx >= 1, pltpu.roll(e_im, 1, axis=0), 0.0).astype(BF16))
    ycar = _bdot(jnp.concatenate(hp_re + hp_im, axis=1), mc_s[...])
    for tt in range(cs):
        y_ref[pl.ds(tt, nch, stride=cs), :] = ycar[:, tt * LANES : (tt + 1) * LANES]

    rt = min(S5_RT, t)
    rmod = lax.broadcasted_iota(jnp.int32, (rt, LANES), 0) % cs
    dvec = d_ref[0]
    glag = cs // S5_TSPLIT

    def tile(i, carry):
        rows = pl.ds(pl.multiple_of(i * rt, rt), rt)
        ut = u_ref[rows, :]
        ush = [jnp.where(rmod >= tau, pltpu.roll(ut, tau, axis=0), 0.0) for tau in range(glag - 1, 0, -1)] + [ut]
        zz = _bdot(jnp.concatenate(ush, axis=1).astype(BF16), kc_s[...])
        acc = y_ref[rows, :] + dvec * ut + zz[:, :LANES]
        for qq in range(1, S5_TSPLIT):
            part = pltpu.roll(zz[:, qq * LANES : (qq + 1) * LANES], qq * glag, axis=0)
            acc = acc + jnp.where(rmod >= qq * glag, part, 0.0)
        y_ref[rows, :] = acc
        return carry

    lax.fori_loop(0, t // rt, tile, 0)


def _s5_core(proj, b, params):
    t = proj.shape[0]
    nchb = t // b // S5_CH
    sw = S5_GPB * 2 * S5_STATE

    def bspec(a):
        return pl.BlockSpec((1,) + a.shape[1:], lambda p: (p, 0, 0))

    return pl.pallas_call(
        functools.partial(_s5_kernel, nchb=nchb),
        grid=(S5_GROUPS // S5_GPB,),
        in_specs=[pl.BlockSpec((t, LANES), lambda p: (0, C_S5 // LANES + p))] + [bspec(a) for a in params],
        out_specs=pl.BlockSpec((t, LANES), lambda p: (0, p)),
        out_shape=jax.ShapeDtypeStruct((t, BRANCH_WIDTH), F32),
        scratch_shapes=[
            pltpu.VMEM((S5_CH * LANES, sw), BF16),
            pltpu.VMEM((sw, S5_CH * LANES), BF16),
            pltpu.VMEM((S5_CH // S5_TSPLIT * LANES, S5_TSPLIT * LANES), BF16),
        ],
        compiler_params=_cparams(("parallel",)),
        name="s5_core",
    )(proj, *params)


def _merge_kernel(
    x_ref, g1_ref, gla_ref, ys5_ref, dil_ref, swa_ref, gluw_ref, glub_ref, wg0, wg1, wg2, wg3, wb_ref, wo_ref,
    o_ref, h_s, s5_s,
):
    j = pl.program_id(1)

    @pl.when(j == 0)
    def _():
        h_s[...] = _rms(x_ref[...], g1_ref[...]).astype(BF16)
        zz = jax.nn.gelu(ys5_ref[...])
        gate = jax.nn.sigmoid(_bdot(zz.astype(BF16), gluw_ref[...]) + glub_ref[...])
        s5_s[...] = (zz * gate).astype(BF16)
        o_ref[...] = jnp.zeros_like(o_ref)

    h = h_s[...]
    branches = (gla_ref[...], s5_s[...], dil_ref[...], swa_ref[...])
    mixed = None
    for m, (wg, br) in enumerate(zip((wg0, wg1, wg2, wg3), branches)):
        term = jax.nn.sigmoid(_bdot(h, wg[...])) * _bdot(br, wb_ref[m])
        mixed = term if mixed is None else mixed + term
    mixed = mixed.astype(BF16)
    d = o_ref.shape[1]
    for c0 in range(0, d, FFN_DOWN_CHUNK):
        cols = slice(c0, c0 + FFN_DOWN_CHUNK)
        o_ref[:, cols] += _bdot(mixed, wo_ref[:, cols])

    @pl.when(j == pl.num_programs(1) - 1)
    def _():
        o_ref[...] = x_ref[...] + o_ref[...]


def _merge(x, g1, o_gla, y_s5, o_dil, o_swa, gluw, glub, w_all, wb, wo, layer):
    t, d = x.shape
    tm, tn = min(TM_MERGE, t), TN_MERGE
    nj = d // tn
    bw = BRANCH_WIDTH

    def rowblk(w):
        return pl.BlockSpec((tm, w), lambda i, j: (i, 0))

    def gate_spec(m):
        return pl.BlockSpec((None, d, tn), lambda i, j, m=m: (layer, 0, N_SMALL // tn + m * nj + j))

    return pl.pallas_call(
        _merge_kernel,
        grid=(t // tm, nj),
        in_specs=[
            rowblk(d),
            pl.BlockSpec((1, d), lambda i, j: (0, 0)),
            rowblk(bw),
            rowblk(bw),
            rowblk(bw),
            rowblk(bw),
            pl.BlockSpec((None, bw, bw), lambda i, j: (layer, 0, 0)),
            pl.BlockSpec((1, bw), lambda i, j: (0, 0)),
            gate_spec(0),
            gate_spec(1),
            gate_spec(2),
            gate_spec(3),
            pl.BlockSpec((N_BRANCH, bw, tn), lambda i, j: (0, 0, j)),
            pl.BlockSpec((tn, d), lambda i, j: (j, 0)),
        ],
        out_specs=rowblk(d),
        out_shape=jax.ShapeDtypeStruct((t, d), F32),
        scratch_shapes=[pltpu.VMEM((tm, d), BF16), pltpu.VMEM((tm, bw), BF16)],
        compiler_params=_cparams(("parallel", "arbitrary")),
        name="merge",
    )(x, g1, o_gla, y_s5, o_dil, o_swa, gluw, glub, w_all, w_all, w_all, w_all, wb, wo)


def _ffn_kernel(x_ref, g2_ref, wg_ref, wu_ref, wd_ref, gf_ref, o_ref, h_s, *, final_norm):
    j = pl.program_id(1)
    tm, d = o_ref.shape
    chunks = [slice(r0, r0 + FFN_ROW_CHUNK) for r0 in range(0, tm, FFN_ROW_CHUNK)]

    @pl.when(j == 0)
    def _():
        for rows in chunks:
            h_s[rows, :] = _rms(x_ref[rows, :], g2_ref[...]).astype(BF16)
        o_ref[...] = jnp.zeros_like(o_ref)

    for rows in chunks:
        h = h_s[rows, :]
        gate = _bdot(h, wg_ref[...])
        act = ((gate * jax.nn.sigmoid(gate)) * _bdot(h, wu_ref[...])).astype(BF16)
        for c0 in range(0, d, FFN_DOWN_CHUNK):
            cols = slice(c0, c0 + FFN_DOWN_CHUNK)
            o_ref[rows, cols] += _bdot(act, wd_ref[:, cols])

    @pl.when(j == pl.num_programs(1) - 1)
    def _():
        for rows in chunks:
            y = x_ref[rows, :] + o_ref[rows, :]
            o_ref[rows, :] = _rms(y, gf_ref[...]) if final_norm else y


def _ffn(x, g2, wg, wu, wd, gf, final_norm):
    t, d = x.shape
    fh = wg.shape[-1]
    tm, tf = min(TM_FFN, t), TF_FFN
    return pl.pallas_call(
        functools.partial(_ffn_kernel, final_norm=final_norm),
        grid=(t // tm, fh // tf),
        in_specs=[
            pl.BlockSpec((tm, d), lambda i, j: (i, 0)),
            pl.BlockSpec((1, d), lambda i, j: (0, 0)),
            pl.BlockSpec((d, tf), lambda i, j: (0, j)),
            pl.BlockSpec((d, tf), lambda i, j: (0, j)),
            pl.BlockSpec((tf, d), lambda i, j: (j, 0)),
            pl.BlockSpec((1, d), lambda i, j: (0, 0)),
        ],
        out_specs=pl.BlockSpec((tm, d), lambda i, j: (i, 0)),
        out_shape=jax.ShapeDtypeStruct((t, d), F32),
        scratch_shapes=[pltpu.VMEM((tm, d), BF16)],
        compiler_params=_cparams(("parallel", "arbitrary"), VMEM_LIMIT_FFN),
        name="ffn",
    )(x, g2, wg, wu, wd, gf)


def _pack_kernel(*refs):
    o_ref = refs[-1]
    for piece, w_ref in enumerate(refs[:-1]):
        x = w_ref[...]
        row = lax.broadcasted_iota(jnp.int32, x.shape, 0)
        blk = pl.program_id(1) * PACK_PIECES + piece
        keep = jnp.where(blk == C_GLR // PACK_CB, GLA_LOWRANK, PACK_CB)
        o_ref[:, piece * PACK_CB : (piece + 1) * PACK_CB] = jnp.where(row < keep, x, 0.0).T.astype(BF16)


def _pack_row0(c):
    cb = PACK_CB
    u = GLA_LOWRANK
    front = jnp.where(c < GLR_ORIG // cb, c * (cb // u), c * (cb // u) + 1)
    back = jnp.where(c == C_GLR // cb, GLR_ORIG // u, c * (cb // u) - (N_SMALL - N_ORIG_SMALL) // u)
    return jnp.where(c < C_GLR // cb, front, back) * u


def _pack_w_in(w_in):
    depth, d, d_in = w_in.shape
    n_out = N_SMALL + d_in - N_ORIG_SMALL
    cb = PACK_CB
    assert GLR_ORIG % cb == 0 and C_GLR % cb == 0 and N_SMALL % cb == 0 and n_out % cb == 0
    assert C_GLR == N_ORIG_SMALL - GLA_LOWRANK and C_GLR + cb == N_SMALL
    np_ = PACK_PIECES
    assert n_out % (cb * np_) == 0
    w_t = jnp.swapaxes(w_in, 1, 2)
    return pl.pallas_call(
        _pack_kernel,
        grid=(depth, n_out // (cb * np_)),
        in_specs=[
            pl.BlockSpec((None, pl.Element(cb), pl.Element(d)), lambda i, c, p=p: (i, _pack_row0(c * np_ + p), 0))
            for p in range(np_)
        ],
        out_specs=pl.BlockSpec((None, d, cb * np_), lambda i, c: (i, 0, c)),
        out_shape=jax.ShapeDtypeStruct((depth, d, n_out), BF16),
        compiler_params=_cparams(("parallel", "parallel")),
        name="pack_w_in",
    )(*([w_t] * np_))


def kernel(x, positions, norm1_g, w_in, gla_a2, gla_a_b, gla_norm_g, s5_lambda_re, s5_lambda_im, s5_log_dt, s5_b_re, s5_b_im, s5_c_re, s5_c_im, s5_d, s5_glu_w, s5_glu_b, swa_sinks, w_branch, w_out, norm2_g, w_ffn_gate, w_ffn_up, w_ffn_down, final_norm_g):
    b, l, d = x.shape
    t = b * l
    depth = w_in.shape[0]
    xs = x.reshape(t, d).astype(F32)
    w_all = _pack_w_in(w_in)
    w_br32 = w_branch.reshape(depth, N_BRANCH * BRANCH_WIDTH, d)
    glu_w = s5_glu_w.astype(BF16)
    tabs = _rope_tables(positions)
    gf = final_norm_g.reshape(1, d).astype(F32)
    s5_all = jax.vmap(functools.partial(_s5_params, nch=l // S5_CH))(
        s5_lambda_re, s5_lambda_im, s5_log_dt, s5_b_re, s5_b_im, s5_c_re, s5_c_im, s5_d)
    for i in range(depth):
        g1 = norm1_g[i].reshape(1, d).astype(F32)
        proj = _inproj(xs, g1, w_all, i)
        proj3 = proj.reshape(b, l, N_SMALL)
        o_gla, w_fg, w_fu, w_br, w_o = _gla(proj, b, gla_a2[i], gla_a_b[i], gla_norm_g[i],
                                            (w_ffn_gate, w_ffn_up, w_br32, w_out), i)
        w_br = w_br.reshape(N_BRANCH, BRANCH_WIDTH, d)
        y_s5 = _s5_core(proj, b, [a[i] for a in s5_all])
        o_dil = _dilated(proj3, tabs).reshape(t, BRANCH_WIDTH)
        o_swa, w_fd = _swa(proj3, tabs, swa_sinks[i], (w_ffn_down,), i)
        o_swa = o_swa.reshape(t, BRANCH_WIDTH)
        xs = _merge(xs, g1, o_gla, y_s5, o_dil, o_swa, glu_w, s5_glu_b[i].reshape(1, -1).astype(F32),
                    w_all, w_br, w_o, i)
        xs = _ffn(xs, norm2_g[i].reshape(1, d).astype(F32), w_fg, w_fu, w_fd, gf, i == depth - 1)
    return xs.reshape(b, l, d).astype(x.dtype)
```

```python
import functools
import math

import jax
import jax.numpy as jnp
import numpy as np
from jax import lax
from jax.experimental import pallas as pl
from jax.experimental.pallas import tpu as pltpu

F32 = jnp.float32
BF16 = jnp.bfloat16

D_MODEL = 2048
N_BRANCH = 4
BRANCH_WIDTH = 512
HEAD_DIM = 64
ATT_BLOCK = 128
ROPE_THETA = 500000.0
ROPE_DIM = HEAD_DIM // 4
NORM_EPS = 1e-6
GLA_HEADS = 4
GLA_DK = 64
GLA_DV = BRANCH_WIDTH // GLA_HEADS
GLA_LOWRANK = 16
GLA_TAU = 16.0
S5_GROUP = 16
S5_GROUPS = BRANCH_WIDTH // S5_GROUP
S5_STATE = 64
DIL_CONFIGS = ((128, 1), (512, 4), (2048, 16))
DIL_SPAN = ATT_BLOCK * 16
SWA_HEADS = BRANCH_WIDTH // HEAD_DIM
SWA_KV_HEADS = 2
SWA_WINDOW = 128

LANES = 128
SUBLANES = 8
VMEM_LIMIT = 56 * 1024 * 1024
VMEM_LIMIT_FFN = 60 * 1024 * 1024

C_GQ, C_GK, C_GV, C_GR, C_S5 = 0, 256, 512, 1024, 1536
C_CQ, C_CK, C_CV, C_SQ, C_SK, C_SV, C_GLR = 2048, 2560, 3072, 3584, 4096, 4224, 4352
N_SMALL = 4608
N_ORIG_SMALL = 4368
GLR_ORIG = 1536

TM_PROJ, TN_PROJ = 512, 1536
TM_MERGE, TN_MERGE = 512, 512
TM_FFN, TF_FFN = 1024, 512
FFN_ROW_CHUNK = 512
FFN_DOWN_CHUNK = 512
GLA_CH = 128
GLA_NCS = 4
GLA_SUB = 8
GLA_LEVELS = (8, 16, 32, 64, 128)
PACK_CB = 256
PACK_PIECES = 5
S5_CH = 16
S5_GPB = LANES // S5_GROUP
S5_RT = 2048
S5_TSPLIT = 4
ATT_UNROLL = 8
SWA_UNROLL = 16


def _cparams(sem, vmem_limit=VMEM_LIMIT):
    return pltpu.CompilerParams(dimension_semantics=sem, vmem_limit_bytes=vmem_limit)


def _rms(x, g):
    ms = jnp.mean(x * x, axis=-1, keepdims=True)
    return x * lax.rsqrt(ms + NORM_EPS) * g


def _bdot(a, b):
    return jnp.dot(a, b, preferred_element_type=F32)


def _bdot_nt(a, b):
    return lax.dot_general(a, b, (((1,), (1,)), ((), ())), preferred_element_type=F32)


def _cast_specs(ws, layer, nsteps, step_of):
    ins, outs, shapes = [], [], []
    for w in ws:
        _, r, c = w.shape
        rb = r // nsteps
        assert rb * nsteps == r and rb % (2 * SUBLANES) == 0
        ins.append(pl.BlockSpec((None, rb, c), lambda *g: (layer, step_of(*g), 0)))
        outs.append(pl.BlockSpec((rb, c), lambda *g: (step_of(*g), 0)))
        shapes.append(jax.ShapeDtypeStruct((r, c), BF16))
    return ins, outs, shapes


def _side_cast(in_refs, out_refs):
    for src, dst in zip(in_refs, out_refs):
        dst[...] = src[...].astype(BF16)


def _rope_apply(x, c, sa, sb):
    half = ROPE_DIM // 2
    return x * c + pltpu.roll(x, LANES - half, axis=1) * sa + pltpu.roll(x, half, axis=1) * sb


def _inproj_kernel(x_ref, g_ref, w_ref, o_ref):
    h = _rms(x_ref[...], g_ref[...]).astype(BF16)
    for c0 in range(0, o_ref.shape[1], TN_PROJ):
        cols = slice(c0, c0 + TN_PROJ)
        o_ref[:, cols] = _bdot(h, w_ref[:, cols])


def _inproj(x, g, w_all, layer):
    t, d = x.shape
    n = N_SMALL
    tm = min(TM_PROJ, t)
    assert n % TN_PROJ == 0
    return pl.pallas_call(
        _inproj_kernel,
        grid=(t // tm,),
        in_specs=[
            pl.BlockSpec((tm, d), lambda i: (i, 0)),
            pl.BlockSpec((1, d), lambda i: (0, 0)),
            pl.BlockSpec((None, d, n), lambda i: (layer, 0, 0), pipeline_mode=pl.Buffered(1)),
        ],
        out_specs=pl.BlockSpec((tm, n), lambda i: (i, 0)),
        out_shape=jax.ShapeDtypeStruct((t, n), F32),
        compiler_params=_cparams(("parallel",)),
        name="inproj",
    )(x, g, w_all)


def _rope_kernel(pos_ref, cos_ref, sa_ref, sb_ref):
    pos = pos_ref[0].astype(F32)
    lane = lax.broadcasted_iota(jnp.int32, (SUBLANES, LANES), 1)
    d = lane % HEAD_DIM
    half = ROPE_DIM // 2
    fi = (d % half).astype(F32) / half
    inv = jnp.power(jnp.full((SUBLANES, LANES), ROPE_THETA, F32), -fi)[0:1]
    d1 = d[0:1]
    ang = pos * inv
    c, s = jnp.cos(ang), jnp.sin(ang)
    cos_ref[0] = jnp.where(d1 < ROPE_DIM, c, 1.0)
    sa_ref[0] = jnp.where(d1 < half, -s, 0.0)
    sb_ref[0] = jnp.where((d1 >= half) & (d1 < ROPE_DIM), s, 0.0)


def _rope_tables(positions):
    b, l = positions.shape
    spec = pl.BlockSpec((1, l, LANES), lambda i: (i, 0, 0))
    shp = jax.ShapeDtypeStruct((b, l, LANES), F32)
    return pl.pallas_call(
        _rope_kernel,
        grid=(b,),
        in_specs=[pl.BlockSpec((1, l, 1), lambda i: (i, 0, 0))],
        out_specs=[spec, spec, spec],
        out_shape=[shp, shp, shp],
        compiler_params=_cparams(("parallel",)),
        name="rope_tables",
    )(positions.reshape(b, l, 1))


def _band_bias(max_dist):
    t = ATT_BLOCK
    assert 0 < max_dist <= t
    qi = np.arange(2 * t)[:, None] % t
    kj = np.arange(2 * t)[None, :]
    own = (kj >= t) & (kj - t <= qi)
    prev = (kj < t) & (t + qi - kj <= max_dist)
    ninf = np.float32(-np.inf)
    return np.stack([np.where(own | prev, 0.0, ninf), np.where(own, 0.0, ninf)]).astype(np.float32)


def _band_block(q, kp, ko, vp, vo, bias):
    t = ATT_BLOCK
    lane = lax.broadcasted_iota(jnp.int32, (t, LANES), 1)
    in_a = lane < HEAD_DIM
    q2 = jnp.concatenate([jnp.where(in_a, q, 0.0), jnp.where(in_a, 0.0, q)], axis=0).astype(BF16)
    s = _bdot_nt(q2, jnp.concatenate([kp, ko], axis=0).astype(BF16)) + bias
    m = jnp.max(s, axis=-1, keepdims=True)
    p = jnp.exp(s - m).astype(BF16)
    ones = jnp.ones((2 * t, LANES), BF16)
    vcat = jnp.concatenate([jnp.concatenate([vp, vo], axis=0).astype(BF16), ones], axis=1)
    od = _bdot(p, vcat)
    o = jnp.where(in_a, od[:t, :LANES], od[t:, :LANES])
    df = jnp.where(in_a, od[:t, LANES:], od[t:, LANES:])
    mf = jnp.where(in_a, m[:t], m[t:])
    return mf, df, o


def _dil_kernel(q_ref, k_ref, v_ref, c_ref, sa_ref, sb_ref, bias_ref, o_ref, qs, ks, qd, kd, vd, ms, ds, os_):
    l = q_ref.shape[1]
    c, sa, sb = c_ref[0], sa_ref[0], sb_ref[0]
    qs[...] = _rope_apply(q_ref[0], c, sa, sb) * (HEAD_DIM**-0.5)
    ks[...] = _rope_apply(k_ref[0], c, sa, sb)
    t = ATT_BLOCK
    nblk = DIL_SPAN // t
    dw = DIL_SPAN // t
    per = l // dw
    for r in range(dw):
        src, dst = pl.ds(r, per, stride=dw), slice(r * per, (r + 1) * per)
        qd[dst, :] = qs[src, :]
        kd[dst, :] = ks[src, :]
        vd[dst, :] = v_ref[0, src, :]

    for sblk in range(l // DIL_SPAN):
        base = sblk * DIL_SPAN
        for ci, (_, dil) in enumerate(DIL_CONFIGS):

            def body_wide(r, carry, ci=ci, sblk=sblk):
                rows = pl.ds(pl.multiple_of(r * per + sblk * t, t), t)
                prow = pl.ds(pl.multiple_of(r * per + max(sblk - 1, 0) * t, t), t)
                mf, df, o = _band_block(qd[rows, :], kd[prow, :], kd[rows, :], vd[prow, :], vd[rows, :],
                                        bias_ref[0 if sblk > 0 else 1])
                lrow = pl.ds(r, t, stride=dw)
                ms[ci, lrow, :] = mf
                ds[ci, lrow, :] = df
                os_[ci, lrow, :] = o
                return carry

            if dil == dw:
                lax.fori_loop(0, nblk, body_wide, 0, unroll=ATT_UNROLL)
                continue

            def body(idx, carry, dil=dil, ci=ci, base=base):
                r = idx % dil
                n = idx // dil
                loc = r + dil * t * n
                start = base + loc
                has_prev = start >= dil * t
                pstart = jnp.where(has_prev, start - dil * t, start)
                if dil == 1:
                    start = pl.multiple_of(start, t)
                    pstart = pl.multiple_of(pstart, t)
                    loc = pl.multiple_of(loc, t)
                    rows, prow, lrow = pl.ds(start, t), pl.ds(pstart, t), pl.ds(loc, t)
                else:
                    rows = pl.ds(start, t, stride=dil)
                    prow = pl.ds(pstart, t, stride=dil)
                    lrow = pl.ds(loc, t, stride=dil)
                mf, df, o = _band_block(
                    qs[rows, :], ks[prow, :], ks[rows, :], v_ref[0, prow, :], v_ref[0, rows, :],
                    bias_ref[jnp.where(has_prev, 0, 1)],
                )
                ms[ci, lrow, :] = mf
                ds[ci, lrow, :] = df
                os_[ci, lrow, :] = o
                return carry

            lax.fori_loop(0, nblk, body, 0, unroll=ATT_UNROLL)

        m0, m1, m2 = ms[0], ms[1], ms[2]
        mx = jnp.maximum(jnp.maximum(m0, m1), m2)
        w0, w1, w2 = jnp.exp(m0 - mx), jnp.exp(m1 - mx), jnp.exp(m2 - mx)
        den = w0 * ds[0] + w1 * ds[1] + w2 * ds[2]
        num = w0 * os_[0] + w1 * os_[1] + w2 * os_[2]
        o_ref[0, base : base + DIL_SPAN, :] = (num / den).astype(o_ref.dtype)


def _dilated(proj3, tabs):
    b, l, _ = proj3.shape
    assert l % DIL_SPAN == 0
    assert all(window // dil == ATT_BLOCK for window, dil in DIL_CONFIGS)
    bias = _band_bias(ATT_BLOCK)

    def col(c0):
        return pl.BlockSpec((1, l, LANES), lambda i, p: (i, 0, c0 // LANES + p))

    tab = pl.BlockSpec((1, l, LANES), lambda i, p: (i, 0, 0))
    return pl.pallas_call(
        _dil_kernel,
        grid=(b, BRANCH_WIDTH // LANES),
        in_specs=[col(C_CQ), col(C_CK), col(C_CV), tab, tab, tab, pl.BlockSpec(bias.shape, lambda i, p: (0, 0, 0))],
        out_specs=pl.BlockSpec((1, l, LANES), lambda i, p: (i, 0, p)),
        out_shape=jax.ShapeDtypeStruct((b, l, BRANCH_WIDTH), BF16),
        scratch_shapes=[pltpu.VMEM((l, LANES), F32)] * 5
        + [pltpu.VMEM((len(DIL_CONFIGS), DIL_SPAN, LANES), F32)] * 3,
        compiler_params=_cparams(("parallel", "parallel")),
        name="dilated_attn",
    )(proj3, proj3, proj3, *tabs, jnp.asarray(bias))


def _swa_kernel(*refs, n_cast):
    q_ref, k_ref, v_ref, c_ref, sa_ref, sb_ref, bias_ref, sink_ref = refs[:8]
    o_ref = refs[8 + n_cast]
    qs, ks, vs = refs[-3:]
    _side_cast(refs[8 : 8 + n_cast], refs[9 + n_cast : 9 + 2 * n_cast])
    l = q_ref.shape[1]
    c, sa, sb = c_ref[0], sa_ref[0], sb_ref[0]
    qs[...] = _rope_apply(q_ref[0], c, sa, sb) * (HEAD_DIM**-0.5)
    g = pl.program_id(1) // (SWA_HEADS // SWA_KV_HEADS // 2)
    lane = lax.broadcasted_iota(jnp.int32, (l, LANES), 1)
    keep = (lane // HEAD_DIM) == g
    k, v = _rope_apply(k_ref[0], c, sa, sb), v_ref[0]
    ks[...] = jnp.where(keep, k, pltpu.roll(k, HEAD_DIM, axis=1))
    vs[...] = jnp.where(keep, v, pltpu.roll(v, HEAD_DIM, axis=1))

    sink = sink_ref[0]
    t = ATT_BLOCK

    def body(n, carry):
        start = pl.multiple_of(n * t, t)
        has_prev = n > 0
        pstart = pl.multiple_of(jnp.where(has_prev, start - t, start), t)
        rows, prow = pl.ds(start, t), pl.ds(pstart, t)
        mf, df, o = _band_block(qs[rows, :], ks[prow, :], ks[rows, :], vs[prow, :], vs[rows, :],
                                bias_ref[jnp.where(has_prev, 0, 1)])
        lse = mf + jnp.log(df)
        o_ref[0, rows, :] = (o / df * jax.nn.sigmoid(lse - sink)).astype(o_ref.dtype)
        return carry

    lax.fori_loop(0, l // t, body, 0, unroll=SWA_UNROLL)


def _swa(proj3, tabs, sinks, cast=(), layer=0):
    b, l, _ = proj3.shape
    npair = BRANCH_WIDTH // LANES
    sink_l = jnp.repeat(sinks.astype(F32), HEAD_DIM).reshape(npair, 1, LANES)
    bias = _band_bias(SWA_WINDOW - 1)
    tab = pl.BlockSpec((1, l, LANES), lambda i, p: (i, 0, 0))
    c_in, c_out, c_shape = _cast_specs(cast, layer, b * npair, lambda i, p: i * npair + p)
    return pl.pallas_call(
        functools.partial(_swa_kernel, n_cast=len(cast)),
        grid=(b, npair),
        in_specs=[
            pl.BlockSpec((1, l, LANES), lambda i, p: (i, 0, C_SQ // LANES + p)),
            pl.BlockSpec((1, l, LANES), lambda i, p: (i, 0, C_SK // LANES)),
            pl.BlockSpec((1, l, LANES), lambda i, p: (i, 0, C_SV // LANES)),
            tab,
            tab,
            tab,
            pl.BlockSpec(bias.shape, lambda i, p: (0, 0, 0)),
            pl.BlockSpec((1, 1, LANES), lambda i, p: (p, 0, 0)),
        ]
        + c_in,
        out_specs=[pl.BlockSpec((1, l, LANES), lambda i, p: (i, 0, p))] + c_out,
        out_shape=[jax.ShapeDtypeStruct((b, l, BRANCH_WIDTH), BF16)] + c_shape,
        scratch_shapes=[pltpu.VMEM((l, LANES), F32)] * 3,
        compiler_params=_cparams(("parallel", "parallel")),
        name="swa_attn",
    )(proj3, proj3, proj3, *tabs, jnp.asarray(bias), sink_l, *cast)


@functools.lru_cache(maxsize=None)
def _gla_consts():
    ch = GLA_CH
    r = np.arange(ch)[:, None]
    j = np.arange(ch)[None, :]
    dstack = (j <= r).astype(np.float32)
    lm = []
    for m in GLA_LEVELS[:-1]:
        mk = (((r // m) % 2 == 1) & ((j // m) == (r // m) - 1)).astype(np.float32)
        lm.append(np.tile(mk, (GLA_HEADS, 1)))
    lmask = np.stack(lm)
    dmask = np.tile(((r // GLA_SUB) == (j // GLA_SUB)).astype(np.float32), (1, GLA_HEADS))
    nsub = ch // GLA_SUB
    e = np.zeros((GLA_SUB, GLA_HEADS * GLA_DK, GLA_HEADS * ch), np.float32)
    for u in range(GLA_SUB):
        for h in range(GLA_HEADS):
            for s in range(nsub):
                e[u, h * GLA_DK : (h + 1) * GLA_DK, h * ch + GLA_SUB * s + u] = 1.0
    rr = np.arange(GLA_HEADS * GLA_DV)[:, None]
    cc = np.arange(GLA_HEADS * GLA_DK)[None, :]
    bd = ((rr // GLA_DV) == (cc // GLA_DK)).astype(np.float32)
    return dstack, lmask, dmask, e, bd


def _split3(x):
    hi = x.astype(BF16)
    r1 = x - hi.astype(F32)
    mid = r1.astype(BF16)
    lo = (r1 - mid.astype(F32)).astype(BF16)
    return hi, mid, lo


def _bcast_grp(x, m, u):
    r, w = x.shape
    x3 = x.reshape(r // m, m, w)
    return jnp.broadcast_to(x3[:, u : u + 1, :], x3.shape).reshape(r, w)


def _bcast_sub(x, u):
    return _bcast_grp(x, GLA_SUB, u)


def _gla_kernel(*refs, n_cast):
    (q_ref, k_ref, v_ref, r_ref, glr_ref, a2_ref, ab_ref, ng_ref, dst_ref, lmask_ref, dmask_ref, e_ref,
     bd_ref) = refs[:13]
    o_ref, st_ref = refs[13 + n_cast], refs[-1]
    _side_cast(refs[13 : 13 + n_cast], refs[14 + n_cast : 14 + 2 * n_cast])
    ch = GLA_CH
    rs = q_ref.shape[0]
    ncs = rs // ch
    nlev = len(GLA_LEVELS)
    hk = GLA_HEADS * GLA_DK

    @pl.when(pl.program_id(1) == 0)
    def _():
        st_ref[...] = jnp.zeros_like(st_ref)

    q = q_ref[...] * (GLA_DK**-0.5)
    k = k_ref[...]
    z = _bdot(glr_ref[...].astype(BF16), a2_ref[...]) + ab_ref[...]
    g = (jnp.minimum(z, 0.0) - jnp.log(1.0 + jnp.exp(-jnp.abs(z)))) * (1.0 / GLA_TAU)
    g3 = _split3(g)
    dst = dst_ref[...]

    cums, excl = [], []
    for c in range(ncs):
        rows = slice(c * ch, (c + 1) * ch)
        cums.append(_bdot(dst, g3[0][rows]) + _bdot(dst, g3[1][rows]) + _bdot(dst, g3[2][rows]))
        excl.append(cums[c] - g[rows])

    def eq(c, li):
        m = GLA_LEVELS[li]
        return cums[c] - _bcast_grp(excl[c], m, 0)

    def ek(c, li):
        m = GLA_LEVELS[li]
        return _bcast_grp(cums[c], m, m - 1) - cums[c]

    cs = jnp.concatenate([eq(c, 0) for c in range(ncs)], axis=0)
    tsub = lax.broadcasted_iota(jnp.int32, (rs, hk), 0) % GLA_SUB
    arep = jnp.zeros((rs, GLA_HEADS * ch), F32)
    for u in range(GLA_SUB):
        dec = jnp.exp(jnp.where(tsub >= u, cs - _bcast_sub(cs, u), -jnp.inf))
        p = q * _bcast_sub(k, u) * dec
        arep = arep + _bdot(p.astype(BF16), e_ref[u])
    arep = arep * jnp.concatenate([dmask_ref[...]] * ncs, axis=0)

    lane_k = lax.broadcasted_iota(jnp.int32, (ch, hk), 1) // GLA_DK
    ng = ng_ref[...]
    bd = bd_ref[...]
    for c in range(ncs):
        rows = slice(c * ch, (c + 1) * ch)
        qc, kc = q[rows], k[rows]
        vc = v_ref[rows, :]
        aoff = jnp.zeros((GLA_HEADS * ch, ch), F32)
        for li in range(nlev - 1):
            qe = qc * jnp.exp(eq(c, li))
            ke = (kc * jnp.exp(ek(c, li))).astype(BF16)
            qst = jnp.concatenate([jnp.where(lane_k == h, qe, 0.0) for h in range(GLA_HEADS)], axis=0).astype(BF16)
            aoff = aoff + _bdot_nt(qst, ke) * lmask_ref[li]
        cum = eq(c, nlev - 1)
        st = st_ref[...]
        o_inter = _bdot_nt((qc * jnp.exp(cum)).astype(BF16), st.astype(BF16))
        vb = vc.astype(BF16)
        outs = []
        for h in range(GLA_HEADS):
            a_h = aoff[h * ch : (h + 1) * ch] + arep[rows, h * ch : (h + 1) * ch]
            o_h = _bdot(a_h.astype(BF16), vb[:, h * GLA_DV : (h + 1) * GLA_DV]) + o_inter[:, h * GLA_DV : (h + 1) * GLA_DV]
            ms = jnp.mean(o_h * o_h, axis=-1, keepdims=True)
            outs.append(o_h * lax.rsqrt(ms + NORM_EPS))
        o = jnp.concatenate(outs, axis=1) * ng
        rc = r_ref[rows, :]
        o_ref[rows, :] = (o * (rc * jax.nn.sigmoid(rc))).astype(o_ref.dtype)
        ke_last = (kc * jnp.exp(ek(c, nlev - 1))).astype(BF16)
        kv = _bdot(vc.T.astype(BF16), ke_last)
        st_ref[...] = st * jnp.exp(cum[ch - 1 : ch, :]) + kv * bd


def _gla(proj, b, a2, ab, ng, cast=(), layer=0):
    t = proj.shape[0]
    l = t // b
    rs = GLA_CH * GLA_NCS
    assert l % rs == 0
    ns = l // rs
    dstack, lmask, dmask, e, bd = _gla_consts()
    c_in, c_out, c_shape = _cast_specs(cast, layer, b * ns, lambda i, s: i * ns + s)
    a2p = jnp.zeros((LANES, GLA_HEADS * GLA_DK), BF16).at[:GLA_LOWRANK].set(a2.astype(BF16))

    def rowblk(w, c0):
        return pl.BlockSpec((rs, w), lambda i, s: (i * ns + s, c0 // w))

    def full(shape):
        nd = len(shape)
        return pl.BlockSpec(shape, lambda i, s: (0,) * nd)

    hk, hv = GLA_HEADS * GLA_DK, GLA_HEADS * GLA_DV
    return pl.pallas_call(
        functools.partial(_gla_kernel, n_cast=len(cast)),
        grid=(b, ns),
        in_specs=[
            rowblk(hk, C_GQ),
            rowblk(hk, C_GK),
            rowblk(hv, C_GV),
            rowblk(hv, C_GR),
            rowblk(LANES, C_GLR),
            full((LANES, hk)),
            full((1, hk)),
            full((1, hv)),
            full(dstack.shape),
            full(lmask.shape),
            full(dmask.shape),
            full(e.shape),
            full(bd.shape),
        ]
        + c_in,
        out_specs=[pl.BlockSpec((rs, hv), lambda i, s: (i * ns + s, 0))] + c_out,
        out_shape=[jax.ShapeDtypeStruct((t, hv), BF16)] + c_shape,
        scratch_shapes=[pltpu.VMEM((hv, hk), F32)],
        compiler_params=_cparams(("parallel", "arbitrary")),
        name="gla",
    )(
        proj, proj, proj, proj, proj, a2p, ab.reshape(1, hk).astype(F32), ng.reshape(1, hv).astype(F32),
        jnp.asarray(dstack, BF16), jnp.asarray(lmask), jnp.asarray(dmask), jnp.asarray(e, BF16), jnp.asarray(bd),
        *cast,
    )


def _s5_params(lam_re, lam_im, log_dt, b_re, b_im, c_re, c_im, d, nch):
    f = F32
    cs = S5_CH
    dt = jnp.exp(log_dt.astype(f))[:, None]
    lr, li = lam_re.astype(f), lam_im.astype(f)
    mag = jnp.exp(lr * dt)
    ab_re, ab_im = mag * jnp.cos(li * dt), mag * jnp.sin(li * dt)
    den = lr * lr + li * li
    z_re = ((ab_re - 1.0) * lr + ab_im * li) / den
    z_im = (ab_im * lr - (ab_re - 1.0) * li) / den
    br, bi = b_re.astype(f), b_im.astype(f)
    bb_re = z_re[..., None] * br - z_im[..., None] * bi
    bb_im = z_re[..., None] * bi + z_im[..., None] * br

    def apow(p):
        p = jnp.asarray(p, f)[:, None, None]
        m = jnp.exp(p * (lr * dt))
        return m * jnp.cos(p * (li * dt)), m * jnp.sin(p * (li * dt))

    cr, ci = c_re.astype(f), c_im.astype(f)
    p_re, p_im = apow(np.arange(cs + 1))
    nb, gpb = S5_GROUPS // S5_GPB, S5_GPB
    hw = gpb * S5_STATE
    eye = jnp.eye(gpb, dtype=f)

    def b_blockdiag(x):
        x = x.reshape(nb, gpb, S5_STATE, S5_GROUP).transpose(0, 1, 3, 2)
        x = (x[:, :, :, None, :] * eye[None, :, None, :, None]).reshape(nb, LANES, hw)
        return jnp.concatenate([x, x], axis=-1)

    def c_blockdiag(x):
        x = x.reshape(nb, gpb, S5_GROUP, S5_STATE).transpose(0, 1, 3, 2)
        return (x[:, :, :, None, :] * eye[None, :, None, :, None]).reshape(nb, hw, LANES)

    def lanes(a, b_):
        p = a.shape[0]
        return jnp.concatenate([a.reshape(p, nb, hw), b_.reshape(p, nb, hw)], axis=-1).transpose(1, 0, 2)

    rev = np.arange(cs - 1, -1, -1)
    pr, pi = p_re[rev], p_im[rev]
    pa, pb = lanes(pr, pi), lanes(-pi, pr)
    acol_re = jnp.broadcast_to(ab_re.reshape(nb, hw, 1), (nb, hw, LANES))
    acol_im = jnp.broadcast_to(ab_im.reshape(nb, hw, 1), (nb, hw, LANES))
    nstep = max(1, int(math.log2(nch)))
    s_re, s_im = apow(cs * (2 ** np.arange(nstep)))
    ar = s_re.reshape(nstep, nb, hw).transpose(1, 0, 2)
    ai = s_im.reshape(nstep, nb, hw).transpose(1, 0, 2)
    dd = d.astype(f).reshape(nb, 1, LANES)
    return (b_blockdiag(bb_re), b_blockdiag(bb_im), c_blockdiag(cr), c_blockdiag(ci), acol_re, acol_im,
            pa, pb, ar, ai, dd)


def _s5_kernel(u_ref, br_ref, bi_ref, cr_ref, ci_ref, acr_ref, aci_ref, pa_ref, pb_ref, ar_ref, ai_ref, d_ref,
               y_ref, mb_s, mc_s, kc_s, *, nchb):
    cs = S5_CH
    t = u_ref.shape[0]
    nch = t // cs
    hw = S5_GPB * S5_STATE
    br, bi = br_ref[0], bi_ref[0]
    pa, pb = pa_ref[0], pb_ref[0]
    for j in range(cs):
        mb_s[j * LANES : (j + 1) * LANES, :] = (br * pa[j : j + 1] + bi * pb[j : j + 1]).astype(BF16)
    x_re, x_im = cr_ref[0], ci_ref[0]
    kc = _bdot(mb_s[...], jnp.concatenate([x_re, -x_im], axis=0).astype(BF16)).astype(BF16)
    a_re, a_im = acr_ref[0], aci_ref[0]
    for tt in range(cs):
        x_re, x_im = x_re * a_re - x_im * a_im, x_re * a_im + x_im * a_re
        mc_s[:hw, tt * LANES : (tt + 1) * LANES] = x_re.astype(BF16)
        mc_s[hw:, tt * LANES : (tt + 1) * LANES] = (-x_im).astype(BF16)
    gl = cs // S5_TSPLIT * LANES
    for qq in range(S5_TSPLIT):
        kc_s[:, qq * LANES : (qq + 1) * LANES] = kc[(S5_TSPLIT - 1 - qq) * gl : (S5_TSPLIT - qq) * gl]
    ucat = jnp.concatenate([u_ref[pl.ds(j, nch, stride=cs), :] for j in range(cs)], axis=1).astype(BF16)
    e_all = _bdot(ucat, mb_s[...])
    cidx = lax.broadcasted_iota(jnp.int32, (nch, LANES), 0) % nchb
    ar, ai = ar_ref[0], ai_ref[0]
    hp_re, hp_im = [], []
    for g in range(hw // LANES):
        lo = g * LANES
        e_re, e_im = e_all[:, lo : lo + LANES], e_all[:, hw + lo : hw + lo + LANES]
        for kstep in range(ar.shape[0]):
            s = 2**kstep
            if s >= nchb:
                break
            s_re = jnp.where(cidx >= s, pltpu.roll(e_re, s, axis=0), 0.0)
            s_im = jnp.where(cidx >= s, pltpu.roll(e_im, s, axis=0), 0.0)
            k_re, k_im = ar[kstep : kstep + 1, lo : lo + LANES], ai[kstep : kstep + 1, lo : lo + LANES]
            e_re, e_im = e_re + s_re * k_re - s_im * k_im, e_im + s_re * k_im + s_im * k_re
        hp_re.append(jnp.where(cidx >= 1, pltpu.roll(e_re, 1, axis=0), 0.0).astype(BF16))
        hp_im.append(jnp.where(cidx >= 1, pltpu.roll(e_im, 1, axis=0), 0.0).astype(BF16))
    ycar = _bdot(jnp.concatenate(hp_re + hp_im, axis=1), mc_s[...])
    for tt in range(cs):
        y_ref[pl.ds(tt, nch, stride=cs), :] = ycar[:, tt * LANES : (tt + 1) * LANES]

    rt = min(S5_RT, t)
    rmod = lax.broadcasted_iota(jnp.int32, (rt, LANES), 0) % cs
    dvec = d_ref[0]
    glag = cs // S5_TSPLIT

    def tile(i, carry):
        rows = pl.ds(pl.multiple_of(i * rt, rt), rt)
        ut = u_ref[rows, :]
        ush = [jnp.where(rmod >= tau, pltpu.roll(ut, tau, axis=0), 0.0) for tau in range(glag - 1, 0, -1)] + [ut]
        zz = _bdot(jnp.concatenate(ush, axis=1).astype(BF16), kc_s[...])
        acc = y_ref[rows, :] + dvec * ut + zz[:, :LANES]
        for qq in range(1, S5_TSPLIT):
            part = pltpu.roll(zz[:, qq * LANES : (qq + 1) * LANES], qq * glag, axis=0)
            acc = acc + jnp.where(rmod >= qq * glag, part, 0.0)
        y_ref[rows, :] = acc
        return carry

    lax.fori_loop(0, t // rt, tile, 0)


def _s5_core(proj, b, params):
    t = proj.shape[0]
    nchb = t // b // S5_CH
    sw = S5_GPB * 2 * S5_STATE

    def bspec(a):
        return pl.BlockSpec((1,) + a.shape[1:], lambda p: (p, 0, 0))

    return pl.pallas_call(
        functools.partial(_s5_kernel, nchb=nchb),
        grid=(S5_GROUPS // S5_GPB,),
        in_specs=[pl.BlockSpec((t, LANES), lambda p: (0, C_S5 // LANES + p))] + [bspec(a) for a in params],
        out_specs=pl.BlockSpec((t, LANES), lambda p: (0, p)),
        out_shape=jax.ShapeDtypeStruct((t, BRANCH_WIDTH), F32),
        scratch_shapes=[
            pltpu.VMEM((S5_CH * LANES, sw), BF16),
            pltpu.VMEM((sw, S5_CH * LANES), BF16),
            pltpu.VMEM((S5_CH // S5_TSPLIT * LANES, S5_TSPLIT * LANES), BF16),
        ],
        compiler_params=_cparams(("parallel",)),
        name="s5_core",
    )(proj, *params)


def _merge_kernel(
    x_ref, g1_ref, gla_ref, ys5_ref, dil_ref, swa_ref, gluw_ref, glub_ref, wg0, wg1, wg2, wg3, wb_ref, wo_ref,
    o_ref, h_s, s5_s,
):
    j = pl.program_id(1)

    @pl.when(j == 0)
    def _():
        h_s[...] = _rms(x_ref[...], g1_ref[...]).astype(BF16)
        zz = jax.nn.gelu(ys5_ref[...])
        gate = jax.nn.sigmoid(_bdot(zz.astype(BF16), gluw_ref[...]) + glub_ref[...])
        s5_s[...] = (zz * gate).astype(BF16)
        o_ref[...] = x_ref[...]

    h = h_s[...]
    branches = (gla_ref[...], s5_s[...], dil_ref[...], swa_ref[...])
    mixed = None
    for m, (wg, br) in enumerate(zip((wg0, wg1, wg2, wg3), branches)):
        term = jax.nn.sigmoid(_bdot(h, wg[...])) * _bdot(br, wb_ref[m])
        mixed = term if mixed is None else mixed + term
    mixed = mixed.astype(BF16)
    d = o_ref.shape[1]
    for c0 in range(0, d, FFN_DOWN_CHUNK):
        cols = slice(c0, c0 + FFN_DOWN_CHUNK)
        o_ref[:, cols] += _bdot(mixed, wo_ref[:, cols])


def _merge(x, g1, o_gla, y_s5, o_dil, o_swa, gluw, glub, w_all, wb, wo, layer):
    t, d = x.shape
    tm, tn = min(TM_MERGE, t), TN_MERGE
    nj = d // tn
    bw = BRANCH_WIDTH

    def rowblk(w):
        return pl.BlockSpec((tm, w), lambda i, j: (i, 0))

    def gate_spec(m):
        return pl.BlockSpec((None, d, tn), lambda i, j, m=m: (layer, 0, N_SMALL // tn + m * nj + j))

    return pl.pallas_call(
        _merge_kernel,
        grid=(t // tm, nj),
        in_specs=[
            rowblk(d),
            pl.BlockSpec((1, d), lambda i, j: (0, 0)),
            rowblk(bw),
            rowblk(bw),
            rowblk(bw),
            rowblk(bw),
            pl.BlockSpec((None, bw, bw), lambda i, j: (layer, 0, 0)),
            pl.BlockSpec((1, bw), lambda i, j: (0, 0)),
            gate_spec(0),
            gate_spec(1),
            gate_spec(2),
            gate_spec(3),
            pl.BlockSpec((N_BRANCH, bw, tn), lambda i, j: (0, 0, j)),
            pl.BlockSpec((tn, d), lambda i, j: (j, 0)),
        ],
        out_specs=rowblk(d),
        out_shape=jax.ShapeDtypeStruct((t, d), F32),
        scratch_shapes=[pltpu.VMEM((tm, d), BF16), pltpu.VMEM((tm, bw), BF16)],
        compiler_params=_cparams(("parallel", "arbitrary")),
        name="merge",
    )(x, g1, o_gla, y_s5, o_dil, o_swa, gluw, glub, w_all, w_all, w_all, w_all, wb, wo)


def _ffn_kernel(x_ref, g2_ref, wg_ref, wu_ref, wd_ref, gf_ref, o_ref, h_s, *, final_norm):
    j = pl.program_id(1)
    tm, d = o_ref.shape
    chunks = [slice(r0, r0 + FFN_ROW_CHUNK) for r0 in range(0, tm, FFN_ROW_CHUNK)]

    @pl.when(j == 0)
    def _():
        for rows in chunks:
            h_s[rows, :] = _rms(x_ref[rows, :], g2_ref[...]).astype(BF16)
            o_ref[rows, :] = x_ref[rows, :]

    for rows in chunks:
        h = h_s[rows, :]
        gate = _bdot(h, wg_ref[...])
        act = ((gate * jax.nn.sigmoid(gate)) * _bdot(h, wu_ref[...])).astype(BF16)
        for c0 in range(0, d, FFN_DOWN_CHUNK):
            cols = slice(c0, c0 + FFN_DOWN_CHUNK)
            o_ref[rows, cols] += _bdot(act, wd_ref[:, cols])

    if final_norm:

        @pl.when(j == pl.num_programs(1) - 1)
        def _():
            for rows in chunks:
                o_ref[rows, :] = _rms(o_ref[rows, :], gf_ref[...])


def _ffn(x, g2, wg, wu, wd, gf, final_norm):
    t, d = x.shape
    fh = wg.shape[-1]
    tm, tf = min(TM_FFN, t), TF_FFN
    return pl.pallas_call(
        functools.partial(_ffn_kernel, final_norm=final_norm),
        grid=(t // tm, fh // tf),
        in_specs=[
            pl.BlockSpec((tm, d), lambda i, j: (i, 0)),
            pl.BlockSpec((1, d), lambda i, j: (0, 0)),
            pl.BlockSpec((d, tf), lambda i, j: (0, j)),
            pl.BlockSpec((d, tf), lambda i, j: (0, j)),
            pl.BlockSpec((tf, d), lambda i, j: (j, 0)),
            pl.BlockSpec((1, d), lambda i, j: (0, 0)),
        ],
        out_specs=pl.BlockSpec((tm, d), lambda i, j: (i, 0)),
        out_shape=jax.ShapeDtypeStruct((t, d), F32),
        scratch_shapes=[pltpu.VMEM((tm, d), BF16)],
        compiler_params=_cparams(("parallel", "arbitrary"), VMEM_LIMIT_FFN),
        name="ffn",
    )(x, g2, wg, wu, wd, gf)


def _pack_kernel(*refs):
    o_ref = refs[-1]
    for piece, w_ref in enumerate(refs[:-1]):
        x = w_ref[...]
        row = lax.broadcasted_iota(jnp.int32, x.shape, 0)
        blk = pl.program_id(1) * PACK_PIECES + piece
        keep = jnp.where(blk == C_GLR // PACK_CB, GLA_LOWRANK, PACK_CB)
        o_ref[:, piece * PACK_CB : (piece + 1) * PACK_CB] = jnp.where(row < keep, x, 0.0).T.astype(BF16)


def _pack_row0(c):
    cb = PACK_CB
    u = GLA_LOWRANK
    front = jnp.where(c < GLR_ORIG // cb, c * (cb // u), c * (cb // u) + 1)
    back = jnp.where(c == C_GLR // cb, GLR_ORIG // u, c * (cb // u) - (N_SMALL - N_ORIG_SMALL) // u)
    return jnp.where(c < C_GLR // cb, front, back) * u


def _pack_w_in(w_in):
    depth, d, d_in = w_in.shape
    n_out = N_SMALL + d_in - N_ORIG_SMALL
    cb = PACK_CB
    assert GLR_ORIG % cb == 0 and C_GLR % cb == 0 and N_SMALL % cb == 0 and n_out % cb == 0
    assert C_GLR == N_ORIG_SMALL - GLA_LOWRANK and C_GLR + cb == N_SMALL
    np_ = PACK_PIECES
    assert n_out % (cb * np_) == 0
    w_t = jnp.swapaxes(w_in, 1, 2)
    return pl.pallas_call(
        _pack_kernel,
        grid=(depth, n_out // (cb * np_)),
        in_specs=[
            pl.BlockSpec((None, pl.Element(cb), pl.Element(d)), lambda i, c, p=p: (i, _pack_row0(c * np_ + p), 0))
            for p in range(np_)
        ],
        out_specs=pl.BlockSpec((None, d, cb * np_), lambda i, c: (i, 0, c)),
        out_shape=jax.ShapeDtypeStruct((depth, d, n_out), BF16),
        compiler_params=_cparams(("parallel", "parallel")),
        name="pack_w_in",
    )(*([w_t] * np_))


def kernel(x, positions, norm1_g, w_in, gla_a2, gla_a_b, gla_norm_g, s5_lambda_re, s5_lambda_im, s5_log_dt, s5_b_re, s5_b_im, s5_c_re, s5_c_im, s5_d, s5_glu_w, s5_glu_b, swa_sinks, w_branch, w_out, norm2_g, w_ffn_gate, w_ffn_up, w_ffn_down, final_norm_g):
    b, l, d = x.shape
    t = b * l
    depth = w_in.shape[0]
    xs = x.reshape(t, d).astype(F32)
    w_all = _pack_w_in(w_in)
    w_br32 = w_branch.reshape(depth, N_BRANCH * BRANCH_WIDTH, d)
    glu_w = s5_glu_w.astype(BF16)
    tabs = _rope_tables(positions)
    gf = final_norm_g.reshape(1, d).astype(F32)
    s5_all = jax.vmap(functools.partial(_s5_params, nch=l // S5_CH))(
        s5_lambda_re, s5_lambda_im, s5_log_dt, s5_b_re, s5_b_im, s5_c_re, s5_c_im, s5_d)
    for i in range(depth):
        g1 = norm1_g[i].reshape(1, d).astype(F32)
        proj = _inproj(xs, g1, w_all, i)
        proj3 = proj.reshape(b, l, N_SMALL)
        o_gla, w_fg, w_fu, w_br, w_o = _gla(proj, b, gla_a2[i], gla_a_b[i], gla_norm_g[i],
                                            (w_ffn_gate, w_ffn_up, w_br32, w_out), i)
        w_br = w_br.reshape(N_BRANCH, BRANCH_WIDTH, d)
        y_s5 = _s5_core(proj, b, [a[i] for a in s5_all])
        o_dil = _dilated(proj3, tabs).reshape(t, BRANCH_WIDTH)
        o_swa, w_fd = _swa(proj3, tabs, swa_sinks[i], (w_ffn_down,), i)
        o_swa = o_swa.reshape(t, BRANCH_WIDTH)
        xs = _merge(xs, g1, o_gla, y_s5, o_dil, o_swa, glu_w, s5_glu_b[i].reshape(1, -1).astype(F32),
                    w_all, w_br, w_o, i)
        xs = _ffn(xs, norm2_g[i].reshape(1, d).astype(F32), w_fg, w_fu, w_fd, gf, i == depth - 1)
    return xs.reshape(b, l, d).astype(x.dtype)
```
